```python
import jax, jax.numpy as jnp
from jax import lax
import numpy as np

D_MODEL = 1024
BATCH = 2
SEQ = 8192
DEPTH = 1
DEC_BATCH = 128
DEC_SEQ = 8
PAST_LEN = 8192
PAGE_SIZE = 128

D_MIX = D_MODEL
HEAD_DIM = 64
N_RET_HEADS = D_MIX // (2 * HEAD_DIM)
N_ATT_HEADS = D_MIX // (2 * HEAD_DIM)
D_RET = N_RET_HEADS * HEAD_DIM
D_ATT = N_ATT_HEADS * HEAD_DIM
D_IN = 4 * D_RET + 3 * D_ATT
RET_CHUNK = 128
DIL_PATTERNS = ((128, 1), (512, 4), (2048, 16))
MAX_WINDOW = 2048
N_EXPERTS = 32
TOP_K = 4
D_FF = D_MODEL
SWIGLU_LIMIT = 7.0
SWIGLU_ALPHA = 1.702
MOE_BLOCK = 128
EPS = 1e-6
NEG_INF = -1e30

kernel_name = 'hybrid_retention_dilated_alibi_moe_step'


def rms_norm(x, g):
    xf = x.astype(jnp.float32)
    y = xf * lax.rsqrt(jnp.mean(xf * xf, axis=-1, keepdims=True) + EPS)
    return (y * g.astype(jnp.float32)).astype(x.dtype)


def ret_log_decay():
    return jnp.log(1.0 - 2.0 ** (-5.0 - jnp.arange(N_RET_HEADS, dtype=jnp.float32)))


def alibi_slopes():
    return 2.0 ** (-8.0 * (jnp.arange(N_ATT_HEADS, dtype=jnp.float32) + 1.0) / N_ATT_HEADS)


def ada_mod(c, w_ada, b_ada):
    m = (jax.nn.silu(c) @ w_ada + b_ada)[:, None, :]
    return jnp.split(m, 6, axis=-1)


def hybrid_proj(h, w_in):
    B, S, _ = h.shape
    z = h @ w_in
    cuts = [D_RET, 2 * D_RET, 3 * D_RET, 4 * D_RET, 4 * D_RET + D_ATT, 4 * D_RET + 2 * D_ATT]
    qr, kr, vr, gr, qa, ka, va = jnp.split(z, cuts, axis=-1)
    hd = lambda t: t.reshape(B, S, -1, HEAD_DIM)
    return hd(qr), hd(kr), hd(vr), gr, hd(qa), hd(ka), hd(va)


def retention_inputs(qr, kr, vr):
    f = jnp.float32
    return qr.astype(f), kr.astype(f) * HEAD_DIM ** -0.5, vr.astype(f)


def retention_chunk(q, k, v, state):
    L = q.shape[1]
    lg = ret_log_decay()
    pos = jnp.arange(L, dtype=jnp.float32)
    diff = pos[:, None] - pos[None, :]
    causal = diff >= 0
    dmask = jnp.where(causal[None], jnp.exp(jnp.where(causal, diff, 0.0)[None] * lg[:, None, None]), 0.0)
    scores = jnp.einsum('bihe,bjhe->bhij', q, k) * dmask[None]
    inner = jnp.einsum('bhij,bjhe->bihe', scores, v)
    cross = jnp.einsum('bihe,bhef->bihf', q, state) * jnp.exp((pos[:, None] + 1.0) * lg[None, :])[None, :, :, None]
    kdec = k * jnp.exp((L - 1.0 - pos)[:, None] * lg[None, :])[None, :, :, None]
    new_state = jnp.exp(L * lg)[None, :, None, None] * state + jnp.einsum('bjhe,bjhf->bhef', kdec, v)
    return inner + cross, new_state


def retention_prompt(q, k, v):
    B, S, H, E = q.shape
    nc = S // RET_CHUNK
    to_chunks = lambda t: t.reshape(B, nc, RET_CHUNK, H, E).swapaxes(0, 1)

    def step(state, qkv):
        o, state = retention_chunk(qkv[0], qkv[1], qkv[2], state)
        return state, o

    state, o = lax.scan(step, jnp.zeros((B, H, E, E), jnp.float32), (to_chunks(q), to_chunks(k), to_chunks(v)))
    return o.swapaxes(0, 1).reshape(B, S, H, E), state


def dilated_prompt_group(q, k, v, window, dilation):
    B, S, H, E = q.shape
    ns = window // dilation
    L = S // dilation
    nb = -(-L // ns)
    Lp = nb * ns

    def prep(t):
        t = t.reshape(B, L, dilation, H, E)
        t = jnp.pad(t, ((0, 0), (0, Lp - L), (0, 0), (0, 0), (0, 0)))
        return t.reshape(B, nb, ns, dilation, H, E)

    def with_prev(t):
        prev = jnp.pad(t[:, :-1], ((0, 0), (1, 0), (0, 0), (0, 0), (0, 0), (0, 0)))
        return jnp.concatenate([prev, t], axis=2)

    qb = prep(q)
    kk = with_prev(prep(k))
    vv = with_prev(prep(v))
    s = jnp.einsum('bnqrhe,bnkrhe->bnrhqk', qb, kk, preferred_element_type=jnp.float32) * HEAD_DIM ** -0.5
    qi = jnp.arange(ns)[:, None]
    kj = jnp.arange(2 * ns)[None, :]
    step = qi + ns - kj
    in_win = (step >= 0) & (step <= ns)
    has_key = (jnp.arange(nb) > 0)[:, None, None] | (kj >= ns)[None]
    mask = in_win[None] & has_key
    bias = -alibi_slopes()[:, None, None] * (step * dilation).astype(jnp.float32)[None]
    s = jnp.where(mask[None, :, None, None], s + bias[None, None, None], NEG_INF)
    m = s.max(axis=-1)
    p = jnp.exp(s - m[..., None])
    l = p.sum(axis=-1)
    o = jnp.einsum('bnrhqk,bnkrhe->bnqrhe', p, vv)
    o = o.reshape(B, Lp, dilation, H, E)[:, :L].reshape(B, S, H, E)
    back = lambda t: t.transpose(0, 1, 4, 2, 3).reshape(B, Lp, dilation, H)[:, :L].reshape(B, S, H)
    return o, back(m), back(l)


def dilated_sample_group(q, ctx_k, ctx_v, window, dilation):
    B, T, H, E = q.shape
    buf = ctx_k.shape[1] - T
    ns = window // dilation
    steps = jnp.arange(ns + 1)
    idx = (buf + jnp.arange(T))[:, None] - steps[None, :] * dilation
    valid = idx >= 0
    idx = jnp.maximum(idx, 0)
    kg = ctx_k[:, idx]
    vg = ctx_v[:, idx]
    s = jnp.einsum('bthe,btjhe->bhtj', q, kg, preferred_element_type=jnp.float32) * HEAD_DIM ** -0.5
    bias = -alibi_slopes()[:, None, None] * (steps * dilation).astype(jnp.float32)[None, None, :]
    s = jnp.where(valid[None, None], s + bias[None], NEG_INF)
    m = s.max(axis=-1)
    p = jnp.exp(s - m[..., None])
    l = p.sum(axis=-1)
    o = jnp.einsum('bhtj,btjhe->bthe', p, vg)
    return o, m.transpose(0, 2, 1), l.transpose(0, 2, 1)


def combine_groups(parts):
    mx = jnp.stack([m for _, m, _ in parts]).max(axis=0)
    num = sum(o * jnp.exp(m - mx)[..., None] for o, m, _ in parts)
    den = sum(l * jnp.exp(m - mx) for _, m, l in parts)
    return num / den[..., None]


def mixer_out(ret_o, gr, att_o, g_ret, w_out, dtype):
    B, S = ret_o.shape[:2]
    mu = jnp.mean(ret_o, axis=-1, keepdims=True)
    var = jnp.mean(jnp.square(ret_o - mu), axis=-1, keepdims=True)
    ret_n = ((ret_o - mu) * lax.rsqrt(var + EPS)).reshape(B, S, D_RET) * g_ret.astype(jnp.float32)
    ret_y = (jax.nn.silu(gr.astype(jnp.float32)) * ret_n).astype(dtype)
    att_y = att_o.reshape(B, S, D_ATT).astype(dtype)
    return jnp.concatenate([ret_y, att_y], axis=-1) @ w_out


def moe(h, w_router, b_router, w_gu, b_gu, w_down, b_down):
    N, D = h.shape
    logits = jnp.dot(h, w_router, preferred_element_type=jnp.float32) + b_router.astype(jnp.float32)
    top_val, top_idx = lax.top_k(logits, TOP_K)
    top_w = jax.nn.softmax(top_val, axis=-1)
    n_assign = N * TOP_K
    flat_e = top_idx.reshape(-1)
    flat_tok = jnp.repeat(jnp.arange(N, dtype=jnp.int32), TOP_K)
    flat_w = top_w.reshape(-1)
    order = jnp.argsort(flat_e)
    se, stok, sw = flat_e[order], flat_tok[order], flat_w[order]
    counts = jnp.bincount(flat_e, length=N_EXPERTS)
    padded = (counts + MOE_BLOCK - 1) // MOE_BLOCK * MOE_BLOCK
    pad_end = jnp.cumsum(padded)
    pad_start = pad_end - padded
    start = jnp.cumsum(counts) - counts
    dest = pad_start[se] + jnp.arange(n_assign) - start[se]
    n_blocks = n_assign // MOE_BLOCK + N_EXPERTS
    n_rows = n_blocks * MOE_BLOCK
    row_tok = jnp.zeros((n_rows,), jnp.int32).at[dest].set(stok)
    row_w = jnp.zeros((n_rows,), jnp.float32).at[dest].set(sw)
    block_e = jnp.minimum(jnp.searchsorted(pad_end, jnp.arange(n_blocks) * MOE_BLOCK, side='right'), N_EXPERTS - 1)

    def expert_block(args):
        toks, e = args
        gu = h[toks] @ w_gu[e] + b_gu[e]
        gate = jnp.minimum(gu[:, 0::2], SWIGLU_LIMIT)
        up = jnp.clip(gu[:, 1::2], -SWIGLU_LIMIT, SWIGLU_LIMIT)
        act = gate * jax.nn.sigmoid(SWIGLU_ALPHA * gate) * (up + 1.0)
        return act @ w_down[e] + b_down[e]

    out = lax.map(expert_block, (row_tok.reshape(n_blocks, MOE_BLOCK), block_e)).reshape(n_rows, D)
    y = jnp.zeros((N, D), jnp.float32).at[row_tok].add(out.astype(jnp.float32) * row_w[:, None])
    return y.astype(h.dtype)


def prompt_mixer(h, w_in):
    qr, kr, vr, gr, qa, ka, va = hybrid_proj(h, w_in)
    q, k, v = retention_inputs(qr, kr, vr)
    ret_o, ret_state = retention_prompt(q, k, v)
    att_o = combine_groups([dilated_prompt_group(qa, ka, va, w, d) for (w, d) in DIL_PATTERNS])
    keep = min(MAX_WINDOW, h.shape[1])
    return ret_o, gr, att_o, ret_state, ka[:, -keep:], va[:, -keep:]


def sample_mixer(h, w_in, state_ret, cache_k, cache_v):
    qr, kr, vr, gr, qa, ka, va = hybrid_proj(h, w_in)
    q, k, v = retention_inputs(qr, kr, vr)
    ret_o, ret_state = retention_chunk(q, k, v, state_ret.astype(jnp.float32))
    ctx_k = jnp.concatenate([cache_k.astype(ka.dtype), ka], axis=1)
    ctx_v = jnp.concatenate([cache_v.astype(va.dtype), va], axis=1)
    att_o = combine_groups([dilated_sample_group(qa, ctx_k, ctx_v, w, d) for (w, d) in DIL_PATTERNS])
    return ret_o, gr, att_o, ret_state, ka, va


def decoder_layer(x, c, mixer, w_ada, b_ada, g_pre_mix, g_post_mix, g_pre_ffn, g_post_ffn,
                  g_ret, w_out, w_router, b_router, w_gu, b_gu, w_down, b_down):
    sh_a, sc_a, gt_a, sh_f, sc_f, gt_f = ada_mod(c, w_ada, b_ada)
    h = rms_norm(x, g_pre_mix) * (1.0 + sc_a) + sh_a
    ret_o, gr, att_o, ret_state, win_k, win_v = mixer(h)
    mix = mixer_out(ret_o, gr, att_o, g_ret, w_out, x.dtype)
    x = x + gt_a * rms_norm(mix, g_post_mix)
    h = rms_norm(x, g_pre_ffn) * (1.0 + sc_f) + sh_f
    B, S, D = h.shape
    f = moe(h.reshape(B * S, D), w_router, b_router, w_gu, b_gu, w_down, b_down).reshape(B, S, D)
    x = x + gt_f * rms_norm(f, g_post_ffn)
    return x, ret_state, win_k, win_v


def setup_inputs(seed: int = 0) -> dict:
    key = jax.random.key(seed)
    ks = jax.random.split(key, 24)
    f32 = jnp.float32
    nrm = lambda k, shape, s: jax.random.normal(k, shape, f32) * s
    buf = min(MAX_WINDOW, PAST_LEN)
    D = D_MODEL
    return {
        'x_prompt': nrm(ks[0], (BATCH, SEQ, D), 1.0),
        'x_sample': nrm(ks[1], (DEC_BATCH, DEC_SEQ, D), 1.0),
        'state_ret': nrm(ks[2], (DEPTH, DEC_BATCH, N_RET_HEADS, HEAD_DIM, HEAD_DIM), 0.1),
        'cache_win_k': nrm(ks[3], (DEPTH, DEC_BATCH, buf, N_ATT_HEADS, HEAD_DIM), 1.0),
        'cache_win_v': nrm(ks[4], (DEPTH, DEC_BATCH, buf, N_ATT_HEADS, HEAD_DIM), 1.0),
        'c_prompt': nrm(ks[5], (BATCH, D), 1.0),
        'c_sample': nrm(ks[6], (DEC_BATCH, D), 1.0),
        'w_ada': nrm(ks[7], (DEPTH, D, 6 * D), 0.5 * D ** -0.5),
        'b_ada': nrm(ks[8], (DEPTH, 6 * D), 0.02),
        'g_pre_mix': 1.0 + nrm(ks[9], (DEPTH, D), 0.05),
        'g_post_mix': 1.0 + nrm(ks[10], (DEPTH, D), 0.05),
        'g_pre_ffn': 1.0 + nrm(ks[11], (DEPTH, D), 0.05),
        'g_post_ffn': 1.0 + nrm(ks[12], (DEPTH, D), 0.05),
        'w_in': nrm(ks[13], (DEPTH, D, D_IN), D ** -0.5),
        'g_ret': 1.0 + nrm(ks[14], (DEPTH, D_RET), 0.05),
        'w_out': nrm(ks[15], (DEPTH, D_MIX, D), D_MIX ** -0.5),
        'w_router': nrm(ks[16], (DEPTH, D, N_EXPERTS), D ** -0.5),
        'b_router': nrm(ks[17], (DEPTH, N_EXPERTS), 0.01),
        'w_gate_up': nrm(ks[18], (DEPTH, N_EXPERTS, D, 2 * D_FF), D ** -0.5),
        'b_gate_up': nrm(ks[19], (DEPTH, N_EXPERTS, 2 * D_FF), 0.02),
        'w_down': nrm(ks[20], (DEPTH, N_EXPERTS, D_FF, D), D_FF ** -0.5),
        'b_down': nrm(ks[21], (DEPTH, N_EXPERTS, D), 0.02),
    }


def reference(x_prompt, x_sample, state_ret, cache_win_k, cache_win_v, c_prompt, c_sample,
              w_ada, b_ada, g_pre_mix, g_post_mix, g_pre_ffn, g_post_ffn, w_in, g_ret, w_out,
              w_router, b_router, w_gate_up, b_gate_up, w_down, b_down):
    yp, ys = x_prompt, x_sample
    rp, rs, kp, vp, kс_dummy = [], [], [], [], None
    kp_list, vp_list, ks_list, vs_list = [], [], [], []
    for layer in range(DEPTH):
        shared = (w_ada[layer], b_ada[layer], g_pre_mix[layer], g_post_mix[layer], g_pre_ffn[layer],
                  g_post_ffn[layer], g_ret[layer], w_out[layer], w_router[layer], b_router[layer],
                  w_gate_up[layer], b_gate_up[layer], w_down[layer], b_down[layer])
        yp, r_p, k_p, v_p = decoder_layer(yp, c_prompt, lambda h: prompt_mixer(h, w_in[layer]), *shared)
        ys, r_s, k_s, v_s = decoder_layer(
            ys, c_sample,
            lambda h: sample_mixer(h, w_in[layer], state_ret[layer], cache_win_k[layer], cache_win_v[layer]),
            *shared)
        rp.append(r_p)
        rs.append(r_s)
        kp_list.append(k_p)
        vp_list.append(v_p)
        ks_list.append(k_s)
        vs_list.append(v_s)
    return (yp, ys, jnp.stack(rp), jnp.stack(rs), jnp.stack(kp_list), jnp.stack(vp_list),
            jnp.stack(ks_list), jnp.stack(vs_list))
```

```python
import functools

import numpy as np
import jax
import jax.numpy as jnp
from jax import lax
from jax.experimental import pallas as pl
from jax.experimental.pallas import tpu as pltpu

F32 = jnp.float32
BF16 = jnp.bfloat16

D_MODEL = 1024
HEAD_DIM = 64
N_HEADS = 8
D_HALF = N_HEADS * HEAD_DIM
N_SEG = 7
LANES = 128
N_PAIRS = D_HALF // LANES
RET_CHUNK = 128
ATT_BLOCK = 128
DIL_PATTERNS = ((128, 1), (512, 4), (2048, 16))
N_EXPERTS = 32
TOP_K = 4
SWIGLU_LIMIT = 7.0
SWIGLU_ALPHA = 1.702
EPS = 1e-6
NEG_INF = -1e30

TM_IN = 512
TM_OUT = 256
TM_MOE = 256
TM_FIN = 128
VMEM_LIMIT = 56 * 1024 * 1024


def _params(n_axes, vmem=None):
    return pltpu.CompilerParams(dimension_semantics=("arbitrary",) * n_axes, vmem_limit_bytes=vmem)


def _ret_log_decay():
    return np.log(1.0 - 2.0 ** (-5.0 - np.arange(N_HEADS, dtype=np.float64)))


def _alibi_slopes():
    return 2.0 ** (-8.0 * (np.arange(N_HEADS, dtype=np.float64) + 1.0) / N_HEADS)


def _per_head_lanes(a):
    return np.repeat(a, HEAD_DIM, axis=-1)


def _rms(x, g):
    return x * lax.rsqrt(jnp.mean(x * x, axis=-1, keepdims=True) + EPS) * g


def _dot(a, b):
    return jnp.dot(a, b, preferred_element_type=F32)


def _dot_nt(a, b):
    return lax.dot_general(a, b, (((1,), (1,)), ((), ())), preferred_element_type=F32)


def _dot_tn(a, b):
    return lax.dot_general(a, b, (((0,), (0,)), ((), ())), preferred_element_type=F32)


def _row_spec(tm, width, per_row, tiles_per_batch):
    if per_row:
        return pl.BlockSpec((tm, width), lambda i: (i, 0))
    return pl.BlockSpec((None, 1, width), lambda i: (i // tiles_per_batch, 0, 0))


def _ada_kernel(c_ref, w_ref, b_ref, o_ref):
    c = c_ref[...]
    s = (c * jax.nn.sigmoid(c)).astype(BF16)
    o_ref[...] = _dot(s, w_ref[...].astype(BF16)) + b_ref[...]


def _ada(c_all, w_ada, b_ada):
    n, d = c_all.shape
    n_out = w_ada.shape[1]
    return pl.pallas_call(
        _ada_kernel,
        grid=(n_out // d,),
        in_specs=[pl.BlockSpec((n, d), lambda j: (0, 0)),
                  pl.BlockSpec((d, d), lambda j: (0, j)),
                  pl.BlockSpec((1, d), lambda j: (0, j))],
        out_specs=pl.BlockSpec((n, d), lambda j: (0, j)),
        out_shape=jax.ShapeDtypeStruct((n, n_out), F32),
        compiler_params=_params(1),
        name="ada",
    )(c_all, w_ada, b_ada.reshape(1, n_out))


def _in_kernel(x_ref, sc_ref, sh_ref, g_ref, w_ref, *o_refs):
    h = (_rms(x_ref[...], g_ref[...]) * (1.0 + sc_ref[...]) + sh_ref[...]).astype(BF16)
    for s, o_ref in enumerate(o_refs):
        o_ref[...] = _dot(h, w_ref[:, s * D_HALF:(s + 1) * D_HALF])


def _in_proj(x, sc, sh, g, w_bf, per_row, rows_per_batch):
    n = x.shape[0]
    tm = min(TM_IN, n)
    mod = _row_spec(tm, D_MODEL, per_row, rows_per_batch // tm if not per_row else 1)
    seg = pl.BlockSpec((tm, D_HALF), lambda i: (i, 0))
    return pl.pallas_call(
        _in_kernel,
        grid=(n // tm,),
        in_specs=[pl.BlockSpec((tm, D_MODEL), lambda i: (i, 0)), mod, mod,
                  pl.BlockSpec((1, D_MODEL), lambda i: (0, 0)),
                  pl.BlockSpec((D_MODEL, N_SEG * D_HALF), lambda i: (0, 0))],
        out_specs=[seg] * N_SEG,
        out_shape=[jax.ShapeDtypeStruct((n, D_HALF), F32)] * N_SEG,
        compiler_params=_params(1, VMEM_LIMIT),
        name="in_proj",
    )(x, sc, sh, g, w_bf)


def _ret_tables(chunk):
    lg = _ret_log_decay()
    pos = np.arange(chunk, dtype=np.float64)
    diff = pos[:, None] - pos[None, :]
    dmask = np.where(diff >= 0, np.exp(np.maximum(diff, 0.0)[None] * lg[:, None, None]), 0.0)
    row_decay = _per_head_lanes(np.exp((pos[:, None] + 1.0) * lg[None, :]))
    key_decay = _per_head_lanes(np.exp((chunk - 1.0 - pos)[:, None] * lg[None, :]))
    state_decay = _per_head_lanes(np.exp(chunk * lg)[None, :])
    return tuple(jnp.asarray(t, F32) for t in (dmask, row_decay, key_decay, state_decay))


def _ret_kernel(q_ref, k_ref, v_ref, gate_ref, s0_ref, gret_ref, dm_ref, rd_ref, kd_ref, sd_ref,
                y_ref, st_ref, state, *, chunk, has_init):
    c = pl.program_id(1)
    lo = lax.broadcasted_iota(jnp.int32, (chunk, LANES), 1) < HEAD_DIM
    blk_r = lax.broadcasted_iota(jnp.int32, (LANES, LANES), 0) < HEAD_DIM
    blk_c = lax.broadcasted_iota(jnp.int32, (LANES, LANES), 1) < HEAD_DIM
    same_head = blk_r == blk_c

    @pl.when(c == 0)
    def _():
        if has_init:
            zero = jnp.zeros((HEAD_DIM, HEAD_DIM), F32)
            for p in range(N_PAIRS):
                top = jnp.concatenate([s0_ref[0, 2 * p], zero], axis=1)
                bot = jnp.concatenate([zero, s0_ref[0, 2 * p + 1]], axis=1)
                state[p] = jnp.concatenate([top, bot], axis=0)
        else:
            state[...] = jnp.zeros_like(state)

    for p in range(N_PAIRS):
        sl = slice(p * LANES, (p + 1) * LANES)
        q = q_ref[:, sl]
        k = k_ref[:, sl] * HEAD_DIM ** -0.5
        vb = v_ref[:, sl].astype(BF16)
        kb = k.astype(BF16)
        inner = []
        for half in range(2):
            qh = jnp.where(lo if half == 0 else ~lo, q, 0.0).astype(BF16)
            scores = _dot_nt(qh, kb) * dm_ref[2 * p + half]
            inner.append(_dot(scores.astype(BF16), vb))
        s_old = state[p]
        cross = _dot(q.astype(BF16), s_old.astype(BF16)) * rd_ref[:, sl]
        o = jnp.where(lo, inner[0], inner[1]) + cross
        upd = _dot_tn((k * kd_ref[:, sl]).astype(BF16), vb)
        s_new = sd_ref[:, sl] * s_old + jnp.where(same_head, upd, 0.0)
        state[p] = s_new

        def head_mean(t):
            m_lo = jnp.sum(jnp.where(lo, t, 0.0), axis=-1, keepdims=True)
            m_hi = jnp.sum(jnp.where(lo, 0.0, t), axis=-1, keepdims=True)
            return jnp.where(lo, m_lo, m_hi) * (1.0 / HEAD_DIM)

        d = o - head_mean(o)
        n = d * lax.rsqrt(head_mean(d * d) + EPS) * gret_ref[:, sl]
        g = gate_ref[:, sl]
        y_ref[:, sl] = (g * jax.nn.sigmoid(g) * n).astype(y_ref.dtype)

        @pl.when(c == pl.num_programs(1) - 1)
        def _():
            st_ref[0, 2 * p] = s_new[:HEAD_DIM, :HEAD_DIM]
            st_ref[0, 2 * p + 1] = s_new[HEAD_DIM:, HEAD_DIM:]


def _retention(q, k, v, gate, state0, g_ret, n_batch, seq, chunk):
    nc = seq // chunk
    has_init = state0 is not None
    if state0 is None:
        state0 = jnp.zeros((1, N_HEADS, HEAD_DIM, HEAD_DIM), F32)
        s0_map = lambda b, c: (0, 0, 0, 0)
    else:
        s0_map = lambda b, c: (b, 0, 0, 0)
    dmask, row_decay, key_decay, state_decay = _ret_tables(chunk)
    rows = pl.BlockSpec((chunk, D_HALF), lambda b, c: (b * nc + c, 0))
    const2 = lambda shape: pl.BlockSpec(shape, lambda b, c: (0, 0))
    st_block = pl.BlockSpec((1, N_HEADS, HEAD_DIM, HEAD_DIM), lambda b, c: (b, 0, 0, 0))
    return pl.pallas_call(
        functools.partial(_ret_kernel, chunk=chunk, has_init=has_init),
        grid=(n_batch, nc),
        in_specs=[rows, rows, rows, rows,
                  pl.BlockSpec((1, N_HEADS, HEAD_DIM, HEAD_DIM), s0_map),
                  const2((1, D_HALF)),
                  pl.BlockSpec((N_HEADS, chunk, chunk), lambda b, c: (0, 0, 0)),
                  const2((chunk, D_HALF)), const2((chunk, D_HALF)), const2((1, D_HALF))],
        out_specs=[rows, st_block],
        out_shape=[jax.ShapeDtypeStruct((n_batch * seq, D_HALF), BF16),
                   jax.ShapeDtypeStruct((n_batch, N_HEADS, HEAD_DIM, HEAD_DIM), F32)],
        scratch_shapes=[pltpu.VMEM((N_PAIRS, LANES, LANES), F32)],
        compiler_params=_params(2),
        name="retention",
    )(q, k, v, gate, state0, g_ret.reshape(1, D_HALF), dmask, row_decay, key_decay, state_decay)


def _attp_table(dilation):
    qi = np.arange(ATT_BLOCK)[:, None]
    kj = np.arange(2 * ATT_BLOCK)[None, :]
    step = qi + ATT_BLOCK - kj
    in_win = (step >= 0) & (step <= ATT_BLOCK)
    bias = -_alibi_slopes()[:, None, None] * (step * dilation).astype(np.float64)[None]
    return jnp.asarray(np.where(in_win[None], bias, NEG_INF), F32)


def _attp_kernel(q_ref, kp_ref, kc_ref, vp_ref, vc_ref, tb_ref, o_ref, ml_ref):
    has_prev = pl.program_id(2) > 0
    lo = lax.broadcasted_iota(jnp.int32, (ATT_BLOCK, LANES), 1) < HEAD_DIM
    lane = lax.broadcasted_iota(jnp.int32, (ATT_BLOCK, LANES), 1)
    key_ok = (lax.broadcasted_iota(jnp.int32, (ATT_BLOCK, 2 * ATT_BLOCK), 1) >= ATT_BLOCK) | has_prev
    ml = jnp.zeros((ATT_BLOCK, LANES), F32)
    for p in range(N_PAIRS):
        sl = slice(p * LANES, (p + 1) * LANES)
        q = q_ref[:, sl]
        kb = jnp.concatenate([kp_ref[:, sl], kc_ref[:, sl]], axis=0).astype(BF16)
        vb = jnp.concatenate([vp_ref[:, sl], vc_ref[:, sl]], axis=0).astype(BF16)
        outs = []
        for half in range(2):
            h = 2 * p + half
            qh = jnp.where(lo if half == 0 else ~lo, q, 0.0).astype(BF16)
            s = _dot_nt(qh, kb) * HEAD_DIM ** -0.5
            s = jnp.where(key_ok, s + tb_ref[h], NEG_INF)
            m = jnp.max(s, axis=-1, keepdims=True)
            e = jnp.exp(s - m)
            l = jnp.sum(e, axis=-1, keepdims=True)
            outs.append(_dot(e.astype(BF16), vb))
            ml = jnp.where(lane == h, m, ml)
            ml = jnp.where(lane == N_HEADS + h, l, ml)
        o_ref[:, sl] = jnp.where(lo, outs[0], outs[1])
    ml_ref[...] = ml


def _att_prompt_pattern(q, k, v, n_batch, seq, dilation):
    sub = seq // dilation
    nb = sub // ATT_BLOCK
    view = lambda t: t.reshape(n_batch, sub, dilation * D_HALF)
    cur = pl.BlockSpec((None, ATT_BLOCK, D_HALF), lambda b, r, n: (b, n, r))
    prev = pl.BlockSpec((None, ATT_BLOCK, D_HALF), lambda b, r, n: (b, jnp.maximum(n - 1, 0), r))
    o, ml = pl.pallas_call(
        _attp_kernel,
        grid=(n_batch, dilation, nb),
        in_specs=[cur, prev, cur, prev, cur,
                  pl.BlockSpec((N_HEADS, ATT_BLOCK, 2 * ATT_BLOCK), lambda b, r, n: (0, 0, 0))],
        out_specs=[cur, pl.BlockSpec((None, ATT_BLOCK, LANES), lambda b, r, n: (b, n, r))],
        out_shape=[jax.ShapeDtypeStruct((n_batch, sub, dilation * D_HALF), F32),
                   jax.ShapeDtypeStruct((n_batch, sub, dilation * LANES), F32)],
        compiler_params=_params(3),
        name=f"att_prompt_d{dilation}",
    )(view(q), view(k), view(k), view(v), view(v), _attp_table(dilation))
    return o.reshape(n_batch * seq, D_HALF), ml.reshape(n_batch * seq, LANES)


def _atts_tables(buf, t_new):
    pos_q = buf + np.arange(t_new)
    pos_k = np.concatenate([np.arange(buf), buf + np.arange(LANES)])
    dist = pos_q[:, None] - pos_k[None, :]
    count = np.zeros(dist.shape)
    for window, dilation in DIL_PATTERNS:
        count += (dist >= 0) & (dist <= window) & (dist % dilation == 0)
    count[:, buf + t_new:] = 0
    bias = -_alibi_slopes()[:, None, None] * dist[None].astype(np.float64)
    table = np.where(count[None] > 0, bias + np.log(np.maximum(count, 1.0))[None], NEG_INF)
    table = table.reshape(N_HEADS * t_new, buf + LANES)
    return jnp.asarray(table[:, :buf], F32), jnp.asarray(table[:, buf:], F32)


def _atts_kernel(q_ref, kc_ref, kn_ref, vc_ref, vn_ref, tc_ref, tn_ref, o_ref, *, t_new):
    rows = N_HEADS * t_new
    row_head = lax.broadcasted_iota(jnp.int32, (rows, D_HALF), 0) // t_new
    col_head = lax.broadcasted_iota(jnp.int32, (rows, D_HALF), 1) // HEAD_DIM
    own = row_head == col_head
    q_heads = jnp.where(own, jnp.concatenate([q_ref[...]] * N_HEADS, axis=0), 0.0).astype(BF16)
    pad = jnp.zeros((LANES - t_new, D_HALF), F32)
    kn = jnp.concatenate([kn_ref[...], pad], axis=0).astype(BF16)
    vn = jnp.concatenate([vn_ref[...], pad], axis=0).astype(BF16)
    scale = HEAD_DIM ** -0.5
    sc = _dot_nt(q_heads, kc_ref[...].astype(BF16)) * scale + tc_ref[...]
    sn = _dot_nt(q_heads, kn) * scale + tn_ref[...]
    m = jnp.maximum(jnp.max(sc, axis=-1, keepdims=True), jnp.max(sn, axis=-1, keepdims=True))
    ec = jnp.exp(sc - m)
    en = jnp.exp(sn - m)
    l = jnp.sum(ec, axis=-1, keepdims=True) + jnp.sum(en, axis=-1, keepdims=True)
    acc = _dot(ec.astype(BF16), vc_ref[...].astype(BF16)) + _dot(en.astype(BF16), vn)
    acc = jnp.where(own, acc, 0.0) / l
    out = acc[:t_new]
    for h in range(1, N_HEADS):
        out = out + acc[h * t_new:(h + 1) * t_new]
    o_ref[...] = out.astype(o_ref.dtype)


def _att_sample(q, k_new, v_new, cache_k, cache_v, n_batch, t_new):
    buf = cache_k.shape[1]
    tc, tn = _atts_tables(buf, t_new)
    rows = pl.BlockSpec((t_new, D_HALF), lambda b: (b, 0))
    cache = pl.BlockSpec((None, buf, D_HALF), lambda b: (b, 0, 0))
    return pl.pallas_call(
        functools.partial(_atts_kernel, t_new=t_new),
        grid=(n_batch,),
        in_specs=[rows, cache, rows, cache, rows,
                  pl.BlockSpec((N_HEADS * t_new, buf), lambda b: (0, 0)),
                  pl.BlockSpec((N_HEADS * t_new, LANES), lambda b: (0, 0))],
        out_specs=rows,
        out_shape=jax.ShapeDtypeStruct((n_batch * t_new, D_HALF), BF16),
        compiler_params=_params(1, VMEM_LIMIT),
        name="att_sample",
    )(q, cache_k, k_new, cache_v, v_new, tc, tn)


def _out_kernel(*refs, n_parts):
    ret_ref = refs[0]
    part_refs = refs[1:1 + 2 * n_parts] if n_parts else refs[1:2]
    (x_ref, gt_ref, sc_ref, sh_ref, w_ref, gpost_ref, gpre_ref, wr_ref, br_ref,
     x1_ref, h_ref, top_ref, xin) = refs[len(part_refs) + 1:]
    tm = x_ref.shape[0]
    xin[:, :D_HALF] = ret_ref[...]
    if n_parts:
        o_refs, ml_refs = part_refs[:n_parts], part_refs[n_parts:]
        for h in range(N_HEADS):
            ms = [r[:, h:h + 1] for r in ml_refs]
            ls = [r[:, N_HEADS + h:N_HEADS + h + 1] for r in ml_refs]
            mx = functools.reduce(jnp.maximum, ms)
            ws = [jnp.exp(m - mx) for m in ms]
            den = sum(l * w for l, w in zip(ls, ws))
            hs = slice(h * HEAD_DIM, (h + 1) * HEAD_DIM)
            att = sum(r[:, hs] * (w / den) for r, w in zip(o_refs, ws))
            xin[:, D_HALF + h * HEAD_DIM:D_HALF + (h + 1) * HEAD_DIM] = att.astype(BF16)
    else:
        xin[:, D_HALF:] = part_refs[0][...]
    mix = _dot(xin[...], w_ref[...])
    x1 = x_ref[...] + gt_ref[...] * _rms(mix, gpost_ref[...])
    x1_ref[...] = x1
    h = _rms(x1, gpre_ref[...]) * (1.0 + sc_ref[...]) + sh_ref[...]
    h_ref[...] = h
    logits = jnp.dot(h, wr_ref[...], preferred_element_type=F32,
                     precision=lax.Precision.HIGHEST) + br_ref[...]
    lane_e = lax.broadcasted_iota(jnp.int32, (tm, N_EXPERTS), 1).astype(F32)
    lane_o = lax.broadcasted_iota(jnp.int32, (tm, LANES), 1)
    vals, idxs = [], []
    work = logits
    for _ in range(TOP_K):
        v = jnp.max(work, axis=-1, keepdims=True)
        i = jnp.min(jnp.where(work == v, lane_e, float(N_EXPERTS)), axis=-1, keepdims=True)
        vals.append(v)
        idxs.append(i)
        work = jnp.where(lane_e == i, -jnp.inf, work)
    es = [jnp.exp(v - vals[0]) for v in vals]
    tot = sum(es)
    packed = jnp.zeros((tm, LANES), F32)
    for j in range(TOP_K):
        packed = jnp.where(lane_o == j, idxs[j], packed)
        packed = jnp.where(lane_o == TOP_K + j, es[j] / tot, packed)
    top_ref[...] = packed


def _mixer_out(ret_y, parts, x, gt, sc, sh, w_out_bf, g_post, g_pre, w_router, b_router,
               per_row, rows_per_batch):
    n = x.shape[0]
    tm = min(TM_OUT, n)
    n_parts = len(parts[0]) if isinstance(parts, tuple) else 0
    half = pl.BlockSpec((tm, D_HALF), lambda i: (i, 0))
    full = pl.BlockSpec((tm, D_MODEL), lambda i: (i, 0))
    stat = pl.BlockSpec((tm, LANES), lambda i: (i, 0))
    vec = pl.BlockSpec((1, D_MODEL), lambda i: (0, 0))
    mod = _row_spec(tm, D_MODEL, per_row, rows_per_batch // tm if not per_row else 1)
    if n_parts:
        part_args = list(parts[0]) + list(parts[1])
        part_specs = [half] * n_parts + [stat] * n_parts
    else:
        part_args, part_specs = [parts], [half]
    return pl.pallas_call(
        functools.partial(_out_kernel, n_parts=n_parts),
        grid=(n // tm,),
        in_specs=[half] + part_specs + [full, mod, mod, mod,
                  pl.BlockSpec((D_MODEL, D_MODEL), lambda i: (0, 0)), vec, vec,
                  pl.BlockSpec((D_MODEL, N_EXPERTS), lambda i: (0, 0)),
                  pl.BlockSpec((1, N_EXPERTS), lambda i: (0, 0))],
        out_specs=[full, full, stat],
        out_shape=[jax.ShapeDtypeStruct((n, D_MODEL), F32),
                   jax.ShapeDtypeStruct((n, D_MODEL), F32),
                   jax.ShapeDtypeStruct((n, LANES), F32)],
        scratch_shapes=[pltpu.VMEM((tm, D_MODEL), BF16)],
        compiler_params=_params(1, VMEM_LIMIT),
        name="mixer_out",
    )(ret_y, *part_args, x, gt, sc, sh, w_out_bf, g_post, g_pre, w_router, b_router)


def _route(top_idx, tm):
    n = top_idx.shape[0]
    n_blocks = n * TOP_K // tm + N_EXPERTS
    onehot = (top_idx[:, :, None] == jnp.arange(N_EXPERTS, dtype=jnp.int32)).any(axis=1).astype(jnp.int32)
    before = jnp.cumsum(onehot, axis=0) - onehot
    counts = before[-1] + onehot[-1]
    blocks_e = (counts + tm - 1) // tm
    blocks_end = jnp.cumsum(blocks_e)
    pad_start = (blocks_end - blocks_e) * tm
    dest = pad_start[top_idx] + jnp.take_along_axis(before, top_idx, axis=1)
    tok = jnp.broadcast_to(jnp.arange(n, dtype=jnp.int32)[:, None], (n, TOP_K))
    row_tok = jnp.zeros((n_blocks * tm,), jnp.int32).at[dest.reshape(-1)].set(tok.reshape(-1))
    block_e = jnp.minimum(jnp.searchsorted(blocks_end, jnp.arange(n_blocks), side="right"),
                          N_EXPERTS - 1).astype(jnp.int32)
    return dest.astype(jnp.int32), row_tok, block_e, blocks_end[-1:].astype(jnp.int32)


def _moe_kernel(be_ref, nu_ref, tok0_ref, tokn_ref, h_hbm, wg_ref, wu_ref, wd_ref,
                bg_ref, bu_ref, bd_ref, o_ref, xbuf, sem, *, tm):
    i = pl.program_id(0)
    n_used = nu_ref[0]
    slot = i % 2

    def gather(tok_ref, dst_slot):
        def body(j, carry):
            pltpu.make_async_copy(h_hbm.at[pl.ds(tok_ref[j], 1)], xbuf.at[dst_slot, pl.ds(j, 1)],
                                  sem.at[dst_slot]).start()
            return carry
        lax.fori_loop(0, tm, body, 0)

    @pl.when(i == 0)
    def _():
        gather(tok0_ref, 0)

    @pl.when(i < n_used)
    def _():
        pltpu.make_async_copy(h_hbm.at[pl.ds(0, tm)], xbuf.at[slot], sem.at[slot]).wait()

    @pl.when(i + 1 < n_used)
    def _():
        gather(tokn_ref, 1 - slot)

    @pl.when(i < n_used)
    def _():
        x = xbuf[slot].astype(BF16)
        gate = jnp.minimum(_dot(x, wg_ref[...]) + bg_ref[...], SWIGLU_LIMIT)
        up = jnp.clip(_dot(x, wu_ref[...]) + bu_ref[...], -SWIGLU_LIMIT, SWIGLU_LIMIT)
        act = gate * jax.nn.sigmoid(SWIGLU_ALPHA * gate) * (up + 1.0)
        o_ref[...] = _dot(act.astype(BF16), wd_ref[...]) + bd_ref[...]

    @pl.when(i >= n_used)
    def _():
        o_ref[...] = jnp.zeros_like(o_ref)


def _moe_blocks(h, row_tok, block_e, n_used, wg, wu, wd, bg, bu, bd, tm):
    n_blocks = block_e.shape[0]
    d_ff = wg.shape[2]
    wspec = lambda k, n: pl.BlockSpec((None, k, n), lambda i, be, nu: (be[i], 0, 0))
    grid_spec = pltpu.PrefetchScalarGridSpec(
        num_scalar_prefetch=2,
        grid=(n_blocks,),
        in_specs=[pl.BlockSpec((tm,), lambda i, be, nu: (0,), memory_space=pltpu.SMEM),
                  pl.BlockSpec((tm,), lambda i, be, nu: (jnp.minimum(i + 1, n_blocks - 1),),
                               memory_space=pltpu.SMEM),
                  pl.BlockSpec(memory_space=pl.ANY),
                  wspec(D_MODEL, d_ff), wspec(D_MODEL, d_ff), wspec(d_ff, D_MODEL),
                  wspec(1, d_ff), wspec(1, d_ff), wspec(1, D_MODEL)],
        out_specs=pl.BlockSpec((tm, D_MODEL), lambda i, be, nu: (i, 0)),
        scratch_shapes=[pltpu.VMEM((2, tm, D_MODEL), F32), pltpu.SemaphoreType.DMA((2,))],
    )
    return pl.pallas_call(
        functools.partial(_moe_kernel, tm=tm),
        grid_spec=grid_spec,
        out_shape=jax.ShapeDtypeStruct((n_blocks * tm, D_MODEL), F32),
        compiler_params=_params(1, VMEM_LIMIT),
        name="moe_blocks",
    )(block_e, n_used, row_tok, row_tok, h, wg, wu, wd, bg, bu, bd)


def _fin_kernel(dest_ref, rows_hbm, top_ref, x1_ref, gt_ref, g_ref, o_ref, buf, sem, *, tm):
    def body(j, carry):
        for k in range(TOP_K):
            pltpu.make_async_copy(rows_hbm.at[pl.ds(dest_ref[j * TOP_K + k], 1)],
                                  buf.at[k, pl.ds(j, 1)], sem.at[0]).start()
        return carry
    lax.fori_loop(0, tm, body, 0)
    for k in range(TOP_K):
        pltpu.make_async_copy(rows_hbm.at[pl.ds(0, tm)], buf.at[k], sem.at[0]).wait()
    top = top_ref[...]
    f = sum(buf[k] * top[:, TOP_K + k:TOP_K + k + 1] for k in range(TOP_K))
    o_ref[...] = x1_ref[...] + gt_ref[...] * _rms(f, g_ref[...])


def _finish(dest, rows, top, x1, gt, g_post, per_row, rows_per_batch):
    n = x1.shape[0]
    tm = min(TM_FIN, n)
    full = pl.BlockSpec((tm, D_MODEL), lambda i: (i, 0))
    mod = _row_spec(tm, D_MODEL, per_row, rows_per_batch // tm if not per_row else 1)
    return pl.pallas_call(
        functools.partial(_fin_kernel, tm=tm),
        grid=(n // tm,),
        in_specs=[pl.BlockSpec((tm * TOP_K,), lambda i: (i,), memory_space=pltpu.SMEM),
                  pl.BlockSpec(memory_space=pl.ANY),
                  pl.BlockSpec((tm, LANES), lambda i: (i, 0)), full, mod,
                  pl.BlockSpec((1, D_MODEL), lambda i: (0, 0))],
        out_specs=full,
        out_shape=jax.ShapeDtypeStruct((n, D_MODEL), F32),
        scratch_shapes=[pltpu.VMEM((TOP_K, tm, D_MODEL), F32), pltpu.SemaphoreType.DMA((1,))],
        compiler_params=_params(1),
        name="finish",
    )(dest, rows, top, x1, gt, g_post)


def kernel(x_prompt, x_sample, state_ret, cache_win_k, cache_win_v, c_prompt, c_sample, w_ada, b_ada,
           g_pre_mix, g_post_mix, g_pre_ffn, g_post_ffn, w_in, g_ret, w_out, w_router, b_router,
           w_gate_up, b_gate_up, w_down, b_down):
    depth = w_in.shape[0]
    assert depth == 1, "single-layer step"
    n_b, seq, d = x_prompt.shape
    n_db, t_new, _ = x_sample.shape
    n_p, n_s = n_b * seq, n_db * t_new
    buf = cache_win_k.shape[2]
    keep = min(DIL_PATTERNS[-1][0], seq)
    vec = lambda g: g[0].reshape(1, -1)

    n_c = n_b + n_db
    c_all = jnp.concatenate([c_prompt, c_sample, jnp.zeros((-n_c % 8, d), F32)], axis=0)
    mod = _ada(c_all, w_ada[0], b_ada[0])
    mods = [mod[:, j * d:(j + 1) * d] for j in range(6)]
    mod_p = [m[:n_b].reshape(n_b, 1, d) for m in mods]
    mod_s = [jnp.repeat(m[n_b:n_c], t_new, axis=0) for m in mods]

    w_in_bf = w_in[0].astype(BF16)
    w_out_bf = w_out[0].astype(BF16)
    xp = x_prompt.reshape(n_p, d)
    xs = x_sample.reshape(n_s, d)

    qr, kr, vr, gr, qa, ka, va = _in_proj(xp, mod_p[1], mod_p[0], vec(g_pre_mix), w_in_bf, False, seq)
    ret_y_p, ret_state_p = _retention(qr, kr, vr, gr, None, g_ret[0], n_b, seq, RET_CHUNK)
    parts = [_att_prompt_pattern(qa, ka, va, n_b, seq, dil) for _, dil in DIL_PATTERNS]
    x1_p, h_p, top_p = _mixer_out(ret_y_p, ([o for o, _ in parts], [ml for _, ml in parts]), xp,
                                  mod_p[2], mod_p[4], mod_p[3], w_out_bf, vec(g_post_mix),
                                  vec(g_pre_ffn), w_router[0], b_router[0].reshape(1, -1), False, seq)

    qr_s, kr_s, vr_s, gr_s, qa_s, ka_s, va_s = _in_proj(xs, mod_s[1], mod_s[0], vec(g_pre_mix),
                                                          w_in_bf, True, t_new)
    ret_y_s, ret_state_s = _retention(qr_s, kr_s, vr_s, gr_s, state_ret[0], g_ret[0], n_db, t_new, t_new)
    att_s = _att_sample(qa_s, ka_s, va_s, cache_win_k[0].reshape(n_db, buf, D_HALF),
                        cache_win_v[0].reshape(n_db, buf, D_HALF), n_db, t_new)
    x1_s, h_s, top_s = _mixer_out(ret_y_s, att_s, xs, mod_s[2], mod_s[4], mod_s[3], w_out_bf,
                                  vec(g_post_mix), vec(g_pre_ffn), w_router[0],
                                  b_router[0].reshape(1, -1), True, t_new)

    h_all = jnp.concatenate([h_p, h_s], axis=0)
    top_all = jnp.concatenate([top_p, top_s], axis=0)
    top_idx = top_all[:, :TOP_K].astype(jnp.int32)
    dest, row_tok, block_e, n_used = _route(top_idx, TM_MOE)
    n_e, _, d_ff2 = w_gate_up[0].shape
    w_gu = w_gate_up[0].reshape(n_e, d, d_ff2 // 2, 2)
    b_gu = b_gate_up[0].reshape(n_e, 1, d_ff2 // 2, 2)
    rows = _moe_blocks(h_all, row_tok, block_e, n_used,
                       w_gu[..., 0].astype(BF16), w_gu[..., 1].astype(BF16), w_down[0].astype(BF16),
                       b_gu[..., 0], b_gu[..., 1], b_down[0].reshape(n_e, 1, d), TM_MOE)
    dest = dest.reshape(-1)
    y_p = _finish(dest[:n_p * TOP_K], rows, top_p, x1_p, mod_p[5], vec(g_post_ffn), False, seq)
    y_s = _finish(dest[n_p * TOP_K:], rows, top_s, x1_s, mod_s[5], vec(g_post_ffn), True, t_new)

    win = lambda t: t.reshape(n_b, seq, N_HEADS, HEAD_DIM)[:, seq - keep:][None]
    new = lambda t: t.reshape(n_db, t_new, N_HEADS, HEAD_DIM)[None]
    return (y_p.reshape(n_b, seq, d), y_s.reshape(n_db, t_new, d), ret_state_p[None], ret_state_s[None],
            win(ka), win(va), new(ka_s), new(va_s))
```

```python
import functools

import numpy as np
import jax
import jax.numpy as jnp
from jax import lax
from jax.experimental import pallas as pl
from jax.experimental.pallas import tpu as pltpu

F32 = jnp.float32
BF16 = jnp.bfloat16

D_MODEL = 1024
HEAD_DIM = 64
N_HEADS = 8
D_HALF = N_HEADS * HEAD_DIM
N_SEG = 7
LANES = 128
N_PAIRS = D_HALF // LANES
RET_CHUNK = 128
ATT_BLOCK = 128
DIL_PATTERNS = ((128, 1), (512, 4), (2048, 16))
N_EXPERTS = 32
TOP_K = 4
SWIGLU_LIMIT = 7.0
SWIGLU_ALPHA = 1.702
EPS = 1e-6
NEG_INF = -1e30
MOD_SHIFT_MIX, MOD_SCALE_MIX, MOD_GATE_MIX, MOD_SHIFT_FFN, MOD_SCALE_FFN, MOD_GATE_FFN = range(6)

TM_IN = 512
TM_OUT = 256
TM_MOE = 256
MOE_COLS = 512
TM_FIN = 128
VMEM_LIMIT = 56 * 1024 * 1024


def _params(n_axes, vmem=None):
    return pltpu.CompilerParams(dimension_semantics=("arbitrary",) * n_axes, vmem_limit_bytes=vmem)


def _ret_log_decay():
    return np.log(1.0 - 2.0 ** (-5.0 - np.arange(N_HEADS, dtype=np.float64)))


def _alibi_slopes():
    return 2.0 ** (-8.0 * (np.arange(N_HEADS, dtype=np.float64) + 1.0) / N_HEADS)


def _per_head_lanes(a):
    return np.repeat(a, HEAD_DIM, axis=-1)


def _rms(x, g):
    return x * lax.rsqrt(jnp.mean(x * x, axis=-1, keepdims=True) + EPS) * g


def _dot(a, b):
    return jnp.dot(a, b, preferred_element_type=F32)


def _dot_nt(a, b):
    return lax.dot_general(a, b, (((1,), (1,)), ((), ())), preferred_element_type=F32)


def _dot_tn(a, b):
    return lax.dot_general(a, b, (((0,), (0,)), ((), ())), preferred_element_type=F32)


SLAB = (D_MODEL // LANES, LANES)


def _slab_load(ref):
    return jnp.concatenate([ref[:, s, :] for s in range(SLAB[0])], axis=1)


def _slab_store(ref, value):
    for s in range(SLAB[0]):
        ref[:, s, :] = value[:, s * LANES:(s + 1) * LANES]


def _mod_spec(tm, col, per_row, tiles_per_batch):
    if per_row:
        return pl.BlockSpec((tm, D_MODEL), lambda i: (i, col))
    return pl.BlockSpec((None, 1, D_MODEL), lambda i: (i // tiles_per_batch, 0, col))


def _ada_kernel(cp_ref, cs_ref, w_ref, b_ref, op_ref, os_ref):
    w = w_ref[...].astype(BF16)

    def mod(c):
        return _dot((c * jax.nn.sigmoid(c)).astype(BF16), w) + b_ref[...]

    op_ref[...] = mod(cp_ref[...])
    ms = mod(cs_ref[...])
    os_ref[...] = jnp.broadcast_to(ms[:, None, :], os_ref.shape)


def _ada(c_prompt, c_sample, w_ada, b_ada, t_new):
    n_b, d = c_prompt.shape
    n_db = c_sample.shape[0]
    n_out = w_ada.shape[1]
    cp = jnp.concatenate([c_prompt, jnp.zeros((-n_b % 8, d), F32)], axis=0)
    return pl.pallas_call(
        _ada_kernel,
        grid=(n_out // d,),
        in_specs=[pl.BlockSpec(cp.shape, lambda j: (0, 0)),
                  pl.BlockSpec((n_db, d), lambda j: (0, 0)),
                  pl.BlockSpec((d, d), lambda j: (0, j)),
                  pl.BlockSpec((1, d), lambda j: (0, j))],
        out_specs=[pl.BlockSpec(cp.shape, lambda j: (0, j)),
                   pl.BlockSpec((n_db, t_new, d), lambda j: (0, 0, j))],
        out_shape=[jax.ShapeDtypeStruct((cp.shape[0], n_out), F32),
                   jax.ShapeDtypeStruct((n_db, t_new, n_out), F32)],
        compiler_params=_params(1, VMEM_LIMIT),
        name="ada",
    )(cp, c_sample, w_ada, b_ada.reshape(1, n_out))


def _in_kernel(x_ref, sc_ref, sh_ref, g_ref, w_ref, *o_refs):
    h = (_rms(x_ref[...], g_ref[...]) * (1.0 + sc_ref[...]) + sh_ref[...]).astype(BF16)
    for s, o_ref in enumerate(o_refs):
        o_ref[...] = _dot(h, w_ref[:, s * D_HALF:(s + 1) * D_HALF])


def _in_proj(x, mod, g, w_bf, per_row, rows_per_batch):
    n = x.shape[0]
    tm = min(TM_IN, n)
    tpb = 1 if per_row else rows_per_batch // tm
    seg = pl.BlockSpec((tm, D_HALF), lambda i: (i, 0))
    return pl.pallas_call(
        _in_kernel,
        grid=(n // tm,),
        in_specs=[pl.BlockSpec((tm, D_MODEL), lambda i: (i, 0)),
                  _mod_spec(tm, MOD_SCALE_MIX, per_row, tpb), _mod_spec(tm, MOD_SHIFT_MIX, per_row, tpb),
                  pl.BlockSpec((1, D_MODEL), lambda i: (0, 0)),
                  pl.BlockSpec((D_MODEL, N_SEG * D_HALF), lambda i: (0, 0))],
        out_specs=[seg] * N_SEG,
        out_shape=[jax.ShapeDtypeStruct((n, D_HALF), F32)] * N_SEG,
        compiler_params=_params(1, VMEM_LIMIT),
        name="in_proj",
    )(x, mod, mod, g, w_bf)


def _ret_tables(chunk):
    lg = _ret_log_decay()
    pos = np.arange(chunk, dtype=np.float64)
    diff = pos[:, None] - pos[None, :]
    dmask = np.where(diff >= 0, np.exp(np.maximum(diff, 0.0)[None] * lg[:, None, None]), 0.0)
    row_decay = _per_head_lanes(np.exp((pos[:, None] + 1.0) * lg[None, :]))
    key_decay = _per_head_lanes(np.exp((chunk - 1.0 - pos)[:, None] * lg[None, :]))
    state_decay = _per_head_lanes(np.exp(chunk * lg)[None, :])
    return tuple(jnp.asarray(t, F32) for t in (dmask, row_decay, key_decay, state_decay))


def _ret_kernel(q_ref, k_ref, v_ref, gate_ref, s0_ref, gret_ref, dm_ref, rd_ref, kd_ref, sd_ref,
                y_ref, st_ref, state, *, chunk, has_init):
    c = pl.program_id(1)
    lo = lax.broadcasted_iota(jnp.int32, (chunk, LANES), 1) < HEAD_DIM
    blk_r = lax.broadcasted_iota(jnp.int32, (LANES, LANES), 0) < HEAD_DIM
    blk_c = lax.broadcasted_iota(jnp.int32, (LANES, LANES), 1) < HEAD_DIM
    same_head = blk_r == blk_c

    @pl.when(c == 0)
    def _():
        if has_init:
            zero = jnp.zeros((HEAD_DIM, HEAD_DIM), F32)
            for p in range(N_PAIRS):
                top = jnp.concatenate([s0_ref[0, 2 * p], zero], axis=1)
                bot = jnp.concatenate([zero, s0_ref[0, 2 * p + 1]], axis=1)
                state[p] = jnp.concatenate([top, bot], axis=0)
        else:
            state[...] = jnp.zeros_like(state)

    for p in range(N_PAIRS):
        sl = slice(p * LANES, (p + 1) * LANES)
        q = q_ref[:, sl]
        k = k_ref[:, sl] * HEAD_DIM ** -0.5
        vb = v_ref[:, sl].astype(BF16)
        kb = k.astype(BF16)
        inner = []
        for half in range(2):
            qh = jnp.where(lo if half == 0 else ~lo, q, 0.0).astype(BF16)
            scores = _dot_nt(qh, kb) * dm_ref[2 * p + half]
            inner.append(_dot(scores.astype(BF16), vb))
        s_old = state[p]
        cross = _dot(q.astype(BF16), s_old.astype(BF16)) * rd_ref[:, sl]
        o = jnp.where(lo, inner[0], inner[1]) + cross
        upd = _dot_tn((k * kd_ref[:, sl]).astype(BF16), vb)
        s_new = sd_ref[:, sl] * s_old + jnp.where(same_head, upd, 0.0)
        state[p] = s_new

        def head_mean(t):
            m_lo = jnp.sum(jnp.where(lo, t, 0.0), axis=-1, keepdims=True)
            m_hi = jnp.sum(jnp.where(lo, 0.0, t), axis=-1, keepdims=True)
            return jnp.where(lo, m_lo, m_hi) * (1.0 / HEAD_DIM)

        d = o - head_mean(o)
        n = d * lax.rsqrt(head_mean(d * d) + EPS) * gret_ref[:, sl]
        g = gate_ref[:, sl]
        y_ref[:, sl] = (g * jax.nn.sigmoid(g) * n).astype(y_ref.dtype)

        @pl.when(c == pl.num_programs(1) - 1)
        def _():
            st_ref[0, 2 * p] = s_new[:HEAD_DIM, :HEAD_DIM]
            st_ref[0, 2 * p + 1] = s_new[HEAD_DIM:, HEAD_DIM:]


def _retention(q, k, v, gate, state0, g_ret, n_batch, seq, chunk):
    nc = seq // chunk
    has_init = state0 is not None
    if state0 is None:
        state0 = jnp.zeros((1, N_HEADS, HEAD_DIM, HEAD_DIM), F32)
        s0_map = lambda b, c: (0, 0, 0, 0)
    else:
        s0_map = lambda b, c: (b, 0, 0, 0)
    dmask, row_decay, key_decay, state_decay = _ret_tables(chunk)
    rows = pl.BlockSpec((chunk, D_HALF), lambda b, c: (b * nc + c, 0))
    const2 = lambda shape: pl.BlockSpec(shape, lambda b, c: (0, 0))
    st_block = pl.BlockSpec((1, N_HEADS, HEAD_DIM, HEAD_DIM), lambda b, c: (b, 0, 0, 0))
    return pl.pallas_call(
        functools.partial(_ret_kernel, chunk=chunk, has_init=has_init),
        grid=(n_batch, nc),
        in_specs=[rows, rows, rows, rows,
                  pl.BlockSpec((1, N_HEADS, HEAD_DIM, HEAD_DIM), s0_map),
                  const2((1, D_HALF)),
                  pl.BlockSpec((N_HEADS, chunk, chunk), lambda b, c: (0, 0, 0)),
                  const2((chunk, D_HALF)), const2((chunk, D_HALF)), const2((1, D_HALF))],
        out_specs=[rows, st_block],
        out_shape=[jax.ShapeDtypeStruct((n_batch * seq, D_HALF), BF16),
                   jax.ShapeDtypeStruct((n_batch, N_HEADS, HEAD_DIM, HEAD_DIM), F32)],
        scratch_shapes=[pltpu.VMEM((N_PAIRS, LANES, LANES), F32)],
        compiler_params=_params(2),
        name="retention",
    )(q, k, v, gate, state0, g_ret.reshape(1, D_HALF), dmask, row_decay, key_decay, state_decay)


def _attp_table(dilation):
    qi = np.arange(ATT_BLOCK)[:, None]
    kj = np.arange(2 * ATT_BLOCK)[None, :]
    step = qi + ATT_BLOCK - kj
    in_win = (step >= 0) & (step <= ATT_BLOCK)
    bias = -_alibi_slopes()[:, None, None] * (step * dilation).astype(np.float64)[None]
    return jnp.asarray(np.where(in_win[None], bias, NEG_INF), F32)


def _attp_kernel(q_ref, kp_ref, kc_ref, vp_ref, vc_ref, tb_ref, o_ref, ml_ref):
    has_prev = pl.program_id(2) > 0
    lo = lax.broadcasted_iota(jnp.int32, (ATT_BLOCK, LANES), 1) < HEAD_DIM
    lane = lax.broadcasted_iota(jnp.int32, (ATT_BLOCK, LANES), 1)
    key_ok = (lax.broadcasted_iota(jnp.int32, (ATT_BLOCK, 2 * ATT_BLOCK), 1) >= ATT_BLOCK) | has_prev
    ml = jnp.zeros((ATT_BLOCK, LANES), F32)
    for p in range(N_PAIRS):
        sl = slice(p * LANES, (p + 1) * LANES)
        q = q_ref[:, sl]
        kb = jnp.concatenate([kp_ref[:, sl], kc_ref[:, sl]], axis=0).astype(BF16)
        vb = jnp.concatenate([vp_ref[:, sl], vc_ref[:, sl]], axis=0).astype(BF16)
        outs = []
        for half in range(2):
            h = 2 * p + half
            qh = jnp.where(lo if half == 0 else ~lo, q, 0.0).astype(BF16)
            s = _dot_nt(qh, kb) * HEAD_DIM ** -0.5
            s = jnp.where(key_ok, s + tb_ref[h], NEG_INF)
            m = jnp.max(s, axis=-1, keepdims=True)
            e = jnp.exp(s - m)
            l = jnp.sum(e, axis=-1, keepdims=True)
            outs.append(_dot(e.astype(BF16), vb))
            ml = jnp.where(lane == h, m, ml)
            ml = jnp.where(lane == N_HEADS + h, l, ml)
        o_ref[:, sl] = jnp.where(lo, outs[0], outs[1])
    ml_ref[...] = ml


def _att_prompt_pattern(q, k, v, n_batch, seq, dilation):
    sub = seq // dilation
    nb = sub // ATT_BLOCK
    view = lambda t: t.reshape(n_batch, sub, dilation * D_HALF)
    cur = pl.BlockSpec((None, ATT_BLOCK, D_HALF), lambda b, r, n: (b, n, r))
    prev = pl.BlockSpec((None, ATT_BLOCK, D_HALF), lambda b, r, n: (b, jnp.maximum(n - 1, 0), r))
    o, ml = pl.pallas_call(
        _attp_kernel,
        grid=(n_batch, dilation, nb),
        in_specs=[cur, prev, cur, prev, cur,
                  pl.BlockSpec((N_HEADS, ATT_BLOCK, 2 * ATT_BLOCK), lambda b, r, n: (0, 0, 0))],
        out_specs=[cur, pl.BlockSpec((None, ATT_BLOCK, LANES), lambda b, r, n: (b, n, r))],
        out_shape=[jax.ShapeDtypeStruct((n_batch, sub, dilation * D_HALF), F32),
                   jax.ShapeDtypeStruct((n_batch, sub, dilation * LANES), F32)],
        compiler_params=_params(3),
        name=f"att_prompt_d{dilation}",
    )(view(q), view(k), view(k), view(v), view(v), _attp_table(dilation))
    return o.reshape(n_batch * seq, D_HALF), ml.reshape(n_batch * seq, LANES)


def _atts_tables(buf, t_new):
    pos_q = buf + np.arange(t_new)
    pos_k = np.concatenate([np.arange(buf), buf + np.arange(LANES)])
    dist = pos_q[:, None] - pos_k[None, :]
    count = np.zeros(dist.shape)
    for window, dilation in DIL_PATTERNS:
        count += (dist >= 0) & (dist <= window) & (dist % dilation == 0)
    count[:, buf + t_new:] = 0
    bias = -_alibi_slopes()[:, None, None] * dist[None].astype(np.float64)
    table = np.where(count[None] > 0, bias + np.log(np.maximum(count, 1.0))[None], NEG_INF)
    table = table.reshape(N_HEADS * t_new, buf + LANES)
    return jnp.asarray(table[:, :buf], F32), jnp.asarray(table[:, buf:], F32)


def _atts_kernel(q_ref, kc_ref, kn_ref, vc_ref, vn_ref, tc_ref, tn_ref, o_ref, *, t_new):
    rows = N_HEADS * t_new
    row_head = lax.broadcasted_iota(jnp.int32, (rows, D_HALF), 0) // t_new
    col_head = lax.broadcasted_iota(jnp.int32, (rows, D_HALF), 1) // HEAD_DIM
    own = row_head == col_head
    q_heads = jnp.where(own, jnp.concatenate([q_ref[...]] * N_HEADS, axis=0), 0.0).astype(BF16)
    pad = jnp.zeros((LANES - t_new, D_HALF), F32)
    kn = jnp.concatenate([kn_ref[...], pad], axis=0).astype(BF16)
    vn = jnp.concatenate([vn_ref[...], pad], axis=0).astype(BF16)
    scale = HEAD_DIM ** -0.5
    sc = _dot_nt(q_heads, kc_ref[...].astype(BF16)) * scale + tc_ref[...]
    sn = _dot_nt(q_heads, kn) * scale + tn_ref[...]
    m = jnp.maximum(jnp.max(sc, axis=-1, keepdims=True), jnp.max(sn, axis=-1, keepdims=True))
    ec = jnp.exp(sc - m)
    en = jnp.exp(sn - m)
    l = jnp.sum(ec, axis=-1, keepdims=True) + jnp.sum(en, axis=-1, keepdims=True)
    acc = _dot(ec.astype(BF16), vc_ref[...].astype(BF16)) + _dot(en.astype(BF16), vn)
    acc = jnp.where(own, acc, 0.0) / l
    out = acc[:t_new]
    for h in range(1, N_HEADS):
        out = out + acc[h * t_new:(h + 1) * t_new]
    o_ref[...] = out.astype(o_ref.dtype)


def _att_sample(q, k_new, v_new, cache_k, cache_v, n_batch, t_new):
    buf = cache_k.shape[1]
    tc, tn = _atts_tables(buf, t_new)
    rows = pl.BlockSpec((t_new, D_HALF), lambda b: (b, 0))
    cache = pl.BlockSpec((None, buf, D_HALF), lambda b: (b, 0, 0))
    return pl.pallas_call(
        functools.partial(_atts_kernel, t_new=t_new),
        grid=(n_batch,),
        in_specs=[rows, cache, rows, cache, rows,
                  pl.BlockSpec((N_HEADS * t_new, buf), lambda b: (0, 0)),
                  pl.BlockSpec((N_HEADS * t_new, LANES), lambda b: (0, 0))],
        out_specs=rows,
        out_shape=jax.ShapeDtypeStruct((n_batch * t_new, D_HALF), BF16),
        compiler_params=_params(1, VMEM_LIMIT),
        name="att_sample",
    )(q, cache_k, k_new, cache_v, v_new, tc, tn)


def _out_kernel(*refs, n_parts):
    ret_ref = refs[0]
    part_refs = refs[1:1 + 2 * n_parts] if n_parts else refs[1:2]
    (x_ref, gt_ref, sc_ref, sh_ref, w_ref, gpost_ref, gpre_ref, wr_ref, br_ref,
     x1_ref, h_ref, top_ref, xin) = refs[len(part_refs) + 1:]
    tm = x_ref.shape[0]
    xin[:, :D_HALF] = ret_ref[...]
    if n_parts:
        o_refs, ml_refs = part_refs[:n_parts], part_refs[n_parts:]
        for h in range(N_HEADS):
            ms = [r[:, h:h + 1] for r in ml_refs]
            ls = [r[:, N_HEADS + h:N_HEADS + h + 1] for r in ml_refs]
            mx = functools.reduce(jnp.maximum, ms)
            ws = [jnp.exp(m - mx) for m in ms]
            den = sum(l * w for l, w in zip(ls, ws))
            hs = slice(h * HEAD_DIM, (h + 1) * HEAD_DIM)
            att = sum(r[:, hs] * (w / den) for r, w in zip(o_refs, ws))
            xin[:, D_HALF + h * HEAD_DIM:D_HALF + (h + 1) * HEAD_DIM] = att.astype(BF16)
    else:
        xin[:, D_HALF:] = part_refs[0][...]
    mix = _dot(xin[...], w_ref[...])
    x1 = x_ref[...] + gt_ref[...] * _rms(mix, gpost_ref[...])
    x1_ref[...] = x1
    h = _rms(x1, gpre_ref[...]) * (1.0 + sc_ref[...]) + sh_ref[...]
    _slab_store(h_ref, h)
    logits = jnp.dot(h, wr_ref[...], preferred_element_type=F32,
                     precision=lax.Precision.HIGHEST) + br_ref[...]
    lane_e = lax.broadcasted_iota(jnp.int32, (tm, N_EXPERTS), 1).astype(F32)
    lane_o = lax.broadcasted_iota(jnp.int32, (tm, LANES), 1)
    vals, idxs = [], []
    work = logits
    for _ in range(TOP_K):
        v = jnp.max(work, axis=-1, keepdims=True)
        i = jnp.min(jnp.where(work == v, lane_e, float(N_EXPERTS)), axis=-1, keepdims=True)
        vals.append(v)
        idxs.append(i)
        work = jnp.where(lane_e == i, -jnp.inf, work)
    es = [jnp.exp(v - vals[0]) for v in vals]
    tot = sum(es)
    packed = jnp.zeros((tm, LANES), F32)
    for j in range(TOP_K):
        packed = jnp.where(lane_o == j, idxs[j], packed)
        packed = jnp.where(lane_o == TOP_K + j, es[j] / tot, packed)
    top_ref[...] = packed


def _mixer_out(ret_y, parts, x, mod, w_out_bf, g_post, g_pre, w_router, b_router,
               per_row, rows_per_batch):
    n = x.shape[0]
    tm = min(TM_OUT, n)
    tpb = 1 if per_row else rows_per_batch // tm
    n_parts = len(parts[0]) if isinstance(parts, tuple) else 0
    half = pl.BlockSpec((tm, D_HALF), lambda i: (i, 0))
    full = pl.BlockSpec((tm, D_MODEL), lambda i: (i, 0))
    stat = pl.BlockSpec((tm, LANES), lambda i: (i, 0))
    vec = pl.BlockSpec((1, D_MODEL), lambda i: (0, 0))
    mods = [_mod_spec(tm, c, per_row, tpb) for c in (MOD_GATE_MIX, MOD_SCALE_FFN, MOD_SHIFT_FFN)]
    if n_parts:
        part_args = list(parts[0]) + list(parts[1])
        part_specs = [half] * n_parts + [stat] * n_parts
    else:
        part_args, part_specs = [parts], [half]
    return pl.pallas_call(
        functools.partial(_out_kernel, n_parts=n_parts),
        grid=(n // tm,),
        in_specs=[half] + part_specs + [full] + mods + [
                  pl.BlockSpec((D_MODEL, D_MODEL), lambda i: (0, 0)), vec, vec,
                  pl.BlockSpec((D_MODEL, N_EXPERTS), lambda i: (0, 0)),
                  pl.BlockSpec((1, N_EXPERTS), lambda i: (0, 0))],
        out_specs=[full, pl.BlockSpec((tm,) + SLAB, lambda i: (i, 0, 0)), stat],
        out_shape=[jax.ShapeDtypeStruct((n, D_MODEL), F32),
                   jax.ShapeDtypeStruct((n,) + SLAB, F32),
                   jax.ShapeDtypeStruct((n, LANES), F32)],
        scratch_shapes=[pltpu.VMEM((tm, D_MODEL), BF16)],
        compiler_params=_params(1, VMEM_LIMIT),
        name="mixer_out",
    )(ret_y, *part_args, x, mod, mod, mod, w_out_bf, g_post, g_pre, w_router, b_router)


def _route_kernel(top_ref, dest_ref, ends_ref, running, starts, *, tm, tm_blk):
    ph, i = pl.program_id(0), pl.program_id(1)
    lane_f = lax.broadcasted_iota(jnp.int32, (tm, LANES), 1).astype(F32)
    top = top_ref[...]
    picks = [lane_f == top[:, k:k + 1] for k in range(TOP_K)]
    chosen = sum(p.astype(F32) for p in picks)
    tile_counts = jnp.sum(chosen, axis=0, keepdims=True)

    @pl.when((ph == 0) & (i == 0))
    def _():
        running[...] = jnp.zeros_like(running)

    @pl.when(ph == 0)
    def _():
        running[...] += tile_counts

    @pl.when((ph == 1) & (i == 0))
    def _():
        lane = lax.broadcasted_iota(jnp.int32, (1, LANES), 1)
        assert tm_blk & (tm_blk - 1) == 0
        blocks = (running[...].astype(jnp.int32) + (tm_blk - 1)) >> (tm_blk.bit_length() - 1)
        ends = blocks
        shift = 1
        while shift < N_EXPERTS:
            ends = ends + jnp.where(lane >= shift, pltpu.roll(ends, shift, axis=1), 0)
            shift *= 2
        ends_ref[...] = ends
        starts[...] = ((ends - blocks) * tm_blk).astype(F32)
        running[...] = jnp.zeros_like(running)

    @pl.when(ph == 1)
    def _():
        earlier = (lax.broadcasted_iota(jnp.int32, (tm, tm), 0)
                   > lax.broadcasted_iota(jnp.int32, (tm, tm), 1))
        ahead = _dot(jnp.where(earlier, 1.0, 0.0).astype(BF16), chosen.astype(BF16))
        slot = starts[...] + running[...] + ahead
        lane_o = lax.broadcasted_iota(jnp.int32, (tm, LANES), 1)
        packed = jnp.zeros((tm, LANES), F32)
        for k, pick in enumerate(picks):
            dest_k = jnp.sum(jnp.where(pick, slot, 0.0), axis=-1, keepdims=True)
            packed = jnp.where(lane_o == k, dest_k, packed)
        dest_ref[...] = packed.astype(jnp.int32)
        running[...] += tile_counts


def _route(top, tm_blk):
    n = top.shape[0]
    tm = 512
    return pl.pallas_call(
        functools.partial(_route_kernel, tm=tm, tm_blk=tm_blk),
        grid=(2, n // tm),
        in_specs=[pl.BlockSpec((tm, LANES), lambda ph, i: (i, 0))],
        out_specs=[pl.BlockSpec((tm, LANES), lambda ph, i: (i * ph, 0)),
                   pl.BlockSpec((1, LANES), lambda ph, i: (0, 0))],
        out_shape=[jax.ShapeDtypeStruct((n, LANES), jnp.int32),
                   jax.ShapeDtypeStruct((1, LANES), jnp.int32)],
        scratch_shapes=[pltpu.VMEM((1, LANES), F32), pltpu.VMEM((1, LANES), F32)],
        compiler_params=_params(2),
        name="route",
    )(top)


def _dispatch_kernel(ends_ref, dest_ref, h_hbm, xs_hbm, zeros, sem, *, tm, tm_blk, n_blocks):
    i = pl.program_id(0)

    def fill(b):
        return pltpu.make_async_copy(zeros, xs_hbm.at[pl.ds(b * tm_blk, tm_blk)], sem.at[1])

    def fill_start(b, carry):
        fill(b).start()
        return carry

    def fill_wait(b, carry):
        fill(b).wait()
        return carry

    @pl.when(i == 0)
    def _():
        zeros[...] = jnp.zeros_like(zeros)
        n_used = ends_ref[N_EXPERTS - 1]
        for do in (fill_start, fill_wait):
            for e in range(N_EXPERTS):
                first = ends_ref[e - 1] if e else 0

                @pl.when(ends_ref[e] > first)
                def _():
                    do(ends_ref[e] - 1, 0)

            lax.fori_loop(n_used, n_blocks, do, 0)

    def body(j, carry):
        for k in range(TOP_K):
            pltpu.make_async_copy(h_hbm.at[i * tm + j], xs_hbm.at[dest_ref[j * TOP_K + k]], sem.at[0]).start()
        return carry

    lax.fori_loop(0, tm, body, 0)
    pltpu.make_async_copy(h_hbm.at[pl.ds(0, tm * TOP_K)], xs_hbm.at[pl.ds(0, tm * TOP_K)], sem.at[0]).wait()


def _dispatch(h, dest_flat, ends, n_blocks, tm_blk):
    n = h.shape[0]
    tm = 256
    grid_spec = pltpu.PrefetchScalarGridSpec(
        num_scalar_prefetch=1,
        grid=(n // tm,),
        in_specs=[pl.BlockSpec((tm * TOP_K,), lambda i, ends: (i,), memory_space=pltpu.SMEM),
                  pl.BlockSpec(memory_space=pl.ANY)],
        out_specs=pl.BlockSpec(memory_space=pl.ANY),
        scratch_shapes=[pltpu.VMEM((tm_blk,) + SLAB, F32), pltpu.SemaphoreType.DMA((2,))],
    )
    return pl.pallas_call(
        functools.partial(_dispatch_kernel, tm=tm, tm_blk=tm_blk, n_blocks=n_blocks),
        grid_spec=grid_spec,
        out_shape=jax.ShapeDtypeStruct((n_blocks * tm_blk,) + SLAB, F32),
        compiler_params=_params(1),
        name="dispatch",
    )(ends, dest_flat, h)


def _moe_kernel(be_ref, nu_ref, x_ref, wgu_ref, wd_ref, bgu_ref, bd_ref, o_ref, wgu_bf, wd_rows, wd_bf):
    i = pl.program_id(0)
    n_used = nu_ref[0]
    d_gu = wgu_ref.shape[1]
    new_expert = (i == 0) | (be_ref[i] != be_ref[jnp.maximum(i - 1, 0)])

    @pl.when((i < n_used) & new_expert)
    def _():
        wgu_bf[...] = wgu_ref[...].astype(BF16)
        for c in range(wd_rows.shape[0]):
            cols = slice(c * LANES, (c + 1) * LANES)
            for r in range(2):
                wd_rows[c, pl.ds(r, d_gu // 2, stride=2), :] = wd_ref[:, cols]
            wd_bf[:, cols] = wd_rows[c].astype(BF16)

    @pl.when(i < n_used)
    def _():
        x = _slab_load(x_ref).astype(BF16)
        even = (lax.broadcasted_iota(jnp.int32, (x.shape[0], MOE_COLS), 1) & 1) == 0
        acc = bd_ref[...]
        for c in range(d_gu // MOE_COLS):
            cols = slice(c * MOE_COLS, (c + 1) * MOE_COLS)
            gu = _dot(x, wgu_bf[:, cols]) + bgu_ref[:, cols]
            nxt = pltpu.roll(gu, MOE_COLS - 1, axis=1)
            gate = jnp.minimum(gu, SWIGLU_LIMIT)
            up = jnp.clip(nxt, -SWIGLU_LIMIT, SWIGLU_LIMIT)
            act = jnp.where(even, gate * jax.nn.sigmoid(SWIGLU_ALPHA * gate) * (up + 1.0), 0.0)
            acc = acc + _dot(act.astype(BF16), wd_bf[cols, :])
        _slab_store(o_ref, acc)

    @pl.when(i >= n_used)
    def _():
        o_ref[...] = jnp.zeros_like(o_ref)


def _moe_blocks(xs, block_e, n_used, w_gu, w_down, b_gu, b_down, tm):
    n_blocks = block_e.shape[0]
    n_e, d, d_gu = w_gu.shape
    wspec = lambda k, n: pl.BlockSpec((None, k, n), lambda i, be, nu: (be[i], 0, 0))
    grid_spec = pltpu.PrefetchScalarGridSpec(
        num_scalar_prefetch=2,
        grid=(n_blocks,),
        in_specs=[pl.BlockSpec((tm,) + SLAB, lambda i, be, nu: (jnp.minimum(i, nu[0] - 1), 0, 0)),
                  wspec(d, d_gu), wspec(d_gu // 2, d), wspec(1, d_gu), wspec(1, d)],
        out_specs=pl.BlockSpec((tm,) + SLAB, lambda i, be, nu: (i, 0, 0)),
        scratch_shapes=[pltpu.VMEM((d, d_gu), BF16), pltpu.VMEM((d // LANES, d_gu, LANES), F32),
                        pltpu.VMEM((d_gu, d), BF16)],
    )
    return pl.pallas_call(
        _moe_kernel,
        grid_spec=grid_spec,
        out_shape=jax.ShapeDtypeStruct((n_blocks * tm,) + SLAB, F32),
        compiler_params=_params(1, VMEM_LIMIT),
        name="moe_blocks",
    )(block_e, n_used, xs, w_gu, w_down, b_gu.reshape(n_e, 1, d_gu), b_down.reshape(n_e, 1, d))


def _fin_kernel(dest_ref, rows_hbm, top_ref, x1_ref, gt_ref, g_ref, o_ref, buf, sem, *, tm):
    def body(j, carry):
        for k in range(TOP_K):
            pltpu.make_async_copy(rows_hbm.at[dest_ref[j * TOP_K + k]], buf.at[k, j], sem.at[0]).start()
        return carry
    lax.fori_loop(0, tm, body, 0)
    for k in range(TOP_K):
        pltpu.make_async_copy(rows_hbm.at[pl.ds(0, tm)], buf.at[k], sem.at[0]).wait()
    top = top_ref[...]
    f = sum(_slab_load(buf.at[k]) * top[:, TOP_K + k:TOP_K + k + 1] for k in range(TOP_K))
    o_ref[...] = x1_ref[...] + gt_ref[...] * _rms(f, g_ref[...])


def _finish(dest, rows, top, x1, mod, g_post, per_row, rows_per_batch):
    n = x1.shape[0]
    tm = min(TM_FIN, n)
    full = pl.BlockSpec((tm, D_MODEL), lambda i: (i, 0))
    gate = _mod_spec(tm, MOD_GATE_FFN, per_row, 1 if per_row else rows_per_batch // tm)
    return pl.pallas_call(
        functools.partial(_fin_kernel, tm=tm),
        grid=(n // tm,),
        in_specs=[pl.BlockSpec((tm * TOP_K,), lambda i: (i,), memory_space=pltpu.SMEM),
                  pl.BlockSpec(memory_space=pl.ANY),
                  pl.BlockSpec((tm, LANES), lambda i: (i, 0)), full, gate,
                  pl.BlockSpec((1, D_MODEL), lambda i: (0, 0))],
        out_specs=full,
        out_shape=jax.ShapeDtypeStruct((n, D_MODEL), F32),
        scratch_shapes=[pltpu.VMEM((TOP_K, tm) + SLAB, F32), pltpu.SemaphoreType.DMA((1,))],
        compiler_params=_params(1),
        name="finish",
    )(dest, rows, top, x1, mod, g_post)


def kernel(x_prompt, x_sample, state_ret, cache_win_k, cache_win_v, c_prompt, c_sample, w_ada, b_ada,
           g_pre_mix, g_post_mix, g_pre_ffn, g_post_ffn, w_in, g_ret, w_out, w_router, b_router,
           w_gate_up, b_gate_up, w_down, b_down):
    depth = w_in.shape[0]
    assert depth == 1, "single-layer step"
    n_b, seq, d = x_prompt.shape
    n_db, t_new, _ = x_sample.shape
    n_p, n_s = n_b * seq, n_db * t_new
    buf = cache_win_k.shape[2]
    keep = min(DIL_PATTERNS[-1][0], seq)
    vec = lambda g: g[0].reshape(1, -1)

    mod_p, mod_s = _ada(c_prompt, c_sample, w_ada[0], b_ada[0], t_new)
    mod_p = mod_p[:n_b].reshape(n_b, 1, -1)
    mod_s = mod_s.reshape(n_s, -1)

    w_in_bf = w_in[0].astype(BF16)
    w_out_bf = w_out[0].astype(BF16)
    xp = x_prompt.reshape(n_p, d)
    xs = x_sample.reshape(n_s, d)
    router = (w_router[0], b_router[0].reshape(1, -1))

    qr, kr, vr, gr, qa, ka, va = _in_proj(xp, mod_p, vec(g_pre_mix), w_in_bf, False, seq)
    ret_y_p, ret_state_p = _retention(qr, kr, vr, gr, None, g_ret[0], n_b, seq, RET_CHUNK)
    parts = [_att_prompt_pattern(qa, ka, va, n_b, seq, dil) for _, dil in DIL_PATTERNS]
    x1_p, h_p, top_p = _mixer_out(ret_y_p, ([o for o, _ in parts], [ml for _, ml in parts]), xp, mod_p,
                                  w_out_bf, vec(g_post_mix), vec(g_pre_ffn), *router, False, seq)

    qr_s, kr_s, vr_s, gr_s, qa_s, ka_s, va_s = _in_proj(xs, mod_s, vec(g_pre_mix), w_in_bf, True, t_new)
    ret_y_s, ret_state_s = _retention(qr_s, kr_s, vr_s, gr_s, state_ret[0], g_ret[0], n_db, t_new, t_new)
    att_s = _att_sample(qa_s, ka_s, va_s, cache_win_k[0].reshape(n_db, buf, D_HALF),
                        cache_win_v[0].reshape(n_db, buf, D_HALF), n_db, t_new)
    x1_s, h_s, top_s = _mixer_out(ret_y_s, att_s, xs, mod_s, w_out_bf, vec(g_post_mix), vec(g_pre_ffn),
                                  *router, True, t_new)

    h_all = jnp.concatenate([h_p, h_s], axis=0)
    n_blocks = (n_p + n_s) * TOP_K // TM_MOE + N_EXPERTS
    dest, ends = _route(jnp.concatenate([top_p, top_s], axis=0), TM_MOE)
    dest = dest[:, :TOP_K].reshape(-1)
    ends = ends[0, :N_EXPERTS]
    block_e = jnp.minimum(jnp.sum(jnp.arange(n_blocks, dtype=jnp.int32)[:, None] >= ends[None, :], axis=1),
                          N_EXPERTS - 1).astype(jnp.int32)
    rows_in = _dispatch(h_all, dest, ends, n_blocks, TM_MOE)
    rows = _moe_blocks(rows_in, block_e, ends[N_EXPERTS - 1:], w_gate_up[0], w_down[0],
                       b_gate_up[0], b_down[0], TM_MOE)
    y_p = _finish(dest[:n_p * TOP_K], rows, top_p, x1_p, mod_p, vec(g_post_ffn), False, seq)
    y_s = _finish(dest[n_p * TOP_K:], rows, top_s, x1_s, mod_s, vec(g_post_ffn), True, t_new)

    win = lambda t: t.reshape(n_b, seq, N_HEADS, HEAD_DIM)[:, seq - keep:][None]
    new = lambda t: t.reshape(n_db, t_new, N_HEADS, HEAD_DIM)[None]
    return (y_p.reshape(n_b, seq, d), y_s.reshape(n_db, t_new, d), ret_state_p[None], ret_state_s[None],
            win(ka), win(va), new(ka_s), new(va_s))
```

```python
import functools

import numpy as np
import jax
import jax.numpy as jnp
from jax import lax
from jax.experimental import pallas as pl
from jax.experimental.pallas import tpu as pltpu

F32 = jnp.float32
BF16 = jnp.bfloat16

D_MODEL = 1024
HEAD_DIM = 64
N_HEADS = 8
D_HALF = N_HEADS * HEAD_DIM
N_SEG = 7
LANES = 128
N_PAIRS = D_HALF // LANES
RET_CHUNK = 128
ATT_BLOCK = 128
DIL_PATTERNS = ((128, 1), (512, 4), (2048, 16))
N_EXPERTS = 32
TOP_K = 4
SWIGLU_LIMIT = 7.0
SWIGLU_ALPHA = 1.702
EPS = 1e-6
NEG_INF = -1e30
MOD_SHIFT_MIX, MOD_SCALE_MIX, MOD_GATE_MIX, MOD_SHIFT_FFN, MOD_SCALE_FFN, MOD_GATE_FFN = range(6)

TM_IN = 512
TM_OUT = 256
TM_MOE = 256
MOE_COLS = 512
TM_FIN = 128
VMEM_LIMIT = 56 * 1024 * 1024


def _params(n_axes, vmem=None):
    return pltpu.CompilerParams(dimension_semantics=("arbitrary",) * n_axes, vmem_limit_bytes=vmem)


def _ret_log_decay():
    return np.log(1.0 - 2.0 ** (-5.0 - np.arange(N_HEADS, dtype=np.float64)))


def _alibi_slopes():
    return 2.0 ** (-8.0 * (np.arange(N_HEADS, dtype=np.float64) + 1.0) / N_HEADS)


def _per_head_lanes(a):
    return np.repeat(a, HEAD_DIM, axis=-1)


def _rms(x, g):
    return x * lax.rsqrt(jnp.mean(x * x, axis=-1, keepdims=True) + EPS) * g


def _dot(a, b):
    return jnp.dot(a, b, preferred_element_type=F32)


def _dot_nt(a, b):
    return lax.dot_general(a, b, (((1,), (1,)), ((), ())), preferred_element_type=F32)


def _dot_tn(a, b):
    return lax.dot_general(a, b, (((0,), (0,)), ((), ())), preferred_element_type=F32)


SLAB = D_MODEL // LANES


def _slab_rows(r):
    return pl.ds(pl.multiple_of(r * SLAB, SLAB), SLAB)


def _slab_load(ref, n):
    return jnp.concatenate([ref[pl.ds(s, n, stride=SLAB), :] for s in range(SLAB)], axis=1)


def _slab_store(ref, value):
    n = value.shape[0]
    for s in range(SLAB):
        ref[pl.ds(s, n, stride=SLAB), :] = value[:, s * LANES:(s + 1) * LANES]


def _mod_spec(tm, col, per_row, tiles_per_batch):
    if per_row:
        return pl.BlockSpec((tm, D_MODEL), lambda i: (i, col))
    return pl.BlockSpec((None, 1, D_MODEL), lambda i: (i // tiles_per_batch, 0, col))


def _ada_kernel(cp_ref, cs_ref, w_ref, b_ref, op_ref, os_ref):
    w = w_ref[...].astype(BF16)

    def mod(c):
        return _dot((c * jax.nn.sigmoid(c)).astype(BF16), w) + b_ref[...]

    op_ref[...] = mod(cp_ref[...])
    ms = mod(cs_ref[...])
    os_ref[...] = jnp.broadcast_to(ms[:, None, :], os_ref.shape)


def _ada(c_prompt, c_sample, w_ada, b_ada, t_new):
    n_b, d = c_prompt.shape
    n_db = c_sample.shape[0]
    n_out = w_ada.shape[1]
    cp = jnp.concatenate([c_prompt, jnp.zeros((-n_b % 8, d), F32)], axis=0)
    return pl.pallas_call(
        _ada_kernel,
        grid=(n_out // d,),
        in_specs=[pl.BlockSpec(cp.shape, lambda j: (0, 0)),
                  pl.BlockSpec((n_db, d), lambda j: (0, 0)),
                  pl.BlockSpec((d, d), lambda j: (0, j)),
                  pl.BlockSpec((1, d), lambda j: (0, j))],
        out_specs=[pl.BlockSpec(cp.shape, lambda j: (0, j)),
                   pl.BlockSpec((n_db, t_new, d), lambda j: (0, 0, j))],
        out_shape=[jax.ShapeDtypeStruct((cp.shape[0], n_out), F32),
                   jax.ShapeDtypeStruct((n_db, t_new, n_out), F32)],
        compiler_params=_params(1, VMEM_LIMIT),
        name="ada",
    )(cp, c_sample, w_ada, b_ada.reshape(1, n_out))


def _in_kernel(x_ref, sc_ref, sh_ref, g_ref, w_ref, *o_refs):
    h = (_rms(x_ref[...], g_ref[...]) * (1.0 + sc_ref[...]) + sh_ref[...]).astype(BF16)
    for s, o_ref in enumerate(o_refs):
        o_ref[...] = _dot(h, w_ref[:, s * D_HALF:(s + 1) * D_HALF])


def _in_proj(x, mod, g, w_bf, per_row, rows_per_batch):
    n = x.shape[0]
    tm = min(TM_IN, n)
    tpb = 1 if per_row else rows_per_batch // tm
    seg = pl.BlockSpec((tm, D_HALF), lambda i: (i, 0))
    return pl.pallas_call(
        _in_kernel,
        grid=(n // tm,),
        in_specs=[pl.BlockSpec((tm, D_MODEL), lambda i: (i, 0)),
                  _mod_spec(tm, MOD_SCALE_MIX, per_row, tpb), _mod_spec(tm, MOD_SHIFT_MIX, per_row, tpb),
                  pl.BlockSpec((1, D_MODEL), lambda i: (0, 0)),
                  pl.BlockSpec((D_MODEL, N_SEG * D_HALF), lambda i: (0, 0))],
        out_specs=[seg] * N_SEG,
        out_shape=[jax.ShapeDtypeStruct((n, D_HALF), F32)] * N_SEG,
        compiler_params=_params(1, VMEM_LIMIT),
        name="in_proj",
    )(x, mod, mod, g, w_bf)


def _ret_tables(chunk):
    lg = _ret_log_decay()
    pos = np.arange(chunk, dtype=np.float64)
    diff = pos[:, None] - pos[None, :]
    dmask = np.where(diff >= 0, np.exp(np.maximum(diff, 0.0)[None] * lg[:, None, None]), 0.0)
    row_decay = _per_head_lanes(np.exp((pos[:, None] + 1.0) * lg[None, :]))
    key_decay = _per_head_lanes(np.exp((chunk - 1.0 - pos)[:, None] * lg[None, :]))
    state_decay = _per_head_lanes(np.exp(chunk * lg)[None, :])
    return tuple(jnp.asarray(t, F32) for t in (dmask, row_decay, key_decay, state_decay))


def _ret_kernel(q_ref, k_ref, v_ref, gate_ref, s0_ref, gret_ref, dm_ref, rd_ref, kd_ref, sd_ref,
                y_ref, st_ref, state, *, chunk, has_init):
    c = pl.program_id(1)
    lo = lax.broadcasted_iota(jnp.int32, (chunk, LANES), 1) < HEAD_DIM
    blk_r = lax.broadcasted_iota(jnp.int32, (LANES, LANES), 0) < HEAD_DIM
    blk_c = lax.broadcasted_iota(jnp.int32, (LANES, LANES), 1) < HEAD_DIM
    same_head = blk_r == blk_c

    @pl.when(c == 0)
    def _():
        if has_init:
            zero = jnp.zeros((HEAD_DIM, HEAD_DIM), F32)
            for p in range(N_PAIRS):
                top = jnp.concatenate([s0_ref[0, 2 * p], zero], axis=1)
                bot = jnp.concatenate([zero, s0_ref[0, 2 * p + 1]], axis=1)
                state[p] = jnp.concatenate([top, bot], axis=0)
        else:
            state[...] = jnp.zeros_like(state)

    for p in range(N_PAIRS):
        sl = slice(p * LANES, (p + 1) * LANES)
        q = q_ref[:, sl]
        k = k_ref[:, sl] * HEAD_DIM ** -0.5
        vb = v_ref[:, sl].astype(BF16)
        kb = k.astype(BF16)
        inner = []
        for half in range(2):
            qh = jnp.where(lo if half == 0 else ~lo, q, 0.0).astype(BF16)
            scores = _dot_nt(qh, kb) * dm_ref[2 * p + half]
            inner.append(_dot(scores.astype(BF16), vb))
        s_old = state[p]
        cross = _dot(q.astype(BF16), s_old.astype(BF16)) * rd_ref[:, sl]
        o = jnp.where(lo, inner[0], inner[1]) + cross
        upd = _dot_tn((k * kd_ref[:, sl]).astype(BF16), vb)
        s_new = sd_ref[:, sl] * s_old + jnp.where(same_head, upd, 0.0)
        state[p] = s_new

        def head_mean(t):
            m_lo = jnp.sum(jnp.where(lo, t, 0.0), axis=-1, keepdims=True)
            m_hi = jnp.sum(jnp.where(lo, 0.0, t), axis=-1, keepdims=True)
            return jnp.where(lo, m_lo, m_hi) * (1.0 / HEAD_DIM)

        d = o - head_mean(o)
        n = d * lax.rsqrt(head_mean(d * d) + EPS) * gret_ref[:, sl]
        g = gate_ref[:, sl]
        y_ref[:, sl] = (g * jax.nn.sigmoid(g) * n).astype(y_ref.dtype)

        @pl.when(c == pl.num_programs(1) - 1)
        def _():
            st_ref[0, 2 * p] = s_new[:HEAD_DIM, :HEAD_DIM]
            st_ref[0, 2 * p + 1] = s_new[HEAD_DIM:, HEAD_DIM:]


def _retention(q, k, v, gate, state0, g_ret, n_batch, seq, chunk):
    nc = seq // chunk
    has_init = state0 is not None
    if state0 is None:
        state0 = jnp.zeros((1, N_HEADS, HEAD_DIM, HEAD_DIM), F32)
        s0_map = lambda b, c: (0, 0, 0, 0)
    else:
        s0_map = lambda b, c: (b, 0, 0, 0)
    dmask, row_decay, key_decay, state_decay = _ret_tables(chunk)
    rows = pl.BlockSpec((chunk, D_HALF), lambda b, c: (b * nc + c, 0))
    const2 = lambda shape: pl.BlockSpec(shape, lambda b, c: (0, 0))
    st_block = pl.BlockSpec((1, N_HEADS, HEAD_DIM, HEAD_DIM), lambda b, c: (b, 0, 0, 0))
    return pl.pallas_call(
        functools.partial(_ret_kernel, chunk=chunk, has_init=has_init),
        grid=(n_batch, nc),
        in_specs=[rows, rows, rows, rows,
                  pl.BlockSpec((1, N_HEADS, HEAD_DIM, HEAD_DIM), s0_map),
                  const2((1, D_HALF)),
                  pl.BlockSpec((N_HEADS, chunk, chunk), lambda b, c: (0, 0, 0)),
                  const2((chunk, D_HALF)), const2((chunk, D_HALF)), const2((1, D_HALF))],
        out_specs=[rows, st_block],
        out_shape=[jax.ShapeDtypeStruct((n_batch * seq, D_HALF), BF16),
                   jax.ShapeDtypeStruct((n_batch, N_HEADS, HEAD_DIM, HEAD_DIM), F32)],
        scratch_shapes=[pltpu.VMEM((N_PAIRS, LANES, LANES), F32)],
        compiler_params=_params(2),
        name="retention",
    )(q, k, v, gate, state0, g_ret.reshape(1, D_HALF), dmask, row_decay, key_decay, state_decay)


def _attp_table(dilation):
    qi = np.arange(ATT_BLOCK)[:, None]
    kj = np.arange(2 * ATT_BLOCK)[None, :]
    step = qi + ATT_BLOCK - kj
    in_win = (step >= 0) & (step <= ATT_BLOCK)
    bias = -_alibi_slopes()[:, None, None] * (step * dilation).astype(np.float64)[None]
    return jnp.asarray(np.where(in_win[None], bias, NEG_INF), F32)


def _attp_kernel(q_ref, kp_ref, kc_ref, vp_ref, vc_ref, tb_ref, o_ref, ml_ref):
    has_prev = pl.program_id(2) > 0
    lo = lax.broadcasted_iota(jnp.int32, (ATT_BLOCK, LANES), 1) < HEAD_DIM
    lane = lax.broadcasted_iota(jnp.int32, (ATT_BLOCK, LANES), 1)
    key_ok = (lax.broadcasted_iota(jnp.int32, (ATT_BLOCK, 2 * ATT_BLOCK), 1) >= ATT_BLOCK) | has_prev
    ml = jnp.zeros((ATT_BLOCK, LANES), F32)
    for p in range(N_PAIRS):
        sl = slice(p * LANES, (p + 1) * LANES)
        q = q_ref[:, sl]
        kb = jnp.concatenate([kp_ref[:, sl], kc_ref[:, sl]], axis=0).astype(BF16)
        vb = jnp.concatenate([vp_ref[:, sl], vc_ref[:, sl]], axis=0).astype(BF16)
        outs = []
        for half in range(2):
            h = 2 * p + half
            qh = jnp.where(lo if half == 0 else ~lo, q, 0.0).astype(BF16)
            s = _dot_nt(qh, kb) * HEAD_DIM ** -0.5
            s = jnp.where(key_ok, s + tb_ref[h], NEG_INF)
            m = jnp.max(s, axis=-1, keepdims=True)
            e = jnp.exp(s - m)
            l = jnp.sum(e, axis=-1, keepdims=True)
            outs.append(_dot(e.astype(BF16), vb))
            ml = jnp.where(lane == h, m, ml)
            ml = jnp.where(lane == N_HEADS + h, l, ml)
        o_ref[:, sl] = jnp.where(lo, outs[0], outs[1])
    ml_ref[...] = ml


def _att_prompt_pattern(q, k, v, n_batch, seq, dilation):
    sub = seq // dilation
    nb = sub // ATT_BLOCK
    view = lambda t: t.reshape(n_batch, sub, dilation * D_HALF)
    cur = pl.BlockSpec((None, ATT_BLOCK, D_HALF), lambda b, r, n: (b, n, r))
    prev = pl.BlockSpec((None, ATT_BLOCK, D_HALF), lambda b, r, n: (b, jnp.maximum(n - 1, 0), r))
    o, ml = pl.pallas_call(
        _attp_kernel,
        grid=(n_batch, dilation, nb),
        in_specs=[cur, prev, cur, prev, cur,
                  pl.BlockSpec((N_HEADS, ATT_BLOCK, 2 * ATT_BLOCK), lambda b, r, n: (0, 0, 0))],
        out_specs=[cur, pl.BlockSpec((None, ATT_BLOCK, LANES), lambda b, r, n: (b, n, r))],
        out_shape=[jax.ShapeDtypeStruct((n_batch, sub, dilation * D_HALF), F32),
                   jax.ShapeDtypeStruct((n_batch, sub, dilation * LANES), F32)],
        compiler_params=_params(3),
        name=f"att_prompt_d{dilation}",
    )(view(q), view(k), view(k), view(v), view(v), _attp_table(dilation))
    return o.reshape(n_batch * seq, D_HALF), ml.reshape(n_batch * seq, LANES)


def _atts_tables(buf, t_new):
    pos_q = buf + np.arange(t_new)
    pos_k = np.concatenate([np.arange(buf), buf + np.arange(LANES)])
    dist = pos_q[:, None] - pos_k[None, :]
    count = np.zeros(dist.shape)
    for window, dilation in DIL_PATTERNS:
        count += (dist >= 0) & (dist <= window) & (dist % dilation == 0)
    count[:, buf + t_new:] = 0
    bias = -_alibi_slopes()[:, None, None] * dist[None].astype(np.float64)
    table = np.where(count[None] > 0, bias + np.log(np.maximum(count, 1.0))[None], NEG_INF)
    table = table.reshape(N_HEADS * t_new, buf + LANES)
    return jnp.asarray(table[:, :buf], F32), jnp.asarray(table[:, buf:], F32)


def _atts_kernel(q_ref, kc_ref, kn_ref, vc_ref, vn_ref, tc_ref, tn_ref, o_ref, *, t_new):
    rows = N_HEADS * t_new
    row_head = lax.broadcasted_iota(jnp.int32, (rows, D_HALF), 0) // t_new
    col_head = lax.broadcasted_iota(jnp.int32, (rows, D_HALF), 1) // HEAD_DIM
    own = row_head == col_head
    q_heads = jnp.where(own, jnp.concatenate([q_ref[...]] * N_HEADS, axis=0), 0.0).astype(BF16)
    pad = jnp.zeros((LANES - t_new, D_HALF), F32)
    kn = jnp.concatenate([kn_ref[...], pad], axis=0).astype(BF16)
    vn = jnp.concatenate([vn_ref[...], pad], axis=0).astype(BF16)
    scale = HEAD_DIM ** -0.5
    sc = _dot_nt(q_heads, kc_ref[...].astype(BF16)) * scale + tc_ref[...]
    sn = _dot_nt(q_heads, kn) * scale + tn_ref[...]
    m = jnp.maximum(jnp.max(sc, axis=-1, keepdims=True), jnp.max(sn, axis=-1, keepdims=True))
    ec = jnp.exp(sc - m)
    en = jnp.exp(sn - m)
    l = jnp.sum(ec, axis=-1, keepdims=True) + jnp.sum(en, axis=-1, keepdims=True)
    acc = _dot(ec.astype(BF16), vc_ref[...].astype(BF16)) + _dot(en.astype(BF16), vn)
    acc = jnp.where(own, acc, 0.0) / l
    out = acc[:t_new]
    for h in range(1, N_HEADS):
        out = out + acc[h * t_new:(h + 1) * t_new]
    o_ref[...] = out.astype(o_ref.dtype)


def _att_sample(q, k_new, v_new, cache_k, cache_v, n_batch, t_new):
    buf = cache_k.shape[1]
    tc, tn = _atts_tables(buf, t_new)
    rows = pl.BlockSpec((t_new, D_HALF), lambda b: (b, 0))
    cache = pl.BlockSpec((None, buf, D_HALF), lambda b: (b, 0, 0))
    return pl.pallas_call(
        functools.partial(_atts_kernel, t_new=t_new),
        grid=(n_batch,),
        in_specs=[rows, cache, rows, cache, rows,
                  pl.BlockSpec((N_HEADS * t_new, buf), lambda b: (0, 0)),
                  pl.BlockSpec((N_HEADS * t_new, LANES), lambda b: (0, 0))],
        out_specs=rows,
        out_shape=jax.ShapeDtypeStruct((n_batch * t_new, D_HALF), BF16),
        compiler_params=_params(1, VMEM_LIMIT),
        name="att_sample",
    )(q, cache_k, k_new, cache_v, v_new, tc, tn)


def _out_kernel(*refs, n_parts):
    ret_ref = refs[0]
    part_refs = refs[1:1 + 2 * n_parts] if n_parts else refs[1:2]
    (x_ref, gt_ref, sc_ref, sh_ref, w_ref, gpost_ref, gpre_ref, wr_ref, br_ref,
     x1_ref, h_ref, top_ref, xin) = refs[len(part_refs) + 1:]
    tm = x_ref.shape[0]
    xin[:, :D_HALF] = ret_ref[...]
    if n_parts:
        o_refs, ml_refs = part_refs[:n_parts], part_refs[n_parts:]
        for h in range(N_HEADS):
            ms = [r[:, h:h + 1] for r in ml_refs]
            ls = [r[:, N_HEADS + h:N_HEADS + h + 1] for r in ml_refs]
            mx = functools.reduce(jnp.maximum, ms)
            ws = [jnp.exp(m - mx) for m in ms]
            den = sum(l * w for l, w in zip(ls, ws))
            hs = slice(h * HEAD_DIM, (h + 1) * HEAD_DIM)
            att = sum(r[:, hs] * (w / den) for r, w in zip(o_refs, ws))
            xin[:, D_HALF + h * HEAD_DIM:D_HALF + (h + 1) * HEAD_DIM] = att.astype(BF16)
    else:
        xin[:, D_HALF:] = part_refs[0][...]
    mix = _dot(xin[...], w_ref[...])
    x1 = x_ref[...] + gt_ref[...] * _rms(mix, gpost_ref[...])
    x1_ref[...] = x1
    h = _rms(x1, gpre_ref[...]) * (1.0 + sc_ref[...]) + sh_ref[...]
    _slab_store(h_ref, h)
    logits = jnp.dot(h, wr_ref[...], preferred_element_type=F32,
                     precision=lax.Precision.HIGHEST) + br_ref[...]
    lane_e = lax.broadcasted_iota(jnp.int32, (tm, N_EXPERTS), 1).astype(F32)
    lane_o = lax.broadcasted_iota(jnp.int32, (tm, LANES), 1)
    vals, idxs = [], []
    work = logits
    for _ in range(TOP_K):
        v = jnp.max(work, axis=-1, keepdims=True)
        i = jnp.min(jnp.where(work == v, lane_e, float(N_EXPERTS)), axis=-1, keepdims=True)
        vals.append(v)
        idxs.append(i)
        work = jnp.where(lane_e == i, -jnp.inf, work)
    es = [jnp.exp(v - vals[0]) for v in vals]
    tot = sum(es)
    packed = jnp.zeros((tm, LANES), F32)
    for j in range(TOP_K):
        packed = jnp.where(lane_o == j, idxs[j], packed)
        packed = jnp.where(lane_o == TOP_K + j, es[j] / tot, packed)
    top_ref[...] = packed


def _mixer_out(ret_y, parts, x, mod, w_out_bf, g_post, g_pre, w_router, b_router,
               per_row, rows_per_batch):
    n = x.shape[0]
    tm = min(TM_OUT, n)
    tpb = 1 if per_row else rows_per_batch // tm
    n_parts = len(parts[0]) if isinstance(parts, tuple) else 0
    half = pl.BlockSpec((tm, D_HALF), lambda i: (i, 0))
    full = pl.BlockSpec((tm, D_MODEL), lambda i: (i, 0))
    stat = pl.BlockSpec((tm, LANES), lambda i: (i, 0))
    vec = pl.BlockSpec((1, D_MODEL), lambda i: (0, 0))
    mods = [_mod_spec(tm, c, per_row, tpb) for c in (MOD_GATE_MIX, MOD_SCALE_FFN, MOD_SHIFT_FFN)]
    if n_parts:
        part_args = list(parts[0]) + list(parts[1])
        part_specs = [half] * n_parts + [stat] * n_parts
    else:
        part_args, part_specs = [parts], [half]
    return pl.pallas_call(
        functools.partial(_out_kernel, n_parts=n_parts),
        grid=(n // tm,),
        in_specs=[half] + part_specs + [full] + mods + [
                  pl.BlockSpec((D_MODEL, D_MODEL), lambda i: (0, 0)), vec, vec,
                  pl.BlockSpec((D_MODEL, N_EXPERTS), lambda i: (0, 0)),
                  pl.BlockSpec((1, N_EXPERTS), lambda i: (0, 0))],
        out_specs=[full, pl.BlockSpec((tm * SLAB, LANES), lambda i: (i, 0)), stat],
        out_shape=[jax.ShapeDtypeStruct((n, D_MODEL), F32),
                   jax.ShapeDtypeStruct((n * SLAB, LANES), F32),
                   jax.ShapeDtypeStruct((n, LANES), F32)],
        scratch_shapes=[pltpu.VMEM((tm, D_MODEL), BF16)],
        compiler_params=_params(1, VMEM_LIMIT),
        name="mixer_out",
    )(ret_y, *part_args, x, mod, mod, mod, w_out_bf, g_post, g_pre, w_router, b_router)


def _route_kernel(top_ref, dest_ref, ends_ref, running, starts, *, tm, tm_blk):
    ph, i = pl.program_id(0), pl.program_id(1)
    lane_f = lax.broadcasted_iota(jnp.int32, (tm, LANES), 1).astype(F32)
    top = top_ref[...]
    picks = [lane_f == top[:, k:k + 1] for k in range(TOP_K)]
    chosen = sum(p.astype(F32) for p in picks)
    tile_counts = jnp.sum(chosen, axis=0, keepdims=True)

    @pl.when((ph == 0) & (i == 0))
    def _():
        running[...] = jnp.zeros_like(running)

    @pl.when(ph == 0)
    def _():
        running[...] += tile_counts

    @pl.when((ph == 1) & (i == 0))
    def _():
        lane = lax.broadcasted_iota(jnp.int32, (1, LANES), 1)
        assert tm_blk & (tm_blk - 1) == 0
        blocks = (running[...].astype(jnp.int32) + (tm_blk - 1)) >> (tm_blk.bit_length() - 1)
        ends = blocks
        shift = 1
        while shift < N_EXPERTS:
            ends = ends + jnp.where(lane >= shift, pltpu.roll(ends, shift, axis=1), 0)
            shift *= 2
        ends_ref[...] = ends
        starts[...] = ((ends - blocks) * tm_blk).astype(F32)
        running[...] = jnp.zeros_like(running)

    @pl.when(ph == 1)
    def _():
        earlier = (lax.broadcasted_iota(jnp.int32, (tm, tm), 0)
                   > lax.broadcasted_iota(jnp.int32, (tm, tm), 1))
        ahead = _dot(jnp.where(earlier, 1.0, 0.0).astype(BF16), chosen.astype(BF16))
        slot = starts[...] + running[...] + ahead
        lane_o = lax.broadcasted_iota(jnp.int32, (tm, LANES), 1)
        packed = jnp.zeros((tm, LANES), F32)
        for k, pick in enumerate(picks):
            dest_k = jnp.sum(jnp.where(pick, slot, 0.0), axis=-1, keepdims=True)
            packed = jnp.where(lane_o == k, dest_k, packed)
        dest_ref[...] = packed.astype(jnp.int32)
        running[...] += tile_counts


def _route(top, tm_blk):
    n = top.shape[0]
    tm = 512
    return pl.pallas_call(
        functools.partial(_route_kernel, tm=tm, tm_blk=tm_blk),
        grid=(2, n // tm),
        in_specs=[pl.BlockSpec((tm, LANES), lambda ph, i: (i, 0))],
        out_specs=[pl.BlockSpec((tm, LANES), lambda ph, i: (i * ph, 0)),
                   pl.BlockSpec((1, LANES), lambda ph, i: (0, 0))],
        out_shape=[jax.ShapeDtypeStruct((n, LANES), jnp.int32),
                   jax.ShapeDtypeStruct((1, LANES), jnp.int32)],
        scratch_shapes=[pltpu.VMEM((1, LANES), F32), pltpu.VMEM((1, LANES), F32)],
        compiler_params=_params(2),
        name="route",
    )(top)


def _dispatch_kernel(ends_ref, dest_ref, h_ref, xs_hbm, zeros, sem, *, tm, tm_blk, n_blocks):
    i = pl.program_id(0)

    def fill(b):
        rows = pl.ds(pl.multiple_of(b * (tm_blk * SLAB), tm_blk * SLAB), tm_blk * SLAB)
        return pltpu.make_async_copy(zeros, xs_hbm.at[rows], sem.at[1])

    def fill_start(b, carry):
        fill(b).start()
        return carry

    def fill_wait(b, carry):
        fill(b).wait()
        return carry

    @pl.when(i == 0)
    def _():
        zeros[...] = jnp.zeros_like(zeros)
        n_used = ends_ref[N_EXPERTS - 1]
        for do in (fill_start, fill_wait):
            for e in range(N_EXPERTS):
                first = ends_ref[e - 1] if e else 0

                @pl.when(ends_ref[e] > first)
                def _():
                    do(ends_ref[e] - 1, 0)

            lax.fori_loop(n_used, n_blocks, do, 0)

    def body(j, carry):
        for k in range(TOP_K):
            pltpu.make_async_copy(h_ref.at[_slab_rows(j)], xs_hbm.at[_slab_rows(dest_ref[j * TOP_K + k])],
                                  sem.at[0]).start(priority=k % 2)
        return carry

    lax.fori_loop(0, tm, body, 0)
    for k in range(TOP_K):
        pltpu.make_async_copy(h_ref, xs_hbm.at[pl.ds(0, tm * SLAB)], sem.at[0]).wait()


def _dispatch(h, dest_flat, ends, n_blocks, tm_blk):
    n = h.shape[0] // SLAB
    tm = 256
    grid_spec = pltpu.PrefetchScalarGridSpec(
        num_scalar_prefetch=1,
        grid=(n // tm,),
        in_specs=[pl.BlockSpec((tm * TOP_K,), lambda i, ends: (i,), memory_space=pltpu.SMEM),
                  pl.BlockSpec((tm * SLAB, LANES), lambda i, ends: (i, 0))],
        out_specs=pl.BlockSpec(memory_space=pl.ANY),
        scratch_shapes=[pltpu.VMEM((tm_blk * SLAB, LANES), F32), pltpu.SemaphoreType.DMA((2,))],
    )
    return pl.pallas_call(
        functools.partial(_dispatch_kernel, tm=tm, tm_blk=tm_blk, n_blocks=n_blocks),
        grid_spec=grid_spec,
        out_shape=jax.ShapeDtypeStruct((n_blocks * tm_blk * SLAB, LANES), F32),
        compiler_params=_params(1),
        name="dispatch",
    )(ends, dest_flat, h)


def _moe_kernel(be_ref, nu_ref, x_ref, wgu_ref, wd_ref, bgu_ref, bd_ref, o_ref, wgu_bf, wd_rows, wd_bf):
    i = pl.program_id(0)
    n_used = nu_ref[0]
    d_gu = wgu_ref.shape[1]
    new_expert = (i == 0) | (be_ref[i] != be_ref[jnp.maximum(i - 1, 0)])

    @pl.when((i < n_used) & new_expert)
    def _():
        wgu_bf[...] = wgu_ref[...].astype(BF16)
        for c in range(wd_rows.shape[0]):
            cols = slice(c * LANES, (c + 1) * LANES)
            for r in range(2):
                wd_rows[c, pl.ds(r, d_gu // 2, stride=2), :] = wd_ref[:, cols]
            wd_bf[:, cols] = wd_rows[c].astype(BF16)

    @pl.when(i < n_used)
    def _():
        x = _slab_load(x_ref, x_ref.shape[0] // SLAB).astype(BF16)
        even = (lax.broadcasted_iota(jnp.int32, (x.shape[0], MOE_COLS), 1) & 1) == 0
        acc = bd_ref[...]
        for c in range(d_gu // MOE_COLS):
            cols = slice(c * MOE_COLS, (c + 1) * MOE_COLS)
            gu = _dot(x, wgu_bf[:, cols]) + bgu_ref[:, cols]
            nxt = pltpu.roll(gu, MOE_COLS - 1, axis=1)
            gate = jnp.minimum(gu, SWIGLU_LIMIT)
            up = jnp.clip(nxt, -SWIGLU_LIMIT, SWIGLU_LIMIT)
            act = jnp.where(even, gate * jax.nn.sigmoid(SWIGLU_ALPHA * gate) * (up + 1.0), 0.0)
            acc = acc + _dot(act.astype(BF16), wd_bf[cols, :])
        _slab_store(o_ref, acc)

    @pl.when(i >= n_used)
    def _():
        o_ref[...] = jnp.zeros_like(o_ref)


def _moe_blocks(xs, block_e, n_used, w_gu, w_down, b_gu, b_down, tm):
    n_blocks = block_e.shape[0]
    n_e, d, d_gu = w_gu.shape
    wspec = lambda k, n: pl.BlockSpec((None, k, n), lambda i, be, nu: (be[i], 0, 0))
    grid_spec = pltpu.PrefetchScalarGridSpec(
        num_scalar_prefetch=2,
        grid=(n_blocks,),
        in_specs=[pl.BlockSpec((tm * SLAB, LANES), lambda i, be, nu: (jnp.minimum(i, nu[0] - 1), 0)),
                  wspec(d, d_gu), wspec(d_gu // 2, d), wspec(1, d_gu), wspec(1, d)],
        out_specs=pl.BlockSpec((tm * SLAB, LANES), lambda i, be, nu: (i, 0)),
        scratch_shapes=[pltpu.VMEM((d, d_gu), BF16), pltpu.VMEM((d // LANES, d_gu, LANES), F32),
                        pltpu.VMEM((d_gu, d), BF16)],
    )
    return pl.pallas_call(
        _moe_kernel,
        grid_spec=grid_spec,
        out_shape=jax.ShapeDtypeStruct((n_blocks * tm * SLAB, LANES), F32),
        compiler_params=_params(1, VMEM_LIMIT),
        name="moe_blocks",
    )(block_e, n_used, xs, w_gu, w_down, b_gu.reshape(n_e, 1, d_gu), b_down.reshape(n_e, 1, d))


def _fin_kernel(dest_ref, rows_hbm, top_ref, x1_ref, gt_ref, g_ref, o_ref, buf, sem, *, tm):
    def body(j, carry):
        for k in range(TOP_K):
            pltpu.make_async_copy(rows_hbm.at[_slab_rows(dest_ref[j * TOP_K + k])], buf.at[k, _slab_rows(j)],
                                  sem.at[0]).start(priority=k % 2)
        return carry
    lax.fori_loop(0, tm, body, 0)
    for k in range(TOP_K):
        pltpu.make_async_copy(rows_hbm.at[pl.ds(0, tm * SLAB)], buf.at[k], sem.at[0]).wait()
    top = top_ref[...]
    f = sum(_slab_load(buf.at[k], tm) * top[:, TOP_K + k:TOP_K + k + 1] for k in range(TOP_K))
    o_ref[...] = x1_ref[...] + gt_ref[...] * _rms(f, g_ref[...])


def _finish(dest, rows, top, x1, mod, g_post, per_row, rows_per_batch):
    n = x1.shape[0]
    tm = min(TM_FIN, n)
    full = pl.BlockSpec((tm, D_MODEL), lambda i: (i, 0))
    gate = _mod_spec(tm, MOD_GATE_FFN, per_row, 1 if per_row else rows_per_batch // tm)
    return pl.pallas_call(
        functools.partial(_fin_kernel, tm=tm),
        grid=(n // tm,),
        in_specs=[pl.BlockSpec((tm * TOP_K,), lambda i: (i,), memory_space=pltpu.SMEM),
                  pl.BlockSpec(memory_space=pl.ANY),
                  pl.BlockSpec((tm, LANES), lambda i: (i, 0)), full, gate,
                  pl.BlockSpec((1, D_MODEL), lambda i: (0, 0))],
        out_specs=full,
        out_shape=jax.ShapeDtypeStruct((n, D_MODEL), F32),
        scratch_shapes=[pltpu.VMEM((TOP_K, tm * SLAB, LANES), F32), pltpu.SemaphoreType.DMA((1,))],
        compiler_params=_params(1),
        name="finish",
    )(dest, rows, top, x1, mod, g_post)


def kernel(x_prompt, x_sample, state_ret, cache_win_k, cache_win_v, c_prompt, c_sample, w_ada, b_ada,
           g_pre_mix, g_post_mix, g_pre_ffn, g_post_ffn, w_in, g_ret, w_out, w_router, b_router,
           w_gate_up, b_gate_up, w_down, b_down):
    depth = w_in.shape[0]
    assert depth == 1, "single-layer step"
    n_b, seq, d = x_prompt.shape
    n_db, t_new, _ = x_sample.shape
    n_p, n_s = n_b * seq, n_db * t_new
    buf = cache_win_k.shape[2]
    keep = min(DIL_PATTERNS[-1][0], seq)
    vec = lambda g: g[0].reshape(1, -1)

    mod_p, mod_s = _ada(c_prompt, c_sample, w_ada[0], b_ada[0], t_new)
    mod_p = mod_p[:n_b].reshape(n_b, 1, -1)
    mod_s = mod_s.reshape(n_s, -1)

    w_in_bf = w_in[0].astype(BF16)
    w_out_bf = w_out[0].astype(BF16)
    xp = x_prompt.reshape(n_p, d)
    xs = x_sample.reshape(n_s, d)
    router = (w_router[0], b_router[0].reshape(1, -1))

    qr, kr, vr, gr, qa, ka, va = _in_proj(xp, mod_p, vec(g_pre_mix), w_in_bf, False, seq)
    ret_y_p, ret_state_p = _retention(qr, kr, vr, gr, None, g_ret[0], n_b, seq, RET_CHUNK)
    parts = [_att_prompt_pattern(qa, ka, va, n_b, seq, dil) for _, dil in DIL_PATTERNS]
    x1_p, h_p, top_p = _mixer_out(ret_y_p, ([o for o, _ in parts], [ml for _, ml in parts]), xp, mod_p,
                                  w_out_bf, vec(g_post_mix), vec(g_pre_ffn), *router, False, seq)

    qr_s, kr_s, vr_s, gr_s, qa_s, ka_s, va_s = _in_proj(xs, mod_s, vec(g_pre_mix), w_in_bf, True, t_new)
    ret_y_s, ret_state_s = _retention(qr_s, kr_s, vr_s, gr_s, state_ret[0], g_ret[0], n_db, t_new, t_new)
    att_s = _att_sample(qa_s, ka_s, va_s, cache_win_k[0].reshape(n_db, buf, D_HALF),
                        cache_win_v[0].reshape(n_db, buf, D_HALF), n_db, t_new)
    x1_s, h_s, top_s = _mixer_out(ret_y_s, att_s, xs, mod_s, w_out_bf, vec(g_post_mix), vec(g_pre_ffn),
                                  *router, True, t_new)

    h_all = jnp.concatenate([h_p, h_s], axis=0)
    n_blocks = (n_p + n_s) * TOP_K // TM_MOE + N_EXPERTS
    dest, ends = _route(jnp.concatenate([top_p, top_s], axis=0), TM_MOE)
    dest = dest[:, :TOP_K].reshape(-1)
    ends = ends[0, :N_EXPERTS]
    block_e = jnp.minimum(jnp.sum(jnp.arange(n_blocks, dtype=jnp.int32)[:, None] >= ends[None, :], axis=1),
                          N_EXPERTS - 1).astype(jnp.int32)
    rows_in = _dispatch(h_all, dest, ends, n_blocks, TM_MOE)
    rows = _moe_blocks(rows_in, block_e, ends[N_EXPERTS - 1:], w_gate_up[0], w_down[0],
                       b_gate_up[0], b_down[0], TM_MOE)
    y_p = _finish(dest[:n_p * TOP_K], rows, top_p, x1_p, mod_p, vec(g_post_ffn), False, seq)
    y_s = _finish(dest[n_p * TOP_K:], rows, top_s, x1_s, mod_s, vec(g_post_ffn), True, t_new)

    win = lambda t: t.reshape(n_b, seq, N_HEADS, HEAD_DIM)[:, seq - keep:][None]
    new = lambda t: t.reshape(n_db, t_new, N_HEADS, HEAD_DIM)[None]
    return (y_p.reshape(n_b, seq, d), y_s.reshape(n_db, t_new, d), ret_state_p[None], ret_state_s[None],
            win(ka), win(va), new(ka_s), new(va_s))
```

```python
import functools

import numpy as np
import jax
import jax.numpy as jnp
from jax import lax
from jax.experimental import pallas as pl
from jax.experimental.pallas import tpu as pltpu

F32 = jnp.float32
BF16 = jnp.bfloat16

D_MODEL = 1024
HEAD_DIM = 64
N_HEADS = 8
D_HALF = N_HEADS * HEAD_DIM
N_SEG = 7
LANES = 128
N_PAIRS = D_HALF // LANES
RET_CHUNK = 128
ATT_BLOCK = 128
DIL_PATTERNS = ((128, 1), (512, 4), (2048, 16))
N_EXPERTS = 32
TOP_K = 4
SWIGLU_LIMIT = 7.0
SWIGLU_ALPHA = 1.702
EPS = 1e-6
NEG_INF = -1e30
MOD_SHIFT_MIX, MOD_SCALE_MIX, MOD_GATE_MIX, MOD_SHIFT_FFN, MOD_SCALE_FFN, MOD_GATE_FFN = range(6)

TM_IN = 512
TM_OUT = 256
TM_MOE = 256
MOE_COLS = 512
TM_FIN = 128
VMEM_LIMIT = 56 * 1024 * 1024


def _params(n_axes, vmem=None):
    return pltpu.CompilerParams(dimension_semantics=("arbitrary",) * n_axes, vmem_limit_bytes=vmem)


def _ret_log_decay():
    return np.log(1.0 - 2.0 ** (-5.0 - np.arange(N_HEADS, dtype=np.float64)))


def _alibi_slopes():
    return 2.0 ** (-8.0 * (np.arange(N_HEADS, dtype=np.float64) + 1.0) / N_HEADS)


def _per_head_lanes(a):
    return np.repeat(a, HEAD_DIM, axis=-1)


def _rms(x, g):
    return x * lax.rsqrt(jnp.mean(x * x, axis=-1, keepdims=True) + EPS) * g


def _dot(a, b):
    return jnp.dot(a, b, preferred_element_type=F32)


def _dot_nt(a, b):
    return lax.dot_general(a, b, (((1,), (1,)), ((), ())), preferred_element_type=F32)


def _dot_tn(a, b):
    return lax.dot_general(a, b, (((0,), (0,)), ((), ())), preferred_element_type=F32)


SLAB = D_MODEL // LANES


def _slab_rows(r):
    return pl.ds(pl.multiple_of(r * SLAB, SLAB), SLAB)


def _slab_load(ref, n):
    return jnp.concatenate([ref[pl.ds(s, n, stride=SLAB), :] for s in range(SLAB)], axis=1)


def _slab_store(ref, value):
    n = value.shape[0]
    for s in range(SLAB):
        ref[pl.ds(s, n, stride=SLAB), :] = value[:, s * LANES:(s + 1) * LANES]


def _mod_spec(tm, col, per_row, tiles_per_batch):
    if per_row:
        return pl.BlockSpec((tm, D_MODEL), lambda i: (i, col))
    return pl.BlockSpec((None, 1, D_MODEL), lambda i: (i // tiles_per_batch, 0, col))


def _ada_kernel(cp_ref, cs_ref, w_ref, b_ref, op_ref, os_ref):
    w = w_ref[...].astype(BF16)

    def mod(c):
        return _dot((c * jax.nn.sigmoid(c)).astype(BF16), w) + b_ref[...]

    op_ref[...] = mod(cp_ref[...])
    ms = mod(cs_ref[...])
    os_ref[...] = jnp.broadcast_to(ms[:, None, :], os_ref.shape)


def _ada(c_prompt, c_sample, w_ada, b_ada, t_new):
    n_b, d = c_prompt.shape
    n_db = c_sample.shape[0]
    n_out = w_ada.shape[1]
    cp = jnp.concatenate([c_prompt, jnp.zeros((-n_b % 8, d), F32)], axis=0)
    return pl.pallas_call(
        _ada_kernel,
        grid=(n_out // d,),
        in_specs=[pl.BlockSpec(cp.shape, lambda j: (0, 0)),
                  pl.BlockSpec((n_db, d), lambda j: (0, 0)),
                  pl.BlockSpec((d, d), lambda j: (0, j)),
                  pl.BlockSpec((1, d), lambda j: (0, j))],
        out_specs=[pl.BlockSpec(cp.shape, lambda j: (0, j)),
                   pl.BlockSpec((n_db, t_new, d), lambda j: (0, 0, j))],
        out_shape=[jax.ShapeDtypeStruct((cp.shape[0], n_out), F32),
                   jax.ShapeDtypeStruct((n_db, t_new, n_out), F32)],
        compiler_params=_params(1, VMEM_LIMIT),
        name="ada",
    )(cp, c_sample, w_ada, b_ada.reshape(1, n_out))


def _in_kernel(x_ref, sc_ref, sh_ref, g_ref, w_ref, *o_refs):
    h = (_rms(x_ref[...], g_ref[...]) * (1.0 + sc_ref[...]) + sh_ref[...]).astype(BF16)
    for s, o_ref in enumerate(o_refs):
        o_ref[...] = _dot(h, w_ref[:, s * D_HALF:(s + 1) * D_HALF])


def _in_proj(x, mod, g, w_bf, per_row, rows_per_batch):
    n = x.shape[0]
    tm = min(TM_IN, n)
    tpb = 1 if per_row else rows_per_batch // tm
    seg = pl.BlockSpec((tm, D_HALF), lambda i: (i, 0))
    return pl.pallas_call(
        _in_kernel,
        grid=(n // tm,),
        in_specs=[pl.BlockSpec((tm, D_MODEL), lambda i: (i, 0)),
                  _mod_spec(tm, MOD_SCALE_MIX, per_row, tpb), _mod_spec(tm, MOD_SHIFT_MIX, per_row, tpb),
                  pl.BlockSpec((1, D_MODEL), lambda i: (0, 0)),
                  pl.BlockSpec((D_MODEL, N_SEG * D_HALF), lambda i: (0, 0))],
        out_specs=[seg] * N_SEG,
        out_shape=[jax.ShapeDtypeStruct((n, D_HALF), F32)] * N_SEG,
        compiler_params=_params(1, VMEM_LIMIT),
        name="in_proj",
    )(x, mod, mod, g, w_bf)


def _ret_tables(chunk):
    lg = _ret_log_decay()
    pos = np.arange(chunk, dtype=np.float64)
    diff = pos[:, None] - pos[None, :]
    dmask = np.where(diff >= 0, np.exp(np.maximum(diff, 0.0)[None] * lg[:, None, None]), 0.0)
    row_decay = _per_head_lanes(np.exp((pos[:, None] + 1.0) * lg[None, :]))
    key_decay = _per_head_lanes(np.exp((chunk - 1.0 - pos)[:, None] * lg[None, :]))
    state_decay = _per_head_lanes(np.exp(chunk * lg)[None, :])
    return tuple(jnp.asarray(t, F32) for t in (dmask, row_decay, key_decay, state_decay))


def _ret_kernel(q_ref, k_ref, v_ref, gate_ref, s0_ref, gret_ref, dm_ref, rd_ref, kd_ref, sd_ref,
                y_ref, st_ref, state, *, chunk, has_init):
    c = pl.program_id(1)
    lo = lax.broadcasted_iota(jnp.int32, (chunk, LANES), 1) < HEAD_DIM
    blk_r = lax.broadcasted_iota(jnp.int32, (LANES, LANES), 0) < HEAD_DIM
    blk_c = lax.broadcasted_iota(jnp.int32, (LANES, LANES), 1) < HEAD_DIM
    same_head = blk_r == blk_c

    @pl.when(c == 0)
    def _():
        if has_init:
            zero = jnp.zeros((HEAD_DIM, HEAD_DIM), F32)
            for p in range(N_PAIRS):
                top = jnp.concatenate([s0_ref[0, 2 * p], zero], axis=1)
                bot = jnp.concatenate([zero, s0_ref[0, 2 * p + 1]], axis=1)
                state[p] = jnp.concatenate([top, bot], axis=0)
        else:
            state[...] = jnp.zeros_like(state)

    for p in range(N_PAIRS):
        sl = slice(p * LANES, (p + 1) * LANES)
        q = q_ref[:, sl]
        k = k_ref[:, sl] * HEAD_DIM ** -0.5
        vb = v_ref[:, sl].astype(BF16)
        kb = k.astype(BF16)
        inner = []
        for half in range(2):
            qh = jnp.where(lo if half == 0 else ~lo, q, 0.0).astype(BF16)
            scores = _dot_nt(qh, kb) * dm_ref[2 * p + half]
            inner.append(_dot(scores.astype(BF16), vb))
        s_old = state[p]
        cross = _dot(q.astype(BF16), s_old.astype(BF16)) * rd_ref[:, sl]
        o = jnp.where(lo, inner[0], inner[1]) + cross
        upd = _dot_tn((k * kd_ref[:, sl]).astype(BF16), vb)
        s_new = sd_ref[:, sl] * s_old + jnp.where(same_head, upd, 0.0)
        state[p] = s_new

        def head_mean(t):
            m_lo = jnp.sum(jnp.where(lo, t, 0.0), axis=-1, keepdims=True)
            m_hi = jnp.sum(jnp.where(lo, 0.0, t), axis=-1, keepdims=True)
            return jnp.where(lo, m_lo, m_hi) * (1.0 / HEAD_DIM)

        d = o - head_mean(o)
        n = d * lax.rsqrt(head_mean(d * d) + EPS) * gret_ref[:, sl]
        g = gate_ref[:, sl]
        y_ref[:, sl] = (g * jax.nn.sigmoid(g) * n).astype(y_ref.dtype)

        @pl.when(c == pl.num_programs(1) - 1)
        def _():
            st_ref[0, 2 * p] = s_new[:HEAD_DIM, :HEAD_DIM]
            st_ref[0, 2 * p + 1] = s_new[HEAD_DIM:, HEAD_DIM:]


def _retention(q, k, v, gate, state0, g_ret, n_batch, seq, chunk):
    nc = seq // chunk
    has_init = state0 is not None
    if state0 is None:
        state0 = jnp.zeros((1, N_HEADS, HEAD_DIM, HEAD_DIM), F32)
        s0_map = lambda b, c: (0, 0, 0, 0)
    else:
        s0_map = lambda b, c: (b, 0, 0, 0)
    dmask, row_decay, key_decay, state_decay = _ret_tables(chunk)
    rows = pl.BlockSpec((chunk, D_HALF), lambda b, c: (b * nc + c, 0))
    const2 = lambda shape: pl.BlockSpec(shape, lambda b, c: (0, 0))
    st_block = pl.BlockSpec((1, N_HEADS, HEAD_DIM, HEAD_DIM), lambda b, c: (b, 0, 0, 0))
    return pl.pallas_call(
        functools.partial(_ret_kernel, chunk=chunk, has_init=has_init),
        grid=(n_batch, nc),
        in_specs=[rows, rows, rows, rows,
                  pl.BlockSpec((1, N_HEADS, HEAD_DIM, HEAD_DIM), s0_map),
                  const2((1, D_HALF)),
                  pl.BlockSpec((N_HEADS, chunk, chunk), lambda b, c: (0, 0, 0)),
                  const2((chunk, D_HALF)), const2((chunk, D_HALF)), const2((1, D_HALF))],
        out_specs=[rows, st_block],
        out_shape=[jax.ShapeDtypeStruct((n_batch * seq, D_HALF), BF16),
                   jax.ShapeDtypeStruct((n_batch, N_HEADS, HEAD_DIM, HEAD_DIM), F32)],
        scratch_shapes=[pltpu.VMEM((N_PAIRS, LANES, LANES), F32)],
        compiler_params=_params(2),
        name="retention",
    )(q, k, v, gate, state0, g_ret.reshape(1, D_HALF), dmask, row_decay, key_decay, state_decay)


def _attp_table(dilation):
    qi = np.arange(ATT_BLOCK)[:, None]
    kj = np.arange(2 * ATT_BLOCK)[None, :]
    step = qi + ATT_BLOCK - kj
    in_win = (step >= 0) & (step <= ATT_BLOCK)
    bias = -_alibi_slopes()[:, None, None] * (step * dilation).astype(np.float64)[None]
    return jnp.asarray(np.where(in_win[None], bias, NEG_INF), F32)


def _attp_kernel(q_ref, kp_ref, kc_ref, vp_ref, vc_ref, tb_ref, o_ref, ml_ref):
    has_prev = pl.program_id(2) > 0
    lo = lax.broadcasted_iota(jnp.int32, (ATT_BLOCK, LANES), 1) < HEAD_DIM
    lane = lax.broadcasted_iota(jnp.int32, (ATT_BLOCK, LANES), 1)
    key_ok = (lax.broadcasted_iota(jnp.int32, (ATT_BLOCK, 2 * ATT_BLOCK), 1) >= ATT_BLOCK) | has_prev
    ml = jnp.zeros((ATT_BLOCK, LANES), F32)
    for p in range(N_PAIRS):
        sl = slice(p * LANES, (p + 1) * LANES)
        q = q_ref[:, sl]
        kb = jnp.concatenate([kp_ref[:, sl], kc_ref[:, sl]], axis=0).astype(BF16)
        vb = jnp.concatenate([vp_ref[:, sl], vc_ref[:, sl]], axis=0).astype(BF16)
        outs = []
        for half in range(2):
            h = 2 * p + half
            qh = jnp.where(lo if half == 0 else ~lo, q, 0.0).astype(BF16)
            s = _dot_nt(qh, kb) * HEAD_DIM ** -0.5
            s = jnp.where(key_ok, s + tb_ref[h], NEG_INF)
            m = jnp.max(s, axis=-1, keepdims=True)
            e = jnp.exp(s - m)
            l = jnp.sum(e, axis=-1, keepdims=True)
            outs.append(_dot(e.astype(BF16), vb))
            ml = jnp.where(lane == h, m, ml)
            ml = jnp.where(lane == N_HEADS + h, l, ml)
        o_ref[:, sl] = jnp.where(lo, outs[0], outs[1])
    ml_ref[...] = ml


def _att_prompt_pattern(q, k, v, n_batch, seq, dilation):
    sub = seq // dilation
    nb = sub // ATT_BLOCK
    view = lambda t: t.reshape(n_batch, sub, dilation * D_HALF)
    cur = pl.BlockSpec((None, ATT_BLOCK, D_HALF), lambda b, r, n: (b, n, r))
    prev = pl.BlockSpec((None, ATT_BLOCK, D_HALF), lambda b, r, n: (b, jnp.maximum(n - 1, 0), r))
    o, ml = pl.pallas_call(
        _attp_kernel,
        grid=(n_batch, dilation, nb),
        in_specs=[cur, prev, cur, prev, cur,
                  pl.BlockSpec((N_HEADS, ATT_BLOCK, 2 * ATT_BLOCK), lambda b, r, n: (0, 0, 0))],
        out_specs=[cur, pl.BlockSpec((None, ATT_BLOCK, LANES), lambda b, r, n: (b, n, r))],
        out_shape=[jax.ShapeDtypeStruct((n_batch, sub, dilation * D_HALF), F32),
                   jax.ShapeDtypeStruct((n_batch, sub, dilation * LANES), F32)],
        compiler_params=_params(3),
        name=f"att_prompt_d{dilation}",
    )(view(q), view(k), view(k), view(v), view(v), _attp_table(dilation))
    return o.reshape(n_batch * seq, D_HALF), ml.reshape(n_batch * seq, LANES)


def _atts_layout(buf, t_new):
    group = DIL_PATTERNS[-1][1]
    far_end = buf - DIL_PATTERNS[-2][0]
    assert buf % group == 0 and far_end % group == 0 and far_end % (buf - far_end) == 0 and t_new <= group
    pos_q = buf + np.arange(t_new)
    pos_k = np.arange(buf + t_new)
    dist = pos_q[:, None] - pos_k[None, :]
    count = np.zeros(dist.shape)
    for window, dilation in DIL_PATTERNS:
        count += (dist >= 0) & (dist <= window) & (dist % dilation == 0)
    far_pos = (np.arange(far_end // group)[:, None] * group + np.arange(t_new)[None, :]).reshape(-1)
    loaded = np.concatenate([far_pos, np.arange(far_end, buf + t_new)])
    assert not count[:, np.setdiff1d(pos_k, loaded)].any(), "a reachable key is not loaded"

    logit = np.where(count > 0, np.log(np.maximum(count, 1.0)), NEG_INF)[None] - (
        _alibi_slopes()[:, None, None] * dist[None].astype(np.float64))
    own_head = np.eye(N_HEADS, dtype=bool)

    def table(positions, head_major):
        t = logit[:, :, positions]
        if head_major:
            full = np.where(own_head[:, None, :, None], t[:, :, None, :], NEG_INF)
        else:
            full = np.where(own_head[:, None, None, :], t[:, :, :, None], NEG_INF)
        return jnp.asarray(np.maximum(full, NEG_INF).reshape(N_HEADS * t_new, -1), F32)

    tables = (table(far_pos, False), table(np.arange(far_end, buf), False),
              table(np.arange(buf, buf + t_new), True))
    return group, far_end, tables


def _atts_kernel(q_ref, kf_ref, kn_ref, kw_ref, vf_ref, vn_ref, vw_ref, tf_ref, tn_ref, tw_ref, o_ref,
                 *, t_new):
    def head_rows(x):
        return jnp.concatenate([x[:, h * HEAD_DIM:(h + 1) * HEAD_DIM] for h in range(N_HEADS)], axis=0)

    q = head_rows(q_ref[...]).astype(BF16)
    keys = (kf_ref[...].reshape(-1, HEAD_DIM), kn_ref[...], head_rows(kw_ref[...]))
    vals = (vf_ref[...].reshape(-1, HEAD_DIM), vn_ref[...], head_rows(vw_ref[...]))
    scale = HEAD_DIM ** -0.5
    s = [_dot_nt(q, k.astype(BF16)) * scale + t[...] for k, t in zip(keys, (tf_ref, tn_ref, tw_ref))]
    m = functools.reduce(jnp.maximum, [jnp.max(x, axis=-1, keepdims=True) for x in s])
    e = [jnp.exp(x - m) for x in s]
    l = sum(jnp.sum(x, axis=-1, keepdims=True) for x in e)
    acc = sum(_dot(x.astype(BF16), v.astype(BF16)) for x, v in zip(e, vals)) / l
    o_ref[...] = jnp.concatenate([acc[h * t_new:(h + 1) * t_new] for h in range(N_HEADS)],
                                 axis=1).astype(o_ref.dtype)


def _att_sample(q, k_new, v_new, cache_k, cache_v, n_batch, t_new):
    buf = cache_k.shape[1]
    group, far_end, tables = _atts_layout(buf, t_new)
    near = buf - far_end
    far_view = lambda c: c.reshape(n_batch, buf // group, group * N_HEADS, HEAD_DIM)
    near_view = lambda c: c.reshape(n_batch * buf * N_HEADS, HEAD_DIM)
    rows = pl.BlockSpec((t_new, D_HALF), lambda b: (b, 0))
    far = pl.BlockSpec((None, far_end // group, t_new * N_HEADS, HEAD_DIM), lambda b: (b, 0, 0, 0))
    near_spec = pl.BlockSpec((near * N_HEADS, HEAD_DIM), lambda b: (b * (buf // near) + far_end // near, 0))
    const = lambda t: pl.BlockSpec(t.shape, lambda b: (0, 0))
    return pl.pallas_call(
        functools.partial(_atts_kernel, t_new=t_new),
        grid=(n_batch,),
        in_specs=[rows, far, near_spec, rows, far, near_spec, rows] + [const(t) for t in tables],
        out_specs=rows,
        out_shape=jax.ShapeDtypeStruct((n_batch * t_new, D_HALF), BF16),
        compiler_params=_params(1, VMEM_LIMIT),
        name="att_sample",
    )(q, far_view(cache_k), near_view(cache_k), k_new, far_view(cache_v), near_view(cache_v), v_new, *tables)


def _out_kernel(*refs, n_parts):
    ret_ref = refs[0]
    part_refs = refs[1:1 + 2 * n_parts] if n_parts else refs[1:2]
    (x_ref, gt_ref, sc_ref, sh_ref, w_ref, gpost_ref, gpre_ref, wr_ref, br_ref,
     x1_ref, h_ref, top_ref, xin) = refs[len(part_refs) + 1:]
    tm = x_ref.shape[0]
    xin[:, :D_HALF] = ret_ref[...]
    if n_parts:
        o_refs, ml_refs = part_refs[:n_parts], part_refs[n_parts:]
        for h in range(N_HEADS):
            ms = [r[:, h:h + 1] for r in ml_refs]
            ls = [r[:, N_HEADS + h:N_HEADS + h + 1] for r in ml_refs]
            mx = functools.reduce(jnp.maximum, ms)
            ws = [jnp.exp(m - mx) for m in ms]
            den = sum(l * w for l, w in zip(ls, ws))
            hs = slice(h * HEAD_DIM, (h + 1) * HEAD_DIM)
            att = sum(r[:, hs] * (w / den) for r, w in zip(o_refs, ws))
            xin[:, D_HALF + h * HEAD_DIM:D_HALF + (h + 1) * HEAD_DIM] = att.astype(BF16)
    else:
        xin[:, D_HALF:] = part_refs[0][...]
    mix = _dot(xin[...], w_ref[...])
    x1 = x_ref[...] + gt_ref[...] * _rms(mix, gpost_ref[...])
    x1_ref[...] = x1
    h = _rms(x1, gpre_ref[...]) * (1.0 + sc_ref[...]) + sh_ref[...]
    _slab_store(h_ref, h)
    logits = jnp.dot(h, wr_ref[...], preferred_element_type=F32,
                     precision=lax.Precision.HIGHEST) + br_ref[...]
    lane_e = lax.broadcasted_iota(jnp.int32, (tm, N_EXPERTS), 1).astype(F32)
    lane_o = lax.broadcasted_iota(jnp.int32, (tm, LANES), 1)
    vals, idxs = [], []
    work = logits
    for _ in range(TOP_K):
        v = jnp.max(work, axis=-1, keepdims=True)
        i = jnp.min(jnp.where(work == v, lane_e, float(N_EXPERTS)), axis=-1, keepdims=True)
        vals.append(v)
        idxs.append(i)
        work = jnp.where(lane_e == i, -jnp.inf, work)
    es = [jnp.exp(v - vals[0]) for v in vals]
    tot = sum(es)
    packed = jnp.zeros((tm, LANES), F32)
    for j in range(TOP_K):
        packed = jnp.where(lane_o == j, idxs[j], packed)
        packed = jnp.where(lane_o == TOP_K + j, es[j] / tot, packed)
    top_ref[...] = packed


def _mixer_out(ret_y, parts, x, mod, w_out_bf, g_post, g_pre, w_router, b_router,
               per_row, rows_per_batch):
    n = x.shape[0]
    tm = min(TM_OUT, n)
    tpb = 1 if per_row else rows_per_batch // tm
    n_parts = len(parts[0]) if isinstance(parts, tuple) else 0
    half = pl.BlockSpec((tm, D_HALF), lambda i: (i, 0))
    full = pl.BlockSpec((tm, D_MODEL), lambda i: (i, 0))
    stat = pl.BlockSpec((tm, LANES), lambda i: (i, 0))
    vec = pl.BlockSpec((1, D_MODEL), lambda i: (0, 0))
    mods = [_mod_spec(tm, c, per_row, tpb) for c in (MOD_GATE_MIX, MOD_SCALE_FFN, MOD_SHIFT_FFN)]
    if n_parts:
        part_args = list(parts[0]) + list(parts[1])
        part_specs = [half] * n_parts + [stat] * n_parts
    else:
        part_args, part_specs = [parts], [half]
    return pl.pallas_call(
        functools.partial(_out_kernel, n_parts=n_parts),
        grid=(n // tm,),
        in_specs=[half] + part_specs + [full] + mods + [
                  pl.BlockSpec((D_MODEL, D_MODEL), lambda i: (0, 0)), vec, vec,
                  pl.BlockSpec((D_MODEL, N_EXPERTS), lambda i: (0, 0)),
                  pl.BlockSpec((1, N_EXPERTS), lambda i: (0, 0))],
        out_specs=[full, pl.BlockSpec((tm * SLAB, LANES), lambda i: (i, 0)), stat],
        out_shape=[jax.ShapeDtypeStruct((n, D_MODEL), F32),
                   jax.ShapeDtypeStruct((n * SLAB, LANES), F32),
                   jax.ShapeDtypeStruct((n, LANES), F32)],
        scratch_shapes=[pltpu.VMEM((tm, D_MODEL), BF16)],
        compiler_params=_params(1, VMEM_LIMIT),
        name="mixer_out",
    )(ret_y, *part_args, x, mod, mod, mod, w_out_bf, g_post, g_pre, w_router, b_router)


def _route_kernel(top_ref, dest_ref, ends_ref, running, starts, *, tm, tm_blk):
    ph, i = pl.program_id(0), pl.program_id(1)
    lane_f = lax.broadcasted_iota(jnp.int32, (tm, LANES), 1).astype(F32)
    top = top_ref[...]
    picks = [lane_f == top[:, k:k + 1] for k in range(TOP_K)]
    chosen = sum(p.astype(F32) for p in picks)
    tile_counts = jnp.sum(chosen, axis=0, keepdims=True)

    @pl.when((ph == 0) & (i == 0))
    def _():
        running[...] = jnp.zeros_like(running)

    @pl.when(ph == 0)
    def _():
        running[...] += tile_counts

    @pl.when((ph == 1) & (i == 0))
    def _():
        lane = lax.broadcasted_iota(jnp.int32, (1, LANES), 1)
        assert tm_blk & (tm_blk - 1) == 0
        blocks = (running[...].astype(jnp.int32) + (tm_blk - 1)) >> (tm_blk.bit_length() - 1)
        ends = blocks
        shift = 1
        while shift < N_EXPERTS:
            ends = ends + jnp.where(lane >= shift, pltpu.roll(ends, shift, axis=1), 0)
            shift *= 2
        ends_ref[...] = ends
        starts[...] = ((ends - blocks) * tm_blk).astype(F32)
        running[...] = jnp.zeros_like(running)

    @pl.when(ph == 1)
    def _():
        earlier = (lax.broadcasted_iota(jnp.int32, (tm, tm), 0)
                   > lax.broadcasted_iota(jnp.int32, (tm, tm), 1))
        ahead = _dot(jnp.where(earlier, 1.0, 0.0).astype(BF16), chosen.astype(BF16))
        slot = starts[...] + running[...] + ahead
        lane_o = lax.broadcasted_iota(jnp.int32, (tm, LANES), 1)
        packed = jnp.zeros((tm, LANES), F32)
        for k, pick in enumerate(picks):
            dest_k = jnp.sum(jnp.where(pick, slot, 0.0), axis=-1, keepdims=True)
            packed = jnp.where(lane_o == k, dest_k, packed)
        dest_ref[...] = packed.astype(jnp.int32)
        running[...] += tile_counts


def _route(top, tm_blk):
    n = top.shape[0]
    tm = 512
    return pl.pallas_call(
        functools.partial(_route_kernel, tm=tm, tm_blk=tm_blk),
        grid=(2, n // tm),
        in_specs=[pl.BlockSpec((tm, LANES), lambda ph, i: (i, 0))],
        out_specs=[pl.BlockSpec((tm, LANES), lambda ph, i: (i * ph, 0)),
                   pl.BlockSpec((1, LANES), lambda ph, i: (0, 0))],
        out_shape=[jax.ShapeDtypeStruct((n, LANES), jnp.int32),
                   jax.ShapeDtypeStruct((1, LANES), jnp.int32)],
        scratch_shapes=[pltpu.VMEM((1, LANES), F32), pltpu.VMEM((1, LANES), F32)],
        compiler_params=_params(2),
        name="route",
    )(top)


def _dispatch_kernel(ends_ref, dest_ref, h_ref, xs_hbm, zeros, sem, *, tm, tm_blk, n_blocks):
    i = pl.program_id(0)

    def fill(b):
        rows = pl.ds(pl.multiple_of(b * (tm_blk * SLAB), tm_blk * SLAB), tm_blk * SLAB)
        return pltpu.make_async_copy(zeros, xs_hbm.at[rows], sem.at[1])

    def fill_start(b, carry):
        fill(b).start()
        return carry

    def fill_wait(b, carry):
        fill(b).wait()
        return carry

    @pl.when(i == 0)
    def _():
        zeros[...] = jnp.zeros_like(zeros)
        n_used = ends_ref[N_EXPERTS - 1]
        for do in (fill_start, fill_wait):
            for e in range(N_EXPERTS):
                first = ends_ref[e - 1] if e else 0

                @pl.when(ends_ref[e] > first)
                def _():
                    do(ends_ref[e] - 1, 0)

            lax.fori_loop(n_used, n_blocks, do, 0)

    def body(j, carry):
        for k in range(TOP_K):
            pltpu.make_async_copy(h_ref.at[_slab_rows(j)], xs_hbm.at[_slab_rows(dest_ref[j * TOP_K + k])],
                                  sem.at[0]).start(priority=k % 2)
        return carry

    lax.fori_loop(0, tm, body, 0)
    for k in range(TOP_K):
        pltpu.make_async_copy(h_ref, xs_hbm.at[pl.ds(0, tm * SLAB)], sem.at[0]).wait()


def _dispatch(h, dest_flat, ends, n_blocks, tm_blk):
    n = h.shape[0] // SLAB
    tm = 256
    grid_spec = pltpu.PrefetchScalarGridSpec(
        num_scalar_prefetch=1,
        grid=(n // tm,),
        in_specs=[pl.BlockSpec((tm * TOP_K,), lambda i, ends: (i,), memory_space=pltpu.SMEM),
                  pl.BlockSpec((tm * SLAB, LANES), lambda i, ends: (i, 0))],
        out_specs=pl.BlockSpec(memory_space=pl.ANY),
        scratch_shapes=[pltpu.VMEM((tm_blk * SLAB, LANES), F32), pltpu.SemaphoreType.DMA((2,))],
    )
    return pl.pallas_call(
        functools.partial(_dispatch_kernel, tm=tm, tm_blk=tm_blk, n_blocks=n_blocks),
        grid_spec=grid_spec,
        out_shape=jax.ShapeDtypeStruct((n_blocks * tm_blk * SLAB, LANES), F32),
        compiler_params=_params(1),
        name="dispatch",
    )(ends, dest_flat, h)


def _moe_kernel(be_ref, nu_ref, x_ref, wgu_ref, wd_ref, bgu_ref, bd_ref, o_ref, wgu_bf, wd_rows, wd_bf):
    i = pl.program_id(0)
    n_used = nu_ref[0]
    d_gu = wgu_ref.shape[1]
    new_expert = (i == 0) | (be_ref[i] != be_ref[jnp.maximum(i - 1, 0)])

    @pl.when((i < n_used) & new_expert)
    def _():
        wgu_bf[...] = wgu_ref[...].astype(BF16)
        for c in range(wd_rows.shape[0]):
            cols = slice(c * LANES, (c + 1) * LANES)
            for r in range(2):
                wd_rows[c, pl.ds(r, d_gu // 2, stride=2), :] = wd_ref[:, cols]
            wd_bf[:, cols] = wd_rows[c].astype(BF16)

    @pl.when(i < n_used)
    def _():
        x = _slab_load(x_ref, x_ref.shape[0] // SLAB).astype(BF16)
        even = (lax.broadcasted_iota(jnp.int32, (x.shape[0], MOE_COLS), 1) & 1) == 0
        acc = bd_ref[...]
        for c in range(d_gu // MOE_COLS):
            cols = slice(c * MOE_COLS, (c + 1) * MOE_COLS)
            gu = _dot(x, wgu_bf[:, cols]) + bgu_ref[:, cols]
            nxt = pltpu.roll(gu, MOE_COLS - 1, axis=1)
            gate = jnp.minimum(gu, SWIGLU_LIMIT)
            up = jnp.clip(nxt, -SWIGLU_LIMIT, SWIGLU_LIMIT)
            act = jnp.where(even, gate * jax.nn.sigmoid(SWIGLU_ALPHA * gate) * (up + 1.0), 0.0)
            acc = acc + _dot(act.astype(BF16), wd_bf[cols, :])
        _slab_store(o_ref, acc)

    @pl.when(i >= n_used)
    def _():
        o_ref[...] = jnp.zeros_like(o_ref)


def _moe_blocks(xs, block_e, n_used, w_gu, w_down, b_gu, b_down, tm):
    n_blocks = block_e.shape[0]
    n_e, d, d_gu = w_gu.shape
    wspec = lambda k, n: pl.BlockSpec((None, k, n), lambda i, be, nu: (be[i], 0, 0))
    grid_spec = pltpu.PrefetchScalarGridSpec(
        num_scalar_prefetch=2,
        grid=(n_blocks,),
        in_specs=[pl.BlockSpec((tm * SLAB, LANES), lambda i, be, nu: (jnp.minimum(i, nu[0] - 1), 0)),
                  wspec(d, d_gu), wspec(d_gu // 2, d), wspec(1, d_gu), wspec(1, d)],
        out_specs=pl.BlockSpec((tm * SLAB, LANES), lambda i, be, nu: (i, 0)),
        scratch_shapes=[pltpu.VMEM((d, d_gu), BF16), pltpu.VMEM((d // LANES, d_gu, LANES), F32),
                        pltpu.VMEM((d_gu, d), BF16)],
    )
    return pl.pallas_call(
        _moe_kernel,
        grid_spec=grid_spec,
        out_shape=jax.ShapeDtypeStruct((n_blocks * tm * SLAB, LANES), F32),
        compiler_params=_params(1, VMEM_LIMIT),
        name="moe_blocks",
    )(block_e, n_used, xs, w_gu, w_down, b_gu.reshape(n_e, 1, d_gu), b_down.reshape(n_e, 1, d))


def _fin_kernel(dest_ref, rows_hbm, top_ref, x1_ref, gt_ref, g_ref, o_ref, buf, sem, *, tm):
    def body(j, carry):
        for k in range(TOP_K):
            pltpu.make_async_copy(rows_hbm.at[_slab_rows(dest_ref[j * TOP_K + k])], buf.at[k, _slab_rows(j)],
                                  sem.at[0]).start(priority=k % 2)
        return carry
    lax.fori_loop(0, tm, body, 0)
    for k in range(TOP_K):
        pltpu.make_async_copy(rows_hbm.at[pl.ds(0, tm * SLAB)], buf.at[k], sem.at[0]).wait()
    top = top_ref[...]
    f = sum(_slab_load(buf.at[k], tm) * top[:, TOP_K + k:TOP_K + k + 1] for k in range(TOP_K))
    o_ref[...] = x1_ref[...] + gt_ref[...] * _rms(f, g_ref[...])


def _finish(dest, rows, top, x1, mod, g_post, per_row, rows_per_batch):
    n = x1.shape[0]
    tm = min(TM_FIN, n)
    full = pl.BlockSpec((tm, D_MODEL), lambda i: (i, 0))
    gate = _mod_spec(tm, MOD_GATE_FFN, per_row, 1 if per_row else rows_per_batch // tm)
    return pl.pallas_call(
        functools.partial(_fin_kernel, tm=tm),
        grid=(n // tm,),
        in_specs=[pl.BlockSpec((tm * TOP_K,), lambda i: (i,), memory_space=pltpu.SMEM),
                  pl.BlockSpec(memory_space=pl.ANY),
                  pl.BlockSpec((tm, LANES), lambda i: (i, 0)), full, gate,
                  pl.BlockSpec((1, D_MODEL), lambda i: (0, 0))],
        out_specs=full,
        out_shape=jax.ShapeDtypeStruct((n, D_MODEL), F32),
        scratch_shapes=[pltpu.VMEM((TOP_K, tm * SLAB, LANES), F32), pltpu.SemaphoreType.DMA((1,))],
        compiler_params=_params(1),
        name="finish",
    )(dest, rows, top, x1, mod, g_post)


def kernel(x_prompt, x_sample, state_ret, cache_win_k, cache_win_v, c_prompt, c_sample, w_ada, b_ada,
           g_pre_mix, g_post_mix, g_pre_ffn, g_post_ffn, w_in, g_ret, w_out, w_router, b_router,
           w_gate_up, b_gate_up, w_down, b_down):
    depth = w_in.shape[0]
    assert depth == 1, "single-layer step"
    n_b, seq, d = x_prompt.shape
    n_db, t_new, _ = x_sample.shape
    n_p, n_s = n_b * seq, n_db * t_new
    buf = cache_win_k.shape[2]
    keep = min(DIL_PATTERNS[-1][0], seq)
    vec = lambda g: g[0].reshape(1, -1)

    mod_p, mod_s = _ada(c_prompt, c_sample, w_ada[0], b_ada[0], t_new)
    mod_p = mod_p[:n_b].reshape(n_b, 1, -1)
    mod_s = mod_s.reshape(n_s, -1)

    w_in_bf = w_in[0].astype(BF16)
    w_out_bf = w_out[0].astype(BF16)
    xp = x_prompt.reshape(n_p, d)
    xs = x_sample.reshape(n_s, d)
    router = (w_router[0], b_router[0].reshape(1, -1))

    qr, kr, vr, gr, qa, ka, va = _in_proj(xp, mod_p, vec(g_pre_mix), w_in_bf, False, seq)
    ret_y_p, ret_state_p = _retention(qr, kr, vr, gr, None, g_ret[0], n_b, seq, RET_CHUNK)
    parts = [_att_prompt_pattern(qa, ka, va, n_b, seq, dil) for _, dil in DIL_PATTERNS]
    x1_p, h_p, top_p = _mixer_out(ret_y_p, ([o for o, _ in parts], [ml for _, ml in parts]), xp, mod_p,
                                  w_out_bf, vec(g_post_mix), vec(g_pre_ffn), *router, False, seq)

    qr_s, kr_s, vr_s, gr_s, qa_s, ka_s, va_s = _in_proj(xs, mod_s, vec(g_pre_mix), w_in_bf, True, t_new)
    ret_y_s, ret_state_s = _retention(qr_s, kr_s, vr_s, gr_s, state_ret[0], g_ret[0], n_db, t_new, t_new)
    att_s = _att_sample(qa_s, ka_s, va_s, cache_win_k[0], cache_win_v[0], n_db, t_new)
    x1_s, h_s, top_s = _mixer_out(ret_y_s, att_s, xs, mod_s, w_out_bf, vec(g_post_mix), vec(g_pre_ffn),
                                  *router, True, t_new)

    h_all = jnp.concatenate([h_p, h_s], axis=0)
    n_blocks = (n_p + n_s) * TOP_K // TM_MOE + N_EXPERTS
    dest, ends = _route(jnp.concatenate([top_p, top_s], axis=0), TM_MOE)
    dest = dest[:, :TOP_K].reshape(-1)
    ends = ends[0, :N_EXPERTS]
    block_e = jnp.minimum(jnp.sum(jnp.arange(n_blocks, dtype=jnp.int32)[:, None] >= ends[None, :], axis=1),
                          N_EXPERTS - 1).astype(jnp.int32)
    rows_in = _dispatch(h_all, dest, ends, n_blocks, TM_MOE)
    rows = _moe_blocks(rows_in, block_e, ends[N_EXPERTS - 1:], w_gate_up[0], w_down[0],
                       b_gate_up[0], b_down[0], TM_MOE)
    y_p = _finish(dest[:n_p * TOP_K], rows, top_p, x1_p, mod_p, vec(g_post_ffn), False, seq)
    y_s = _finish(dest[n_p * TOP_K:], rows, top_s, x1_s, mod_s, vec(g_post_ffn), True, t_new)

    win = lambda t: t.reshape(n_b, seq, N_HEADS, HEAD_DIM)[:, seq - keep:][None]
    new = lambda t: t.reshape(n_db, t_new, N_HEADS, HEAD_DIM)[None]
    return (y_p.reshape(n_b, seq, d), y_s.reshape(n_db, t_new, d), ret_state_p[None], ret_state_s[None],
            win(ka), win(va), new(ka_s), new(va_s))
```

```python
import functools

import numpy as np
import jax
import jax.numpy as jnp
from jax import lax
from jax.experimental import pallas as pl
from jax.experimental.pallas import tpu as pltpu

F32 = jnp.float32
BF16 = jnp.bfloat16

D_MODEL = 1024
HEAD_DIM = 64
N_HEADS = 8
D_HALF = N_HEADS * HEAD_DIM
N_SEG = 7
LANES = 128
N_PAIRS = D_HALF // LANES
RET_CHUNK = 128
ATT_BLOCK = 128
DIL_PATTERNS = ((128, 1), (512, 4), (2048, 16))
N_EXPERTS = 32
TOP_K = 4
SWIGLU_LIMIT = 7.0
SWIGLU_ALPHA = 1.702
EPS = 1e-6
NEG_INF = -1e30
MOD_SHIFT_MIX, MOD_SCALE_MIX, MOD_GATE_MIX, MOD_SHIFT_FFN, MOD_SCALE_FFN, MOD_GATE_FFN = range(6)

TM_IN = 512
TM_OUT = 256
TM_MOE = 256
MOE_COLS = 512
TM_FIN = 128
VMEM_LIMIT = 56 * 1024 * 1024


def _params(n_axes, vmem=None):
    return pltpu.CompilerParams(dimension_semantics=("arbitrary",) * n_axes, vmem_limit_bytes=vmem)


def _ret_log_decay():
    return np.log(1.0 - 2.0 ** (-5.0 - np.arange(N_HEADS, dtype=np.float64)))


def _alibi_slopes():
    return 2.0 ** (-8.0 * (np.arange(N_HEADS, dtype=np.float64) + 1.0) / N_HEADS)


def _per_head_lanes(a):
    return np.repeat(a, HEAD_DIM, axis=-1)


def _rms(x, g):
    return x * lax.rsqrt(jnp.mean(x * x, axis=-1, keepdims=True) + EPS) * g


def _dot(a, b):
    return jnp.dot(a, b, preferred_element_type=F32)


def _dot_nt(a, b):
    return lax.dot_general(a, b, (((1,), (1,)), ((), ())), preferred_element_type=F32)


def _dot_tn(a, b):
    return lax.dot_general(a, b, (((0,), (0,)), ((), ())), preferred_element_type=F32)


SLAB = D_MODEL // LANES


def _slab_rows(r):
    return pl.ds(pl.multiple_of(r * SLAB, SLAB), SLAB)


def _slab_load(ref, n):
    return jnp.concatenate([ref[pl.ds(s, n, stride=SLAB), :] for s in range(SLAB)], axis=1)


def _slab_store(ref, value):
    n = value.shape[0]
    for s in range(SLAB):
        ref[pl.ds(s, n, stride=SLAB), :] = value[:, s * LANES:(s + 1) * LANES]


def _mod_spec(tm, col, per_row, tiles_per_batch):
    if per_row:
        return pl.BlockSpec((tm, D_MODEL), lambda i: (i, col))
    return pl.BlockSpec((None, 1, D_MODEL), lambda i: (i // tiles_per_batch, 0, col))


def _ada_kernel(cp_ref, cs_ref, w_ref, b_ref, op_ref, os_ref):
    w = w_ref[...].astype(BF16)

    def mod(c):
        return _dot((c * jax.nn.sigmoid(c)).astype(BF16), w) + b_ref[...]

    op_ref[...] = mod(cp_ref[...])
    ms = mod(cs_ref[...])
    os_ref[...] = jnp.broadcast_to(ms[:, None, :], os_ref.shape)


def _ada(c_prompt, c_sample, w_ada, b_ada, t_new):
    n_b, d = c_prompt.shape
    n_db = c_sample.shape[0]
    n_out = w_ada.shape[1]
    cp = jnp.concatenate([c_prompt, jnp.zeros((-n_b % 8, d), F32)], axis=0)
    return pl.pallas_call(
        _ada_kernel,
        grid=(n_out // d,),
        in_specs=[pl.BlockSpec(cp.shape, lambda j: (0, 0)),
                  pl.BlockSpec((n_db, d), lambda j: (0, 0)),
                  pl.BlockSpec((d, d), lambda j: (0, j)),
                  pl.BlockSpec((1, d), lambda j: (0, j))],
        out_specs=[pl.BlockSpec(cp.shape, lambda j: (0, j)),
                   pl.BlockSpec((n_db, t_new, d), lambda j: (0, 0, j))],
        out_shape=[jax.ShapeDtypeStruct((cp.shape[0], n_out), F32),
                   jax.ShapeDtypeStruct((n_db, t_new, n_out), F32)],
        compiler_params=_params(1, VMEM_LIMIT),
        name="ada",
    )(cp, c_sample, w_ada, b_ada.reshape(1, n_out))


def _in_kernel(x_ref, sc_ref, sh_ref, g_ref, w_ref, *o_refs):
    h = (_rms(x_ref[...], g_ref[...]) * (1.0 + sc_ref[...]) + sh_ref[...]).astype(BF16)
    for s, o_ref in enumerate(o_refs):
        o_ref[...] = _dot(h, w_ref[:, s * D_HALF:(s + 1) * D_HALF])


def _in_proj(x, mod, g, w_bf, per_row, rows_per_batch):
    n = x.shape[0]
    tm = min(TM_IN, n)
    tpb = 1 if per_row else rows_per_batch // tm
    seg = pl.BlockSpec((tm, D_HALF), lambda i: (i, 0))
    return pl.pallas_call(
        _in_kernel,
        grid=(n // tm,),
        in_specs=[pl.BlockSpec((tm, D_MODEL), lambda i: (i, 0)),
                  _mod_spec(tm, MOD_SCALE_MIX, per_row, tpb), _mod_spec(tm, MOD_SHIFT_MIX, per_row, tpb),
                  pl.BlockSpec((1, D_MODEL), lambda i: (0, 0)),
                  pl.BlockSpec((D_MODEL, N_SEG * D_HALF), lambda i: (0, 0))],
        out_specs=[seg] * N_SEG,
        out_shape=[jax.ShapeDtypeStruct((n, D_HALF), F32)] * N_SEG,
        compiler_params=_params(1, VMEM_LIMIT),
        name="in_proj",
    )(x, mod, mod, g, w_bf)


def _ret_tables(chunk):
    lg = _ret_log_decay()
    pos = np.arange(chunk, dtype=np.float64)
    diff = pos[:, None] - pos[None, :]
    dmask = np.where(diff >= 0, np.exp(np.maximum(diff, 0.0)[None] * lg[:, None, None]), 0.0)
    row_decay = _per_head_lanes(np.exp((pos[:, None] + 1.0) * lg[None, :]))
    key_decay = _per_head_lanes(np.exp((chunk - 1.0 - pos)[:, None] * lg[None, :]))
    state_decay = _per_head_lanes(np.exp(chunk * lg)[None, :])
    return tuple(jnp.asarray(t, F32) for t in (dmask, row_decay, key_decay, state_decay))


def _ret_kernel(q_ref, k_ref, v_ref, gate_ref, s0_ref, gret_ref, dm_ref, rd_ref, kd_ref, sd_ref,
                y_ref, st_ref, state, *, chunk, has_init):
    c = pl.program_id(1)
    lo = lax.broadcasted_iota(jnp.int32, (chunk, LANES), 1) < HEAD_DIM
    blk_r = lax.broadcasted_iota(jnp.int32, (LANES, LANES), 0) < HEAD_DIM
    blk_c = lax.broadcasted_iota(jnp.int32, (LANES, LANES), 1) < HEAD_DIM
    same_head = blk_r == blk_c

    @pl.when(c == 0)
    def _():
        if has_init:
            zero = jnp.zeros((HEAD_DIM, HEAD_DIM), F32)
            for p in range(N_PAIRS):
                top = jnp.concatenate([s0_ref[0, 2 * p], zero], axis=1)
                bot = jnp.concatenate([zero, s0_ref[0, 2 * p + 1]], axis=1)
                state[p] = jnp.concatenate([top, bot], axis=0)
        else:
            state[...] = jnp.zeros_like(state)

    for p in range(N_PAIRS):
        sl = slice(p * LANES, (p + 1) * LANES)
        q = q_ref[:, sl]
        k = k_ref[:, sl] * HEAD_DIM ** -0.5
        vb = v_ref[:, sl].astype(BF16)
        kb = k.astype(BF16)
        inner = []
        for half in range(2):
            qh = jnp.where(lo if half == 0 else ~lo, q, 0.0).astype(BF16)
            scores = _dot_nt(qh, kb) * dm_ref[2 * p + half]
            inner.append(_dot(scores.astype(BF16), vb))
        s_old = state[p]
        cross = _dot(q.astype(BF16), s_old.astype(BF16)) * rd_ref[:, sl]
        o = jnp.where(lo, inner[0], inner[1]) + cross
        upd = _dot_tn((k * kd_ref[:, sl]).astype(BF16), vb)
        s_new = sd_ref[:, sl] * s_old + jnp.where(same_head, upd, 0.0)
        state[p] = s_new

        def head_mean(t):
            m_lo = jnp.sum(jnp.where(lo, t, 0.0), axis=-1, keepdims=True)
            m_hi = jnp.sum(jnp.where(lo, 0.0, t), axis=-1, keepdims=True)
            return jnp.where(lo, m_lo, m_hi) * (1.0 / HEAD_DIM)

        d = o - head_mean(o)
        n = d * lax.rsqrt(head_mean(d * d) + EPS) * gret_ref[:, sl]
        g = gate_ref[:, sl]
        y_ref[:, sl] = (g * jax.nn.sigmoid(g) * n).astype(y_ref.dtype)

        @pl.when(c == pl.num_programs(1) - 1)
        def _():
            st_ref[0, 2 * p] = s_new[:HEAD_DIM, :HEAD_DIM]
            st_ref[0, 2 * p + 1] = s_new[HEAD_DIM:, HEAD_DIM:]


def _retention(q, k, v, gate, state0, g_ret, n_batch, seq, chunk):
    nc = seq // chunk
    has_init = state0 is not None
    if state0 is None:
        state0 = jnp.zeros((1, N_HEADS, HEAD_DIM, HEAD_DIM), F32)
        s0_map = lambda b, c: (0, 0, 0, 0)
    else:
        s0_map = lambda b, c: (b, 0, 0, 0)
    dmask, row_decay, key_decay, state_decay = _ret_tables(chunk)
    rows = pl.BlockSpec((chunk, D_HALF), lambda b, c: (b * nc + c, 0))
    const2 = lambda shape: pl.BlockSpec(shape, lambda b, c: (0, 0))
    st_block = pl.BlockSpec((1, N_HEADS, HEAD_DIM, HEAD_DIM), lambda b, c: (b, 0, 0, 0))
    return pl.pallas_call(
        functools.partial(_ret_kernel, chunk=chunk, has_init=has_init),
        grid=(n_batch, nc),
        in_specs=[rows, rows, rows, rows,
                  pl.BlockSpec((1, N_HEADS, HEAD_DIM, HEAD_DIM), s0_map),
                  const2((1, D_HALF)),
                  pl.BlockSpec((N_HEADS, chunk, chunk), lambda b, c: (0, 0, 0)),
                  const2((chunk, D_HALF)), const2((chunk, D_HALF)), const2((1, D_HALF))],
        out_specs=[rows, st_block],
        out_shape=[jax.ShapeDtypeStruct((n_batch * seq, D_HALF), BF16),
                   jax.ShapeDtypeStruct((n_batch, N_HEADS, HEAD_DIM, HEAD_DIM), F32)],
        scratch_shapes=[pltpu.VMEM((N_PAIRS, LANES, LANES), F32)],
        compiler_params=_params(2),
        name="retention",
    )(q, k, v, gate, state0, g_ret.reshape(1, D_HALF), dmask, row_decay, key_decay, state_decay)


def _attp_table(dilation):
    qi = np.arange(ATT_BLOCK)[:, None]
    kj = np.arange(2 * ATT_BLOCK)[None, :]
    step = qi + ATT_BLOCK - kj
    in_win = (step >= 0) & (step <= ATT_BLOCK)
    bias = -_alibi_slopes()[:, None, None] * (step * dilation).astype(np.float64)[None]
    return jnp.asarray(np.where(in_win[None], bias, NEG_INF), F32)


def _attp_kernel(q_ref, kp_ref, kc_ref, vp_ref, vc_ref, tb_ref, o_ref, ml_ref):
    has_prev = pl.program_id(2) > 0
    lo = lax.broadcasted_iota(jnp.int32, (ATT_BLOCK, LANES), 1) < HEAD_DIM
    lane = lax.broadcasted_iota(jnp.int32, (ATT_BLOCK, LANES), 1)
    key_ok = (lax.broadcasted_iota(jnp.int32, (ATT_BLOCK, 2 * ATT_BLOCK), 1) >= ATT_BLOCK) | has_prev
    ml = jnp.zeros((ATT_BLOCK, LANES), F32)
    for p in range(N_PAIRS):
        sl = slice(p * LANES, (p + 1) * LANES)
        q = q_ref[:, sl]
        kb = jnp.concatenate([kp_ref[:, sl], kc_ref[:, sl]], axis=0).astype(BF16)
        vb = jnp.concatenate([vp_ref[:, sl], vc_ref[:, sl]], axis=0).astype(BF16)
        outs = []
        for half in range(2):
            h = 2 * p + half
            qh = jnp.where(lo if half == 0 else ~lo, q, 0.0).astype(BF16)
            s = _dot_nt(qh, kb) * HEAD_DIM ** -0.5
            s = jnp.where(key_ok, s + tb_ref[h], NEG_INF)
            m = jnp.max(s, axis=-1, keepdims=True)
            e = jnp.exp(s - m)
            l = jnp.sum(e, axis=-1, keepdims=True)
            outs.append(_dot(e.astype(BF16), vb))
            ml = jnp.where(lane == h, m, ml)
            ml = jnp.where(lane == N_HEADS + h, l, ml)
        o_ref[:, sl] = jnp.where(lo, outs[0], outs[1])
    ml_ref[...] = ml


def _att_prompt_pattern(q, k, v, n_batch, seq, dilation):
    sub = seq // dilation
    nb = sub // ATT_BLOCK
    view = lambda t: t.reshape(n_batch, sub, dilation * D_HALF)
    cur = pl.BlockSpec((None, ATT_BLOCK, D_HALF), lambda b, r, n: (b, n, r))
    prev = pl.BlockSpec((None, ATT_BLOCK, D_HALF), lambda b, r, n: (b, jnp.maximum(n - 1, 0), r))
    o, ml = pl.pallas_call(
        _attp_kernel,
        grid=(n_batch, dilation, nb),
        in_specs=[cur, prev, cur, prev, cur,
                  pl.BlockSpec((N_HEADS, ATT_BLOCK, 2 * ATT_BLOCK), lambda b, r, n: (0, 0, 0))],
        out_specs=[cur, pl.BlockSpec((None, ATT_BLOCK, LANES), lambda b, r, n: (b, n, r))],
        out_shape=[jax.ShapeDtypeStruct((n_batch, sub, dilation * D_HALF), F32),
                   jax.ShapeDtypeStruct((n_batch, sub, dilation * LANES), F32)],
        compiler_params=_params(3),
        name=f"att_prompt_d{dilation}",
    )(view(q), view(k), view(k), view(v), view(v), _attp_table(dilation))
    return o.reshape(n_batch * seq, D_HALF), ml.reshape(n_batch * seq, LANES)


def _atts_tables(buf, t_new):
    pos_q = buf + np.arange(t_new)
    pos_k = np.concatenate([np.arange(buf), buf + np.arange(LANES)])
    dist = pos_q[:, None] - pos_k[None, :]
    count = np.zeros(dist.shape)
    for window, dilation in DIL_PATTERNS:
        count += (dist >= 0) & (dist <= window) & (dist % dilation == 0)
    count[:, buf + t_new:] = 0
    bias = -_alibi_slopes()[:, None, None] * dist[None].astype(np.float64)
    table = np.where(count[None] > 0, bias + np.log(np.maximum(count, 1.0))[None], NEG_INF)
    table = table.reshape(N_HEADS * t_new, buf + LANES)
    return jnp.asarray(table[:, :buf], F32), jnp.asarray(table[:, buf:], F32)


def _atts_kernel(q_ref, kt_ref, kn_ref, vt_ref, vn_ref, tc_ref, tn_ref, o_ref, *, t_new):
    rows = N_HEADS * t_new
    row_head = lax.broadcasted_iota(jnp.int32, (rows, D_HALF), 0) // t_new
    col_head = lax.broadcasted_iota(jnp.int32, (rows, D_HALF), 1) // HEAD_DIM
    own = row_head == col_head
    q_heads = jnp.where(own, jnp.concatenate([q_ref[...]] * N_HEADS, axis=0), 0.0).astype(BF16)
    pad = jnp.zeros((LANES - t_new, D_HALF), F32)
    kn = jnp.concatenate([kn_ref[...], pad], axis=0).astype(BF16)
    vn = jnp.concatenate([vn_ref[...], pad], axis=0).astype(BF16)
    scale = HEAD_DIM ** -0.5
    sc = _dot(q_heads, kt_ref[...].astype(BF16)) * scale + tc_ref[...]
    sn = _dot_nt(q_heads, kn) * scale + tn_ref[...]
    m = jnp.maximum(jnp.max(sc, axis=-1, keepdims=True), jnp.max(sn, axis=-1, keepdims=True))
    ec = jnp.exp(sc - m)
    en = jnp.exp(sn - m)
    l = jnp.sum(ec, axis=-1, keepdims=True) + jnp.sum(en, axis=-1, keepdims=True)
    acc = _dot_nt(ec.astype(BF16), vt_ref[...].astype(BF16)) + _dot(en.astype(BF16), vn)
    acc = jnp.where(own, acc, 0.0) / l
    out = acc[:t_new]
    for h in range(1, N_HEADS):
        out = out + acc[h * t_new:(h + 1) * t_new]
    o_ref[...] = out.astype(o_ref.dtype)


def _att_sample(q, k_new, v_new, cache_k, cache_v, n_batch, t_new):
    buf = cache_k.shape[1]
    tc, tn = _atts_tables(buf, t_new)
    by_pos = lambda c: c.transpose(0, 2, 3, 1).reshape(n_batch, D_HALF, buf)
    rows = pl.BlockSpec((t_new, D_HALF), lambda b: (b, 0))
    cache = pl.BlockSpec((None, D_HALF, buf), lambda b: (b, 0, 0))
    return pl.pallas_call(
        functools.partial(_atts_kernel, t_new=t_new),
        grid=(n_batch,),
        in_specs=[rows, cache, rows, cache, rows,
                  pl.BlockSpec((N_HEADS * t_new, buf), lambda b: (0, 0)),
                  pl.BlockSpec((N_HEADS * t_new, LANES), lambda b: (0, 0))],
        out_specs=rows,
        out_shape=jax.ShapeDtypeStruct((n_batch * t_new, D_HALF), BF16),
        compiler_params=_params(1, VMEM_LIMIT),
        name="att_sample",
    )(q, by_pos(cache_k), k_new, by_pos(cache_v), v_new, tc, tn)


def _out_kernel(*refs, n_parts):
    ret_ref = refs[0]
    part_refs = refs[1:1 + 2 * n_parts] if n_parts else refs[1:2]
    (x_ref, gt_ref, sc_ref, sh_ref, w_ref, gpost_ref, gpre_ref, wr_ref, br_ref,
     x1_ref, h_ref, top_ref, xin) = refs[len(part_refs) + 1:]
    tm = x_ref.shape[0]
    xin[:, :D_HALF] = ret_ref[...]
    if n_parts:
        o_refs, ml_refs = part_refs[:n_parts], part_refs[n_parts:]
        for h in range(N_HEADS):
            ms = [r[:, h:h + 1] for r in ml_refs]
            ls = [r[:, N_HEADS + h:N_HEADS + h + 1] for r in ml_refs]
            mx = functools.reduce(jnp.maximum, ms)
            ws = [jnp.exp(m - mx) for m in ms]
            den = sum(l * w for l, w in zip(ls, ws))
            hs = slice(h * HEAD_DIM, (h + 1) * HEAD_DIM)
            att = sum(r[:, hs] * (w / den) for r, w in zip(o_refs, ws))
            xin[:, D_HALF + h * HEAD_DIM:D_HALF + (h + 1) * HEAD_DIM] = att.astype(BF16)
    else:
        xin[:, D_HALF:] = part_refs[0][...]
    mix = _dot(xin[...], w_ref[...])
    x1 = x_ref[...] + gt_ref[...] * _rms(mix, gpost_ref[...])
    x1_ref[...] = x1
    h = _rms(x1, gpre_ref[...]) * (1.0 + sc_ref[...]) + sh_ref[...]
    _slab_store(h_ref, h)
    logits = jnp.dot(h, wr_ref[...], preferred_element_type=F32,
                     precision=lax.Precision.HIGHEST) + br_ref[...]
    lane_e = lax.broadcasted_iota(jnp.int32, (tm, N_EXPERTS), 1).astype(F32)
    lane_o = lax.broadcasted_iota(jnp.int32, (tm, LANES), 1)
    vals, idxs = [], []
    work = logits
    for _ in range(TOP_K):
        v = jnp.max(work, axis=-1, keepdims=True)
        i = jnp.min(jnp.where(work == v, lane_e, float(N_EXPERTS)), axis=-1, keepdims=True)
        vals.append(v)
        idxs.append(i)
        work = jnp.where(lane_e == i, -jnp.inf, work)
    es = [jnp.exp(v - vals[0]) for v in vals]
    tot = sum(es)
    packed = jnp.zeros((tm, LANES), F32)
    for j in range(TOP_K):
        packed = jnp.where(lane_o == j, idxs[j], packed)
        packed = jnp.where(lane_o == TOP_K + j, es[j] / tot, packed)
    top_ref[...] = packed


def _mixer_out(ret_y, parts, x, mod, w_out_bf, g_post, g_pre, w_router, b_router,
               per_row, rows_per_batch):
    n = x.shape[0]
    tm = min(TM_OUT, n)
    tpb = 1 if per_row else rows_per_batch // tm
    n_parts = len(parts[0]) if isinstance(parts, tuple) else 0
    half = pl.BlockSpec((tm, D_HALF), lambda i: (i, 0))
    full = pl.BlockSpec((tm, D_MODEL), lambda i: (i, 0))
    stat = pl.BlockSpec((tm, LANES), lambda i: (i, 0))
    vec = pl.BlockSpec((1, D_MODEL), lambda i: (0, 0))
    mods = [_mod_spec(tm, c, per_row, tpb) for c in (MOD_GATE_MIX, MOD_SCALE_FFN, MOD_SHIFT_FFN)]
    if n_parts:
        part_args = list(parts[0]) + list(parts[1])
        part_specs = [half] * n_parts + [stat] * n_parts
    else:
        part_args, part_specs = [parts], [half]
    return pl.pallas_call(
        functools.partial(_out_kernel, n_parts=n_parts),
        grid=(n // tm,),
        in_specs=[half] + part_specs + [full] + mods + [
                  pl.BlockSpec((D_MODEL, D_MODEL), lambda i: (0, 0)), vec, vec,
                  pl.BlockSpec((D_MODEL, N_EXPERTS), lambda i: (0, 0)),
                  pl.BlockSpec((1, N_EXPERTS), lambda i: (0, 0))],
        out_specs=[full, pl.BlockSpec((tm * SLAB, LANES), lambda i: (i, 0)), stat],
        out_shape=[jax.ShapeDtypeStruct((n, D_MODEL), F32),
                   jax.ShapeDtypeStruct((n * SLAB, LANES), F32),
                   jax.ShapeDtypeStruct((n, LANES), F32)],
        scratch_shapes=[pltpu.VMEM((tm, D_MODEL), BF16)],
        compiler_params=_params(1, VMEM_LIMIT),
        name="mixer_out",
    )(ret_y, *part_args, x, mod, mod, mod, w_out_bf, g_post, g_pre, w_router, b_router)


def _route_kernel(top_ref, dest_ref, ends_ref, running, starts, *, tm, tm_blk):
    ph, i = pl.program_id(0), pl.program_id(1)
    lane_f = lax.broadcasted_iota(jnp.int32, (tm, LANES), 1).astype(F32)
    top = top_ref[...]
    picks = [lane_f == top[:, k:k + 1] for k in range(TOP_K)]
    chosen = sum(p.astype(F32) for p in picks)
    tile_counts = jnp.sum(chosen, axis=0, keepdims=True)

    @pl.when((ph == 0) & (i == 0))
    def _():
        running[...] = jnp.zeros_like(running)

    @pl.when(ph == 0)
    def _():
        running[...] += tile_counts

    @pl.when((ph == 1) & (i == 0))
    def _():
        lane = lax.broadcasted_iota(jnp.int32, (1, LANES), 1)
        assert tm_blk & (tm_blk - 1) == 0
        blocks = (running[...].astype(jnp.int32) + (tm_blk - 1)) >> (tm_blk.bit_length() - 1)
        ends = blocks
        shift = 1
        while shift < N_EXPERTS:
            ends = ends + jnp.where(lane >= shift, pltpu.roll(ends, shift, axis=1), 0)
            shift *= 2
        ends_ref[...] = ends
        starts[...] = ((ends - blocks) * tm_blk).astype(F32)
        running[...] = jnp.zeros_like(running)

    @pl.when(ph == 1)
    def _():
        earlier = (lax.broadcasted_iota(jnp.int32, (tm, tm), 0)
                   > lax.broadcasted_iota(jnp.int32, (tm, tm), 1))
        ahead = _dot(jnp.where(earlier, 1.0, 0.0).astype(BF16), chosen.astype(BF16))
        slot = starts[...] + running[...] + ahead
        lane_o = lax.broadcasted_iota(jnp.int32, (tm, LANES), 1)
        packed = jnp.zeros((tm, LANES), F32)
        for k, pick in enumerate(picks):
            dest_k = jnp.sum(jnp.where(pick, slot, 0.0), axis=-1, keepdims=True)
            packed = jnp.where(lane_o == k, dest_k, packed)
        dest_ref[...] = packed.astype(jnp.int32)
        running[...] += tile_counts


def _route(top, tm_blk):
    n = top.shape[0]
    tm = 512
    return pl.pallas_call(
        functools.partial(_route_kernel, tm=tm, tm_blk=tm_blk),
        grid=(2, n // tm),
        in_specs=[pl.BlockSpec((tm, LANES), lambda ph, i: (i, 0))],
        out_specs=[pl.BlockSpec((tm, LANES), lambda ph, i: (i * ph, 0)),
                   pl.BlockSpec((1, LANES), lambda ph, i: (0, 0))],
        out_shape=[jax.ShapeDtypeStruct((n, LANES), jnp.int32),
                   jax.ShapeDtypeStruct((1, LANES), jnp.int32)],
        scratch_shapes=[pltpu.VMEM((1, LANES), F32), pltpu.VMEM((1, LANES), F32)],
        compiler_params=_params(2),
        name="route",
    )(top)


def _dispatch_kernel(ends_ref, dest_ref, h_ref, xs_hbm, zeros, sem, *, tm, tm_blk, n_blocks):
    i = pl.program_id(0)

    def fill(b):
        rows = pl.ds(pl.multiple_of(b * (tm_blk * SLAB), tm_blk * SLAB), tm_blk * SLAB)
        return pltpu.make_async_copy(zeros, xs_hbm.at[rows], sem.at[1])

    def fill_start(b, carry):
        fill(b).start()
        return carry

    def fill_wait(b, carry):
        fill(b).wait()
        return carry

    @pl.when(i == 0)
    def _():
        zeros[...] = jnp.zeros_like(zeros)
        n_used = ends_ref[N_EXPERTS - 1]
        for do in (fill_start, fill_wait):
            for e in range(N_EXPERTS):
                first = ends_ref[e - 1] if e else 0

                @pl.when(ends_ref[e] > first)
                def _():
                    do(ends_ref[e] - 1, 0)

            lax.fori_loop(n_used, n_blocks, do, 0)

    def body(j, carry):
        for k in range(TOP_K):
            pltpu.make_async_copy(h_ref.at[_slab_rows(j)], xs_hbm.at[_slab_rows(dest_ref[j * TOP_K + k])],
                                  sem.at[0]).start(priority=k % 2)
        return carry

    lax.fori_loop(0, tm, body, 0)
    for k in range(TOP_K):
        pltpu.make_async_copy(h_ref, xs_hbm.at[pl.ds(0, tm * SLAB)], sem.at[0]).wait()


def _dispatch(h, dest_flat, ends, n_blocks, tm_blk):
    n = h.shape[0] // SLAB
    tm = 256
    grid_spec = pltpu.PrefetchScalarGridSpec(
        num_scalar_prefetch=1,
        grid=(n // tm,),
        in_specs=[pl.BlockSpec((tm * TOP_K,), lambda i, ends: (i,), memory_space=pltpu.SMEM),
                  pl.BlockSpec((tm * SLAB, LANES), lambda i, ends: (i, 0))],
        out_specs=pl.BlockSpec(memory_space=pl.ANY),
        scratch_shapes=[pltpu.VMEM((tm_blk * SLAB, LANES), F32), pltpu.SemaphoreType.DMA((2,))],
    )
    return pl.pallas_call(
        functools.partial(_dispatch_kernel, tm=tm, tm_blk=tm_blk, n_blocks=n_blocks),
        grid_spec=grid_spec,
        out_shape=jax.ShapeDtypeStruct((n_blocks * tm_blk * SLAB, LANES), F32),
        compiler_params=_params(1),
        name="dispatch",
    )(ends, dest_flat, h)


def _moe_kernel(be_ref, nu_ref, x_ref, wgu_ref, wd_ref, bgu_ref, bd_ref, o_ref, wgu_bf, wd_rows, wd_bf):
    i = pl.program_id(0)
    n_used = nu_ref[0]
    d_gu = wgu_ref.shape[1]
    new_expert = (i == 0) | (be_ref[i] != be_ref[jnp.maximum(i - 1, 0)])

    @pl.when((i < n_used) & new_expert)
    def _():
        wgu_bf[...] = wgu_ref[...].astype(BF16)
        for c in range(wd_rows.shape[0]):
            cols = slice(c * LANES, (c + 1) * LANES)
            for r in range(2):
                wd_rows[c, pl.ds(r, d_gu // 2, stride=2), :] = wd_ref[:, cols]
            wd_bf[:, cols] = wd_rows[c].astype(BF16)

    @pl.when(i < n_used)
    def _():
        x = _slab_load(x_ref, x_ref.shape[0] // SLAB).astype(BF16)
        even = (lax.broadcasted_iota(jnp.int32, (x.shape[0], MOE_COLS), 1) & 1) == 0
        acc = bd_ref[...]
        for c in range(d_gu // MOE_COLS):
            cols = slice(c * MOE_COLS, (c + 1) * MOE_COLS)
            gu = _dot(x, wgu_bf[:, cols]) + bgu_ref[:, cols]
            nxt = pltpu.roll(gu, MOE_COLS - 1, axis=1)
            gate = jnp.minimum(gu, SWIGLU_LIMIT)
            up = jnp.clip(nxt, -SWIGLU_LIMIT, SWIGLU_LIMIT)
            act = jnp.where(even, gate * jax.nn.sigmoid(SWIGLU_ALPHA * gate) * (up + 1.0), 0.0)
            acc = acc + _dot(act.astype(BF16), wd_bf[cols, :])
        _slab_store(o_ref, acc)

    @pl.when(i >= n_used)
    def _():
        o_ref[...] = jnp.zeros_like(o_ref)


def _moe_blocks(xs, block_e, n_used, w_gu, w_down, b_gu, b_down, tm):
    n_blocks = block_e.shape[0]
    n_e, d, d_gu = w_gu.shape
    wspec = lambda k, n: pl.BlockSpec((None, k, n), lambda i, be, nu: (be[i], 0, 0))
    grid_spec = pltpu.PrefetchScalarGridSpec(
        num_scalar_prefetch=2,
        grid=(n_blocks,),
        in_specs=[pl.BlockSpec((tm * SLAB, LANES), lambda i, be, nu: (jnp.minimum(i, nu[0] - 1), 0)),
                  wspec(d, d_gu), wspec(d_gu // 2, d), wspec(1, d_gu), wspec(1, d)],
        out_specs=pl.BlockSpec((tm * SLAB, LANES), lambda i, be, nu: (i, 0)),
        scratch_shapes=[pltpu.VMEM((d, d_gu), BF16), pltpu.VMEM((d // LANES, d_gu, LANES), F32),
                        pltpu.VMEM((d_gu, d), BF16)],
    )
    return pl.pallas_call(
        _moe_kernel,
        grid_spec=grid_spec,
        out_shape=jax.ShapeDtypeStruct((n_blocks * tm * SLAB, LANES), F32),
        compiler_params=_params(1, VMEM_LIMIT),
        name="moe_blocks",
    )(block_e, n_used, xs, w_gu, w_down, b_gu.reshape(n_e, 1, d_gu), b_down.reshape(n_e, 1, d))


def _fin_kernel(dest_ref, rows_hbm, top_ref, x1_ref, gt_ref, g_ref, o_ref, buf, sem, *, tm):
    def body(j, carry):
        for k in range(TOP_K):
            pltpu.make_async_copy(rows_hbm.at[_slab_rows(dest_ref[j * TOP_K + k])], buf.at[k, _slab_rows(j)],
                                  sem.at[0]).start(priority=k % 2)
        return carry
    lax.fori_loop(0, tm, body, 0)
    for k in range(TOP_K):
        pltpu.make_async_copy(rows_hbm.at[pl.ds(0, tm * SLAB)], buf.at[k], sem.at[0]).wait()
    top = top_ref[...]
    f = sum(_slab_load(buf.at[k], tm) * top[:, TOP_K + k:TOP_K + k + 1] for k in range(TOP_K))
    o_ref[...] = x1_ref[...] + gt_ref[...] * _rms(f, g_ref[...])


def _finish(dest, rows, top, x1, mod, g_post, per_row, rows_per_batch):
    n = x1.shape[0]
    tm = min(TM_FIN, n)
    full = pl.BlockSpec((tm, D_MODEL), lambda i: (i, 0))
    gate = _mod_spec(tm, MOD_GATE_FFN, per_row, 1 if per_row else rows_per_batch // tm)
    return pl.pallas_call(
        functools.partial(_fin_kernel, tm=tm),
        grid=(n // tm,),
        in_specs=[pl.BlockSpec((tm * TOP_K,), lambda i: (i,), memory_space=pltpu.SMEM),
                  pl.BlockSpec(memory_space=pl.ANY),
                  pl.BlockSpec((tm, LANES), lambda i: (i, 0)), full, gate,
                  pl.BlockSpec((1, D_MODEL), lambda i: (0, 0))],
        out_specs=full,
        out_shape=jax.ShapeDtypeStruct((n, D_MODEL), F32),
        scratch_shapes=[pltpu.VMEM((TOP_K, tm * SLAB, LANES), F32), pltpu.SemaphoreType.DMA((1,))],
        compiler_params=_params(1),
        name="finish",
    )(dest, rows, top, x1, mod, g_post)


def kernel(x_prompt, x_sample, state_ret, cache_win_k, cache_win_v, c_prompt, c_sample, w_ada, b_ada,
           g_pre_mix, g_post_mix, g_pre_ffn, g_post_ffn, w_in, g_ret, w_out, w_router, b_router,
           w_gate_up, b_gate_up, w_down, b_down):
    depth = w_in.shape[0]
    assert depth == 1, "single-layer step"
    n_b, seq, d = x_prompt.shape
    n_db, t_new, _ = x_sample.shape
    n_p, n_s = n_b * seq, n_db * t_new
    buf = cache_win_k.shape[2]
    keep = min(DIL_PATTERNS[-1][0], seq)
    vec = lambda g: g[0].reshape(1, -1)

    mod_p, mod_s = _ada(c_prompt, c_sample, w_ada[0], b_ada[0], t_new)
    mod_p = mod_p[:n_b].reshape(n_b, 1, -1)
    mod_s = mod_s.reshape(n_s, -1)

    w_in_bf = w_in[0].astype(BF16)
    w_out_bf = w_out[0].astype(BF16)
    xp = x_prompt.reshape(n_p, d)
    xs = x_sample.reshape(n_s, d)
    router = (w_router[0], b_router[0].reshape(1, -1))

    qr, kr, vr, gr, qa, ka, va = _in_proj(xp, mod_p, vec(g_pre_mix), w_in_bf, False, seq)
    ret_y_p, ret_state_p = _retention(qr, kr, vr, gr, None, g_ret[0], n_b, seq, RET_CHUNK)
    parts = [_att_prompt_pattern(qa, ka, va, n_b, seq, dil) for _, dil in DIL_PATTERNS]
    x1_p, h_p, top_p = _mixer_out(ret_y_p, ([o for o, _ in parts], [ml for _, ml in parts]), xp, mod_p,
                                  w_out_bf, vec(g_post_mix), vec(g_pre_ffn), *router, False, seq)

    qr_s, kr_s, vr_s, gr_s, qa_s, ka_s, va_s = _in_proj(xs, mod_s, vec(g_pre_mix), w_in_bf, True, t_new)
    ret_y_s, ret_state_s = _retention(qr_s, kr_s, vr_s, gr_s, state_ret[0], g_ret[0], n_db, t_new, t_new)
    att_s = _att_sample(qa_s, ka_s, va_s, cache_win_k[0], cache_win_v[0], n_db, t_new)
    x1_s, h_s, top_s = _mixer_out(ret_y_s, att_s, xs, mod_s, w_out_bf, vec(g_post_mix), vec(g_pre_ffn),
                                  *router, True, t_new)

    h_all = jnp.concatenate([h_p, h_s], axis=0)
    n_blocks = (n_p + n_s) * TOP_K // TM_MOE + N_EXPERTS
    dest, ends = _route(jnp.concatenate([top_p, top_s], axis=0), TM_MOE)
    dest = dest[:, :TOP_K].reshape(-1)
    ends = ends[0, :N_EXPERTS]
    block_e = jnp.minimum(jnp.sum(jnp.arange(n_blocks, dtype=jnp.int32)[:, None] >= ends[None, :], axis=1),
                          N_EXPERTS - 1).astype(jnp.int32)
    rows_in = _dispatch(h_all, dest, ends, n_blocks, TM_MOE)
    rows = _moe_blocks(rows_in, block_e, ends[N_EXPERTS - 1:], w_gate_up[0], w_down[0],
                       b_gate_up[0], b_down[0], TM_MOE)
    y_p = _finish(dest[:n_p * TOP_K], rows, top_p, x1_p, mod_p, vec(g_post_ffn), False, seq)
    y_s = _finish(dest[n_p * TOP_K:], rows, top_s, x1_s, mod_s, vec(g_post_ffn), True, t_new)

    win = lambda t: t.reshape(n_b, seq, N_HEADS, HEAD_DIM)[:, seq - keep:][None]
    new = lambda t: t.reshape(n_db, t_new, N_HEADS, HEAD_DIM)[None]
    return (y_p.reshape(n_b, seq, d), y_s.reshape(n_db, t_new, d), ret_state_p[None], ret_state_s[None],
            win(ka), win(va), new(ka_s), new(va_s))
```

```python
import functools

import numpy as np
import jax
import jax.numpy as jnp
from jax import lax
from jax.experimental import pallas as pl
from jax.experimental.pallas import tpu as pltpu

F32 = jnp.float32
BF16 = jnp.bfloat16

D_MODEL = 1024
HEAD_DIM = 64
N_HEADS = 8
D_HALF = N_HEADS * HEAD_DIM
N_SEG = 7
LANES = 128
N_PAIRS = D_HALF // LANES
RET_CHUNK = 128
ATT_BLOCK = 128
ATT_Q = 1024
ATT_BACK = 2
DIL_PATTERNS = ((128, 1), (512, 4), (2048, 16))
N_EXPERTS = 32
TOP_K = 4
SWIGLU_LIMIT = 7.0
SWIGLU_ALPHA = 1.702
EPS = 1e-6
NEG_INF = -1e30
MOD_SHIFT_MIX, MOD_SCALE_MIX, MOD_GATE_MIX, MOD_SHIFT_FFN, MOD_SCALE_FFN, MOD_GATE_FFN = range(6)

TM_IN = 512
TM_OUT = 256
TM_MOE = 256
MOE_COLS = 512
TM_FIN = 128
VMEM_LIMIT = 56 * 1024 * 1024


def _params(n_axes, vmem=None):
    return pltpu.CompilerParams(dimension_semantics=("arbitrary",) * n_axes, vmem_limit_bytes=vmem)


def _ret_log_decay():
    return np.log(1.0 - 2.0 ** (-5.0 - np.arange(N_HEADS, dtype=np.float64)))


def _alibi_slopes():
    return 2.0 ** (-8.0 * (np.arange(N_HEADS, dtype=np.float64) + 1.0) / N_HEADS)


def _per_head_lanes(a):
    return np.repeat(a, HEAD_DIM, axis=-1)


def _rms(x, g):
    return x * lax.rsqrt(jnp.mean(x * x, axis=-1, keepdims=True) + EPS) * g


def _dot(a, b):
    return jnp.dot(a, b, preferred_element_type=F32)


def _dot_nt(a, b):
    return lax.dot_general(a, b, (((1,), (1,)), ((), ())), preferred_element_type=F32)


def _dot_tn(a, b):
    return lax.dot_general(a, b, (((0,), (0,)), ((), ())), preferred_element_type=F32)


SLAB = D_MODEL // LANES


def _slab_rows(r):
    return pl.ds(pl.multiple_of(r * SLAB, SLAB), SLAB)


def _slab_load(ref, n):
    return jnp.concatenate([ref[pl.ds(s, n, stride=SLAB), :] for s in range(SLAB)], axis=1)


def _slab_store(ref, value):
    n = value.shape[0]
    for s in range(SLAB):
        ref[pl.ds(s, n, stride=SLAB), :] = value[:, s * LANES:(s + 1) * LANES]


def _mod_spec(tm, col, per_row, tiles_per_batch):
    if per_row:
        return pl.BlockSpec((tm, D_MODEL), lambda i: (i, col))
    return pl.BlockSpec((None, 1, D_MODEL), lambda i: (i // tiles_per_batch, 0, col))


def _ada_kernel(cp_ref, cs_ref, w_ref, b_ref, op_ref, os_ref):
    w = w_ref[...].astype(BF16)

    def mod(c):
        return _dot((c * jax.nn.sigmoid(c)).astype(BF16), w) + b_ref[...]

    op_ref[...] = mod(cp_ref[...])
    ms = mod(cs_ref[...])
    os_ref[...] = jnp.broadcast_to(ms[:, None, :], os_ref.shape)


def _ada(c_prompt, c_sample, w_ada, b_ada, t_new):
    n_b, d = c_prompt.shape
    n_db = c_sample.shape[0]
    n_out = w_ada.shape[1]
    cp = jnp.concatenate([c_prompt, jnp.zeros((-n_b % 8, d), F32)], axis=0)
    return pl.pallas_call(
        _ada_kernel,
        grid=(n_out // d,),
        in_specs=[pl.BlockSpec(cp.shape, lambda j: (0, 0)),
                  pl.BlockSpec((n_db, d), lambda j: (0, 0)),
                  pl.BlockSpec((d, d), lambda j: (0, j)),
                  pl.BlockSpec((1, d), lambda j: (0, j))],
        out_specs=[pl.BlockSpec(cp.shape, lambda j: (0, j)),
                   pl.BlockSpec((n_db, t_new, d), lambda j: (0, 0, j))],
        out_shape=[jax.ShapeDtypeStruct((cp.shape[0], n_out), F32),
                   jax.ShapeDtypeStruct((n_db, t_new, n_out), F32)],
        compiler_params=_params(1, VMEM_LIMIT),
        name="ada",
    )(cp, c_sample, w_ada, b_ada.reshape(1, n_out))


def _in_kernel(x_ref, sc_ref, sh_ref, g_ref, w_ref, *o_refs):
    h = (_rms(x_ref[...], g_ref[...]) * (1.0 + sc_ref[...]) + sh_ref[...]).astype(BF16)
    for s, o_ref in enumerate(o_refs[:N_SEG]):
        o_ref[...] = _dot(h, w_ref[:, s * D_HALF:(s + 1) * D_HALF])
    for o_ref, src in zip(o_refs[N_SEG:], o_refs[N_SEG - 3:N_SEG]):
        for p in range(N_PAIRS):
            o_ref[p] = src[:, p * LANES:(p + 1) * LANES]


def _in_proj(x, mod, g, w_bf, per_row, rows_per_batch, pair_major):
    n = x.shape[0]
    tm = min(TM_IN, n)
    tpb = 1 if per_row else rows_per_batch // tm
    seg = pl.BlockSpec((tm, D_HALF), lambda i: (i, 0))
    pm = pl.BlockSpec((N_PAIRS, tm, LANES), lambda i: (0, i, 0))
    n_pm = 3 if pair_major else 0
    return pl.pallas_call(
        _in_kernel,
        grid=(n // tm,),
        in_specs=[pl.BlockSpec((tm, D_MODEL), lambda i: (i, 0)),
                  _mod_spec(tm, MOD_SCALE_MIX, per_row, tpb), _mod_spec(tm, MOD_SHIFT_MIX, per_row, tpb),
                  pl.BlockSpec((1, D_MODEL), lambda i: (0, 0)),
                  pl.BlockSpec((D_MODEL, N_SEG * D_HALF), lambda i: (0, 0))],
        out_specs=[seg] * N_SEG + [pm] * n_pm,
        out_shape=[jax.ShapeDtypeStruct((n, D_HALF), F32)] * N_SEG
        + [jax.ShapeDtypeStruct((N_PAIRS, n, LANES), F32)] * n_pm,
        compiler_params=_params(1, VMEM_LIMIT),
        name="in_proj",
    )(x, mod, mod, g, w_bf)


def _ret_tables(chunk):
    lg = _ret_log_decay()
    pos = np.arange(chunk, dtype=np.float64)
    diff = pos[:, None] - pos[None, :]
    dmask = np.where(diff >= 0, np.exp(np.maximum(diff, 0.0)[None] * lg[:, None, None]), 0.0)
    row_decay = _per_head_lanes(np.exp((pos[:, None] + 1.0) * lg[None, :]))
    key_decay = _per_head_lanes(np.exp((chunk - 1.0 - pos)[:, None] * lg[None, :]))
    state_decay = _per_head_lanes(np.exp(chunk * lg)[None, :])
    return tuple(jnp.asarray(t, F32) for t in (dmask, row_decay, key_decay, state_decay))


def _ret_kernel(q_ref, k_ref, v_ref, gate_ref, s0_ref, gret_ref, dm_ref, rd_ref, kd_ref, sd_ref,
                y_ref, st_ref, state, *, chunk, has_init):
    c = pl.program_id(1)
    lo = lax.broadcasted_iota(jnp.int32, (chunk, LANES), 1) < HEAD_DIM
    blk_r = lax.broadcasted_iota(jnp.int32, (LANES, LANES), 0) < HEAD_DIM
    blk_c = lax.broadcasted_iota(jnp.int32, (LANES, LANES), 1) < HEAD_DIM
    same_head = blk_r == blk_c

    @pl.when(c == 0)
    def _():
        if has_init:
            zero = jnp.zeros((HEAD_DIM, HEAD_DIM), F32)
            for p in range(N_PAIRS):
                top = jnp.concatenate([s0_ref[0, 2 * p], zero], axis=1)
                bot = jnp.concatenate([zero, s0_ref[0, 2 * p + 1]], axis=1)
                state[p] = jnp.concatenate([top, bot], axis=0)
        else:
            state[...] = jnp.zeros_like(state)

    for p in range(N_PAIRS):
        sl = slice(p * LANES, (p + 1) * LANES)
        q = q_ref[:, sl]
        k = k_ref[:, sl] * HEAD_DIM ** -0.5
        vb = v_ref[:, sl].astype(BF16)
        kb = k.astype(BF16)
        inner = []
        for half in range(2):
            qh = jnp.where(lo if half == 0 else ~lo, q, 0.0).astype(BF16)
            scores = _dot_nt(qh, kb) * dm_ref[2 * p + half]
            inner.append(_dot(scores.astype(BF16), vb))
        s_old = state[p]
        cross = _dot(q.astype(BF16), s_old.astype(BF16)) * rd_ref[:, sl]
        o = jnp.where(lo, inner[0], inner[1]) + cross
        upd = _dot_tn((k * kd_ref[:, sl]).astype(BF16), vb)
        s_new = sd_ref[:, sl] * s_old + jnp.where(same_head, upd, 0.0)
        state[p] = s_new

        def head_mean(t):
            m_lo = jnp.sum(jnp.where(lo, t, 0.0), axis=-1, keepdims=True)
            m_hi = jnp.sum(jnp.where(lo, 0.0, t), axis=-1, keepdims=True)
            return jnp.where(lo, m_lo, m_hi) * (1.0 / HEAD_DIM)

        d = o - head_mean(o)
        n = d * lax.rsqrt(head_mean(d * d) + EPS) * gret_ref[:, sl]
        g = gate_ref[:, sl]
        y_ref[:, sl] = (g * jax.nn.sigmoid(g) * n).astype(y_ref.dtype)

        @pl.when(c == pl.num_programs(1) - 1)
        def _():
            st_ref[0, 2 * p] = s_new[:HEAD_DIM, :HEAD_DIM]
            st_ref[0, 2 * p + 1] = s_new[HEAD_DIM:, HEAD_DIM:]


def _retention(q, k, v, gate, state0, g_ret, n_batch, seq, chunk):
    nc = seq // chunk
    has_init = state0 is not None
    if state0 is None:
        state0 = jnp.zeros((1, N_HEADS, HEAD_DIM, HEAD_DIM), F32)
        s0_map = lambda b, c: (0, 0, 0, 0)
    else:
        s0_map = lambda b, c: (b, 0, 0, 0)
    dmask, row_decay, key_decay, state_decay = _ret_tables(chunk)
    rows = pl.BlockSpec((chunk, D_HALF), lambda b, c: (b * nc + c, 0))
    const2 = lambda shape: pl.BlockSpec(shape, lambda b, c: (0, 0))
    st_block = pl.BlockSpec((1, N_HEADS, HEAD_DIM, HEAD_DIM), lambda b, c: (b, 0, 0, 0))
    return pl.pallas_call(
        functools.partial(_ret_kernel, chunk=chunk, has_init=has_init),
        grid=(n_batch, nc),
        in_specs=[rows, rows, rows, rows,
                  pl.BlockSpec((1, N_HEADS, HEAD_DIM, HEAD_DIM), s0_map),
                  const2((1, D_HALF)),
                  pl.BlockSpec((N_HEADS, chunk, chunk), lambda b, c: (0, 0, 0)),
                  const2((chunk, D_HALF)), const2((chunk, D_HALF)), const2((1, D_HALF))],
        out_specs=[rows, st_block],
        out_shape=[jax.ShapeDtypeStruct((n_batch * seq, D_HALF), BF16),
                   jax.ShapeDtypeStruct((n_batch, N_HEADS, HEAD_DIM, HEAD_DIM), F32)],
        scratch_shapes=[pltpu.VMEM((N_PAIRS, LANES, LANES), F32)],
        compiler_params=_params(2),
        name="retention",
    )(q, k, v, gate, state0, g_ret.reshape(1, D_HALF), dmask, row_decay, key_decay, state_decay)


def _band_table(n_q, n_k, dilation):
    step = np.arange(n_q)[:, None] + (n_k - n_q) - np.arange(n_k)[None, :]
    in_win = (step >= 0) & (step <= ATT_BLOCK)
    bias = -_alibi_slopes()[:, None, None] * (step * dilation).astype(np.float64)[None]
    return jnp.asarray(np.where(in_win[None], bias, NEG_INF).reshape(N_PAIRS, 2 * n_q, n_k), F32)


def _band_attention(q, k, v, table_ref, key_ok, pair):
    n_q = q.shape[0]
    lo = lax.broadcasted_iota(jnp.int32, q.shape, 1) < HEAD_DIM
    q2 = jnp.concatenate([jnp.where(lo, q, 0.0), jnp.where(lo, 0.0, q)], axis=0).astype(BF16)
    s = _dot_nt(q2, k.astype(BF16)) * HEAD_DIM ** -0.5 + table_ref[pair]
    s = jnp.where(key_ok, s, NEG_INF)
    m = jnp.max(s, axis=-1, keepdims=True)
    e = jnp.exp(s - m)
    l = jnp.sum(e, axis=-1, keepdims=True)
    o = _dot(e.astype(BF16), v.astype(BF16))
    return tuple(jnp.where(lo, t[:n_q], t[n_q:]) for t in (o, m, l))


def _attp_kernel(*refs):
    n_back = ATT_BACK
    q_ref = refs[0]
    k_refs = refs[1:2 + n_back]
    v_refs = refs[2 + n_back:3 + 2 * n_back]
    t_refs = refs[3 + 2 * n_back:3 + 2 * n_back + len(DIL_PATTERNS)]
    o_ref, parts, kcat, vcat = refs[3 + 2 * n_back + len(DIL_PATTERNS):]
    n = pl.program_id(1)

    def key_col(n_q, n_k):
        return lax.broadcasted_iota(jnp.int32, (2 * n_q, n_k), 1)

    def fold(first, rows, news):
        if not first:
            olds = [[parts[kind, p, rows, :] for kind in range(3)] for p in range(N_PAIRS)]
            merged = []
            for (o_old, m_old, l_old), (o_new, m_new, l_new) in zip(olds, news):
                m = jnp.maximum(m_old, m_new)
                a, b = jnp.exp(m_old - m), jnp.exp(m_new - m)
                merged.append((o_old * a + o_new * b, m, l_old * a + l_new * b))
            news = merged
        for p, new in enumerate(news):
            for kind, val in enumerate(new):
                parts[kind, p, rows, :] = val

    for pat, (window, dil) in enumerate(DIL_PATTERNS):
        t_ref = t_refs[pat]
        if dil == 1:
            for cat, src in ((kcat, k_refs), (vcat, v_refs)):
                cat[:, :ATT_BLOCK, :] = src[1][:, ATT_Q - ATT_BLOCK:, :]
                cat[:, ATT_BLOCK:, :] = src[0][...]

            def body(sub, carry, t_ref=t_ref, first=pat == 0):
                r0 = pl.multiple_of(sub * ATT_BLOCK, ATT_BLOCK)
                ok = (key_col(ATT_BLOCK, 2 * ATT_BLOCK) >= ATT_BLOCK) | (n > 0) | (sub > 0)
                fold(first, pl.ds(r0, ATT_BLOCK),
                     [_band_attention(q_ref[p, pl.ds(r0, ATT_BLOCK), :], kcat[p, pl.ds(r0, 2 * ATT_BLOCK), :],
                                      vcat[p, pl.ds(r0, 2 * ATT_BLOCK), :], t_ref, ok, p) for p in range(N_PAIRS)])
                return carry

            lax.fori_loop(0, ATT_Q // ATT_BLOCK, body, 0)
            continue
        sub_rows = min(window, ATT_Q)
        n_q = sub_rows // dil
        n_prev = window // sub_rows
        for s in range(ATT_Q // sub_rows):
            pieces = []
            for i in range(n_prev, -1, -1):
                rel = (s - i) * sub_rows
                j = -(rel // ATT_Q)
                pieces.append((j, rel + j * ATT_Q))
            missing = sum(jnp.where(n < j, n_q, 0) for j, _ in pieces)

            def body(r, carry, dil=dil, n_q=n_q, t_ref=t_ref, first=pat == 0, pieces=pieces, missing=missing,
                     q_off=s * sub_rows):
                ok = key_col(n_q, len(pieces) * n_q) >= missing
                rows = pl.ds(q_off + r, n_q, stride=dil)
                gather = lambda refs, p: jnp.concatenate(
                    [refs[j][p, pl.ds(off + r, n_q, stride=dil), :] for j, off in pieces], axis=0)
                loaded = [(q_ref[p, rows, :], gather(k_refs, p), gather(v_refs, p)) for p in range(N_PAIRS)]
                fold(first, rows, [_band_attention(q, k, v, t_ref, ok, p) for p, (q, k, v) in enumerate(loaded)])
                return carry

            lax.fori_loop(0, dil, body, 0)

    for p in range(N_PAIRS):
        o_ref[:, p * LANES:(p + 1) * LANES] = (parts[0, p] / parts[2, p]).astype(o_ref.dtype)


def _att_prompt(q, k, v, n_batch, seq):
    nb = seq // ATT_Q
    assert all(w // d == ATT_BLOCK and ATT_Q % min(w, ATT_Q) == 0 and w % min(w, ATT_Q) == 0
               for w, d in DIL_PATTERNS)
    assert ATT_BACK == DIL_PATTERNS[-1][0] // ATT_Q and DIL_PATTERNS[0] == (ATT_BLOCK, 1)
    blk = lambda j: pl.BlockSpec((N_PAIRS, ATT_Q, LANES),
                                 lambda b, i: (0, b * nb + jnp.maximum(i - j, 0), 0))
    tables = [_band_table(ATT_BLOCK, 2 * ATT_BLOCK, 1)] + [
        _band_table(min(w, ATT_Q) // d, (w // min(w, ATT_Q) + 1) * (min(w, ATT_Q) // d), d)
        for w, d in DIL_PATTERNS[1:]]
    back = [blk(j) for j in range(ATT_BACK + 1)]
    return pl.pallas_call(
        _attp_kernel,
        grid=(n_batch, nb),
        in_specs=[blk(0)] + back + back + [pl.BlockSpec(t.shape, lambda b, i: (0, 0, 0)) for t in tables],
        out_specs=pl.BlockSpec((ATT_Q, D_HALF), lambda b, i: (b * nb + i, 0)),
        out_shape=jax.ShapeDtypeStruct((n_batch * seq, D_HALF), BF16),
        scratch_shapes=[pltpu.VMEM((3, N_PAIRS, ATT_Q, LANES), F32),
                        pltpu.VMEM((N_PAIRS, ATT_Q + ATT_BLOCK, LANES), F32),
                        pltpu.VMEM((N_PAIRS, ATT_Q + ATT_BLOCK, LANES), F32)],
        compiler_params=_params(2, VMEM_LIMIT),
        name="att_prompt",
    )(q, *([k] * (ATT_BACK + 1)), *([v] * (ATT_BACK + 1)), *tables)


def _atts_tables(buf, t_new):
    pos_q = buf + np.arange(t_new)
    pos_k = np.concatenate([np.arange(buf), buf + np.arange(LANES)])
    dist = pos_q[:, None] - pos_k[None, :]
    count = np.zeros(dist.shape)
    for window, dilation in DIL_PATTERNS:
        count += (dist >= 0) & (dist <= window) & (dist % dilation == 0)
    count[:, buf + t_new:] = 0
    bias = -_alibi_slopes()[:, None, None] * dist[None].astype(np.float64)
    table = np.where(count[None] > 0, bias + np.log(np.maximum(count, 1.0))[None], NEG_INF)
    table = table.reshape(N_HEADS * t_new, buf + LANES)
    return jnp.asarray(table[:, :buf], F32), jnp.asarray(table[:, buf:], F32)


def _atts_kernel(q_ref, kt_ref, kn_ref, vt_ref, vn_ref, tc_ref, tn_ref, o_ref, *, t_new):
    rows = N_HEADS * t_new
    row_head = lax.broadcasted_iota(jnp.int32, (rows, D_HALF), 0) // t_new
    col_head = lax.broadcasted_iota(jnp.int32, (rows, D_HALF), 1) // HEAD_DIM
    own = row_head == col_head
    q_heads = jnp.where(own, jnp.concatenate([q_ref[...]] * N_HEADS, axis=0), 0.0).astype(BF16)
    pad = jnp.zeros((LANES - t_new, D_HALF), F32)
    kn = jnp.concatenate([kn_ref[...], pad], axis=0).astype(BF16)
    vn = jnp.concatenate([vn_ref[...], pad], axis=0).astype(BF16)
    scale = HEAD_DIM ** -0.5
    sc = _dot(q_heads, kt_ref[...].astype(BF16)) * scale + tc_ref[...]
    sn = _dot_nt(q_heads, kn) * scale + tn_ref[...]
    m = jnp.maximum(jnp.max(sc, axis=-1, keepdims=True), jnp.max(sn, axis=-1, keepdims=True))
    ec = jnp.exp(sc - m)
    en = jnp.exp(sn - m)
    l = jnp.sum(ec, axis=-1, keepdims=True) + jnp.sum(en, axis=-1, keepdims=True)
    acc = _dot_nt(ec.astype(BF16), vt_ref[...].astype(BF16)) + _dot(en.astype(BF16), vn)
    acc = jnp.where(own, acc, 0.0) / l
    out = acc[:t_new]
    for h in range(1, N_HEADS):
        out = out + acc[h * t_new:(h + 1) * t_new]
    o_ref[...] = out.astype(o_ref.dtype)


def _att_sample(q, k_new, v_new, cache_k, cache_v, n_batch, t_new):
    buf = cache_k.shape[1]
    tc, tn = _atts_tables(buf, t_new)
    by_pos = lambda c: c.transpose(0, 2, 3, 1).reshape(n_batch, D_HALF, buf)
    rows = pl.BlockSpec((t_new, D_HALF), lambda b: (b, 0))
    cache = pl.BlockSpec((None, D_HALF, buf), lambda b: (b, 0, 0))
    return pl.pallas_call(
        functools.partial(_atts_kernel, t_new=t_new),
        grid=(n_batch,),
        in_specs=[rows, cache, rows, cache, rows,
                  pl.BlockSpec((N_HEADS * t_new, buf), lambda b: (0, 0)),
                  pl.BlockSpec((N_HEADS * t_new, LANES), lambda b: (0, 0))],
        out_specs=rows,
        out_shape=jax.ShapeDtypeStruct((n_batch * t_new, D_HALF), BF16),
        compiler_params=_params(1, VMEM_LIMIT),
        name="att_sample",
    )(q, by_pos(cache_k), k_new, by_pos(cache_v), v_new, tc, tn)


def _out_kernel(ret_ref, att_ref, x_ref, gt_ref, sc_ref, sh_ref, w_ref, gpost_ref, gpre_ref, wr_ref, br_ref,
                x1_ref, h_ref, top_ref):
    tm = x_ref.shape[0]
    mix = _dot(ret_ref[...], w_ref[:D_HALF, :]) + _dot(att_ref[...], w_ref[D_HALF:, :])
    x1 = x_ref[...] + gt_ref[...] * _rms(mix, gpost_ref[...])
    x1_ref[...] = x1
    h = _rms(x1, gpre_ref[...]) * (1.0 + sc_ref[...]) + sh_ref[...]
    _slab_store(h_ref, h)
    logits = jnp.dot(h, wr_ref[...], preferred_element_type=F32,
                     precision=lax.Precision.HIGHEST) + br_ref[...]
    lane_e = lax.broadcasted_iota(jnp.int32, (tm, N_EXPERTS), 1).astype(F32)
    lane_o = lax.broadcasted_iota(jnp.int32, (tm, LANES), 1)
    vals, idxs = [], []
    work = logits
    for _ in range(TOP_K):
        v = jnp.max(work, axis=-1, keepdims=True)
        i = jnp.min(jnp.where(work == v, lane_e, float(N_EXPERTS)), axis=-1, keepdims=True)
        vals.append(v)
        idxs.append(i)
        work = jnp.where(lane_e == i, -jnp.inf, work)
    es = [jnp.exp(v - vals[0]) for v in vals]
    tot = sum(es)
    packed = jnp.zeros((tm, LANES), F32)
    for j in range(TOP_K):
        packed = jnp.where(lane_o == j, idxs[j], packed)
        packed = jnp.where(lane_o == TOP_K + j, es[j] / tot, packed)
    top_ref[...] = packed


def _mixer_out(ret_y, att, x, mod, w_out_bf, g_post, g_pre, w_router, b_router, per_row, rows_per_batch):
    n = x.shape[0]
    tm = min(TM_OUT, n)
    tpb = 1 if per_row else rows_per_batch // tm
    half = pl.BlockSpec((tm, D_HALF), lambda i: (i, 0))
    full = pl.BlockSpec((tm, D_MODEL), lambda i: (i, 0))
    stat = pl.BlockSpec((tm, LANES), lambda i: (i, 0))
    vec = pl.BlockSpec((1, D_MODEL), lambda i: (0, 0))
    mods = [_mod_spec(tm, c, per_row, tpb) for c in (MOD_GATE_MIX, MOD_SCALE_FFN, MOD_SHIFT_FFN)]
    return pl.pallas_call(
        _out_kernel,
        grid=(n // tm,),
        in_specs=[half, half, full] + mods + [
                  pl.BlockSpec((D_MODEL, D_MODEL), lambda i: (0, 0)), vec, vec,
                  pl.BlockSpec((D_MODEL, N_EXPERTS), lambda i: (0, 0)),
                  pl.BlockSpec((1, N_EXPERTS), lambda i: (0, 0))],
        out_specs=[full, pl.BlockSpec((tm * SLAB, LANES), lambda i: (i, 0)), stat],
        out_shape=[jax.ShapeDtypeStruct((n, D_MODEL), F32),
                   jax.ShapeDtypeStruct((n * SLAB, LANES), F32),
                   jax.ShapeDtypeStruct((n, LANES), F32)],
        compiler_params=_params(1, VMEM_LIMIT),
        name="mixer_out",
    )(ret_y, att, x, mod, mod, mod, w_out_bf, g_post, g_pre, w_router, b_router)


def _route_kernel(top_ref, dest_ref, ends_ref, running, starts, *, tm, tm_blk):
    ph, i = pl.program_id(0), pl.program_id(1)
    lane_f = lax.broadcasted_iota(jnp.int32, (tm, LANES), 1).astype(F32)
    top = top_ref[...]
    picks = [lane_f == top[:, k:k + 1] for k in range(TOP_K)]
    chosen = sum(p.astype(F32) for p in picks)
    tile_counts = jnp.sum(chosen, axis=0, keepdims=True)

    @pl.when((ph == 0) & (i == 0))
    def _():
        running[...] = jnp.zeros_like(running)

    @pl.when(ph == 0)
    def _():
        running[...] += tile_counts

    @pl.when((ph == 1) & (i == 0))
    def _():
        lane = lax.broadcasted_iota(jnp.int32, (1, LANES), 1)
        assert tm_blk & (tm_blk - 1) == 0
        blocks = (running[...].astype(jnp.int32) + (tm_blk - 1)) >> (tm_blk.bit_length() - 1)
        ends = blocks
        shift = 1
        while shift < N_EXPERTS:
            ends = ends + jnp.where(lane >= shift, pltpu.roll(ends, shift, axis=1), 0)
            shift *= 2
        ends_ref[...] = ends
        starts[...] = ((ends - blocks) * tm_blk).astype(F32)
        running[...] = jnp.zeros_like(running)

    @pl.when(ph == 1)
    def _():
        earlier = (lax.broadcasted_iota(jnp.int32, (tm, tm), 0)
                   > lax.broadcasted_iota(jnp.int32, (tm, tm), 1))
        ahead = _dot(jnp.where(earlier, 1.0, 0.0).astype(BF16), chosen.astype(BF16))
        slot = starts[...] + running[...] + ahead
        lane_o = lax.broadcasted_iota(jnp.int32, (tm, LANES), 1)
        packed = jnp.zeros((tm, LANES), F32)
        for k, pick in enumerate(picks):
            dest_k = jnp.sum(jnp.where(pick, slot, 0.0), axis=-1, keepdims=True)
            packed = jnp.where(lane_o == k, dest_k, packed)
        dest_ref[...] = packed.astype(jnp.int32)
        running[...] += tile_counts


def _route(top, tm_blk):
    n = top.shape[0]
    tm = 512
    return pl.pallas_call(
        functools.partial(_route_kernel, tm=tm, tm_blk=tm_blk),
        grid=(2, n // tm),
        in_specs=[pl.BlockSpec((tm, LANES), lambda ph, i: (i, 0))],
        out_specs=[pl.BlockSpec((tm, LANES), lambda ph, i: (i * ph, 0)),
                   pl.BlockSpec((1, LANES), lambda ph, i: (0, 0))],
        out_shape=[jax.ShapeDtypeStruct((n, LANES), jnp.int32),
                   jax.ShapeDtypeStruct((1, LANES), jnp.int32)],
        scratch_shapes=[pltpu.VMEM((1, LANES), F32), pltpu.VMEM((1, LANES), F32)],
        compiler_params=_params(2),
        name="route",
    )(top)


def _dispatch_kernel(ends_ref, dest_ref, h_ref, xs_hbm, zeros, sem, *, tm, tm_blk, n_blocks):
    i = pl.program_id(0)

    def fill(b):
        rows = pl.ds(pl.multiple_of(b * (tm_blk * SLAB), tm_blk * SLAB), tm_blk * SLAB)
        return pltpu.make_async_copy(zeros, xs_hbm.at[rows], sem.at[1])

    def fill_start(b, carry):
        fill(b).start()
        return carry

    def fill_wait(b, carry):
        fill(b).wait()
        return carry

    @pl.when(i == 0)
    def _():
        zeros[...] = jnp.zeros_like(zeros)
        n_used = ends_ref[N_EXPERTS - 1]
        for do in (fill_start, fill_wait):
            for e in range(N_EXPERTS):
                first = ends_ref[e - 1] if e else 0

                @pl.when(ends_ref[e] > first)
                def _():
                    do(ends_ref[e] - 1, 0)

            lax.fori_loop(n_used, n_blocks, do, 0)

    def body(j, carry):
        for k in range(TOP_K):
            pltpu.make_async_copy(h_ref.at[_slab_rows(j)], xs_hbm.at[_slab_rows(dest_ref[j * TOP_K + k])],
                                  sem.at[0]).start(priority=k % 2)
        return carry

    lax.fori_loop(0, tm, body, 0)
    for k in range(TOP_K):
        pltpu.make_async_copy(h_ref, xs_hbm.at[pl.ds(0, tm * SLAB)], sem.at[0]).wait()


def _dispatch(h, dest_flat, ends, n_blocks, tm_blk):
    n = h.shape[0] // SLAB
    tm = 256
    grid_spec = pltpu.PrefetchScalarGridSpec(
        num_scalar_prefetch=1,
        grid=(n // tm,),
        in_specs=[pl.BlockSpec((tm * TOP_K,), lambda i, ends: (i,), memory_space=pltpu.SMEM),
                  pl.BlockSpec((tm * SLAB, LANES), lambda i, ends: (i, 0))],
        out_specs=pl.BlockSpec(memory_space=pl.ANY),
        scratch_shapes=[pltpu.VMEM((tm_blk * SLAB, LANES), F32), pltpu.SemaphoreType.DMA((2,))],
    )
    return pl.pallas_call(
        functools.partial(_dispatch_kernel, tm=tm, tm_blk=tm_blk, n_blocks=n_blocks),
        grid_spec=grid_spec,
        out_shape=jax.ShapeDtypeStruct((n_blocks * tm_blk * SLAB, LANES), F32),
        compiler_params=_params(1),
        name="dispatch",
    )(ends, dest_flat, h)


def _moe_kernel(be_ref, nu_ref, x_ref, wgu_ref, wd_ref, bgu_ref, bd_ref, o_ref, wgu_bf, wd_rows, wd_bf):
    i = pl.program_id(0)
    n_used = nu_ref[0]
    d_gu = wgu_ref.shape[1]
    new_expert = (i == 0) | (be_ref[i] != be_ref[jnp.maximum(i - 1, 0)])

    @pl.when((i < n_used) & new_expert)
    def _():
        wgu_bf[...] = wgu_ref[...].astype(BF16)
        for c in range(wd_rows.shape[0]):
            cols = slice(c * LANES, (c + 1) * LANES)
            for r in range(2):
                wd_rows[c, pl.ds(r, d_gu // 2, stride=2), :] = wd_ref[:, cols]
            wd_bf[:, cols] = wd_rows[c].astype(BF16)

    @pl.when(i < n_used)
    def _():
        x = _slab_load(x_ref, x_ref.shape[0] // SLAB).astype(BF16)
        even = (lax.broadcasted_iota(jnp.int32, (x.shape[0], MOE_COLS), 1) & 1) == 0
        acc = bd_ref[...]
        for c in range(d_gu // MOE_COLS):
            cols = slice(c * MOE_COLS, (c + 1) * MOE_COLS)
            gu = _dot(x, wgu_bf[:, cols]) + bgu_ref[:, cols]
            nxt = pltpu.roll(gu, MOE_COLS - 1, axis=1)
            gate = jnp.minimum(gu, SWIGLU_LIMIT)
            up = jnp.clip(nxt, -SWIGLU_LIMIT, SWIGLU_LIMIT)
            act = jnp.where(even, gate * jax.nn.sigmoid(SWIGLU_ALPHA * gate) * (up + 1.0), 0.0)
            acc = acc + _dot(act.astype(BF16), wd_bf[cols, :])
        _slab_store(o_ref, acc)

    @pl.when(i >= n_used)
    def _():
        o_ref[...] = jnp.zeros_like(o_ref)


def _moe_blocks(xs, block_e, n_used, w_gu, w_down, b_gu, b_down, tm):
    n_blocks = block_e.shape[0]
    n_e, d, d_gu = w_gu.shape
    wspec = lambda k, n: pl.BlockSpec((None, k, n), lambda i, be, nu: (be[i], 0, 0))
    grid_spec = pltpu.PrefetchScalarGridSpec(
        num_scalar_prefetch=2,
        grid=(n_blocks,),
        in_specs=[pl.BlockSpec((tm * SLAB, LANES), lambda i, be, nu: (jnp.minimum(i, nu[0] - 1), 0)),
                  wspec(d, d_gu), wspec(d_gu // 2, d), wspec(1, d_gu), wspec(1, d)],
        out_specs=pl.BlockSpec((tm * SLAB, LANES), lambda i, be, nu: (i, 0)),
        scratch_shapes=[pltpu.VMEM((d, d_gu), BF16), pltpu.VMEM((d // LANES, d_gu, LANES), F32),
                        pltpu.VMEM((d_gu, d), BF16)],
    )
    return pl.pallas_call(
        _moe_kernel,
        grid_spec=grid_spec,
        out_shape=jax.ShapeDtypeStruct((n_blocks * tm * SLAB, LANES), F32),
        compiler_params=_params(1, VMEM_LIMIT),
        name="moe_blocks",
    )(block_e, n_used, xs, w_gu, w_down, b_gu.reshape(n_e, 1, d_gu), b_down.reshape(n_e, 1, d))


def _fin_kernel(dest_ref, rows_hbm, top_ref, x1_ref, gt_ref, g_ref, o_ref, buf, sem, *, tm):
    def body(j, carry):
        for k in range(TOP_K):
            pltpu.make_async_copy(rows_hbm.at[_slab_rows(dest_ref[j * TOP_K + k])], buf.at[k, _slab_rows(j)],
                                  sem.at[0]).start(priority=k % 2)
        return carry
    lax.fori_loop(0, tm, body, 0)
    for k in range(TOP_K):
        pltpu.make_async_copy(rows_hbm.at[pl.ds(0, tm * SLAB)], buf.at[k], sem.at[0]).wait()
    top = top_ref[...]
    f = sum(_slab_load(buf.at[k], tm) * top[:, TOP_K + k:TOP_K + k + 1] for k in range(TOP_K))
    o_ref[...] = x1_ref[...] + gt_ref[...] * _rms(f, g_ref[...])


def _finish(dest, rows, top, x1, mod, g_post, per_row, rows_per_batch):
    n = x1.shape[0]
    tm = min(TM_FIN, n)
    full = pl.BlockSpec((tm, D_MODEL), lambda i: (i, 0))
    gate = _mod_spec(tm, MOD_GATE_FFN, per_row, 1 if per_row else rows_per_batch // tm)
    return pl.pallas_call(
        functools.partial(_fin_kernel, tm=tm),
        grid=(n // tm,),
        in_specs=[pl.BlockSpec((tm * TOP_K,), lambda i: (i,), memory_space=pltpu.SMEM),
                  pl.BlockSpec(memory_space=pl.ANY),
                  pl.BlockSpec((tm, LANES), lambda i: (i, 0)), full, gate,
                  pl.BlockSpec((1, D_MODEL), lambda i: (0, 0))],
        out_specs=full,
        out_shape=jax.ShapeDtypeStruct((n, D_MODEL), F32),
        scratch_shapes=[pltpu.VMEM((TOP_K, tm * SLAB, LANES), F32), pltpu.SemaphoreType.DMA((1,))],
        compiler_params=_params(1),
        name="finish",
    )(dest, rows, top, x1, mod, g_post)


def kernel(x_prompt, x_sample, state_ret, cache_win_k, cache_win_v, c_prompt, c_sample, w_ada, b_ada,
           g_pre_mix, g_post_mix, g_pre_ffn, g_post_ffn, w_in, g_ret, w_out, w_router, b_router,
           w_gate_up, b_gate_up, w_down, b_down):
    depth = w_in.shape[0]
    assert depth == 1, "single-layer step"
    n_b, seq, d = x_prompt.shape
    n_db, t_new, _ = x_sample.shape
    n_p, n_s = n_b * seq, n_db * t_new
    buf = cache_win_k.shape[2]
    keep = min(DIL_PATTERNS[-1][0], seq)
    vec = lambda g: g[0].reshape(1, -1)

    mod_p, mod_s = _ada(c_prompt, c_sample, w_ada[0], b_ada[0], t_new)
    mod_p = mod_p[:n_b].reshape(n_b, 1, -1)
    mod_s = mod_s.reshape(n_s, -1)

    w_in_bf = w_in[0].astype(BF16)
    w_out_bf = w_out[0].astype(BF16)
    xp = x_prompt.reshape(n_p, d)
    xs = x_sample.reshape(n_s, d)
    router = (w_router[0], b_router[0].reshape(1, -1))

    qr, kr, vr, gr, _, ka, va, qa_pm, ka_pm, va_pm = _in_proj(xp, mod_p, vec(g_pre_mix), w_in_bf, False, seq, True)
    ret_y_p, ret_state_p = _retention(qr, kr, vr, gr, None, g_ret[0], n_b, seq, RET_CHUNK)
    att_p = _att_prompt(qa_pm, ka_pm, va_pm, n_b, seq)
    x1_p, h_p, top_p = _mixer_out(ret_y_p, att_p, xp, mod_p, w_out_bf, vec(g_post_mix), vec(g_pre_ffn),
                                  *router, False, seq)

    qr_s, kr_s, vr_s, gr_s, qa_s, ka_s, va_s = _in_proj(xs, mod_s, vec(g_pre_mix), w_in_bf, True, t_new, False)
    ret_y_s, ret_state_s = _retention(qr_s, kr_s, vr_s, gr_s, state_ret[0], g_ret[0], n_db, t_new, t_new)
    att_s = _att_sample(qa_s, ka_s, va_s, cache_win_k[0], cache_win_v[0], n_db, t_new)
    x1_s, h_s, top_s = _mixer_out(ret_y_s, att_s, xs, mod_s, w_out_bf, vec(g_post_mix), vec(g_pre_ffn),
                                  *router, True, t_new)

    h_all = jnp.concatenate([h_p, h_s], axis=0)
    n_blocks = (n_p + n_s) * TOP_K // TM_MOE + N_EXPERTS
    dest, ends = _route(jnp.concatenate([top_p, top_s], axis=0), TM_MOE)
    dest = dest[:, :TOP_K].reshape(-1)
    ends = ends[0, :N_EXPERTS]
    block_e = jnp.minimum(jnp.sum(jnp.arange(n_blocks, dtype=jnp.int32)[:, None] >= ends[None, :], axis=1),
                          N_EXPERTS - 1).astype(jnp.int32)
    rows_in = _dispatch(h_all, dest, ends, n_blocks, TM_MOE)
    rows = _moe_blocks(rows_in, block_e, ends[N_EXPERTS - 1:], w_gate_up[0], w_down[0],
                       b_gate_up[0], b_down[0], TM_MOE)
    y_p = _finish(dest[:n_p * TOP_K], rows, top_p, x1_p, mod_p, vec(g_post_ffn), False, seq)
    y_s = _finish(dest[n_p * TOP_K:], rows, top_s, x1_s, mod_s, vec(g_post_ffn), True, t_new)

    win = lambda t: t.reshape(n_b, seq, N_HEADS, HEAD_DIM)[:, seq - keep:][None]
    new = lambda t: t.reshape(n_db, t_new, N_HEADS, HEAD_DIM)[None]
    return (y_p.reshape(n_b, seq, d), y_s.reshape(n_db, t_new, d), ret_state_p[None], ret_state_s[None],
            win(ka), win(va), new(ka_s), new(va_s))
```

```python
import functools

import numpy as np
import jax
import jax.numpy as jnp
from jax import lax
from jax.experimental import pallas as pl
from jax.experimental.pallas import tpu as pltpu

F32 = jnp.float32
BF16 = jnp.bfloat16

D_MODEL = 1024
HEAD_DIM = 64
N_HEADS = 8
D_HALF = N_HEADS * HEAD_DIM
N_SEG = 7
LANES = 128
N_PAIRS = D_HALF // LANES
RET_CHUNK = 128
ATT_BLOCK = 128
ATT_Q = 1024
ATT_BACK = 2
DIL_PATTERNS = ((128, 1), (512, 4), (2048, 16))
N_EXPERTS = 32
TOP_K = 4
SWIGLU_LIMIT = 7.0
SWIGLU_ALPHA = 1.702
EPS = 1e-6
NEG_INF = -1e30
MOD_SHIFT_MIX, MOD_SCALE_MIX, MOD_GATE_MIX, MOD_SHIFT_FFN, MOD_SCALE_FFN, MOD_GATE_FFN = range(6)

TM_IN = 512
TM_OUT = 256
TM_MOE = 512
MOE_COLS = 512
TM_FIN = 128
VMEM_LIMIT = 56 * 1024 * 1024


def _params(n_axes, vmem=None):
    return pltpu.CompilerParams(dimension_semantics=("arbitrary",) * n_axes, vmem_limit_bytes=vmem)


def _ret_log_decay():
    return np.log(1.0 - 2.0 ** (-5.0 - np.arange(N_HEADS, dtype=np.float64)))


def _alibi_slopes():
    return 2.0 ** (-8.0 * (np.arange(N_HEADS, dtype=np.float64) + 1.0) / N_HEADS)


def _per_head_lanes(a):
    return np.repeat(a, HEAD_DIM, axis=-1)


def _rms(x, g):
    return x * lax.rsqrt(jnp.mean(x * x, axis=-1, keepdims=True) + EPS) * g


def _dot(a, b):
    return jnp.dot(a, b, preferred_element_type=F32)


def _dot_nt(a, b):
    return lax.dot_general(a, b, (((1,), (1,)), ((), ())), preferred_element_type=F32)


def _dot_tn(a, b):
    return lax.dot_general(a, b, (((0,), (0,)), ((), ())), preferred_element_type=F32)


SLAB = D_MODEL // LANES


def _slab_rows(r):
    return pl.ds(pl.multiple_of(r * SLAB, SLAB), SLAB)


def _slab_load(ref, n):
    return jnp.concatenate([ref[pl.ds(s, n, stride=SLAB), :] for s in range(SLAB)], axis=1)


def _slab_store(ref, value):
    n = value.shape[0]
    for s in range(SLAB):
        ref[pl.ds(s, n, stride=SLAB), :] = value[:, s * LANES:(s + 1) * LANES]


def _mod_spec(tm, col, per_row, tiles_per_batch):
    if per_row:
        return pl.BlockSpec((tm, D_MODEL), lambda i: (i, col))
    return pl.BlockSpec((None, 1, D_MODEL), lambda i: (i // tiles_per_batch, 0, col))


def _ada_kernel(cp_ref, cs_ref, w_ref, b_ref, op_ref, os_ref):
    w = w_ref[...].astype(BF16)

    def mod(c):
        return _dot((c * jax.nn.sigmoid(c)).astype(BF16), w) + b_ref[...]

    op_ref[...] = mod(cp_ref[...])
    ms = mod(cs_ref[...])
    os_ref[...] = jnp.broadcast_to(ms[:, None, :], os_ref.shape)


def _ada(c_prompt, c_sample, w_ada, b_ada, t_new):
    n_b, d = c_prompt.shape
    n_db = c_sample.shape[0]
    n_out = w_ada.shape[1]
    cp = jnp.concatenate([c_prompt, jnp.zeros((-n_b % 8, d), F32)], axis=0)
    return pl.pallas_call(
        _ada_kernel,
        grid=(n_out // d,),
        in_specs=[pl.BlockSpec(cp.shape, lambda j: (0, 0)),
                  pl.BlockSpec((n_db, d), lambda j: (0, 0)),
                  pl.BlockSpec((d, d), lambda j: (0, j)),
                  pl.BlockSpec((1, d), lambda j: (0, j))],
        out_specs=[pl.BlockSpec(cp.shape, lambda j: (0, j)),
                   pl.BlockSpec((n_db, t_new, d), lambda j: (0, 0, j))],
        out_shape=[jax.ShapeDtypeStruct((cp.shape[0], n_out), F32),
                   jax.ShapeDtypeStruct((n_db, t_new, n_out), F32)],
        compiler_params=_params(1, VMEM_LIMIT),
        name="ada",
    )(cp, c_sample, w_ada, b_ada.reshape(1, n_out))


def _in_kernel(x_ref, sc_ref, sh_ref, g_ref, w_ref, *o_refs):
    h = (_rms(x_ref[...], g_ref[...]) * (1.0 + sc_ref[...]) + sh_ref[...]).astype(BF16)
    for s, o_ref in enumerate(o_refs[:N_SEG]):
        o_ref[...] = _dot(h, w_ref[:, s * D_HALF:(s + 1) * D_HALF])
    for o_ref, src in zip(o_refs[N_SEG:], o_refs[N_SEG - 3:N_SEG]):
        for p in range(N_PAIRS):
            o_ref[p] = src[:, p * LANES:(p + 1) * LANES]


def _in_proj(x, mod, g, w_bf, per_row, rows_per_batch, pair_major):
    n = x.shape[0]
    tm = min(TM_IN, n)
    tpb = 1 if per_row else rows_per_batch // tm
    seg = pl.BlockSpec((tm, D_HALF), lambda i: (i, 0))
    pm = pl.BlockSpec((N_PAIRS, tm, LANES), lambda i: (0, i, 0))
    n_pm = 3 if pair_major else 0
    return pl.pallas_call(
        _in_kernel,
        grid=(n // tm,),
        in_specs=[pl.BlockSpec((tm, D_MODEL), lambda i: (i, 0)),
                  _mod_spec(tm, MOD_SCALE_MIX, per_row, tpb), _mod_spec(tm, MOD_SHIFT_MIX, per_row, tpb),
                  pl.BlockSpec((1, D_MODEL), lambda i: (0, 0)),
                  pl.BlockSpec((D_MODEL, N_SEG * D_HALF), lambda i: (0, 0))],
        out_specs=[seg] * N_SEG + [pm] * n_pm,
        out_shape=[jax.ShapeDtypeStruct((n, D_HALF), F32)] * N_SEG
        + [jax.ShapeDtypeStruct((N_PAIRS, n, LANES), F32)] * n_pm,
        compiler_params=_params(1, VMEM_LIMIT),
        name="in_proj",
    )(x, mod, mod, g, w_bf)


def _ret_tables(chunk):
    lg = _ret_log_decay()
    pos = np.arange(chunk, dtype=np.float64)
    diff = pos[:, None] - pos[None, :]
    dmask = np.where(diff >= 0, np.exp(np.maximum(diff, 0.0)[None] * lg[:, None, None]), 0.0)
    dmask = dmask.reshape(N_PAIRS, 2 * chunk, chunk)
    row_decay = _per_head_lanes(np.exp((pos[:, None] + 1.0) * lg[None, :]))
    key_decay = _per_head_lanes(np.exp((chunk - 1.0 - pos)[:, None] * lg[None, :]))
    state_decay = _per_head_lanes(np.exp(chunk * lg)[None, :])
    return tuple(jnp.asarray(t, F32) for t in (dmask, row_decay, key_decay, state_decay))


def _ret_kernel(q_ref, k_ref, v_ref, gate_ref, s0_ref, gret_ref, dm_ref, rd_ref, kd_ref, sd_ref,
                y_ref, st_ref, state, *, chunk, has_init):
    c = pl.program_id(1)
    lo = lax.broadcasted_iota(jnp.int32, (chunk, LANES), 1) < HEAD_DIM
    blk_r = lax.broadcasted_iota(jnp.int32, (LANES, LANES), 0) < HEAD_DIM
    blk_c = lax.broadcasted_iota(jnp.int32, (LANES, LANES), 1) < HEAD_DIM
    same_head = blk_r == blk_c

    @pl.when(c == 0)
    def _():
        if has_init:
            zero = jnp.zeros((HEAD_DIM, HEAD_DIM), F32)
            for p in range(N_PAIRS):
                top = jnp.concatenate([s0_ref[0, 2 * p], zero], axis=1)
                bot = jnp.concatenate([zero, s0_ref[0, 2 * p + 1]], axis=1)
                state[p] = jnp.concatenate([top, bot], axis=0)
        else:
            state[...] = jnp.zeros_like(state)

    new_states = []
    for p in range(N_PAIRS):
        sl = slice(p * LANES, (p + 1) * LANES)
        q = q_ref[:, sl]
        k = k_ref[:, sl] * HEAD_DIM ** -0.5
        vb = v_ref[:, sl].astype(BF16)
        kb = k.astype(BF16)
        q2 = jnp.concatenate([jnp.where(lo, q, 0.0), jnp.where(lo, 0.0, q)], axis=0).astype(BF16)
        inner = _dot((_dot_nt(q2, kb) * dm_ref[p]).astype(BF16), vb)
        s_old = state[p]
        cross = _dot(q.astype(BF16), s_old.astype(BF16)) * rd_ref[:, sl]
        o = jnp.where(lo, inner[:chunk], inner[chunk:]) + cross
        upd = _dot_tn((k * kd_ref[:, sl]).astype(BF16), vb)
        s_new = sd_ref[:, sl] * s_old + jnp.where(same_head, upd, 0.0)
        state[p] = s_new
        new_states.append(s_new)

        def head_mean(t):
            m_lo = jnp.sum(jnp.where(lo, t, 0.0), axis=-1, keepdims=True)
            m_hi = jnp.sum(jnp.where(lo, 0.0, t), axis=-1, keepdims=True)
            return jnp.where(lo, m_lo, m_hi) * (1.0 / HEAD_DIM)

        d = o - head_mean(o)
        n = d * lax.rsqrt(head_mean(d * d) + EPS) * gret_ref[:, sl]
        g = gate_ref[:, sl]
        y_ref[:, sl] = (g * jax.nn.sigmoid(g) * n).astype(y_ref.dtype)

    @pl.when(c == pl.num_programs(1) - 1)
    def _():
        for p, s_new in enumerate(new_states):
            st_ref[0, 2 * p] = s_new[:HEAD_DIM, :HEAD_DIM]
            st_ref[0, 2 * p + 1] = s_new[HEAD_DIM:, HEAD_DIM:]


def _retention(q, k, v, gate, state0, g_ret, n_batch, seq, chunk):
    nc = seq // chunk
    has_init = state0 is not None
    if state0 is None:
        state0 = jnp.zeros((1, N_HEADS, HEAD_DIM, HEAD_DIM), F32)
        s0_map = lambda b, c: (0, 0, 0, 0)
    else:
        s0_map = lambda b, c: (b, 0, 0, 0)
    dmask, row_decay, key_decay, state_decay = _ret_tables(chunk)
    rows = pl.BlockSpec((chunk, D_HALF), lambda b, c: (b * nc + c, 0))
    const2 = lambda shape: pl.BlockSpec(shape, lambda b, c: (0, 0))
    st_block = pl.BlockSpec((1, N_HEADS, HEAD_DIM, HEAD_DIM), lambda b, c: (b, 0, 0, 0))
    return pl.pallas_call(
        functools.partial(_ret_kernel, chunk=chunk, has_init=has_init),
        grid=(n_batch, nc),
        in_specs=[rows, rows, rows, rows,
                  pl.BlockSpec((1, N_HEADS, HEAD_DIM, HEAD_DIM), s0_map),
                  const2((1, D_HALF)),
                  pl.BlockSpec((N_PAIRS, 2 * chunk, chunk), lambda b, c: (0, 0, 0)),
                  const2((chunk, D_HALF)), const2((chunk, D_HALF)), const2((1, D_HALF))],
        out_specs=[rows, st_block],
        out_shape=[jax.ShapeDtypeStruct((n_batch * seq, D_HALF), BF16),
                   jax.ShapeDtypeStruct((n_batch, N_HEADS, HEAD_DIM, HEAD_DIM), F32)],
        scratch_shapes=[pltpu.VMEM((N_PAIRS, LANES, LANES), F32)],
        compiler_params=_params(2),
        name="retention",
    )(q, k, v, gate, state0, g_ret.reshape(1, D_HALF), dmask, row_decay, key_decay, state_decay)


def _band_table(n_q, n_k, dilation):
    step = np.arange(n_q)[:, None] + (n_k - n_q) - np.arange(n_k)[None, :]
    in_win = (step >= 0) & (step <= ATT_BLOCK)
    bias = -_alibi_slopes()[:, None, None] * (step * dilation).astype(np.float64)[None]
    return jnp.asarray(np.where(in_win[None], bias, NEG_INF).reshape(N_PAIRS, 2 * n_q, n_k), F32)


def _band_attention(q, k, v, table_ref, key_ok, pair):
    n_q = q.shape[0]
    lo = lax.broadcasted_iota(jnp.int32, q.shape, 1) < HEAD_DIM
    q2 = jnp.concatenate([jnp.where(lo, q, 0.0), jnp.where(lo, 0.0, q)], axis=0).astype(BF16)
    s = _dot_nt(q2, k.astype(BF16)) * HEAD_DIM ** -0.5 + table_ref[pair]
    s = jnp.where(key_ok, s, NEG_INF)
    m = jnp.max(s, axis=-1, keepdims=True)
    e = jnp.exp(s - m)
    l = jnp.sum(e, axis=-1, keepdims=True)
    o = _dot(e.astype(BF16), v.astype(BF16))
    return tuple(jnp.where(lo, t[:n_q], t[n_q:]) for t in (o, m, l))


def _attp_kernel(*refs):
    n_back = ATT_BACK
    q_ref = refs[0]
    k_refs = refs[1:2 + n_back]
    v_refs = refs[2 + n_back:3 + 2 * n_back]
    t_refs = refs[3 + 2 * n_back:3 + 2 * n_back + len(DIL_PATTERNS)]
    o_ref, parts, kcat, vcat = refs[3 + 2 * n_back + len(DIL_PATTERNS):]
    n = pl.program_id(1)

    def key_col(n_q, n_k):
        return lax.broadcasted_iota(jnp.int32, (2 * n_q, n_k), 1)

    def fold(first, rows, news):
        if not first:
            olds = [[parts[kind, p, rows, :] for kind in range(3)] for p in range(N_PAIRS)]
            merged = []
            for (o_old, m_old, l_old), (o_new, m_new, l_new) in zip(olds, news):
                m = jnp.maximum(m_old, m_new)
                a, b = jnp.exp(m_old - m), jnp.exp(m_new - m)
                merged.append((o_old * a + o_new * b, m, l_old * a + l_new * b))
            news = merged
        for p, new in enumerate(news):
            for kind, val in enumerate(new):
                parts[kind, p, rows, :] = val

    for pat, (window, dil) in enumerate(DIL_PATTERNS):
        t_ref = t_refs[pat]
        if dil == 1:
            for cat, src in ((kcat, k_refs), (vcat, v_refs)):
                cat[:, :ATT_BLOCK, :] = src[1][:, ATT_Q - ATT_BLOCK:, :]
                cat[:, ATT_BLOCK:, :] = src[0][...]

            def body(sub, carry, t_ref=t_ref, first=pat == 0):
                r0 = pl.multiple_of(sub * ATT_BLOCK, ATT_BLOCK)
                ok = (key_col(ATT_BLOCK, 2 * ATT_BLOCK) >= ATT_BLOCK) | (n > 0) | (sub > 0)
                fold(first, pl.ds(r0, ATT_BLOCK),
                     [_band_attention(q_ref[p, pl.ds(r0, ATT_BLOCK), :], kcat[p, pl.ds(r0, 2 * ATT_BLOCK), :],
                                      vcat[p, pl.ds(r0, 2 * ATT_BLOCK), :], t_ref, ok, p) for p in range(N_PAIRS)])
                return carry

            lax.fori_loop(0, ATT_Q // ATT_BLOCK, body, 0)
            continue
        sub_rows = min(window, ATT_Q)
        n_q = sub_rows // dil
        n_prev = window // sub_rows
        for s in range(ATT_Q // sub_rows):
            pieces = []
            for i in range(n_prev, -1, -1):
                rel = (s - i) * sub_rows
                j = -(rel // ATT_Q)
                pieces.append((j, rel + j * ATT_Q))
            missing = sum(jnp.where(n < j, n_q, 0) for j, _ in pieces)

            def body(r, carry, dil=dil, n_q=n_q, t_ref=t_ref, first=pat == 0, pieces=pieces, missing=missing,
                     q_off=s * sub_rows):
                ok = key_col(n_q, len(pieces) * n_q) >= missing
                rows = pl.ds(q_off + r, n_q, stride=dil)
                gather = lambda refs, p: jnp.concatenate(
                    [refs[j][p, pl.ds(off + r, n_q, stride=dil), :] for j, off in pieces], axis=0)
                loaded = [(q_ref[p, rows, :], gather(k_refs, p), gather(v_refs, p)) for p in range(N_PAIRS)]
                fold(first, rows, [_band_attention(q, k, v, t_ref, ok, p) for p, (q, k, v) in enumerate(loaded)])
                return carry

            lax.fori_loop(0, dil, body, 0)

    for p in range(N_PAIRS):
        o_ref[:, p * LANES:(p + 1) * LANES] = (parts[0, p] / parts[2, p]).astype(o_ref.dtype)


def _att_prompt(q, k, v, n_batch, seq):
    nb = seq // ATT_Q
    assert all(w // d == ATT_BLOCK and ATT_Q % min(w, ATT_Q) == 0 and w % min(w, ATT_Q) == 0
               for w, d in DIL_PATTERNS)
    assert ATT_BACK == DIL_PATTERNS[-1][0] // ATT_Q and DIL_PATTERNS[0] == (ATT_BLOCK, 1)
    blk = lambda j: pl.BlockSpec((N_PAIRS, ATT_Q, LANES),
                                 lambda b, i: (0, b * nb + jnp.maximum(i - j, 0), 0))
    tables = [_band_table(ATT_BLOCK, 2 * ATT_BLOCK, 1)] + [
        _band_table(min(w, ATT_Q) // d, (w // min(w, ATT_Q) + 1) * (min(w, ATT_Q) // d), d)
        for w, d in DIL_PATTERNS[1:]]
    back = [blk(j) for j in range(ATT_BACK + 1)]
    return pl.pallas_call(
        _attp_kernel,
        grid=(n_batch, nb),
        in_specs=[blk(0)] + back + back + [pl.BlockSpec(t.shape, lambda b, i: (0, 0, 0)) for t in tables],
        out_specs=pl.BlockSpec((ATT_Q, D_HALF), lambda b, i: (b * nb + i, 0)),
        out_shape=jax.ShapeDtypeStruct((n_batch * seq, D_HALF), BF16),
        scratch_shapes=[pltpu.VMEM((3, N_PAIRS, ATT_Q, LANES), F32),
                        pltpu.VMEM((N_PAIRS, ATT_Q + ATT_BLOCK, LANES), F32),
                        pltpu.VMEM((N_PAIRS, ATT_Q + ATT_BLOCK, LANES), F32)],
        compiler_params=_params(2, VMEM_LIMIT),
        name="att_prompt",
    )(q, *([k] * (ATT_BACK + 1)), *([v] * (ATT_BACK + 1)), *tables)


def _atts_tables(buf, t_new):
    pos_q = buf + np.arange(t_new)
    pos_k = np.concatenate([np.arange(buf), buf + np.arange(LANES)])
    dist = pos_q[:, None] - pos_k[None, :]
    count = np.zeros(dist.shape)
    for window, dilation in DIL_PATTERNS:
        count += (dist >= 0) & (dist <= window) & (dist % dilation == 0)
    count[:, buf + t_new:] = 0
    bias = -_alibi_slopes()[:, None, None] * dist[None].astype(np.float64)
    table = np.where(count[None] > 0, bias + np.log(np.maximum(count, 1.0))[None], NEG_INF)
    table = table.reshape(N_HEADS * t_new, buf + LANES)
    return jnp.asarray(table[:, :buf], F32), jnp.asarray(table[:, buf:], F32)


def _atts_kernel(q_ref, kt_ref, kn_ref, vt_ref, vn_ref, tc_ref, tn_ref, o_ref, *, t_new):
    rows = N_HEADS * t_new
    row_head = lax.broadcasted_iota(jnp.int32, (rows, D_HALF), 0) // t_new
    col_head = lax.broadcasted_iota(jnp.int32, (rows, D_HALF), 1) // HEAD_DIM
    own = row_head == col_head
    q_heads = jnp.where(own, jnp.concatenate([q_ref[...]] * N_HEADS, axis=0), 0.0).astype(BF16)
    pad = jnp.zeros((LANES - t_new, D_HALF), F32)
    kn = jnp.concatenate([kn_ref[...], pad], axis=0).astype(BF16)
    vn = jnp.concatenate([vn_ref[...], pad], axis=0).astype(BF16)
    scale = HEAD_DIM ** -0.5
    sc = _dot(q_heads, kt_ref[...].astype(BF16)) * scale + tc_ref[...]
    sn = _dot_nt(q_heads, kn) * scale + tn_ref[...]
    m = jnp.maximum(jnp.max(sc, axis=-1, keepdims=True), jnp.max(sn, axis=-1, keepdims=True))
    ec = jnp.exp(sc - m)
    en = jnp.exp(sn - m)
    l = jnp.sum(ec, axis=-1, keepdims=True) + jnp.sum(en, axis=-1, keepdims=True)
    acc = _dot_nt(ec.astype(BF16), vt_ref[...].astype(BF16)) + _dot(en.astype(BF16), vn)
    acc = jnp.where(own, acc, 0.0) / l
    out = acc[:t_new]
    for h in range(1, N_HEADS):
        out = out + acc[h * t_new:(h + 1) * t_new]
    o_ref[...] = out.astype(o_ref.dtype)


def _att_sample(q, k_new, v_new, cache_k, cache_v, n_batch, t_new):
    buf = cache_k.shape[1]
    tc, tn = _atts_tables(buf, t_new)
    by_pos = lambda c: c.transpose(0, 2, 3, 1).reshape(n_batch, D_HALF, buf)
    rows = pl.BlockSpec((t_new, D_HALF), lambda b: (b, 0))
    cache = pl.BlockSpec((None, D_HALF, buf), lambda b: (b, 0, 0))
    return pl.pallas_call(
        functools.partial(_atts_kernel, t_new=t_new),
        grid=(n_batch,),
        in_specs=[rows, cache, rows, cache, rows,
                  pl.BlockSpec((N_HEADS * t_new, buf), lambda b: (0, 0)),
                  pl.BlockSpec((N_HEADS * t_new, LANES), lambda b: (0, 0))],
        out_specs=rows,
        out_shape=jax.ShapeDtypeStruct((n_batch * t_new, D_HALF), BF16),
        compiler_params=_params(1, VMEM_LIMIT),
        name="att_sample",
    )(q, by_pos(cache_k), k_new, by_pos(cache_v), v_new, tc, tn)


def _out_kernel(ret_ref, att_ref, x_ref, gt_ref, sc_ref, sh_ref, w_ref, gpost_ref, gpre_ref, wr_ref, br_ref,
                x1_ref, h_ref, top_ref):
    tm = x_ref.shape[0]
    mix = _dot(ret_ref[...], w_ref[:D_HALF, :]) + _dot(att_ref[...], w_ref[D_HALF:, :])
    x1 = x_ref[...] + gt_ref[...] * _rms(mix, gpost_ref[...])
    x1_ref[...] = x1
    h = _rms(x1, gpre_ref[...]) * (1.0 + sc_ref[...]) + sh_ref[...]
    _slab_store(h_ref, h)
    logits = jnp.dot(h, wr_ref[...], preferred_element_type=F32,
                     precision=lax.Precision.HIGHEST) + br_ref[...]
    lane_e = lax.broadcasted_iota(jnp.int32, (tm, N_EXPERTS), 1).astype(F32)
    lane_o = lax.broadcasted_iota(jnp.int32, (tm, LANES), 1)
    vals, idxs = [], []
    work = logits
    for _ in range(TOP_K):
        v = jnp.max(work, axis=-1, keepdims=True)
        i = jnp.min(jnp.where(work == v, lane_e, float(N_EXPERTS)), axis=-1, keepdims=True)
        vals.append(v)
        idxs.append(i)
        work = jnp.where(lane_e == i, -jnp.inf, work)
    es = [jnp.exp(v - vals[0]) for v in vals]
    tot = sum(es)
    packed = jnp.zeros((tm, LANES), F32)
    for j in range(TOP_K):
        packed = jnp.where(lane_o == j, idxs[j], packed)
        packed = jnp.where(lane_o == TOP_K + j, es[j] / tot, packed)
    top_ref[...] = packed


def _mixer_out(ret_y, att, x, mod, w_out_bf, g_post, g_pre, w_router, b_router, per_row, rows_per_batch):
    n = x.shape[0]
    tm = min(TM_OUT, n)
    tpb = 1 if per_row else rows_per_batch // tm
    half = pl.BlockSpec((tm, D_HALF), lambda i: (i, 0))
    full = pl.BlockSpec((tm, D_MODEL), lambda i: (i, 0))
    stat = pl.BlockSpec((tm, LANES), lambda i: (i, 0))
    vec = pl.BlockSpec((1, D_MODEL), lambda i: (0, 0))
    mods = [_mod_spec(tm, c, per_row, tpb) for c in (MOD_GATE_MIX, MOD_SCALE_FFN, MOD_SHIFT_FFN)]
    return pl.pallas_call(
        _out_kernel,
        grid=(n // tm,),
        in_specs=[half, half, full] + mods + [
                  pl.BlockSpec((D_MODEL, D_MODEL), lambda i: (0, 0)), vec, vec,
                  pl.BlockSpec((D_MODEL, N_EXPERTS), lambda i: (0, 0)),
                  pl.BlockSpec((1, N_EXPERTS), lambda i: (0, 0))],
        out_specs=[full, pl.BlockSpec((tm * SLAB, LANES), lambda i: (i, 0)), stat],
        out_shape=[jax.ShapeDtypeStruct((n, D_MODEL), F32),
                   jax.ShapeDtypeStruct((n * SLAB, LANES), F32),
                   jax.ShapeDtypeStruct((n, LANES), F32)],
        compiler_params=_params(1, VMEM_LIMIT),
        name="mixer_out",
    )(ret_y, att, x, mod, mod, mod, w_out_bf, g_post, g_pre, w_router, b_router)


def _route_kernel(top_ref, dest_ref, ends_ref, running, starts, *, tm, tm_blk):
    ph, i = pl.program_id(0), pl.program_id(1)
    lane_f = lax.broadcasted_iota(jnp.int32, (tm, LANES), 1).astype(F32)
    top = top_ref[...]
    picks = [lane_f == top[:, k:k + 1] for k in range(TOP_K)]
    chosen = sum(p.astype(F32) for p in picks)
    tile_counts = jnp.sum(chosen, axis=0, keepdims=True)

    @pl.when((ph == 0) & (i == 0))
    def _():
        running[...] = jnp.zeros_like(running)

    @pl.when(ph == 0)
    def _():
        running[...] += tile_counts

    @pl.when((ph == 1) & (i == 0))
    def _():
        lane = lax.broadcasted_iota(jnp.int32, (1, LANES), 1)
        assert tm_blk & (tm_blk - 1) == 0
        blocks = (running[...].astype(jnp.int32) + (tm_blk - 1)) >> (tm_blk.bit_length() - 1)
        ends = blocks
        shift = 1
        while shift < N_EXPERTS:
            ends = ends + jnp.where(lane >= shift, pltpu.roll(ends, shift, axis=1), 0)
            shift *= 2
        ends_ref[...] = ends
        starts[...] = ((ends - blocks) * tm_blk).astype(F32)
        running[...] = jnp.zeros_like(running)

    @pl.when(ph == 1)
    def _():
        earlier = (lax.broadcasted_iota(jnp.int32, (tm, tm), 0)
                   > lax.broadcasted_iota(jnp.int32, (tm, tm), 1))
        ahead = _dot(jnp.where(earlier, 1.0, 0.0).astype(BF16), chosen.astype(BF16))
        slot = starts[...] + running[...] + ahead
        lane_o = lax.broadcasted_iota(jnp.int32, (tm, LANES), 1)
        packed = jnp.zeros((tm, LANES), F32)
        for k, pick in enumerate(picks):
            dest_k = jnp.sum(jnp.where(pick, slot, 0.0), axis=-1, keepdims=True)
            packed = jnp.where(lane_o == k, dest_k, packed)
        dest_ref[...] = packed.astype(jnp.int32)
        running[...] += tile_counts


def _route(top, tm_blk):
    n = top.shape[0]
    tm = 512
    return pl.pallas_call(
        functools.partial(_route_kernel, tm=tm, tm_blk=tm_blk),
        grid=(2, n // tm),
        in_specs=[pl.BlockSpec((tm, LANES), lambda ph, i: (i, 0))],
        out_specs=[pl.BlockSpec((tm, LANES), lambda ph, i: (i * ph, 0)),
                   pl.BlockSpec((1, LANES), lambda ph, i: (0, 0))],
        out_shape=[jax.ShapeDtypeStruct((n, LANES), jnp.int32),
                   jax.ShapeDtypeStruct((1, LANES), jnp.int32)],
        scratch_shapes=[pltpu.VMEM((1, LANES), F32), pltpu.VMEM((1, LANES), F32)],
        compiler_params=_params(2),
        name="route",
    )(top)


def _dispatch_kernel(ends_ref, dest_ref, h_ref, xs_hbm, zeros, sem, *, tm, tm_blk, n_blocks):
    i = pl.program_id(0)

    def fill(b):
        rows = pl.ds(pl.multiple_of(b * (tm_blk * SLAB), tm_blk * SLAB), tm_blk * SLAB)
        return pltpu.make_async_copy(zeros, xs_hbm.at[rows], sem.at[1])

    def fill_start(b, carry):
        fill(b).start()
        return carry

    def fill_wait(b, carry):
        fill(b).wait()
        return carry

    @pl.when(i == 0)
    def _():
        zeros[...] = jnp.zeros_like(zeros)
        n_used = ends_ref[N_EXPERTS - 1]
        for do in (fill_start, fill_wait):
            for e in range(N_EXPERTS):
                first = ends_ref[e - 1] if e else 0

                @pl.when(ends_ref[e] > first)
                def _():
                    do(ends_ref[e] - 1, 0)

            lax.fori_loop(n_used, n_blocks, do, 0)

    def body(j, carry):
        for k in range(TOP_K):
            pltpu.make_async_copy(h_ref.at[_slab_rows(j)], xs_hbm.at[_slab_rows(dest_ref[j * TOP_K + k])],
                                  sem.at[0]).start(priority=k % 2)
        return carry

    lax.fori_loop(0, tm, body, 0)
    for k in range(TOP_K):
        pltpu.make_async_copy(h_ref, xs_hbm.at[pl.ds(0, tm * SLAB)], sem.at[0]).wait()


def _dispatch(h, dest_flat, ends, n_blocks, tm_blk):
    n = h.shape[0] // SLAB
    tm = 256
    grid_spec = pltpu.PrefetchScalarGridSpec(
        num_scalar_prefetch=1,
        grid=(n // tm,),
        in_specs=[pl.BlockSpec((tm * TOP_K,), lambda i, ends: (i,), memory_space=pltpu.SMEM),
                  pl.BlockSpec((tm * SLAB, LANES), lambda i, ends: (i, 0))],
        out_specs=pl.BlockSpec(memory_space=pl.ANY),
        scratch_shapes=[pltpu.VMEM((tm_blk * SLAB, LANES), F32), pltpu.SemaphoreType.DMA((2,))],
    )
    return pl.pallas_call(
        functools.partial(_dispatch_kernel, tm=tm, tm_blk=tm_blk, n_blocks=n_blocks),
        grid_spec=grid_spec,
        out_shape=jax.ShapeDtypeStruct((n_blocks * tm_blk * SLAB, LANES), F32),
        compiler_params=_params(1),
        name="dispatch",
    )(ends, dest_flat, h)


def _moe_kernel(be_ref, nu_ref, x_ref, wgu_ref, wd_ref, bgu_ref, bd_ref, o_ref, wgu_bf, wd_rows, wd_bf):
    i = pl.program_id(0)
    n_used = nu_ref[0]
    d_gu = wgu_ref.shape[1]
    new_expert = (i == 0) | (be_ref[i] != be_ref[jnp.maximum(i - 1, 0)])

    @pl.when((i < n_used) & new_expert)
    def _():
        wgu_bf[...] = wgu_ref[...].astype(BF16)
        for c in range(wd_ref.shape[1] // LANES):
            cols = slice(c * LANES, (c + 1) * LANES)
            for r in range(2):
                wd_rows[pl.ds(r, d_gu // 2, stride=2), :] = wd_ref[:, cols]
            wd_bf[:, cols] = wd_rows[...].astype(BF16)

    @pl.when(i < n_used)
    def _():
        x = _slab_load(x_ref, x_ref.shape[0] // SLAB).astype(BF16)
        even = (lax.broadcasted_iota(jnp.int32, (x.shape[0], MOE_COLS), 1) & 1) == 0
        acc = bd_ref[...]
        for c in range(d_gu // MOE_COLS):
            cols = slice(c * MOE_COLS, (c + 1) * MOE_COLS)
            gu = _dot(x, wgu_bf[:, cols]) + bgu_ref[:, cols]
            nxt = pltpu.roll(gu, MOE_COLS - 1, axis=1)
            gate = jnp.minimum(gu, SWIGLU_LIMIT)
            up = jnp.clip(nxt, -SWIGLU_LIMIT, SWIGLU_LIMIT)
            act = jnp.where(even, gate * jax.nn.sigmoid(SWIGLU_ALPHA * gate) * (up + 1.0), 0.0)
            acc = acc + _dot(act.astype(BF16), wd_bf[cols, :])
        _slab_store(o_ref, acc)

    @pl.when(i >= n_used)
    def _():
        o_ref[...] = jnp.zeros_like(o_ref)


def _moe_blocks(xs, block_e, n_used, w_gu, w_down, b_gu, b_down, tm):
    n_blocks = block_e.shape[0]
    n_e, d, d_gu = w_gu.shape
    wspec = lambda k, n: pl.BlockSpec((None, k, n), lambda i, be, nu: (be[i], 0, 0))
    grid_spec = pltpu.PrefetchScalarGridSpec(
        num_scalar_prefetch=2,
        grid=(n_blocks,),
        in_specs=[pl.BlockSpec((tm * SLAB, LANES), lambda i, be, nu: (jnp.minimum(i, nu[0] - 1), 0)),
                  wspec(d, d_gu), wspec(d_gu // 2, d), wspec(1, d_gu), wspec(1, d)],
        out_specs=pl.BlockSpec((tm * SLAB, LANES), lambda i, be, nu: (i, 0)),
        scratch_shapes=[pltpu.VMEM((d, d_gu), BF16), pltpu.VMEM((d_gu, LANES), F32),
                        pltpu.VMEM((d_gu, d), BF16)],
    )
    return pl.pallas_call(
        _moe_kernel,
        grid_spec=grid_spec,
        out_shape=jax.ShapeDtypeStruct((n_blocks * tm * SLAB, LANES), F32),
        compiler_params=_params(1, VMEM_LIMIT),
        name="moe_blocks",
    )(block_e, n_used, xs, w_gu, w_down, b_gu.reshape(n_e, 1, d_gu), b_down.reshape(n_e, 1, d))


def _fin_kernel(dest_ref, rows_hbm, top_ref, x1_ref, gt_ref, g_ref, o_ref, buf, sem, *, tm):
    def body(j, carry):
        for k in range(TOP_K):
            pltpu.make_async_copy(rows_hbm.at[_slab_rows(dest_ref[j * TOP_K + k])], buf.at[k, _slab_rows(j)],
                                  sem.at[0]).start(priority=k % 2)
        return carry
    lax.fori_loop(0, tm, body, 0)
    for k in range(TOP_K):
        pltpu.make_async_copy(rows_hbm.at[pl.ds(0, tm * SLAB)], buf.at[k], sem.at[0]).wait()
    top = top_ref[...]
    f = sum(_slab_load(buf.at[k], tm) * top[:, TOP_K + k:TOP_K + k + 1] for k in range(TOP_K))
    o_ref[...] = x1_ref[...] + gt_ref[...] * _rms(f, g_ref[...])


def _finish(dest, rows, top, x1, mod, g_post, per_row, rows_per_batch):
    n = x1.shape[0]
    tm = min(TM_FIN, n)
    full = pl.BlockSpec((tm, D_MODEL), lambda i: (i, 0))
    gate = _mod_spec(tm, MOD_GATE_FFN, per_row, 1 if per_row else rows_per_batch // tm)
    return pl.pallas_call(
        functools.partial(_fin_kernel, tm=tm),
        grid=(n // tm,),
        in_specs=[pl.BlockSpec((tm * TOP_K,), lambda i: (i,), memory_space=pltpu.SMEM),
                  pl.BlockSpec(memory_space=pl.ANY),
                  pl.BlockSpec((tm, LANES), lambda i: (i, 0)), full, gate,
                  pl.BlockSpec((1, D_MODEL), lambda i: (0, 0))],
        out_specs=full,
        out_shape=jax.ShapeDtypeStruct((n, D_MODEL), F32),
        scratch_shapes=[pltpu.VMEM((TOP_K, tm * SLAB, LANES), F32), pltpu.SemaphoreType.DMA((1,))],
        compiler_params=_params(1),
        name="finish",
    )(dest, rows, top, x1, mod, g_post)


def kernel(x_prompt, x_sample, state_ret, cache_win_k, cache_win_v, c_prompt, c_sample, w_ada, b_ada,
           g_pre_mix, g_post_mix, g_pre_ffn, g_post_ffn, w_in, g_ret, w_out, w_router, b_router,
           w_gate_up, b_gate_up, w_down, b_down):
    depth = w_in.shape[0]
    assert depth == 1, "single-layer step"
    n_b, seq, d = x_prompt.shape
    n_db, t_new, _ = x_sample.shape
    n_p, n_s = n_b * seq, n_db * t_new
    buf = cache_win_k.shape[2]
    keep = min(DIL_PATTERNS[-1][0], seq)
    vec = lambda g: g[0].reshape(1, -1)

    mod_p, mod_s = _ada(c_prompt, c_sample, w_ada[0], b_ada[0], t_new)
    mod_p = mod_p[:n_b].reshape(n_b, 1, -1)
    mod_s = mod_s.reshape(n_s, -1)

    w_in_bf = w_in[0].astype(BF16)
    w_out_bf = w_out[0].astype(BF16)
    xp = x_prompt.reshape(n_p, d)
    xs = x_sample.reshape(n_s, d)
    router = (w_router[0], b_router[0].reshape(1, -1))

    qr, kr, vr, gr, _, ka, va, qa_pm, ka_pm, va_pm = _in_proj(xp, mod_p, vec(g_pre_mix), w_in_bf, False, seq, True)
    ret_y_p, ret_state_p = _retention(qr, kr, vr, gr, None, g_ret[0], n_b, seq, RET_CHUNK)
    att_p = _att_prompt(qa_pm, ka_pm, va_pm, n_b, seq)
    x1_p, h_p, top_p = _mixer_out(ret_y_p, att_p, xp, mod_p, w_out_bf, vec(g_post_mix), vec(g_pre_ffn),
                                  *router, False, seq)

    qr_s, kr_s, vr_s, gr_s, qa_s, ka_s, va_s = _in_proj(xs, mod_s, vec(g_pre_mix), w_in_bf, True, t_new, False)
    ret_y_s, ret_state_s = _retention(qr_s, kr_s, vr_s, gr_s, state_ret[0], g_ret[0], n_db, t_new, t_new)
    att_s = _att_sample(qa_s, ka_s, va_s, cache_win_k[0], cache_win_v[0], n_db, t_new)
    x1_s, h_s, top_s = _mixer_out(ret_y_s, att_s, xs, mod_s, w_out_bf, vec(g_post_mix), vec(g_pre_ffn),
                                  *router, True, t_new)

    h_all = jnp.concatenate([h_p, h_s], axis=0)
    n_blocks = (n_p + n_s) * TOP_K // TM_MOE + N_EXPERTS
    dest, ends = _route(jnp.concatenate([top_p, top_s], axis=0), TM_MOE)
    dest = dest[:, :TOP_K].reshape(-1)
    ends = ends[0, :N_EXPERTS]
    block_e = jnp.minimum(jnp.sum(jnp.arange(n_blocks, dtype=jnp.int32)[:, None] >= ends[None, :], axis=1),
                          N_EXPERTS - 1).astype(jnp.int32)
    rows_in = _dispatch(h_all, dest, ends, n_blocks, TM_MOE)
    rows = _moe_blocks(rows_in, block_e, ends[N_EXPERTS - 1:], w_gate_up[0], w_down[0],
                       b_gate_up[0], b_down[0], TM_MOE)
    y_p = _finish(dest[:n_p * TOP_K], rows, top_p, x1_p, mod_p, vec(g_post_ffn), False, seq)
    y_s = _finish(dest[n_p * TOP_K:], rows, top_s, x1_s, mod_s, vec(g_post_ffn), True, t_new)

    win = lambda t: t.reshape(n_b, seq, N_HEADS, HEAD_DIM)[:, seq - keep:][None]
    new = lambda t: t.reshape(n_db, t_new, N_HEADS, HEAD_DIM)[None]
    return (y_p.reshape(n_b, seq, d), y_s.reshape(n_db, t_new, d), ret_state_p[None], ret_state_s[None],
            win(ka), win(va), new(ka_s), new(va_s))
```

```python
import functools

import numpy as np
import jax
import jax.numpy as jnp
from jax import lax
from jax.experimental import pallas as pl
from jax.experimental.pallas import tpu as pltpu

F32 = jnp.float32
BF16 = jnp.bfloat16

D_MODEL = 1024
HEAD_DIM = 64
N_HEADS = 8
D_HALF = N_HEADS * HEAD_DIM
N_SEG = 7
LANES = 128
N_PAIRS = D_HALF // LANES
RET_CHUNK = 128
ATT_BLOCK = 128
ATT_Q = 1024
ATT_BACK = 2
DIL_PATTERNS = ((128, 1), (512, 4), (2048, 16))
N_EXPERTS = 32
TOP_K = 4
SWIGLU_LIMIT = 7.0
SWIGLU_ALPHA = 1.702
EPS = 1e-6
NEG_INF = -1e30
MOD_SHIFT_MIX, MOD_SCALE_MIX, MOD_GATE_MIX, MOD_SHIFT_FFN, MOD_SCALE_FFN, MOD_GATE_FFN = range(6)

TM_IN = 512
TM_OUT = 256
TM_MOE = 512
MOE_COLS = 512
TM_FIN = 128
VMEM_LIMIT = 56 * 1024 * 1024


def _params(n_axes, vmem=None):
    return pltpu.CompilerParams(dimension_semantics=("arbitrary",) * n_axes, vmem_limit_bytes=vmem)


def _ret_log_decay():
    return np.log(1.0 - 2.0 ** (-5.0 - np.arange(N_HEADS, dtype=np.float64)))


def _alibi_slopes():
    return 2.0 ** (-8.0 * (np.arange(N_HEADS, dtype=np.float64) + 1.0) / N_HEADS)


def _per_head_lanes(a):
    return np.repeat(a, HEAD_DIM, axis=-1)


def _rms(x, g):
    return x * lax.rsqrt(jnp.mean(x * x, axis=-1, keepdims=True) + EPS) * g


def _dot(a, b):
    return jnp.dot(a, b, preferred_element_type=F32)


def _dot_nt(a, b):
    return lax.dot_general(a, b, (((1,), (1,)), ((), ())), preferred_element_type=F32)


def _dot_tn(a, b):
    return lax.dot_general(a, b, (((0,), (0,)), ((), ())), preferred_element_type=F32)


SLAB = D_MODEL // LANES


def _slab_rows(r):
    return pl.ds(pl.multiple_of(r * SLAB, SLAB), SLAB)


def _slab_load(ref, n):
    return jnp.concatenate([ref[pl.ds(s, n, stride=SLAB), :] for s in range(SLAB)], axis=1)


def _slab_store(ref, value):
    n = value.shape[0]
    for s in range(SLAB):
        ref[pl.ds(s, n, stride=SLAB), :] = value[:, s * LANES:(s + 1) * LANES]


def _mod_spec(tm, col, per_row, tiles_per_batch):
    if per_row:
        return pl.BlockSpec((tm, D_MODEL), lambda i: (i, col))
    return pl.BlockSpec((None, 1, D_MODEL), lambda i: (i // tiles_per_batch, 0, col))


def _ada_kernel(cp_ref, cs_ref, w_ref, b_ref, op_ref, os_ref):
    w = w_ref[...].astype(BF16)

    def mod(c):
        return _dot((c * jax.nn.sigmoid(c)).astype(BF16), w) + b_ref[...]

    op_ref[...] = mod(cp_ref[...])
    ms = mod(cs_ref[...])
    os_ref[...] = jnp.broadcast_to(ms[:, None, :], os_ref.shape)


def _ada(c_prompt, c_sample, w_ada, b_ada, t_new):
    n_b, d = c_prompt.shape
    n_db = c_sample.shape[0]
    n_out = w_ada.shape[1]
    cp = jnp.concatenate([c_prompt, jnp.zeros((-n_b % 8, d), F32)], axis=0)
    return pl.pallas_call(
        _ada_kernel,
        grid=(n_out // d,),
        in_specs=[pl.BlockSpec(cp.shape, lambda j: (0, 0)),
                  pl.BlockSpec((n_db, d), lambda j: (0, 0)),
                  pl.BlockSpec((d, d), lambda j: (0, j)),
                  pl.BlockSpec((1, d), lambda j: (0, j))],
        out_specs=[pl.BlockSpec(cp.shape, lambda j: (0, j)),
                   pl.BlockSpec((n_db, t_new, d), lambda j: (0, 0, j))],
        out_shape=[jax.ShapeDtypeStruct((cp.shape[0], n_out), F32),
                   jax.ShapeDtypeStruct((n_db, t_new, n_out), F32)],
        compiler_params=_params(1, VMEM_LIMIT),
        name="ada",
    )(cp, c_sample, w_ada, b_ada.reshape(1, n_out))


def _in_kernel(x_ref, sc_ref, sh_ref, g_ref, w_ref, *o_refs, n_plain, window):
    h = (_rms(x_ref[...], g_ref[...]) * (1.0 + sc_ref[...]) + sh_ref[...]).astype(BF16)
    segs = [_dot(h, w_ref[:, s * D_HALF:(s + 1) * D_HALF]) for s in range(N_SEG)]
    for o_ref, seg in zip(o_refs[:n_plain], segs):
        o_ref[...] = seg
    if n_plain == N_SEG:
        return
    for o_ref, seg in zip(o_refs[n_plain:N_SEG], segs[n_plain:]):
        for p in range(N_PAIRS):
            o_ref[p] = seg[:, p * LANES:(p + 1) * LANES]
    tiles_per_seq, first = window
    @pl.when(pl.program_id(0) % tiles_per_seq >= first)
    def _():
        for o_ref, seg in zip(o_refs[N_SEG:], segs[N_SEG - 2:]):
            o_ref[...] = seg.T


def _in_proj(x, mod, g, w_bf, per_row, rows_per_batch, keep=None):
    n = x.shape[0]
    tm = min(TM_IN, n)
    tpb = 1 if per_row else rows_per_batch // tm
    seg = pl.BlockSpec((tm, D_HALF), lambda i: (i, 0))
    seg_shape = jax.ShapeDtypeStruct((n, D_HALF), F32)
    if keep is None:
        n_plain, window = N_SEG, None
        out_specs, out_shape = [seg] * N_SEG, [seg_shape] * N_SEG
    else:
        n_plain, window = N_SEG - 3, (tpb, (rows_per_batch - keep) // tm)
        pm = pl.BlockSpec((N_PAIRS, tm, LANES), lambda i: (0, i, 0))
        win = pl.BlockSpec((None, D_HALF, tm), lambda i: (i // tpb, 0, jnp.maximum(i % tpb - window[1], 0)))
        out_specs = [seg] * n_plain + [pm] * 3 + [win] * 2
        out_shape = ([seg_shape] * n_plain + [jax.ShapeDtypeStruct((N_PAIRS, n, LANES), F32)] * 3
                     + [jax.ShapeDtypeStruct((n // rows_per_batch, D_HALF, keep), F32)] * 2)
    return pl.pallas_call(
        functools.partial(_in_kernel, n_plain=n_plain, window=window),
        grid=(n // tm,),
        in_specs=[pl.BlockSpec((tm, D_MODEL), lambda i: (i, 0)),
                  _mod_spec(tm, MOD_SCALE_MIX, per_row, tpb), _mod_spec(tm, MOD_SHIFT_MIX, per_row, tpb),
                  pl.BlockSpec((1, D_MODEL), lambda i: (0, 0)),
                  pl.BlockSpec((D_MODEL, N_SEG * D_HALF), lambda i: (0, 0))],
        out_specs=out_specs,
        out_shape=out_shape,
        compiler_params=_params(1, VMEM_LIMIT),
        name="in_proj",
    )(x, mod, mod, g, w_bf)


def _ret_tables(chunk):
    lg = _ret_log_decay()
    pos = np.arange(chunk, dtype=np.float64)
    diff = pos[:, None] - pos[None, :]
    dmask = np.where(diff >= 0, np.exp(np.maximum(diff, 0.0)[None] * lg[:, None, None]), 0.0)
    dmask = dmask.reshape(N_PAIRS, 2 * chunk, chunk)
    row_decay = _per_head_lanes(np.exp((pos[:, None] + 1.0) * lg[None, :]))
    key_decay = _per_head_lanes(np.exp((chunk - 1.0 - pos)[:, None] * lg[None, :]))
    state_decay = _per_head_lanes(np.exp(chunk * lg)[None, :])
    return tuple(jnp.asarray(t, F32) for t in (dmask, row_decay, key_decay, state_decay))


def _ret_kernel(q_ref, k_ref, v_ref, gate_ref, s0_ref, gret_ref, dm_ref, rd_ref, kd_ref, sd_ref,
                y_ref, st_ref, state, *, chunk, has_init):
    c = pl.program_id(1)
    lo = lax.broadcasted_iota(jnp.int32, (chunk, LANES), 1) < HEAD_DIM
    blk_r = lax.broadcasted_iota(jnp.int32, (LANES, LANES), 0) < HEAD_DIM
    blk_c = lax.broadcasted_iota(jnp.int32, (LANES, LANES), 1) < HEAD_DIM
    same_head = blk_r == blk_c

    @pl.when(c == 0)
    def _():
        if has_init:
            zero = jnp.zeros((HEAD_DIM, HEAD_DIM), F32)
            for p in range(N_PAIRS):
                top = jnp.concatenate([s0_ref[0, 2 * p], zero], axis=1)
                bot = jnp.concatenate([zero, s0_ref[0, 2 * p + 1]], axis=1)
                state[p] = jnp.concatenate([top, bot], axis=0)
        else:
            state[...] = jnp.zeros_like(state)

    new_states = []
    for p in range(N_PAIRS):
        sl = slice(p * LANES, (p + 1) * LANES)
        q = q_ref[:, sl]
        k = k_ref[:, sl] * HEAD_DIM ** -0.5
        vb = v_ref[:, sl].astype(BF16)
        kb = k.astype(BF16)
        q2 = jnp.concatenate([jnp.where(lo, q, 0.0), jnp.where(lo, 0.0, q)], axis=0).astype(BF16)
        inner = _dot((_dot_nt(q2, kb) * dm_ref[p]).astype(BF16), vb)
        s_old = state[p]
        cross = _dot(q.astype(BF16), s_old.astype(BF16)) * rd_ref[:, sl]
        o = jnp.where(lo, inner[:chunk], inner[chunk:]) + cross
        upd = _dot_tn((k * kd_ref[:, sl]).astype(BF16), vb)
        s_new = sd_ref[:, sl] * s_old + jnp.where(same_head, upd, 0.0)
        state[p] = s_new
        new_states.append(s_new)

        def head_mean(t):
            m_lo = jnp.sum(jnp.where(lo, t, 0.0), axis=-1, keepdims=True)
            m_hi = jnp.sum(jnp.where(lo, 0.0, t), axis=-1, keepdims=True)
            return jnp.where(lo, m_lo, m_hi) * (1.0 / HEAD_DIM)

        d = o - head_mean(o)
        n = d * lax.rsqrt(head_mean(d * d) + EPS) * gret_ref[:, sl]
        g = gate_ref[:, sl]
        y_ref[:, sl] = (g * jax.nn.sigmoid(g) * n).astype(y_ref.dtype)

    @pl.when(c == pl.num_programs(1) - 1)
    def _():
        for p, s_new in enumerate(new_states):
            st_ref[0, 2 * p] = s_new[:HEAD_DIM, :HEAD_DIM]
            st_ref[0, 2 * p + 1] = s_new[HEAD_DIM:, HEAD_DIM:]


def _retention(q, k, v, gate, state0, g_ret, n_batch, seq, chunk):
    nc = seq // chunk
    has_init = state0 is not None
    if state0 is None:
        state0 = jnp.zeros((1, N_HEADS, HEAD_DIM, HEAD_DIM), F32)
        s0_map = lambda b, c: (0, 0, 0, 0)
    else:
        s0_map = lambda b, c: (b, 0, 0, 0)
    dmask, row_decay, key_decay, state_decay = _ret_tables(chunk)
    rows = pl.BlockSpec((chunk, D_HALF), lambda b, c: (b * nc + c, 0))
    const2 = lambda shape: pl.BlockSpec(shape, lambda b, c: (0, 0))
    st_block = pl.BlockSpec((1, N_HEADS, HEAD_DIM, HEAD_DIM), lambda b, c: (b, 0, 0, 0))
    return pl.pallas_call(
        functools.partial(_ret_kernel, chunk=chunk, has_init=has_init),
        grid=(n_batch, nc),
        in_specs=[rows, rows, rows, rows,
                  pl.BlockSpec((1, N_HEADS, HEAD_DIM, HEAD_DIM), s0_map),
                  const2((1, D_HALF)),
                  pl.BlockSpec((N_PAIRS, 2 * chunk, chunk), lambda b, c: (0, 0, 0)),
                  const2((chunk, D_HALF)), const2((chunk, D_HALF)), const2((1, D_HALF))],
        out_specs=[rows, st_block],
        out_shape=[jax.ShapeDtypeStruct((n_batch * seq, D_HALF), BF16),
                   jax.ShapeDtypeStruct((n_batch, N_HEADS, HEAD_DIM, HEAD_DIM), F32)],
        scratch_shapes=[pltpu.VMEM((N_PAIRS, LANES, LANES), F32)],
        compiler_params=_params(2),
        name="retention",
    )(q, k, v, gate, state0, g_ret.reshape(1, D_HALF), dmask, row_decay, key_decay, state_decay)


def _band_table(n_q, n_k, dilation):
    step = np.arange(n_q)[:, None] + (n_k - n_q) - np.arange(n_k)[None, :]
    in_win = (step >= 0) & (step <= ATT_BLOCK)
    bias = -_alibi_slopes()[:, None, None] * (step * dilation).astype(np.float64)[None]
    return jnp.asarray(np.where(in_win[None], bias, NEG_INF).reshape(N_PAIRS, 2 * n_q, n_k), F32)


def _band_attention(q, k, v, table_ref, key_ok, pair):
    n_q = q.shape[0]
    lo = lax.broadcasted_iota(jnp.int32, q.shape, 1) < HEAD_DIM
    q2 = jnp.concatenate([jnp.where(lo, q, 0.0), jnp.where(lo, 0.0, q)], axis=0).astype(BF16)
    s = _dot_nt(q2, k.astype(BF16)) * HEAD_DIM ** -0.5 + table_ref[pair]
    s = jnp.where(key_ok, s, NEG_INF)
    m = jnp.max(s, axis=-1, keepdims=True)
    e = jnp.exp(s - m)
    l = jnp.sum(e, axis=-1, keepdims=True)
    o = _dot(e.astype(BF16), v.astype(BF16))
    return tuple(jnp.where(lo, t[:n_q], t[n_q:]) for t in (o, m, l))


def _attp_kernel(*refs):
    n_back = ATT_BACK
    q_ref = refs[0]
    k_refs = refs[1:2 + n_back]
    v_refs = refs[2 + n_back:3 + 2 * n_back]
    t_refs = refs[3 + 2 * n_back:3 + 2 * n_back + len(DIL_PATTERNS)]
    o_ref, parts, kcat, vcat = refs[3 + 2 * n_back + len(DIL_PATTERNS):]
    n = pl.program_id(1)

    def key_col(n_q, n_k):
        return lax.broadcasted_iota(jnp.int32, (2 * n_q, n_k), 1)

    def fold(first, rows, news):
        if not first:
            olds = [[parts[kind, p, rows, :] for kind in range(3)] for p in range(N_PAIRS)]
            merged = []
            for (o_old, m_old, l_old), (o_new, m_new, l_new) in zip(olds, news):
                m = jnp.maximum(m_old, m_new)
                a, b = jnp.exp(m_old - m), jnp.exp(m_new - m)
                merged.append((o_old * a + o_new * b, m, l_old * a + l_new * b))
            news = merged
        for p, new in enumerate(news):
            for kind, val in enumerate(new):
                parts[kind, p, rows, :] = val

    for pat, (window, dil) in enumerate(DIL_PATTERNS):
        t_ref = t_refs[pat]
        if dil == 1:
            for cat, src in ((kcat, k_refs), (vcat, v_refs)):
                cat[:, :ATT_BLOCK, :] = src[1][:, ATT_Q - ATT_BLOCK:, :]
                cat[:, ATT_BLOCK:, :] = src[0][...]

            def body(sub, carry, t_ref=t_ref, first=pat == 0):
                r0 = pl.multiple_of(sub * ATT_BLOCK, ATT_BLOCK)
                ok = (key_col(ATT_BLOCK, 2 * ATT_BLOCK) >= ATT_BLOCK) | (n > 0) | (sub > 0)
                fold(first, pl.ds(r0, ATT_BLOCK),
                     [_band_attention(q_ref[p, pl.ds(r0, ATT_BLOCK), :], kcat[p, pl.ds(r0, 2 * ATT_BLOCK), :],
                                      vcat[p, pl.ds(r0, 2 * ATT_BLOCK), :], t_ref, ok, p) for p in range(N_PAIRS)])
                return carry

            lax.fori_loop(0, ATT_Q // ATT_BLOCK, body, 0)
            continue
        sub_rows = min(window, ATT_Q)
        n_q = sub_rows // dil
        n_prev = window // sub_rows
        for s in range(ATT_Q // sub_rows):
            pieces = []
            for i in range(n_prev, -1, -1):
                rel = (s - i) * sub_rows
                j = -(rel // ATT_Q)
                pieces.append((j, rel + j * ATT_Q))
            missing = sum(jnp.where(n < j, n_q, 0) for j, _ in pieces)

            def body(r, carry, dil=dil, n_q=n_q, t_ref=t_ref, first=pat == 0, pieces=pieces, missing=missing,
                     q_off=s * sub_rows):
                ok = key_col(n_q, len(pieces) * n_q) >= missing
                rows = pl.ds(q_off + r, n_q, stride=dil)
                gather = lambda refs, p: jnp.concatenate(
                    [refs[j][p, pl.ds(off + r, n_q, stride=dil), :] for j, off in pieces], axis=0)
                loaded = [(q_ref[p, rows, :], gather(k_refs, p), gather(v_refs, p)) for p in range(N_PAIRS)]
                fold(first, rows, [_band_attention(q, k, v, t_ref, ok, p) for p, (q, k, v) in enumerate(loaded)])
                return carry

            lax.fori_loop(0, dil, body, 0)

    for p in range(N_PAIRS):
        o_ref[:, p * LANES:(p + 1) * LANES] = (parts[0, p] / parts[2, p]).astype(o_ref.dtype)


def _att_prompt(q, k, v, n_batch, seq):
    nb = seq // ATT_Q
    assert all(w // d == ATT_BLOCK and ATT_Q % min(w, ATT_Q) == 0 and w % min(w, ATT_Q) == 0
               for w, d in DIL_PATTERNS)
    assert ATT_BACK == DIL_PATTERNS[-1][0] // ATT_Q and DIL_PATTERNS[0] == (ATT_BLOCK, 1)
    blk = lambda j: pl.BlockSpec((N_PAIRS, ATT_Q, LANES),
                                 lambda b, i: (0, b * nb + jnp.maximum(i - j, 0), 0))
    tables = [_band_table(ATT_BLOCK, 2 * ATT_BLOCK, 1)] + [
        _band_table(min(w, ATT_Q) // d, (w // min(w, ATT_Q) + 1) * (min(w, ATT_Q) // d), d)
        for w, d in DIL_PATTERNS[1:]]
    back = [blk(j) for j in range(ATT_BACK + 1)]
    return pl.pallas_call(
        _attp_kernel,
        grid=(n_batch, nb),
        in_specs=[blk(0)] + back + back + [pl.BlockSpec(t.shape, lambda b, i: (0, 0, 0)) for t in tables],
        out_specs=pl.BlockSpec((ATT_Q, D_HALF), lambda b, i: (b * nb + i, 0)),
        out_shape=jax.ShapeDtypeStruct((n_batch * seq, D_HALF), BF16),
        scratch_shapes=[pltpu.VMEM((3, N_PAIRS, ATT_Q, LANES), F32),
                        pltpu.VMEM((N_PAIRS, ATT_Q + ATT_BLOCK, LANES), F32),
                        pltpu.VMEM((N_PAIRS, ATT_Q + ATT_BLOCK, LANES), F32)],
        compiler_params=_params(2, VMEM_LIMIT),
        name="att_prompt",
    )(q, *([k] * (ATT_BACK + 1)), *([v] * (ATT_BACK + 1)), *tables)


def _atts_tables(buf, t_new):
    pos_q = buf + np.arange(t_new)
    pos_k = np.concatenate([np.arange(buf), buf + np.arange(LANES)])
    dist = pos_q[:, None] - pos_k[None, :]
    count = np.zeros(dist.shape)
    for window, dilation in DIL_PATTERNS:
        count += (dist >= 0) & (dist <= window) & (dist % dilation == 0)
    count[:, buf + t_new:] = 0
    bias = -_alibi_slopes()[:, None, None] * dist[None].astype(np.float64)
    table = np.where(count[None] > 0, bias + np.log(np.maximum(count, 1.0))[None], NEG_INF)
    table = table.reshape(N_HEADS * t_new, buf + LANES)
    return jnp.asarray(table[:, :buf], F32), jnp.asarray(table[:, buf:], F32)


def _atts_kernel(q_ref, kt_ref, kn_ref, vt_ref, vn_ref, tc_ref, tn_ref, o_ref, *, t_new):
    rows = N_HEADS * t_new
    row_head = lax.broadcasted_iota(jnp.int32, (rows, D_HALF), 0) // t_new
    col_head = lax.broadcasted_iota(jnp.int32, (rows, D_HALF), 1) // HEAD_DIM
    own = row_head == col_head
    q_heads = jnp.where(own, jnp.concatenate([q_ref[...]] * N_HEADS, axis=0), 0.0).astype(BF16)
    pad = jnp.zeros((LANES - t_new, D_HALF), F32)
    kn = jnp.concatenate([kn_ref[...], pad], axis=0).astype(BF16)
    vn = jnp.concatenate([vn_ref[...], pad], axis=0).astype(BF16)
    scale = HEAD_DIM ** -0.5
    sc = _dot(q_heads, kt_ref[...].astype(BF16)) * scale + tc_ref[...]
    sn = _dot_nt(q_heads, kn) * scale + tn_ref[...]
    m = jnp.maximum(jnp.max(sc, axis=-1, keepdims=True), jnp.max(sn, axis=-1, keepdims=True))
    ec = jnp.exp(sc - m)
    en = jnp.exp(sn - m)
    l = jnp.sum(ec, axis=-1, keepdims=True) + jnp.sum(en, axis=-1, keepdims=True)
    acc = _dot_nt(ec.astype(BF16), vt_ref[...].astype(BF16)) + _dot(en.astype(BF16), vn)
    acc = jnp.where(own, acc, 0.0) / l
    out = acc[:t_new]
    for h in range(1, N_HEADS):
        out = out + acc[h * t_new:(h + 1) * t_new]
    o_ref[...] = out.astype(o_ref.dtype)


def _att_sample(q, k_new, v_new, cache_k, cache_v, n_batch, t_new):
    buf = cache_k.shape[1]
    tc, tn = _atts_tables(buf, t_new)
    by_pos = lambda c: c.transpose(0, 2, 3, 1).reshape(n_batch, D_HALF, buf)
    rows = pl.BlockSpec((t_new, D_HALF), lambda b: (b, 0))
    cache = pl.BlockSpec((None, D_HALF, buf), lambda b: (b, 0, 0))
    return pl.pallas_call(
        functools.partial(_atts_kernel, t_new=t_new),
        grid=(n_batch,),
        in_specs=[rows, cache, rows, cache, rows,
                  pl.BlockSpec((N_HEADS * t_new, buf), lambda b: (0, 0)),
                  pl.BlockSpec((N_HEADS * t_new, LANES), lambda b: (0, 0))],
        out_specs=rows,
        out_shape=jax.ShapeDtypeStruct((n_batch * t_new, D_HALF), BF16),
        compiler_params=_params(1, VMEM_LIMIT),
        name="att_sample",
    )(q, by_pos(cache_k), k_new, by_pos(cache_v), v_new, tc, tn)


def _out_kernel(ret_ref, att_ref, x_ref, gt_ref, sc_ref, sh_ref, w_ref, gpost_ref, gpre_ref, wr_ref, br_ref,
                x1_ref, h_ref, top_ref):
    tm = x_ref.shape[0]
    mix = _dot(ret_ref[...], w_ref[:D_HALF, :]) + _dot(att_ref[...], w_ref[D_HALF:, :])
    x1 = x_ref[...] + gt_ref[...] * _rms(mix, gpost_ref[...])
    x1_ref[...] = x1
    h = _rms(x1, gpre_ref[...]) * (1.0 + sc_ref[...]) + sh_ref[...]
    _slab_store(h_ref, h)
    logits = jnp.dot(h, wr_ref[...], preferred_element_type=F32,
                     precision=lax.Precision.HIGHEST) + br_ref[...]
    lane_e = lax.broadcasted_iota(jnp.int32, (tm, N_EXPERTS), 1).astype(F32)
    lane_o = lax.broadcasted_iota(jnp.int32, (tm, LANES), 1)
    vals, idxs = [], []
    work = logits
    for _ in range(TOP_K):
        v = jnp.max(work, axis=-1, keepdims=True)
        i = jnp.min(jnp.where(work == v, lane_e, float(N_EXPERTS)), axis=-1, keepdims=True)
        vals.append(v)
        idxs.append(i)
        work = jnp.where(lane_e == i, -jnp.inf, work)
    es = [jnp.exp(v - vals[0]) for v in vals]
    tot = sum(es)
    packed = jnp.zeros((tm, LANES), F32)
    for j in range(TOP_K):
        packed = jnp.where(lane_o == j, idxs[j], packed)
        packed = jnp.where(lane_o == TOP_K + j, es[j] / tot, packed)
    top_ref[...] = packed


def _mixer_out(ret_y, att, x, mod, w_out_bf, g_post, g_pre, w_router, b_router, per_row, rows_per_batch):
    n = x.shape[0]
    tm = min(TM_OUT, n)
    tpb = 1 if per_row else rows_per_batch // tm
    half = pl.BlockSpec((tm, D_HALF), lambda i: (i, 0))
    full = pl.BlockSpec((tm, D_MODEL), lambda i: (i, 0))
    stat = pl.BlockSpec((tm, LANES), lambda i: (i, 0))
    vec = pl.BlockSpec((1, D_MODEL), lambda i: (0, 0))
    mods = [_mod_spec(tm, c, per_row, tpb) for c in (MOD_GATE_MIX, MOD_SCALE_FFN, MOD_SHIFT_FFN)]
    return pl.pallas_call(
        _out_kernel,
        grid=(n // tm,),
        in_specs=[half, half, full] + mods + [
                  pl.BlockSpec((D_MODEL, D_MODEL), lambda i: (0, 0)), vec, vec,
                  pl.BlockSpec((D_MODEL, N_EXPERTS), lambda i: (0, 0)),
                  pl.BlockSpec((1, N_EXPERTS), lambda i: (0, 0))],
        out_specs=[full, pl.BlockSpec((tm * SLAB, LANES), lambda i: (i, 0)), stat],
        out_shape=[jax.ShapeDtypeStruct((n, D_MODEL), F32),
                   jax.ShapeDtypeStruct((n * SLAB, LANES), F32),
                   jax.ShapeDtypeStruct((n, LANES), F32)],
        compiler_params=_params(1, VMEM_LIMIT),
        name="mixer_out",
    )(ret_y, att, x, mod, mod, mod, w_out_bf, g_post, g_pre, w_router, b_router)


def _route_kernel(top_ref, dest_ref, ends_ref, running, starts, *, tm, tm_blk):
    ph, i = pl.program_id(0), pl.program_id(1)
    lane_f = lax.broadcasted_iota(jnp.int32, (tm, LANES), 1).astype(F32)
    top = top_ref[...]
    picks = [lane_f == top[:, k:k + 1] for k in range(TOP_K)]
    chosen = sum(p.astype(F32) for p in picks)
    tile_counts = jnp.sum(chosen, axis=0, keepdims=True)

    @pl.when((ph == 0) & (i == 0))
    def _():
        running[...] = jnp.zeros_like(running)

    @pl.when(ph == 0)
    def _():
        running[...] += tile_counts

    @pl.when((ph == 1) & (i == 0))
    def _():
        lane = lax.broadcasted_iota(jnp.int32, (1, LANES), 1)
        assert tm_blk & (tm_blk - 1) == 0
        blocks = (running[...].astype(jnp.int32) + (tm_blk - 1)) >> (tm_blk.bit_length() - 1)
        ends = blocks
        shift = 1
        while shift < N_EXPERTS:
            ends = ends + jnp.where(lane >= shift, pltpu.roll(ends, shift, axis=1), 0)
            shift *= 2
        ends_ref[...] = ends
        starts[...] = ((ends - blocks) * tm_blk).astype(F32)
        running[...] = jnp.zeros_like(running)

    @pl.when(ph == 1)
    def _():
        earlier = (lax.broadcasted_iota(jnp.int32, (tm, tm), 0)
                   > lax.broadcasted_iota(jnp.int32, (tm, tm), 1))
        ahead = _dot(jnp.where(earlier, 1.0, 0.0).astype(BF16), chosen.astype(BF16))
        slot = starts[...] + running[...] + ahead
        lane_o = lax.broadcasted_iota(jnp.int32, (tm, LANES), 1)
        packed = jnp.zeros((tm, LANES), F32)
        for k, pick in enumerate(picks):
            dest_k = jnp.sum(jnp.where(pick, slot, 0.0), axis=-1, keepdims=True)
            packed = jnp.where(lane_o == k, dest_k, packed)
        dest_ref[...] = packed.astype(jnp.int32)
        running[...] += tile_counts


def _route(top, tm_blk):
    n = top.shape[0]
    tm = 512
    return pl.pallas_call(
        functools.partial(_route_kernel, tm=tm, tm_blk=tm_blk),
        grid=(2, n // tm),
        in_specs=[pl.BlockSpec((tm, LANES), lambda ph, i: (i, 0))],
        out_specs=[pl.BlockSpec((tm, LANES), lambda ph, i: (i * ph, 0)),
                   pl.BlockSpec((1, LANES), lambda ph, i: (0, 0))],
        out_shape=[jax.ShapeDtypeStruct((n, LANES), jnp.int32),
                   jax.ShapeDtypeStruct((1, LANES), jnp.int32)],
        scratch_shapes=[pltpu.VMEM((1, LANES), F32), pltpu.VMEM((1, LANES), F32)],
        compiler_params=_params(2),
        name="route",
    )(top)


def _dispatch_kernel(ends_ref, dest_ref, *rest, tm, tm_blk, n_blocks, tiles):
    h_refs, (xs_hbm, zeros, sem) = rest[:-3], rest[-3:]
    i = pl.program_id(0)

    def fill(b):
        rows = pl.ds(pl.multiple_of(b * (tm_blk * SLAB), tm_blk * SLAB), tm_blk * SLAB)
        return pltpu.make_async_copy(zeros, xs_hbm.at[rows], sem.at[1])

    def fill_start(b, carry):
        fill(b).start()
        return carry

    def fill_wait(b, carry):
        fill(b).wait()
        return carry

    def zero_padding():
        zeros[...] = jnp.zeros_like(zeros)
        n_used = ends_ref[N_EXPERTS - 1]
        for do in (fill_start, fill_wait):
            for e in range(N_EXPERTS):
                first = ends_ref[e - 1] if e else 0

                @pl.when(ends_ref[e] > first)
                def _():
                    do(ends_ref[e] - 1, 0)

            lax.fori_loop(n_used, n_blocks, do, 0)

    pl.when(i == 0)(zero_padding)

    def scatter(h_ref):
        def body(j, carry):
            for k in range(TOP_K):
                pltpu.make_async_copy(h_ref.at[_slab_rows(j)], xs_hbm.at[_slab_rows(dest_ref[j * TOP_K + k])],
                                      sem.at[0]).start(priority=k % 2)
            return carry

        lax.fori_loop(0, tm, body, 0)
        for k in range(TOP_K):
            pltpu.make_async_copy(h_ref, xs_hbm.at[pl.ds(0, tm * SLAB)], sem.at[0]).wait()

    start = 0
    for h_ref, n_tiles in zip(h_refs, tiles):
        pl.when((i >= start) & (i < start + n_tiles))(functools.partial(scatter, h_ref))
        start += n_tiles


def _dispatch(hs, dest_flat, ends, n_blocks, tm_blk):
    tm = 256
    tiles = [h.shape[0] // SLAB // tm for h in hs]
    starts = [sum(tiles[:s]) for s in range(len(hs))]
    src = lambda s: pl.BlockSpec((tm * SLAB, LANES),
                                 lambda i, ends: (jnp.clip(i - starts[s], 0, tiles[s] - 1), 0))
    grid_spec = pltpu.PrefetchScalarGridSpec(
        num_scalar_prefetch=1,
        grid=(sum(tiles),),
        in_specs=[pl.BlockSpec((tm * TOP_K,), lambda i, ends: (i,), memory_space=pltpu.SMEM)]
        + [src(s) for s in range(len(hs))],
        out_specs=pl.BlockSpec(memory_space=pl.ANY),
        scratch_shapes=[pltpu.VMEM((tm_blk * SLAB, LANES), F32), pltpu.SemaphoreType.DMA((2,))],
    )
    return pl.pallas_call(
        functools.partial(_dispatch_kernel, tm=tm, tm_blk=tm_blk, n_blocks=n_blocks, tiles=tiles),
        grid_spec=grid_spec,
        out_shape=jax.ShapeDtypeStruct((n_blocks * tm_blk * SLAB, LANES), F32),
        compiler_params=_params(1),
        name="dispatch",
    )(ends, dest_flat, *hs)


def _moe_kernel(be_ref, nu_ref, x_ref, wgu_ref, wd_ref, bg_ref, bu_ref, bd_ref, o_ref, wt, wg_t, wu_t, wd_bf):
    i = pl.program_id(0)
    n_used = nu_ref[0]
    d, d_gu = wgu_ref.shape
    d_ff = d_gu // 2
    new_expert = (i == 0) | (be_ref[i] != be_ref[jnp.maximum(i - 1, 0)])

    @pl.when((i < n_used) & new_expert)
    def _():
        for c in range(d // LANES):
            rows = slice(c * LANES, (c + 1) * LANES)
            wt[...] = wgu_ref[rows, :].T
            wg_t[:, rows] = wt[pl.ds(0, d_ff, stride=2), :].astype(BF16)
            wu_t[:, rows] = wt[pl.ds(1, d_ff, stride=2), :].astype(BF16)
        wd_bf[...] = wd_ref[...].astype(BF16)

    @pl.when(i < n_used)
    def _():
        x = _slab_load(x_ref, x_ref.shape[0] // SLAB).astype(BF16)
        acc = bd_ref[...]
        for c in range(d_ff // MOE_COLS):
            cols = slice(c * MOE_COLS, (c + 1) * MOE_COLS)
            gate = jnp.minimum(_dot_nt(x, wg_t[cols, :]) + bg_ref[:, cols], SWIGLU_LIMIT)
            up = jnp.clip(_dot_nt(x, wu_t[cols, :]) + bu_ref[:, cols], -SWIGLU_LIMIT, SWIGLU_LIMIT)
            act = gate * jax.nn.sigmoid(SWIGLU_ALPHA * gate) * (up + 1.0)
            acc = acc + _dot(act.astype(BF16), wd_bf[cols, :])
        _slab_store(o_ref, acc)

    @pl.when(i >= n_used)
    def _():
        o_ref[...] = jnp.zeros_like(o_ref)


def _moe_blocks(xs, block_e, n_used, w_gu, w_down, b_gu, b_down, tm):
    n_blocks = block_e.shape[0]
    n_e, d, d_gu = w_gu.shape
    d_ff = d_gu // 2
    wspec = lambda k, n: pl.BlockSpec((None, k, n), lambda i, be, nu: (be[i], 0, 0))
    grid_spec = pltpu.PrefetchScalarGridSpec(
        num_scalar_prefetch=2,
        grid=(n_blocks,),
        in_specs=[pl.BlockSpec((tm * SLAB, LANES), lambda i, be, nu: (jnp.minimum(i, nu[0] - 1), 0)),
                  wspec(d, d_gu), wspec(d_ff, d), wspec(1, d_ff), wspec(1, d_ff), wspec(1, d)],
        out_specs=pl.BlockSpec((tm * SLAB, LANES), lambda i, be, nu: (i, 0)),
        scratch_shapes=[pltpu.VMEM((d_gu, LANES), F32), pltpu.VMEM((d_ff, d), BF16),
                        pltpu.VMEM((d_ff, d), BF16), pltpu.VMEM((d_ff, d), BF16)],
    )
    return pl.pallas_call(
        _moe_kernel,
        grid_spec=grid_spec,
        out_shape=jax.ShapeDtypeStruct((n_blocks * tm * SLAB, LANES), F32),
        compiler_params=_params(1, VMEM_LIMIT),
        name="moe_blocks",
    )(block_e, n_used, xs, w_gu, w_down, b_gu[:, None, 0::2], b_gu[:, None, 1::2], b_down.reshape(n_e, 1, d))


def _fin_kernel(dest_ref, rows_hbm, top_ref, x1_ref, gt_ref, g_ref, o_ref, buf, sem, *, tm):
    def body(j, carry):
        for k in range(TOP_K):
            pltpu.make_async_copy(rows_hbm.at[_slab_rows(dest_ref[j * TOP_K + k])], buf.at[k, _slab_rows(j)],
                                  sem.at[0]).start(priority=k % 2)
        return carry
    lax.fori_loop(0, tm, body, 0)
    for k in range(TOP_K):
        pltpu.make_async_copy(rows_hbm.at[pl.ds(0, tm * SLAB)], buf.at[k], sem.at[0]).wait()
    top = top_ref[...]
    f = sum(_slab_load(buf.at[k], tm) * top[:, TOP_K + k:TOP_K + k + 1] for k in range(TOP_K))
    o_ref[...] = x1_ref[...] + gt_ref[...] * _rms(f, g_ref[...])


def _finish(dest, rows, top, x1, mod, g_post, per_row, rows_per_batch):
    n = x1.shape[0]
    tm = min(TM_FIN, n)
    full = pl.BlockSpec((tm, D_MODEL), lambda i: (i, 0))
    gate = _mod_spec(tm, MOD_GATE_FFN, per_row, 1 if per_row else rows_per_batch // tm)
    return pl.pallas_call(
        functools.partial(_fin_kernel, tm=tm),
        grid=(n // tm,),
        in_specs=[pl.BlockSpec((tm * TOP_K,), lambda i: (i,), memory_space=pltpu.SMEM),
                  pl.BlockSpec(memory_space=pl.ANY),
                  pl.BlockSpec((tm, LANES), lambda i: (i, 0)), full, gate,
                  pl.BlockSpec((1, D_MODEL), lambda i: (0, 0))],
        out_specs=full,
        out_shape=jax.ShapeDtypeStruct((n, D_MODEL), F32),
        scratch_shapes=[pltpu.VMEM((TOP_K, tm * SLAB, LANES), F32), pltpu.SemaphoreType.DMA((1,))],
        compiler_params=_params(1),
        name="finish",
    )(dest, rows, top, x1, mod, g_post)


def kernel(x_prompt, x_sample, state_ret, cache_win_k, cache_win_v, c_prompt, c_sample, w_ada, b_ada,
           g_pre_mix, g_post_mix, g_pre_ffn, g_post_ffn, w_in, g_ret, w_out, w_router, b_router,
           w_gate_up, b_gate_up, w_down, b_down):
    depth = w_in.shape[0]
    assert depth == 1, "single-layer step"
    n_b, seq, d = x_prompt.shape
    n_db, t_new, _ = x_sample.shape
    n_p, n_s = n_b * seq, n_db * t_new
    buf = cache_win_k.shape[2]
    keep = min(DIL_PATTERNS[-1][0], seq)
    vec = lambda g: g[0].reshape(1, -1)

    mod_p, mod_s = _ada(c_prompt, c_sample, w_ada[0], b_ada[0], t_new)
    mod_p = mod_p[:n_b].reshape(n_b, 1, -1)
    mod_s = mod_s.reshape(n_s, -1)

    w_in_bf = w_in[0].astype(BF16)
    w_out_bf = w_out[0].astype(BF16)
    xp = x_prompt.reshape(n_p, d)
    xs = x_sample.reshape(n_s, d)
    router = (w_router[0], b_router[0].reshape(1, -1))

    qr, kr, vr, gr, qa_pm, ka_pm, va_pm, k_win, v_win = _in_proj(xp, mod_p, vec(g_pre_mix), w_in_bf, False, seq, keep)
    ret_y_p, ret_state_p = _retention(qr, kr, vr, gr, None, g_ret[0], n_b, seq, RET_CHUNK)
    att_p = _att_prompt(qa_pm, ka_pm, va_pm, n_b, seq)
    x1_p, h_p, top_p = _mixer_out(ret_y_p, att_p, xp, mod_p, w_out_bf, vec(g_post_mix), vec(g_pre_ffn),
                                  *router, False, seq)

    qr_s, kr_s, vr_s, gr_s, qa_s, ka_s, va_s = _in_proj(xs, mod_s, vec(g_pre_mix), w_in_bf, True, t_new)
    ret_y_s, ret_state_s = _retention(qr_s, kr_s, vr_s, gr_s, state_ret[0], g_ret[0], n_db, t_new, t_new)
    att_s = _att_sample(qa_s, ka_s, va_s, cache_win_k[0], cache_win_v[0], n_db, t_new)
    x1_s, h_s, top_s = _mixer_out(ret_y_s, att_s, xs, mod_s, w_out_bf, vec(g_post_mix), vec(g_pre_ffn),
                                  *router, True, t_new)

    n_blocks = (n_p + n_s) * TOP_K // TM_MOE + N_EXPERTS
    dest, ends = _route(jnp.concatenate([top_p, top_s], axis=0), TM_MOE)
    dest = dest[:, :TOP_K].reshape(-1)
    ends = ends[0, :N_EXPERTS]
    block_e = jnp.minimum(jnp.sum(jnp.arange(n_blocks, dtype=jnp.int32)[:, None] >= ends[None, :], axis=1),
                          N_EXPERTS - 1).astype(jnp.int32)
    rows_in = _dispatch([h_p, h_s], dest, ends, n_blocks, TM_MOE)
    rows = _moe_blocks(rows_in, block_e, ends[N_EXPERTS - 1:], w_gate_up[0], w_down[0],
                       b_gate_up[0], b_down[0], TM_MOE)
    y_p = _finish(dest[:n_p * TOP_K], rows, top_p, x1_p, mod_p, vec(g_post_ffn), False, seq)
    y_s = _finish(dest[n_p * TOP_K:], rows, top_s, x1_s, mod_s, vec(g_post_ffn), True, t_new)

    win = lambda t: t.reshape(n_b, N_HEADS, HEAD_DIM, keep).transpose(0, 3, 1, 2)[None]
    new = lambda t: t.reshape(n_db, t_new, N_HEADS, HEAD_DIM)[None]
    return (y_p.reshape(n_b, seq, d), y_s.reshape(n_db, t_new, d), ret_state_p[None], ret_state_s[None],
            win(k_win), win(v_win), new(ka_s), new(va_s))
```

```python
import functools

import numpy as np
import jax
import jax.numpy as jnp
from jax import lax
from jax.experimental import pallas as pl
from jax.experimental.pallas import tpu as pltpu

F32 = jnp.float32
BF16 = jnp.bfloat16

D_MODEL = 1024
HEAD_DIM = 64
N_HEADS = 8
D_HALF = N_HEADS * HEAD_DIM
N_SEG = 7
LANES = 128
N_PAIRS = D_HALF // LANES
RET_CHUNK = 128
RET_GROUP_SAMPLE = 8
ATT_BLOCK = 128
ATT_Q = 1024
ATT_BACK = 2
DIL_PATTERNS = ((128, 1), (512, 4), (2048, 16))
N_EXPERTS = 32
TOP_K = 4
SWIGLU_LIMIT = 7.0
SWIGLU_ALPHA = 1.702
EPS = 1e-6
NEG_INF = -1e30
MOD_SHIFT_MIX, MOD_SCALE_MIX, MOD_GATE_MIX, MOD_SHIFT_FFN, MOD_SCALE_FFN, MOD_GATE_FFN = range(6)

TM_IN = 512
TM_OUT = 256
TM_MOE = 512
MOE_COLS = 512
TM_FIN = 128
VMEM_LIMIT = 56 * 1024 * 1024


def _params(n_axes, vmem=None):
    return pltpu.CompilerParams(dimension_semantics=("arbitrary",) * n_axes, vmem_limit_bytes=vmem)


def _ret_log_decay():
    return np.log(1.0 - 2.0 ** (-5.0 - np.arange(N_HEADS, dtype=np.float64)))


def _alibi_slopes():
    return 2.0 ** (-8.0 * (np.arange(N_HEADS, dtype=np.float64) + 1.0) / N_HEADS)


def _per_head_lanes(a):
    return np.repeat(a, HEAD_DIM, axis=-1)


def _rms(x, g):
    return x * lax.rsqrt(jnp.mean(x * x, axis=-1, keepdims=True) + EPS) * g


def _dot(a, b):
    return jnp.dot(a, b, preferred_element_type=F32)


def _dot_nt(a, b):
    return lax.dot_general(a, b, (((1,), (1,)), ((), ())), preferred_element_type=F32)


def _dot_tn(a, b):
    return lax.dot_general(a, b, (((0,), (0,)), ((), ())), preferred_element_type=F32)


SLAB = D_MODEL // LANES


def _slab_rows(r):
    return pl.ds(pl.multiple_of(r * SLAB, SLAB), SLAB)


def _slab_load(ref, n):
    return jnp.concatenate([ref[pl.ds(s, n, stride=SLAB), :] for s in range(SLAB)], axis=1)


def _slab_store(ref, value):
    n = value.shape[0]
    for s in range(SLAB):
        ref[pl.ds(s, n, stride=SLAB), :] = value[:, s * LANES:(s + 1) * LANES]


def _mod_spec(tm, col, per_row, tiles_per_batch):
    if per_row:
        return pl.BlockSpec((tm, D_MODEL), lambda i: (i, col))
    return pl.BlockSpec((None, 1, D_MODEL), lambda i: (i // tiles_per_batch, 0, col))


def _ada_kernel(cp_ref, cs_ref, w_ref, b_ref, op_ref, os_ref):
    w = w_ref[...].astype(BF16)

    def mod(c):
        return _dot((c * jax.nn.sigmoid(c)).astype(BF16), w) + b_ref[...]

    op_ref[...] = mod(cp_ref[...])
    ms = mod(cs_ref[...])
    os_ref[...] = jnp.broadcast_to(ms[:, None, :], os_ref.shape)


def _ada(c_prompt, c_sample, w_ada, b_ada, t_new):
    n_b, d = c_prompt.shape
    n_db = c_sample.shape[0]
    n_out = w_ada.shape[1]
    cp = jnp.concatenate([c_prompt, jnp.zeros((-n_b % 8, d), F32)], axis=0)
    return pl.pallas_call(
        _ada_kernel,
        grid=(n_out // d,),
        in_specs=[pl.BlockSpec(cp.shape, lambda j: (0, 0)),
                  pl.BlockSpec((n_db, d), lambda j: (0, 0)),
                  pl.BlockSpec((d, d), lambda j: (0, j)),
                  pl.BlockSpec((1, d), lambda j: (0, j))],
        out_specs=[pl.BlockSpec(cp.shape, lambda j: (0, j)),
                   pl.BlockSpec((n_db, t_new, d), lambda j: (0, 0, j))],
        out_shape=[jax.ShapeDtypeStruct((cp.shape[0], n_out), F32),
                   jax.ShapeDtypeStruct((n_db, t_new, n_out), F32)],
        compiler_params=_params(1, VMEM_LIMIT),
        name="ada",
    )(cp, c_sample, w_ada, b_ada.reshape(1, n_out))


def _in_kernel(x_ref, sc_ref, sh_ref, g_ref, w_ref, *o_refs, n_plain, window):
    h = (_rms(x_ref[...], g_ref[...]) * (1.0 + sc_ref[...]) + sh_ref[...]).astype(BF16)
    segs = [_dot(h, w_ref[:, s * D_HALF:(s + 1) * D_HALF]) for s in range(N_SEG)]
    for o_ref, seg in zip(o_refs[:n_plain], segs):
        o_ref[...] = seg
    if n_plain == N_SEG:
        return
    for o_ref, seg in zip(o_refs[n_plain:N_SEG], segs[n_plain:]):
        for p in range(N_PAIRS):
            o_ref[p] = seg[:, p * LANES:(p + 1) * LANES]
    tiles_per_seq, first = window
    @pl.when(pl.program_id(0) % tiles_per_seq >= first)
    def _():
        for o_ref, seg in zip(o_refs[N_SEG:], segs[N_SEG - 2:]):
            o_ref[...] = seg.T


def _in_proj(x, mod, g, w_bf, per_row, rows_per_batch, keep=None):
    n = x.shape[0]
    tm = min(TM_IN, n)
    tpb = 1 if per_row else rows_per_batch // tm
    seg = pl.BlockSpec((tm, D_HALF), lambda i: (i, 0))
    seg_shape = jax.ShapeDtypeStruct((n, D_HALF), F32)
    if keep is None:
        n_plain, window = N_SEG, None
        out_specs, out_shape = [seg] * N_SEG, [seg_shape] * N_SEG
    else:
        n_plain, window = N_SEG - 3, (tpb, (rows_per_batch - keep) // tm)
        pm = pl.BlockSpec((N_PAIRS, tm, LANES), lambda i: (0, i, 0))
        win = pl.BlockSpec((None, D_HALF, tm), lambda i: (i // tpb, 0, jnp.maximum(i % tpb - window[1], 0)))
        out_specs = [seg] * n_plain + [pm] * 3 + [win] * 2
        out_shape = ([seg_shape] * n_plain + [jax.ShapeDtypeStruct((N_PAIRS, n, LANES), F32)] * 3
                     + [jax.ShapeDtypeStruct((n // rows_per_batch, D_HALF, keep), F32)] * 2)
    return pl.pallas_call(
        functools.partial(_in_kernel, n_plain=n_plain, window=window),
        grid=(n // tm,),
        in_specs=[pl.BlockSpec((tm, D_MODEL), lambda i: (i, 0)),
                  _mod_spec(tm, MOD_SCALE_MIX, per_row, tpb), _mod_spec(tm, MOD_SHIFT_MIX, per_row, tpb),
                  pl.BlockSpec((1, D_MODEL), lambda i: (0, 0)),
                  pl.BlockSpec((D_MODEL, N_SEG * D_HALF), lambda i: (0, 0))],
        out_specs=out_specs,
        out_shape=out_shape,
        compiler_params=_params(1, VMEM_LIMIT),
        name="in_proj",
    )(x, mod, mod, g, w_bf)


def _ret_tables(chunk):
    lg = _ret_log_decay()
    pos = np.arange(chunk, dtype=np.float64)
    diff = pos[:, None] - pos[None, :]
    dmask = np.where(diff >= 0, np.exp(np.maximum(diff, 0.0)[None] * lg[:, None, None]), 0.0)
    dmask = dmask.reshape(N_PAIRS, 2 * chunk, chunk)
    row_decay = _per_head_lanes(np.exp((pos[:, None] + 1.0) * lg[None, :]))
    key_decay = _per_head_lanes(np.exp((chunk - 1.0 - pos)[:, None] * lg[None, :]))
    state_decay = _per_head_lanes(np.exp(chunk * lg)[None, :])
    return tuple(jnp.asarray(t, F32) for t in (dmask, row_decay, key_decay, state_decay))


def _ret_kernel(q_ref, k_ref, v_ref, gate_ref, s0_ref, gret_ref, dm_ref, rd_ref, kd_ref, sd_ref,
                y_ref, st_ref, state, *, chunk, group, has_init):
    c = pl.program_id(1)
    lo = lax.broadcasted_iota(jnp.int32, (chunk, LANES), 1) < HEAD_DIM
    blk_r = lax.broadcasted_iota(jnp.int32, (LANES, LANES), 0) < HEAD_DIM
    blk_c = lax.broadcasted_iota(jnp.int32, (LANES, LANES), 1) < HEAD_DIM
    same_head = blk_r == blk_c

    @pl.when(c == 0)
    def _():
        if has_init:
            zero = jnp.zeros((HEAD_DIM, HEAD_DIM), F32)
            for b in range(group):
                for p in range(N_PAIRS):
                    top = jnp.concatenate([s0_ref[b, 2 * p], zero], axis=1)
                    bot = jnp.concatenate([zero, s0_ref[b, 2 * p + 1]], axis=1)
                    state[b * N_PAIRS + p] = jnp.concatenate([top, bot], axis=0)
        else:
            state[...] = jnp.zeros_like(state)

    new_states = []
    for b, p in [(b, p) for b in range(group) for p in range(N_PAIRS)]:
        sl = slice(p * LANES, (p + 1) * LANES)
        q = q_ref[b, :, sl]
        k = k_ref[b, :, sl] * HEAD_DIM ** -0.5
        vb = v_ref[b, :, sl].astype(BF16)
        kb = k.astype(BF16)
        q2 = jnp.concatenate([jnp.where(lo, q, 0.0), jnp.where(lo, 0.0, q)], axis=0).astype(BF16)
        inner = _dot((_dot_nt(q2, kb) * dm_ref[p]).astype(BF16), vb)
        s_old = state[b * N_PAIRS + p]
        cross = _dot(q.astype(BF16), s_old.astype(BF16)) * rd_ref[:, sl]
        o = jnp.where(lo, inner[:chunk], inner[chunk:]) + cross
        upd = _dot_tn((k * kd_ref[:, sl]).astype(BF16), vb)
        s_new = sd_ref[:, sl] * s_old + jnp.where(same_head, upd, 0.0)
        state[b * N_PAIRS + p] = s_new
        new_states.append(s_new)

        def head_mean(t):
            m_lo = jnp.sum(jnp.where(lo, t, 0.0), axis=-1, keepdims=True)
            m_hi = jnp.sum(jnp.where(lo, 0.0, t), axis=-1, keepdims=True)
            return jnp.where(lo, m_lo, m_hi) * (1.0 / HEAD_DIM)

        d = o - head_mean(o)
        n = d * lax.rsqrt(head_mean(d * d) + EPS) * gret_ref[:, sl]
        g = gate_ref[b, :, sl]
        y_ref[b, :, sl] = (g * jax.nn.sigmoid(g) * n).astype(y_ref.dtype)

    @pl.when(c == pl.num_programs(1) - 1)
    def _():
        for bp, s_new in enumerate(new_states):
            b, p = divmod(bp, N_PAIRS)
            st_ref[b, 2 * p] = s_new[:HEAD_DIM, :HEAD_DIM]
            st_ref[b, 2 * p + 1] = s_new[HEAD_DIM:, HEAD_DIM:]


def _retention(q, k, v, gate, state0, g_ret, n_batch, seq, chunk, group):
    nc = seq // chunk
    by_seq = lambda t: t.reshape(n_batch, seq, D_HALF)
    has_init = state0 is not None
    if state0 is None:
        state0 = jnp.zeros((group, N_HEADS, HEAD_DIM, HEAD_DIM), F32)
        s0_map = lambda b, c: (0, 0, 0, 0)
    else:
        s0_map = lambda b, c: (b, 0, 0, 0)
    dmask, row_decay, key_decay, state_decay = _ret_tables(chunk)
    rows = pl.BlockSpec((group, chunk, D_HALF), lambda b, c: (b, c, 0))
    const2 = lambda shape: pl.BlockSpec(shape, lambda b, c: (0, 0))
    st_block = pl.BlockSpec((group, N_HEADS, HEAD_DIM, HEAD_DIM), lambda b, c: (b, 0, 0, 0))
    y, state = pl.pallas_call(
        functools.partial(_ret_kernel, chunk=chunk, group=group, has_init=has_init),
        grid=(n_batch // group, nc),
        in_specs=[rows, rows, rows, rows,
                  pl.BlockSpec((group, N_HEADS, HEAD_DIM, HEAD_DIM), s0_map),
                  const2((1, D_HALF)),
                  pl.BlockSpec((N_PAIRS, 2 * chunk, chunk), lambda b, c: (0, 0, 0)),
                  const2((chunk, D_HALF)), const2((chunk, D_HALF)), const2((1, D_HALF))],
        out_specs=[rows, st_block],
        out_shape=[jax.ShapeDtypeStruct((n_batch, seq, D_HALF), BF16),
                   jax.ShapeDtypeStruct((n_batch, N_HEADS, HEAD_DIM, HEAD_DIM), F32)],
        scratch_shapes=[pltpu.VMEM((group * N_PAIRS, LANES, LANES), F32)],
        compiler_params=_params(2),
        name="retention",
    )(by_seq(q), by_seq(k), by_seq(v), by_seq(gate), state0, g_ret.reshape(1, D_HALF), dmask, row_decay,
      key_decay, state_decay)
    return y.reshape(n_batch * seq, D_HALF), state


def _band_table(n_q, n_k, dilation):
    step = np.arange(n_q)[:, None] + (n_k - n_q) - np.arange(n_k)[None, :]
    in_win = (step >= 0) & (step <= ATT_BLOCK)
    bias = -_alibi_slopes()[:, None, None] * (step * dilation).astype(np.float64)[None]
    return jnp.asarray(np.where(in_win[None], bias, NEG_INF).reshape(N_PAIRS, 2 * n_q, n_k), F32)


def _band_attention(q, k, v, table_ref, key_ok, pair):
    n_q = q.shape[0]
    lo = lax.broadcasted_iota(jnp.int32, q.shape, 1) < HEAD_DIM
    q2 = jnp.concatenate([jnp.where(lo, q, 0.0), jnp.where(lo, 0.0, q)], axis=0).astype(BF16)
    s = _dot_nt(q2, k.astype(BF16)) * HEAD_DIM ** -0.5 + table_ref[pair]
    s = jnp.where(key_ok, s, NEG_INF)
    m = jnp.max(s, axis=-1, keepdims=True)
    e = jnp.exp(s - m)
    l = jnp.sum(e, axis=-1, keepdims=True)
    o = _dot(e.astype(BF16), v.astype(BF16))
    return tuple(jnp.where(lo, t[:n_q], t[n_q:]) for t in (o, m, l))


def _attp_kernel(*refs):
    n_back = ATT_BACK
    q_ref = refs[0]
    k_refs = refs[1:2 + n_back]
    v_refs = refs[2 + n_back:3 + 2 * n_back]
    t_refs = refs[3 + 2 * n_back:3 + 2 * n_back + len(DIL_PATTERNS)]
    o_ref, parts, kcat, vcat = refs[3 + 2 * n_back + len(DIL_PATTERNS):]
    n = pl.program_id(1)

    def key_col(n_q, n_k):
        return lax.broadcasted_iota(jnp.int32, (2 * n_q, n_k), 1)

    def fold(first, rows, news):
        if not first:
            olds = [[parts[kind, p, rows, :] for kind in range(3)] for p in range(N_PAIRS)]
            merged = []
            for (o_old, m_old, l_old), (o_new, m_new, l_new) in zip(olds, news):
                m = jnp.maximum(m_old, m_new)
                a, b = jnp.exp(m_old - m), jnp.exp(m_new - m)
                merged.append((o_old * a + o_new * b, m, l_old * a + l_new * b))
            news = merged
        for p, new in enumerate(news):
            for kind, val in enumerate(new):
                parts[kind, p, rows, :] = val

    for pat, (window, dil) in enumerate(DIL_PATTERNS):
        t_ref = t_refs[pat]
        if dil == 1:
            for cat, src in ((kcat, k_refs), (vcat, v_refs)):
                cat[:, :ATT_BLOCK, :] = src[1][:, ATT_Q - ATT_BLOCK:, :]
                cat[:, ATT_BLOCK:, :] = src[0][...]

            def body(sub, carry, t_ref=t_ref, first=pat == 0):
                r0 = pl.multiple_of(sub * ATT_BLOCK, ATT_BLOCK)
                ok = (key_col(ATT_BLOCK, 2 * ATT_BLOCK) >= ATT_BLOCK) | (n > 0) | (sub > 0)
                fold(first, pl.ds(r0, ATT_BLOCK),
                     [_band_attention(q_ref[p, pl.ds(r0, ATT_BLOCK), :], kcat[p, pl.ds(r0, 2 * ATT_BLOCK), :],
                                      vcat[p, pl.ds(r0, 2 * ATT_BLOCK), :], t_ref, ok, p) for p in range(N_PAIRS)])
                return carry

            lax.fori_loop(0, ATT_Q // ATT_BLOCK, body, 0)
            continue
        sub_rows = min(window, ATT_Q)
        n_q = sub_rows // dil
        n_prev = window // sub_rows
        for s in range(ATT_Q // sub_rows):
            pieces = []
            for i in range(n_prev, -1, -1):
                rel = (s - i) * sub_rows
                j = -(rel // ATT_Q)
                pieces.append((j, rel + j * ATT_Q))
            missing = sum(jnp.where(n < j, n_q, 0) for j, _ in pieces)

            def body(r, carry, dil=dil, n_q=n_q, t_ref=t_ref, first=pat == 0, pieces=pieces, missing=missing,
                     q_off=s * sub_rows):
                ok = key_col(n_q, len(pieces) * n_q) >= missing
                rows = pl.ds(q_off + r, n_q, stride=dil)
                gather = lambda refs, p: jnp.concatenate(
                    [refs[j][p, pl.ds(off + r, n_q, stride=dil), :] for j, off in pieces], axis=0)
                loaded = [(q_ref[p, rows, :], gather(k_refs, p), gather(v_refs, p)) for p in range(N_PAIRS)]
                fold(first, rows, [_band_attention(q, k, v, t_ref, ok, p) for p, (q, k, v) in enumerate(loaded)])
                return carry

            lax.fori_loop(0, dil, body, 0)

    for p in range(N_PAIRS):
        o_ref[:, p * LANES:(p + 1) * LANES] = (parts[0, p] / parts[2, p]).astype(o_ref.dtype)


def _att_prompt(q, k, v, n_batch, seq):
    nb = seq // ATT_Q
    assert all(w // d == ATT_BLOCK and ATT_Q % min(w, ATT_Q) == 0 and w % min(w, ATT_Q) == 0
               for w, d in DIL_PATTERNS)
    assert ATT_BACK == DIL_PATTERNS[-1][0] // ATT_Q and DIL_PATTERNS[0] == (ATT_BLOCK, 1)
    blk = lambda j: pl.BlockSpec((N_PAIRS, ATT_Q, LANES),
                                 lambda b, i: (0, b * nb + jnp.maximum(i - j, 0), 0))
    tables = [_band_table(ATT_BLOCK, 2 * ATT_BLOCK, 1)] + [
        _band_table(min(w, ATT_Q) // d, (w // min(w, ATT_Q) + 1) * (min(w, ATT_Q) // d), d)
        for w, d in DIL_PATTERNS[1:]]
    back = [blk(j) for j in range(ATT_BACK + 1)]
    return pl.pallas_call(
        _attp_kernel,
        grid=(n_batch, nb),
        in_specs=[blk(0)] + back + back + [pl.BlockSpec(t.shape, lambda b, i: (0, 0, 0)) for t in tables],
        out_specs=pl.BlockSpec((ATT_Q, D_HALF), lambda b, i: (b * nb + i, 0)),
        out_shape=jax.ShapeDtypeStruct((n_batch * seq, D_HALF), BF16),
        scratch_shapes=[pltpu.VMEM((3, N_PAIRS, ATT_Q, LANES), F32),
                        pltpu.VMEM((N_PAIRS, ATT_Q + ATT_BLOCK, LANES), F32),
                        pltpu.VMEM((N_PAIRS, ATT_Q + ATT_BLOCK, LANES), F32)],
        compiler_params=_params(2, VMEM_LIMIT),
        name="att_prompt",
    )(q, *([k] * (ATT_BACK + 1)), *([v] * (ATT_BACK + 1)), *tables)


def _atts_tables(buf, t_new):
    pos_q = buf + np.arange(t_new)
    pos_k = np.concatenate([np.arange(buf), buf + np.arange(LANES)])
    dist = pos_q[:, None] - pos_k[None, :]
    count = np.zeros(dist.shape)
    for window, dilation in DIL_PATTERNS:
        count += (dist >= 0) & (dist <= window) & (dist % dilation == 0)
    count[:, buf + t_new:] = 0
    bias = -_alibi_slopes()[:, None, None] * dist[None].astype(np.float64)
    table = np.where(count[None] > 0, bias + np.log(np.maximum(count, 1.0))[None], NEG_INF)
    table = table.reshape(N_HEADS * t_new, buf + LANES)
    return jnp.asarray(table[:, :buf], F32), jnp.asarray(table[:, buf:], F32)


def _atts_kernel(q_ref, kt_ref, kn_ref, vt_ref, vn_ref, tc_ref, tn_ref, o_ref, *, t_new):
    rows = N_HEADS * t_new
    row_head = lax.broadcasted_iota(jnp.int32, (rows, D_HALF), 0) // t_new
    col_head = lax.broadcasted_iota(jnp.int32, (rows, D_HALF), 1) // HEAD_DIM
    own = row_head == col_head
    q_heads = jnp.where(own, jnp.concatenate([q_ref[...]] * N_HEADS, axis=0), 0.0).astype(BF16)
    pad = jnp.zeros((LANES - t_new, D_HALF), F32)
    kn = jnp.concatenate([kn_ref[...], pad], axis=0).astype(BF16)
    vn = jnp.concatenate([vn_ref[...], pad], axis=0).astype(BF16)
    scale = HEAD_DIM ** -0.5
    sc = _dot(q_heads, kt_ref[...].astype(BF16)) * scale + tc_ref[...]
    sn = _dot_nt(q_heads, kn) * scale + tn_ref[...]
    m = jnp.maximum(jnp.max(sc, axis=-1, keepdims=True), jnp.max(sn, axis=-1, keepdims=True))
    ec = jnp.exp(sc - m)
    en = jnp.exp(sn - m)
    l = jnp.sum(ec, axis=-1, keepdims=True) + jnp.sum(en, axis=-1, keepdims=True)
    acc = _dot_nt(ec.astype(BF16), vt_ref[...].astype(BF16)) + _dot(en.astype(BF16), vn)
    acc = jnp.where(own, acc, 0.0) / l
    out = acc[:t_new]
    for h in range(1, N_HEADS):
        out = out + acc[h * t_new:(h + 1) * t_new]
    o_ref[...] = out.astype(o_ref.dtype)


def _att_sample(q, k_new, v_new, cache_k, cache_v, n_batch, t_new):
    buf = cache_k.shape[1]
    tc, tn = _atts_tables(buf, t_new)
    by_pos = lambda c: c.transpose(0, 2, 3, 1).reshape(n_batch, D_HALF, buf)
    rows = pl.BlockSpec((t_new, D_HALF), lambda b: (b, 0))
    cache = pl.BlockSpec((None, D_HALF, buf), lambda b: (b, 0, 0))
    return pl.pallas_call(
        functools.partial(_atts_kernel, t_new=t_new),
        grid=(n_batch,),
        in_specs=[rows, cache, rows, cache, rows,
                  pl.BlockSpec((N_HEADS * t_new, buf), lambda b: (0, 0)),
                  pl.BlockSpec((N_HEADS * t_new, LANES), lambda b: (0, 0))],
        out_specs=rows,
        out_shape=jax.ShapeDtypeStruct((n_batch * t_new, D_HALF), BF16),
        compiler_params=_params(1, VMEM_LIMIT),
        name="att_sample",
    )(q, by_pos(cache_k), k_new, by_pos(cache_v), v_new, tc, tn)


def _out_kernel(ret_ref, att_ref, x_ref, gt_ref, sc_ref, sh_ref, w_ref, gpost_ref, gpre_ref, wr_ref, br_ref,
                x1_ref, h_ref, top_ref):
    tm = x_ref.shape[0]
    mix = _dot(ret_ref[...], w_ref[:D_HALF, :]) + _dot(att_ref[...], w_ref[D_HALF:, :])
    x1 = x_ref[...] + gt_ref[...] * _rms(mix, gpost_ref[...])
    x1_ref[...] = x1
    h = _rms(x1, gpre_ref[...]) * (1.0 + sc_ref[...]) + sh_ref[...]
    _slab_store(h_ref, h)
    split = lambda t: (t.astype(BF16), (t - t.astype(BF16).astype(F32)).astype(BF16))
    (h_hi, h_lo), (w_hi, w_lo) = split(h), split(wr_ref[...])
    logits = _dot(h_hi, w_hi) + _dot(h_lo, w_hi) + _dot(h_hi, w_lo) + br_ref[...]
    lane_e = lax.broadcasted_iota(jnp.int32, (tm, N_EXPERTS), 1).astype(F32)
    lane_o = lax.broadcasted_iota(jnp.int32, (tm, LANES), 1)
    vals, idxs = [], []
    work = logits
    for _ in range(TOP_K):
        v = jnp.max(work, axis=-1, keepdims=True)
        i = jnp.min(jnp.where(work == v, lane_e, float(N_EXPERTS)), axis=-1, keepdims=True)
        vals.append(v)
        idxs.append(i)
        work = jnp.where(lane_e == i, -jnp.inf, work)
    es = [jnp.exp(v - vals[0]) for v in vals]
    tot = sum(es)
    packed = jnp.zeros((tm, LANES), F32)
    for j in range(TOP_K):
        packed = jnp.where(lane_o == j, idxs[j], packed)
        packed = jnp.where(lane_o == TOP_K + j, es[j] / tot, packed)
    top_ref[...] = packed


def _mixer_out(ret_y, att, x, mod, w_out_bf, g_post, g_pre, w_router, b_router, per_row, rows_per_batch):
    n = x.shape[0]
    tm = min(TM_OUT, n)
    tpb = 1 if per_row else rows_per_batch // tm
    half = pl.BlockSpec((tm, D_HALF), lambda i: (i, 0))
    full = pl.BlockSpec((tm, D_MODEL), lambda i: (i, 0))
    stat = pl.BlockSpec((tm, LANES), lambda i: (i, 0))
    vec = pl.BlockSpec((1, D_MODEL), lambda i: (0, 0))
    mods = [_mod_spec(tm, c, per_row, tpb) for c in (MOD_GATE_MIX, MOD_SCALE_FFN, MOD_SHIFT_FFN)]
    return pl.pallas_call(
        _out_kernel,
        grid=(n // tm,),
        in_specs=[half, half, full] + mods + [
                  pl.BlockSpec((D_MODEL, D_MODEL), lambda i: (0, 0)), vec, vec,
                  pl.BlockSpec((D_MODEL, N_EXPERTS), lambda i: (0, 0)),
                  pl.BlockSpec((1, N_EXPERTS), lambda i: (0, 0))],
        out_specs=[full, pl.BlockSpec((tm * SLAB, LANES), lambda i: (i, 0)), stat],
        out_shape=[jax.ShapeDtypeStruct((n, D_MODEL), F32),
                   jax.ShapeDtypeStruct((n * SLAB, LANES), F32),
                   jax.ShapeDtypeStruct((n, LANES), F32)],
        compiler_params=_params(1, VMEM_LIMIT),
        name="mixer_out",
    )(ret_y, att, x, mod, mod, mod, w_out_bf, g_post, g_pre, w_router, b_router)


def _route_kernel(top_ref, dest_ref, ends_ref, running, starts, *, tm, tm_blk):
    ph, i = pl.program_id(0), pl.program_id(1)
    lane_f = lax.broadcasted_iota(jnp.int32, (tm, LANES), 1).astype(F32)
    top = top_ref[...]
    picks = [lane_f == top[:, k:k + 1] for k in range(TOP_K)]
    chosen = sum(p.astype(F32) for p in picks)
    tile_counts = jnp.sum(chosen, axis=0, keepdims=True)

    @pl.when((ph == 0) & (i == 0))
    def _():
        running[...] = jnp.zeros_like(running)

    @pl.when(ph == 0)
    def _():
        running[...] += tile_counts

    @pl.when((ph == 1) & (i == 0))
    def _():
        lane = lax.broadcasted_iota(jnp.int32, (1, LANES), 1)
        assert tm_blk & (tm_blk - 1) == 0
        blocks = (running[...].astype(jnp.int32) + (tm_blk - 1)) >> (tm_blk.bit_length() - 1)
        ends = blocks
        shift = 1
        while shift < N_EXPERTS:
            ends = ends + jnp.where(lane >= shift, pltpu.roll(ends, shift, axis=1), 0)
            shift *= 2
        ends_ref[...] = ends
        starts[...] = ((ends - blocks) * tm_blk).astype(F32)
        running[...] = jnp.zeros_like(running)

    @pl.when(ph == 1)
    def _():
        earlier = (lax.broadcasted_iota(jnp.int32, (tm, tm), 0)
                   > lax.broadcasted_iota(jnp.int32, (tm, tm), 1))
        ahead = _dot(jnp.where(earlier, 1.0, 0.0).astype(BF16), chosen.astype(BF16))
        slot = starts[...] + running[...] + ahead
        lane_o = lax.broadcasted_iota(jnp.int32, (tm, LANES), 1)
        packed = jnp.zeros((tm, LANES), F32)
        for k, pick in enumerate(picks):
            dest_k = jnp.sum(jnp.where(pick, slot, 0.0), axis=-1, keepdims=True)
            packed = jnp.where(lane_o == k, dest_k, packed)
        dest_ref[...] = packed.astype(jnp.int32)
        running[...] += tile_counts


def _route(top, tm_blk):
    n = top.shape[0]
    tm = 512
    return pl.pallas_call(
        functools.partial(_route_kernel, tm=tm, tm_blk=tm_blk),
        grid=(2, n // tm),
        in_specs=[pl.BlockSpec((tm, LANES), lambda ph, i: (i, 0))],
        out_specs=[pl.BlockSpec((tm, LANES), lambda ph, i: (i * ph, 0)),
                   pl.BlockSpec((1, LANES), lambda ph, i: (0, 0))],
        out_shape=[jax.ShapeDtypeStruct((n, LANES), jnp.int32),
                   jax.ShapeDtypeStruct((1, LANES), jnp.int32)],
        scratch_shapes=[pltpu.VMEM((1, LANES), F32), pltpu.VMEM((1, LANES), F32)],
        compiler_params=_params(2),
        name="route",
    )(top)


def _dispatch_kernel(ends_ref, dest_ref, *rest, tm, tm_blk, n_blocks, tiles):
    h_refs, (xs_hbm, zeros, sem) = rest[:-3], rest[-3:]
    i = pl.program_id(0)

    def fill(b):
        rows = pl.ds(pl.multiple_of(b * (tm_blk * SLAB), tm_blk * SLAB), tm_blk * SLAB)
        return pltpu.make_async_copy(zeros, xs_hbm.at[rows], sem.at[1])

    def fill_start(b, carry):
        fill(b).start()
        return carry

    def fill_wait(b, carry):
        fill(b).wait()
        return carry

    def zero_padding():
        zeros[...] = jnp.zeros_like(zeros)
        n_used = ends_ref[N_EXPERTS - 1]
        for do in (fill_start, fill_wait):
            for e in range(N_EXPERTS):
                first = ends_ref[e - 1] if e else 0

                @pl.when(ends_ref[e] > first)
                def _():
                    do(ends_ref[e] - 1, 0)

            lax.fori_loop(n_used, n_blocks, do, 0)

    pl.when(i == 0)(zero_padding)

    def scatter(h_ref):
        def body(j, carry):
            for k in range(TOP_K):
                pltpu.make_async_copy(h_ref.at[_slab_rows(j)], xs_hbm.at[_slab_rows(dest_ref[j * TOP_K + k])],
                                      sem.at[0]).start(priority=k % 2)
            return carry

        lax.fori_loop(0, tm, body, 0)
        for k in range(TOP_K):
            pltpu.make_async_copy(h_ref, xs_hbm.at[pl.ds(0, tm * SLAB)], sem.at[0]).wait()

    start = 0
    for h_ref, n_tiles in zip(h_refs, tiles):
        pl.when((i >= start) & (i < start + n_tiles))(functools.partial(scatter, h_ref))
        start += n_tiles


def _dispatch(hs, dest_flat, ends, n_blocks, tm_blk):
    tm = 256
    tiles = [h.shape[0] // SLAB // tm for h in hs]
    starts = [sum(tiles[:s]) for s in range(len(hs))]
    src = lambda s: pl.BlockSpec((tm * SLAB, LANES),
                                 lambda i, ends: (jnp.clip(i - starts[s], 0, tiles[s] - 1), 0))
    grid_spec = pltpu.PrefetchScalarGridSpec(
        num_scalar_prefetch=1,
        grid=(sum(tiles),),
        in_specs=[pl.BlockSpec((tm * TOP_K,), lambda i, ends: (i,), memory_space=pltpu.SMEM)]
        + [src(s) for s in range(len(hs))],
        out_specs=pl.BlockSpec(memory_space=pl.ANY),
        scratch_shapes=[pltpu.VMEM((tm_blk * SLAB, LANES), F32), pltpu.SemaphoreType.DMA((2,))],
    )
    return pl.pallas_call(
        functools.partial(_dispatch_kernel, tm=tm, tm_blk=tm_blk, n_blocks=n_blocks, tiles=tiles),
        grid_spec=grid_spec,
        out_shape=jax.ShapeDtypeStruct((n_blocks * tm_blk * SLAB, LANES), F32),
        compiler_params=_params(1),
        name="dispatch",
    )(ends, dest_flat, *hs)


def _moe_kernel(be_ref, nu_ref, x_ref, wgu_ref, wd_ref, bg_ref, bu_ref, bd_ref, o_ref, wt, wg_t, wu_t, wd_bf):
    i = pl.program_id(0)
    n_used = nu_ref[0]
    d, d_gu = wgu_ref.shape
    d_ff = d_gu // 2
    new_expert = (i == 0) | (be_ref[i] != be_ref[jnp.maximum(i - 1, 0)])

    @pl.when((i < n_used) & new_expert)
    def _():
        for c in range(d // LANES):
            rows = slice(c * LANES, (c + 1) * LANES)
            wt[...] = wgu_ref[rows, :].T
            wg_t[:, rows] = wt[pl.ds(0, d_ff, stride=2), :].astype(BF16)
            wu_t[:, rows] = wt[pl.ds(1, d_ff, stride=2), :].astype(BF16)
        wd_bf[...] = wd_ref[...].astype(BF16)

    @pl.when(i < n_used)
    def _():
        x = _slab_load(x_ref, x_ref.shape[0] // SLAB).astype(BF16)
        acc = bd_ref[...]
        for c in range(d_ff // MOE_COLS):
            cols = slice(c * MOE_COLS, (c + 1) * MOE_COLS)
            gate = jnp.minimum(_dot_nt(x, wg_t[cols, :]) + bg_ref[:, cols], SWIGLU_LIMIT)
            up = jnp.clip(_dot_nt(x, wu_t[cols, :]) + bu_ref[:, cols], -SWIGLU_LIMIT, SWIGLU_LIMIT)
            act = gate * jax.nn.sigmoid(SWIGLU_ALPHA * gate) * (up + 1.0)
            acc = acc + _dot(act.astype(BF16), wd_bf[cols, :])
        _slab_store(o_ref, acc)

    @pl.when(i >= n_used)
    def _():
        o_ref[...] = jnp.zeros_like(o_ref)


def _moe_blocks(xs, block_e, n_used, w_gu, w_down, b_gu, b_down, tm):
    n_blocks = block_e.shape[0]
    n_e, d, d_gu = w_gu.shape
    d_ff = d_gu // 2
    wspec = lambda k, n: pl.BlockSpec((None, k, n), lambda i, be, nu: (be[i], 0, 0))
    grid_spec = pltpu.PrefetchScalarGridSpec(
        num_scalar_prefetch=2,
        grid=(n_blocks,),
        in_specs=[pl.BlockSpec((tm * SLAB, LANES), lambda i, be, nu: (jnp.minimum(i, nu[0] - 1), 0)),
                  wspec(d, d_gu), wspec(d_ff, d), wspec(1, d_ff), wspec(1, d_ff), wspec(1, d)],
        out_specs=pl.BlockSpec((tm * SLAB, LANES), lambda i, be, nu: (i, 0)),
        scratch_shapes=[pltpu.VMEM((d_gu, LANES), F32), pltpu.VMEM((d_ff, d), BF16),
                        pltpu.VMEM((d_ff, d), BF16), pltpu.VMEM((d_ff, d), BF16)],
    )
    return pl.pallas_call(
        _moe_kernel,
        grid_spec=grid_spec,
        out_shape=jax.ShapeDtypeStruct((n_blocks * tm * SLAB, LANES), F32),
        compiler_params=_params(1, VMEM_LIMIT),
        name="moe_blocks",
    )(block_e, n_used, xs, w_gu, w_down, b_gu[:, None, 0::2], b_gu[:, None, 1::2], b_down.reshape(n_e, 1, d))


def _fin_kernel(dest_ref, next_ref, rows_hbm, top_ref, x1_ref, gt_ref, g_ref, o_ref, buf, sem, *, tm):
    i = pl.program_id(0)
    slot = i % 2

    def gather(idx_ref, s):
        def body(j, carry):
            for k in range(TOP_K):
                pltpu.make_async_copy(rows_hbm.at[_slab_rows(idx_ref[j * TOP_K + k])],
                                      buf.at[s, k, _slab_rows(j)], sem.at[s]).start(priority=k % 2)
            return carry
        lax.fori_loop(0, tm, body, 0)

    @pl.when(i == 0)
    def _():
        gather(dest_ref, 0)

    @pl.when(i + 1 < pl.num_programs(0))
    def _():
        gather(next_ref, 1 - slot)

    for k in range(TOP_K):
        pltpu.make_async_copy(rows_hbm.at[pl.ds(0, tm * SLAB)], buf.at[slot, k], sem.at[slot]).wait()
    top = top_ref[...]
    f = sum(_slab_load(buf.at[slot, k], tm) * top[:, TOP_K + k:TOP_K + k + 1] for k in range(TOP_K))
    o_ref[...] = x1_ref[...] + gt_ref[...] * _rms(f, g_ref[...])


def _finish(dest, rows, top, x1, mod, g_post, per_row, rows_per_batch):
    n = x1.shape[0]
    tm = min(TM_FIN, n)
    full = pl.BlockSpec((tm, D_MODEL), lambda i: (i, 0))
    gate = _mod_spec(tm, MOD_GATE_FFN, per_row, 1 if per_row else rows_per_batch // tm)
    return pl.pallas_call(
        functools.partial(_fin_kernel, tm=tm),
        grid=(n // tm,),
        in_specs=[pl.BlockSpec((tm * TOP_K,), lambda i: (i,), memory_space=pltpu.SMEM),
                  pl.BlockSpec((tm * TOP_K,), lambda i: (jnp.minimum(i + 1, n // tm - 1),),
                               memory_space=pltpu.SMEM),
                  pl.BlockSpec(memory_space=pl.ANY),
                  pl.BlockSpec((tm, LANES), lambda i: (i, 0)), full, gate,
                  pl.BlockSpec((1, D_MODEL), lambda i: (0, 0))],
        out_specs=full,
        out_shape=jax.ShapeDtypeStruct((n, D_MODEL), F32),
        scratch_shapes=[pltpu.VMEM((2, TOP_K, tm * SLAB, LANES), F32), pltpu.SemaphoreType.DMA((2,))],
        compiler_params=_params(1),
        name="finish",
    )(dest, dest, rows, top, x1, mod, g_post)


def kernel(x_prompt, x_sample, state_ret, cache_win_k, cache_win_v, c_prompt, c_sample, w_ada, b_ada,
           g_pre_mix, g_post_mix, g_pre_ffn, g_post_ffn, w_in, g_ret, w_out, w_router, b_router,
           w_gate_up, b_gate_up, w_down, b_down):
    depth = w_in.shape[0]
    assert depth == 1, "single-layer step"
    n_b, seq, d = x_prompt.shape
    n_db, t_new, _ = x_sample.shape
    n_p, n_s = n_b * seq, n_db * t_new
    buf = cache_win_k.shape[2]
    keep = min(DIL_PATTERNS[-1][0], seq)
    vec = lambda g: g[0].reshape(1, -1)

    mod_p, mod_s = _ada(c_prompt, c_sample, w_ada[0], b_ada[0], t_new)
    mod_p = mod_p[:n_b].reshape(n_b, 1, -1)
    mod_s = mod_s.reshape(n_s, -1)

    w_in_bf = w_in[0].astype(BF16)
    w_out_bf = w_out[0].astype(BF16)
    xp = x_prompt.reshape(n_p, d)
    xs = x_sample.reshape(n_s, d)
    router = (w_router[0], b_router[0].reshape(1, -1))

    qr, kr, vr, gr, qa_pm, ka_pm, va_pm, k_win, v_win = _in_proj(xp, mod_p, vec(g_pre_mix), w_in_bf, False, seq, keep)
    ret_y_p, ret_state_p = _retention(qr, kr, vr, gr, None, g_ret[0], n_b, seq, RET_CHUNK, n_b)
    att_p = _att_prompt(qa_pm, ka_pm, va_pm, n_b, seq)
    x1_p, h_p, top_p = _mixer_out(ret_y_p, att_p, xp, mod_p, w_out_bf, vec(g_post_mix), vec(g_pre_ffn),
                                  *router, False, seq)

    qr_s, kr_s, vr_s, gr_s, qa_s, ka_s, va_s = _in_proj(xs, mod_s, vec(g_pre_mix), w_in_bf, True, t_new)
    ret_y_s, ret_state_s = _retention(qr_s, kr_s, vr_s, gr_s, state_ret[0], g_ret[0], n_db, t_new, t_new,
                                      RET_GROUP_SAMPLE)
    att_s = _att_sample(qa_s, ka_s, va_s, cache_win_k[0], cache_win_v[0], n_db, t_new)
    x1_s, h_s, top_s = _mixer_out(ret_y_s, att_s, xs, mod_s, w_out_bf, vec(g_post_mix), vec(g_pre_ffn),
                                  *router, True, t_new)

    n_blocks = (n_p + n_s) * TOP_K // TM_MOE + N_EXPERTS
    dest, ends = _route(jnp.concatenate([top_p, top_s], axis=0), TM_MOE)
    dest = dest[:, :TOP_K].reshape(-1)
    ends = ends[0, :N_EXPERTS]
    block_e = jnp.minimum(jnp.sum(jnp.arange(n_blocks, dtype=jnp.int32)[:, None] >= ends[None, :], axis=1),
                          N_EXPERTS - 1).astype(jnp.int32)
    rows_in = _dispatch([h_p, h_s], dest, ends, n_blocks, TM_MOE)
    rows = _moe_blocks(rows_in, block_e, ends[N_EXPERTS - 1:], w_gate_up[0], w_down[0],
                       b_gate_up[0], b_down[0], TM_MOE)
    y_p = _finish(dest[:n_p * TOP_K], rows, top_p, x1_p, mod_p, vec(g_post_ffn), False, seq)
    y_s = _finish(dest[n_p * TOP_K:], rows, top_s, x1_s, mod_s, vec(g_post_ffn), True, t_new)

    win = lambda t: t.reshape(n_b, N_HEADS, HEAD_DIM, keep).transpose(0, 3, 1, 2)[None]
    new = lambda t: t.reshape(n_db, t_new, N_HEADS, HEAD_DIM)[None]
    return (y_p.reshape(n_b, seq, d), y_s.reshape(n_db, t_new, d), ret_state_p[None], ret_state_s[None],
            win(k_win), win(v_win), new(ka_s), new(va_s))
```

```python
import functools

import numpy as np
import jax
import jax.numpy as jnp
from jax import lax
from jax.experimental import pallas as pl
from jax.experimental.pallas import tpu as pltpu

F32 = jnp.float32
BF16 = jnp.bfloat16

D_MODEL = 1024
HEAD_DIM = 64
N_HEADS = 8
D_HALF = N_HEADS * HEAD_DIM
N_SEG = 7
LANES = 128
N_PAIRS = D_HALF // LANES
RET_CHUNK = 128
RET_GROUP_SAMPLE = 8
ATT_BLOCK = 128
ATT_Q = 1024
ATT_BACK = 2
ATT_UNROLL = 2
DIL_PATTERNS = ((128, 1), (512, 4), (2048, 16))
N_EXPERTS = 32
TOP_K = 4
SWIGLU_LIMIT = 7.0
SWIGLU_ALPHA = 1.702
EPS = 1e-6
NEG_INF = -1e30
MOD_SHIFT_MIX, MOD_SCALE_MIX, MOD_GATE_MIX, MOD_SHIFT_FFN, MOD_SCALE_FFN, MOD_GATE_FFN = range(6)

TM_IN = 512
TM_OUT = 256
TM_MOE = 512
MOE_COLS = 512
TM_FIN = 128
DMA_UNROLL = 4
VMEM_LIMIT = 56 * 1024 * 1024


def _params(n_axes, vmem=None):
    return pltpu.CompilerParams(dimension_semantics=("arbitrary",) * n_axes, vmem_limit_bytes=vmem)


def _ret_log_decay():
    return np.log(1.0 - 2.0 ** (-5.0 - np.arange(N_HEADS, dtype=np.float64)))


def _alibi_slopes():
    return 2.0 ** (-8.0 * (np.arange(N_HEADS, dtype=np.float64) + 1.0) / N_HEADS)


def _per_head_lanes(a):
    return np.repeat(a, HEAD_DIM, axis=-1)


def _rms(x, g):
    return x * lax.rsqrt(jnp.mean(x * x, axis=-1, keepdims=True) + EPS) * g


def _dot(a, b):
    return jnp.dot(a, b, preferred_element_type=F32)


def _dot_nt(a, b):
    return lax.dot_general(a, b, (((1,), (1,)), ((), ())), preferred_element_type=F32)


def _dot_tn(a, b):
    return lax.dot_general(a, b, (((0,), (0,)), ((), ())), preferred_element_type=F32)


SLAB = D_MODEL // LANES


def _slab_rows(r):
    return pl.ds(pl.multiple_of(r * SLAB, SLAB), SLAB)


def _slab_load(ref, n):
    return jnp.concatenate([ref[pl.ds(s, n, stride=SLAB), :] for s in range(SLAB)], axis=1)


def _slab_store(ref, value):
    n = value.shape[0]
    for s in range(SLAB):
        ref[pl.ds(s, n, stride=SLAB), :] = value[:, s * LANES:(s + 1) * LANES]


def _mod_spec(tm, col, per_row, tiles_per_batch):
    if per_row:
        return pl.BlockSpec((tm, D_MODEL), lambda i: (i, col))
    return pl.BlockSpec((None, 1, D_MODEL), lambda i: (i // tiles_per_batch, 0, col))


def _ada_kernel(cp_ref, cs_ref, w_ref, b_ref, op_ref, os_ref):
    w = w_ref[...].astype(BF16)

    def mod(c):
        return _dot((c * jax.nn.sigmoid(c)).astype(BF16), w) + b_ref[...]

    op_ref[...] = mod(cp_ref[...])
    ms = mod(cs_ref[...])
    os_ref[...] = jnp.broadcast_to(ms[:, None, :], os_ref.shape)


def _ada(c_prompt, c_sample, w_ada, b_ada, t_new):
    n_b, d = c_prompt.shape
    n_db = c_sample.shape[0]
    n_out = w_ada.shape[1]
    cp = jnp.concatenate([c_prompt, jnp.zeros((-n_b % 8, d), F32)], axis=0)
    return pl.pallas_call(
        _ada_kernel,
        grid=(n_out // d,),
        in_specs=[pl.BlockSpec(cp.shape, lambda j: (0, 0)),
                  pl.BlockSpec((n_db, d), lambda j: (0, 0)),
                  pl.BlockSpec((d, d), lambda j: (0, j)),
                  pl.BlockSpec((1, d), lambda j: (0, j))],
        out_specs=[pl.BlockSpec(cp.shape, lambda j: (0, j)),
                   pl.BlockSpec((n_db, t_new, d), lambda j: (0, 0, j))],
        out_shape=[jax.ShapeDtypeStruct((cp.shape[0], n_out), F32),
                   jax.ShapeDtypeStruct((n_db, t_new, n_out), F32)],
        compiler_params=_params(1, VMEM_LIMIT),
        name="ada",
    )(cp, c_sample, w_ada, b_ada.reshape(1, n_out))


def _in_kernel(x_ref, sc_ref, sh_ref, g_ref, w_ref, *o_refs, n_plain, window):
    h = (_rms(x_ref[...], g_ref[...]) * (1.0 + sc_ref[...]) + sh_ref[...]).astype(BF16)
    segs = [_dot(h, w_ref[:, s * D_HALF:(s + 1) * D_HALF]) for s in range(N_SEG)]
    for o_ref, seg in zip(o_refs[:n_plain], segs):
        o_ref[...] = seg
    if n_plain == N_SEG:
        return
    for o_ref, seg in zip(o_refs[n_plain:N_SEG], segs[n_plain:]):
        for p in range(N_PAIRS):
            o_ref[p] = seg[:, p * LANES:(p + 1) * LANES]
    tiles_per_seq, first = window
    @pl.when(pl.program_id(0) % tiles_per_seq >= first)
    def _():
        for o_ref, seg in zip(o_refs[N_SEG:], segs[N_SEG - 2:]):
            o_ref[...] = seg.T


def _in_proj(x, mod, g, w_bf, per_row, rows_per_batch, keep=None):
    n = x.shape[0]
    tm = min(TM_IN, n)
    tpb = 1 if per_row else rows_per_batch // tm
    seg = pl.BlockSpec((tm, D_HALF), lambda i: (i, 0))
    seg_shape = jax.ShapeDtypeStruct((n, D_HALF), F32)
    if keep is None:
        n_plain, window = N_SEG, None
        out_specs, out_shape = [seg] * N_SEG, [seg_shape] * N_SEG
    else:
        n_plain, window = N_SEG - 3, (tpb, (rows_per_batch - keep) // tm)
        pm = pl.BlockSpec((N_PAIRS, tm, LANES), lambda i: (0, i, 0))
        win = pl.BlockSpec((None, D_HALF, tm), lambda i: (i // tpb, 0, jnp.maximum(i % tpb - window[1], 0)))
        out_specs = [seg] * n_plain + [pm] * 3 + [win] * 2
        out_shape = ([seg_shape] * n_plain + [jax.ShapeDtypeStruct((N_PAIRS, n, LANES), F32)] * 3
                     + [jax.ShapeDtypeStruct((n // rows_per_batch, D_HALF, keep), F32)] * 2)
    return pl.pallas_call(
        functools.partial(_in_kernel, n_plain=n_plain, window=window),
        grid=(n // tm,),
        in_specs=[pl.BlockSpec((tm, D_MODEL), lambda i: (i, 0)),
                  _mod_spec(tm, MOD_SCALE_MIX, per_row, tpb), _mod_spec(tm, MOD_SHIFT_MIX, per_row, tpb),
                  pl.BlockSpec((1, D_MODEL), lambda i: (0, 0)),
                  pl.BlockSpec((D_MODEL, N_SEG * D_HALF), lambda i: (0, 0))],
        out_specs=out_specs,
        out_shape=out_shape,
        compiler_params=_params(1, VMEM_LIMIT),
        name="in_proj",
    )(x, mod, mod, g, w_bf)


def _ret_tables(chunk):
    lg = _ret_log_decay()
    pos = np.arange(chunk, dtype=np.float64)
    diff = pos[:, None] - pos[None, :]
    dmask = np.where(diff >= 0, np.exp(np.maximum(diff, 0.0)[None] * lg[:, None, None]), 0.0)
    dmask = dmask.reshape(N_PAIRS, 2 * chunk, chunk)
    row_decay = _per_head_lanes(np.exp((pos[:, None] + 1.0) * lg[None, :]))
    key_decay = _per_head_lanes(np.exp((chunk - 1.0 - pos)[:, None] * lg[None, :]))
    state_decay = _per_head_lanes(np.exp(chunk * lg)[None, :])
    return tuple(jnp.asarray(t, F32) for t in (dmask, row_decay, key_decay, state_decay))


def _ret_kernel(q_ref, k_ref, v_ref, gate_ref, s0_ref, gret_ref, dm_ref, rd_ref, kd_ref, sd_ref,
                y_ref, st_ref, state, *, chunk, group, has_init):
    c = pl.program_id(1)
    lo = lax.broadcasted_iota(jnp.int32, (chunk, LANES), 1) < HEAD_DIM
    blk_r = lax.broadcasted_iota(jnp.int32, (LANES, LANES), 0) < HEAD_DIM
    blk_c = lax.broadcasted_iota(jnp.int32, (LANES, LANES), 1) < HEAD_DIM
    same_head = blk_r == blk_c

    @pl.when(c == 0)
    def _():
        if has_init:
            zero = jnp.zeros((HEAD_DIM, HEAD_DIM), F32)
            for b in range(group):
                for p in range(N_PAIRS):
                    top = jnp.concatenate([s0_ref[b, 2 * p], zero], axis=1)
                    bot = jnp.concatenate([zero, s0_ref[b, 2 * p + 1]], axis=1)
                    state[b * N_PAIRS + p] = jnp.concatenate([top, bot], axis=0)
        else:
            state[...] = jnp.zeros_like(state)

    new_states = []
    for b, p in [(b, p) for b in range(group) for p in range(N_PAIRS)]:
        sl = slice(p * LANES, (p + 1) * LANES)
        q = q_ref[b, :, sl]
        k = k_ref[b, :, sl] * HEAD_DIM ** -0.5
        vb = v_ref[b, :, sl].astype(BF16)
        kb = k.astype(BF16)
        q2 = jnp.concatenate([jnp.where(lo, q, 0.0), jnp.where(lo, 0.0, q)], axis=0).astype(BF16)
        inner = _dot((_dot_nt(q2, kb) * dm_ref[p]).astype(BF16), vb)
        s_old = state[b * N_PAIRS + p]
        cross = _dot(q.astype(BF16), s_old.astype(BF16)) * rd_ref[:, sl]
        o = jnp.where(lo, inner[:chunk], inner[chunk:]) + cross
        upd = _dot_tn((k * kd_ref[:, sl]).astype(BF16), vb)
        s_new = sd_ref[:, sl] * s_old + jnp.where(same_head, upd, 0.0)
        state[b * N_PAIRS + p] = s_new
        new_states.append(s_new)

        def head_mean(t):
            m_lo = jnp.sum(jnp.where(lo, t, 0.0), axis=-1, keepdims=True)
            m_hi = jnp.sum(jnp.where(lo, 0.0, t), axis=-1, keepdims=True)
            return jnp.where(lo, m_lo, m_hi) * (1.0 / HEAD_DIM)

        d = o - head_mean(o)
        n = d * lax.rsqrt(head_mean(d * d) + EPS) * gret_ref[:, sl]
        g = gate_ref[b, :, sl]
        y_ref[b, :, sl] = (g * jax.nn.sigmoid(g) * n).astype(y_ref.dtype)

    @pl.when(c == pl.num_programs(1) - 1)
    def _():
        for bp, s_new in enumerate(new_states):
            b, p = divmod(bp, N_PAIRS)
            st_ref[b, 2 * p] = s_new[:HEAD_DIM, :HEAD_DIM]
            st_ref[b, 2 * p + 1] = s_new[HEAD_DIM:, HEAD_DIM:]


def _retention(q, k, v, gate, state0, g_ret, n_batch, seq, chunk, group):
    nc = seq // chunk
    by_seq = lambda t: t.reshape(n_batch, seq, D_HALF)
    has_init = state0 is not None
    if state0 is None:
        state0 = jnp.zeros((group, N_HEADS, HEAD_DIM, HEAD_DIM), F32)
        s0_map = lambda b, c: (0, 0, 0, 0)
    else:
        s0_map = lambda b, c: (b, 0, 0, 0)
    dmask, row_decay, key_decay, state_decay = _ret_tables(chunk)
    rows = pl.BlockSpec((group, chunk, D_HALF), lambda b, c: (b, c, 0))
    const2 = lambda shape: pl.BlockSpec(shape, lambda b, c: (0, 0))
    st_block = pl.BlockSpec((group, N_HEADS, HEAD_DIM, HEAD_DIM), lambda b, c: (b, 0, 0, 0))
    y, state = pl.pallas_call(
        functools.partial(_ret_kernel, chunk=chunk, group=group, has_init=has_init),
        grid=(n_batch // group, nc),
        in_specs=[rows, rows, rows, rows,
                  pl.BlockSpec((group, N_HEADS, HEAD_DIM, HEAD_DIM), s0_map),
                  const2((1, D_HALF)),
                  pl.BlockSpec((N_PAIRS, 2 * chunk, chunk), lambda b, c: (0, 0, 0)),
                  const2((chunk, D_HALF)), const2((chunk, D_HALF)), const2((1, D_HALF))],
        out_specs=[rows, st_block],
        out_shape=[jax.ShapeDtypeStruct((n_batch, seq, D_HALF), BF16),
                   jax.ShapeDtypeStruct((n_batch, N_HEADS, HEAD_DIM, HEAD_DIM), F32)],
        scratch_shapes=[pltpu.VMEM((group * N_PAIRS, LANES, LANES), F32)],
        compiler_params=_params(2),
        name="retention",
    )(by_seq(q), by_seq(k), by_seq(v), by_seq(gate), state0, g_ret.reshape(1, D_HALF), dmask, row_decay,
      key_decay, state_decay)
    return y.reshape(n_batch * seq, D_HALF), state


def _band_table(n_q, n_k, dilation):
    step = np.arange(n_q)[:, None] + (n_k - n_q) - np.arange(n_k)[None, :]
    in_win = (step >= 0) & (step <= ATT_BLOCK)
    bias = -_alibi_slopes()[:, None, None] * (step * dilation).astype(np.float64)[None]
    return jnp.asarray(np.where(in_win[None], bias, NEG_INF).reshape(N_PAIRS, 2 * n_q, n_k), F32)


def _band_attention(q, k, v, table_ref, key_ok, pair):
    n_q = q.shape[0]
    lo = lax.broadcasted_iota(jnp.int32, q.shape, 1) < HEAD_DIM
    q2 = jnp.concatenate([jnp.where(lo, q, 0.0), jnp.where(lo, 0.0, q)], axis=0).astype(BF16)
    s = _dot_nt(q2, k.astype(BF16)) * HEAD_DIM ** -0.5 + table_ref[pair]
    s = jnp.where(key_ok, s, NEG_INF)
    m = jnp.max(s, axis=-1, keepdims=True)
    e = jnp.exp(s - m)
    l = jnp.sum(e, axis=-1, keepdims=True)
    o = _dot(e.astype(BF16), v.astype(BF16))
    return tuple(jnp.where(lo, t[:n_q], t[n_q:]) for t in (o, m, l))


def _attp_kernel(*refs):
    n_back = ATT_BACK
    q_ref = refs[0]
    k_refs = refs[1:2 + n_back]
    v_refs = refs[2 + n_back:3 + 2 * n_back]
    t_refs = refs[3 + 2 * n_back:3 + 2 * n_back + len(DIL_PATTERNS)]
    o_ref, parts, kcat, vcat = refs[3 + 2 * n_back + len(DIL_PATTERNS):]
    n = pl.program_id(1)

    def key_col(n_q, n_k):
        return lax.broadcasted_iota(jnp.int32, (2 * n_q, n_k), 1)

    def fold(first, rows, news):
        if not first:
            olds = [[parts[kind, p, rows, :] for kind in range(3)] for p in range(N_PAIRS)]
            merged = []
            for (o_old, m_old, l_old), (o_new, m_new, l_new) in zip(olds, news):
                m = jnp.maximum(m_old, m_new)
                a, b = jnp.exp(m_old - m), jnp.exp(m_new - m)
                merged.append((o_old * a + o_new * b, m, l_old * a + l_new * b))
            news = merged
        for p, new in enumerate(news):
            for kind, val in enumerate(new):
                parts[kind, p, rows, :] = val

    for pat, (window, dil) in enumerate(DIL_PATTERNS):
        t_ref = t_refs[pat]
        if dil == 1:
            for cat, src in ((kcat, k_refs), (vcat, v_refs)):
                cat[:, :ATT_BLOCK, :] = src[1][:, ATT_Q - ATT_BLOCK:, :]
                cat[:, ATT_BLOCK:, :] = src[0][...]

            def body(it, carry, t_ref=t_ref, first=pat == 0):
                for u in range(ATT_UNROLL):
                    sub = it * ATT_UNROLL + u
                    r0 = pl.multiple_of(sub * ATT_BLOCK, ATT_BLOCK)
                    ok = (key_col(ATT_BLOCK, 2 * ATT_BLOCK) >= ATT_BLOCK) | (n > 0) | (sub > 0)
                    fold(first, pl.ds(r0, ATT_BLOCK),
                         [_band_attention(q_ref[p, pl.ds(r0, ATT_BLOCK), :], kcat[p, pl.ds(r0, 2 * ATT_BLOCK), :],
                                          vcat[p, pl.ds(r0, 2 * ATT_BLOCK), :], t_ref, ok, p)
                          for p in range(N_PAIRS)])
                return carry

            lax.fori_loop(0, ATT_Q // ATT_BLOCK // ATT_UNROLL, body, 0)
            continue
        sub_rows = min(window, ATT_Q)
        n_q = sub_rows // dil
        n_prev = window // sub_rows
        for s in range(ATT_Q // sub_rows):
            pieces = []
            for i in range(n_prev, -1, -1):
                rel = (s - i) * sub_rows
                j = -(rel // ATT_Q)
                pieces.append((j, rel + j * ATT_Q))
            missing = sum(jnp.where(n < j, n_q, 0) for j, _ in pieces)

            unroll = ATT_UNROLL if n_q == ATT_BLOCK else 1

            def body(it, carry, dil=dil, n_q=n_q, t_ref=t_ref, first=pat == 0, pieces=pieces, missing=missing,
                     q_off=s * sub_rows, unroll=unroll):
                ok = key_col(n_q, len(pieces) * n_q) >= missing
                for u in range(unroll):
                    r = it * unroll + u
                    rows = pl.ds(q_off + r, n_q, stride=dil)
                    gather = lambda refs, p: jnp.concatenate(
                        [refs[j][p, pl.ds(off + r, n_q, stride=dil), :] for j, off in pieces], axis=0)
                    fold(first, rows, [_band_attention(q_ref[p, rows, :], gather(k_refs, p), gather(v_refs, p),
                                                       t_ref, ok, p) for p in range(N_PAIRS)])
                return carry

            lax.fori_loop(0, dil // unroll, body, 0)

    for p in range(N_PAIRS):
        o_ref[:, p * LANES:(p + 1) * LANES] = (parts[0, p] / parts[2, p]).astype(o_ref.dtype)


def _att_prompt(q, k, v, n_batch, seq):
    nb = seq // ATT_Q
    assert all(w // d == ATT_BLOCK and ATT_Q % min(w, ATT_Q) == 0 and w % min(w, ATT_Q) == 0
               for w, d in DIL_PATTERNS)
    assert ATT_BACK == DIL_PATTERNS[-1][0] // ATT_Q and DIL_PATTERNS[0] == (ATT_BLOCK, 1)
    blk = lambda j: pl.BlockSpec((N_PAIRS, ATT_Q, LANES),
                                 lambda b, i: (0, b * nb + jnp.maximum(i - j, 0), 0))
    tables = [_band_table(ATT_BLOCK, 2 * ATT_BLOCK, 1)] + [
        _band_table(min(w, ATT_Q) // d, (w // min(w, ATT_Q) + 1) * (min(w, ATT_Q) // d), d)
        for w, d in DIL_PATTERNS[1:]]
    back = [blk(j) for j in range(ATT_BACK + 1)]
    return pl.pallas_call(
        _attp_kernel,
        grid=(n_batch, nb),
        in_specs=[blk(0)] + back + back + [pl.BlockSpec(t.shape, lambda b, i: (0, 0, 0)) for t in tables],
        out_specs=pl.BlockSpec((ATT_Q, D_HALF), lambda b, i: (b * nb + i, 0)),
        out_shape=jax.ShapeDtypeStruct((n_batch * seq, D_HALF), BF16),
        scratch_shapes=[pltpu.VMEM((3, N_PAIRS, ATT_Q, LANES), F32),
                        pltpu.VMEM((N_PAIRS, ATT_Q + ATT_BLOCK, LANES), F32),
                        pltpu.VMEM((N_PAIRS, ATT_Q + ATT_BLOCK, LANES), F32)],
        compiler_params=_params(2, VMEM_LIMIT),
        name="att_prompt",
    )(q, *([k] * (ATT_BACK + 1)), *([v] * (ATT_BACK + 1)), *tables)


def _atts_tables(buf, t_new):
    pos_q = buf + np.arange(t_new)
    pos_k = np.concatenate([np.arange(buf), buf + np.arange(LANES)])
    dist = pos_q[:, None] - pos_k[None, :]
    count = np.zeros(dist.shape)
    for window, dilation in DIL_PATTERNS:
        count += (dist >= 0) & (dist <= window) & (dist % dilation == 0)
    count[:, buf + t_new:] = 0
    bias = -_alibi_slopes()[:, None, None] * dist[None].astype(np.float64)
    table = np.where(count[None] > 0, bias + np.log(np.maximum(count, 1.0))[None], NEG_INF)
    table = table.reshape(N_HEADS * t_new, buf + LANES)
    return jnp.asarray(table[:, :buf], F32), jnp.asarray(table[:, buf:], F32)


def _atts_kernel(q_ref, kt_ref, kn_ref, vt_ref, vn_ref, tc_ref, tn_ref, o_ref, *, t_new):
    rows = N_HEADS * t_new
    row_head = lax.broadcasted_iota(jnp.int32, (rows, D_HALF), 0) // t_new
    col_head = lax.broadcasted_iota(jnp.int32, (rows, D_HALF), 1) // HEAD_DIM
    own = row_head == col_head
    q_heads = jnp.where(own, jnp.concatenate([q_ref[...]] * N_HEADS, axis=0), 0.0).astype(BF16)
    pad = jnp.zeros((LANES - t_new, D_HALF), F32)
    kn = jnp.concatenate([kn_ref[...], pad], axis=0).astype(BF16)
    vn = jnp.concatenate([vn_ref[...], pad], axis=0).astype(BF16)
    scale = HEAD_DIM ** -0.5
    sc = _dot(q_heads, kt_ref[...].astype(BF16)) * scale + tc_ref[...]
    sn = _dot_nt(q_heads, kn) * scale + tn_ref[...]
    m = jnp.maximum(jnp.max(sc, axis=-1, keepdims=True), jnp.max(sn, axis=-1, keepdims=True))
    ec = jnp.exp(sc - m)
    en = jnp.exp(sn - m)
    l = jnp.sum(ec, axis=-1, keepdims=True) + jnp.sum(en, axis=-1, keepdims=True)
    acc = _dot_nt(ec.astype(BF16), vt_ref[...].astype(BF16)) + _dot(en.astype(BF16), vn)
    acc = jnp.where(own, acc, 0.0) / l
    out = acc[:t_new]
    for h in range(1, N_HEADS):
        out = out + acc[h * t_new:(h + 1) * t_new]
    o_ref[...] = out.astype(o_ref.dtype)


def _att_sample(q, k_new, v_new, cache_k, cache_v, n_batch, t_new):
    buf = cache_k.shape[1]
    tc, tn = _atts_tables(buf, t_new)
    by_pos = lambda c: c.transpose(0, 2, 3, 1).reshape(n_batch, D_HALF, buf)
    rows = pl.BlockSpec((t_new, D_HALF), lambda b: (b, 0))
    cache = pl.BlockSpec((None, D_HALF, buf), lambda b: (b, 0, 0))
    return pl.pallas_call(
        functools.partial(_atts_kernel, t_new=t_new),
        grid=(n_batch,),
        in_specs=[rows, cache, rows, cache, rows,
                  pl.BlockSpec((N_HEADS * t_new, buf), lambda b: (0, 0)),
                  pl.BlockSpec((N_HEADS * t_new, LANES), lambda b: (0, 0))],
        out_specs=rows,
        out_shape=jax.ShapeDtypeStruct((n_batch * t_new, D_HALF), BF16),
        compiler_params=_params(1, VMEM_LIMIT),
        name="att_sample",
    )(q, by_pos(cache_k), k_new, by_pos(cache_v), v_new, tc, tn)


def _out_kernel(ret_ref, att_ref, x_ref, gt_ref, sc_ref, sh_ref, w_ref, gpost_ref, gpre_ref, wr_ref, br_ref,
                x1_ref, h_ref, top_ref):
    tm = x_ref.shape[0]
    mix = _dot(ret_ref[...], w_ref[:D_HALF, :]) + _dot(att_ref[...], w_ref[D_HALF:, :])
    x1 = x_ref[...] + gt_ref[...] * _rms(mix, gpost_ref[...])
    x1_ref[...] = x1
    h = _rms(x1, gpre_ref[...]) * (1.0 + sc_ref[...]) + sh_ref[...]
    _slab_store(h_ref, h)
    split = lambda t: (t.astype(BF16), (t - t.astype(BF16).astype(F32)).astype(BF16))
    (h_hi, h_lo), (w_hi, w_lo) = split(h), split(wr_ref[...])
    logits = _dot(h_hi, w_hi) + _dot(h_lo, w_hi) + _dot(h_hi, w_lo) + br_ref[...]
    lane_e = lax.broadcasted_iota(jnp.int32, (tm, N_EXPERTS), 1).astype(F32)
    lane_o = lax.broadcasted_iota(jnp.int32, (tm, LANES), 1)
    vals, idxs = [], []
    work = logits
    for _ in range(TOP_K):
        v = jnp.max(work, axis=-1, keepdims=True)
        i = jnp.min(jnp.where(work == v, lane_e, float(N_EXPERTS)), axis=-1, keepdims=True)
        vals.append(v)
        idxs.append(i)
        work = jnp.where(lane_e == i, -jnp.inf, work)
    es = [jnp.exp(v - vals[0]) for v in vals]
    tot = sum(es)
    packed = jnp.zeros((tm, LANES), F32)
    for j in range(TOP_K):
        packed = jnp.where(lane_o == j, idxs[j], packed)
        packed = jnp.where(lane_o == TOP_K + j, es[j] / tot, packed)
    top_ref[...] = packed


def _mixer_out(ret_y, att, x, mod, w_out_bf, g_post, g_pre, w_router, b_router, per_row, rows_per_batch):
    n = x.shape[0]
    tm = min(TM_OUT, n)
    tpb = 1 if per_row else rows_per_batch // tm
    half = pl.BlockSpec((tm, D_HALF), lambda i: (i, 0))
    full = pl.BlockSpec((tm, D_MODEL), lambda i: (i, 0))
    stat = pl.BlockSpec((tm, LANES), lambda i: (i, 0))
    vec = pl.BlockSpec((1, D_MODEL), lambda i: (0, 0))
    mods = [_mod_spec(tm, c, per_row, tpb) for c in (MOD_GATE_MIX, MOD_SCALE_FFN, MOD_SHIFT_FFN)]
    return pl.pallas_call(
        _out_kernel,
        grid=(n // tm,),
        in_specs=[half, half, full] + mods + [
                  pl.BlockSpec((D_MODEL, D_MODEL), lambda i: (0, 0)), vec, vec,
                  pl.BlockSpec((D_MODEL, N_EXPERTS), lambda i: (0, 0)),
                  pl.BlockSpec((1, N_EXPERTS), lambda i: (0, 0))],
        out_specs=[full, pl.BlockSpec((tm * SLAB, LANES), lambda i: (i, 0)), stat],
        out_shape=[jax.ShapeDtypeStruct((n, D_MODEL), F32),
                   jax.ShapeDtypeStruct((n * SLAB, LANES), F32),
                   jax.ShapeDtypeStruct((n, LANES), F32)],
        compiler_params=_params(1, VMEM_LIMIT),
        name="mixer_out",
    )(ret_y, att, x, mod, mod, mod, w_out_bf, g_post, g_pre, w_router, b_router)


def _route_kernel(top_ref, dest_ref, ends_ref, counts_ref, running, starts, *, tm, tm_blk):
    ph, i = pl.program_id(0), pl.program_id(1)
    lane_f = lax.broadcasted_iota(jnp.int32, (tm, LANES), 1).astype(F32)
    top = top_ref[...]
    picks = [lane_f == top[:, k:k + 1] for k in range(TOP_K)]
    chosen = sum(p.astype(F32) for p in picks)
    tile_counts = jnp.sum(chosen, axis=0, keepdims=True)

    @pl.when((ph == 0) & (i == 0))
    def _():
        running[...] = jnp.zeros_like(running)

    @pl.when(ph == 0)
    def _():
        running[...] += tile_counts

    @pl.when((ph == 1) & (i == 0))
    def _():
        lane = lax.broadcasted_iota(jnp.int32, (1, LANES), 1)
        assert tm_blk & (tm_blk - 1) == 0
        counts = running[...].astype(jnp.int32)
        counts_ref[...] = counts
        blocks = (counts + (tm_blk - 1)) >> (tm_blk.bit_length() - 1)
        ends = blocks
        shift = 1
        while shift < N_EXPERTS:
            ends = ends + jnp.where(lane >= shift, pltpu.roll(ends, shift, axis=1), 0)
            shift *= 2
        ends_ref[...] = ends
        starts[...] = ((ends - blocks) * tm_blk).astype(F32)
        running[...] = jnp.zeros_like(running)

    @pl.when(ph == 1)
    def _():
        earlier = (lax.broadcasted_iota(jnp.int32, (tm, tm), 0)
                   > lax.broadcasted_iota(jnp.int32, (tm, tm), 1))
        ahead = _dot(jnp.where(earlier, 1.0, 0.0).astype(BF16), chosen.astype(BF16))
        slot = starts[...] + running[...] + ahead
        lane_o = lax.broadcasted_iota(jnp.int32, (tm, LANES), 1)
        packed = jnp.zeros((tm, LANES), F32)
        for k, pick in enumerate(picks):
            dest_k = jnp.sum(jnp.where(pick, slot, 0.0), axis=-1, keepdims=True)
            packed = jnp.where(lane_o == k, dest_k, packed)
        dest_ref[...] = packed.astype(jnp.int32)
        running[...] += tile_counts


def _route(top, tm_blk):
    n = top.shape[0]
    tm = 512
    per_expert = pl.BlockSpec((1, LANES), lambda ph, i: (0, 0))
    return pl.pallas_call(
        functools.partial(_route_kernel, tm=tm, tm_blk=tm_blk),
        grid=(2, n // tm),
        in_specs=[pl.BlockSpec((tm, LANES), lambda ph, i: (i, 0))],
        out_specs=[pl.BlockSpec((tm, LANES), lambda ph, i: (i * ph, 0)), per_expert, per_expert],
        out_shape=[jax.ShapeDtypeStruct((n, LANES), jnp.int32),
                   jax.ShapeDtypeStruct((1, LANES), jnp.int32), jax.ShapeDtypeStruct((1, LANES), jnp.int32)],
        scratch_shapes=[pltpu.VMEM((1, LANES), F32), pltpu.VMEM((1, LANES), F32)],
        compiler_params=_params(2),
        name="route",
    )(top)


def _dispatch_kernel(ends_ref, dest_ref, *rest, tm, tm_blk, n_blocks, tiles):
    h_refs, (xs_hbm, zeros, sem) = rest[:-3], rest[-3:]
    i = pl.program_id(0)

    def fill(b):
        rows = pl.ds(pl.multiple_of(b * (tm_blk * SLAB), tm_blk * SLAB), tm_blk * SLAB)
        return pltpu.make_async_copy(zeros, xs_hbm.at[rows], sem.at[1])

    def fill_start(b, carry):
        fill(b).start()
        return carry

    def fill_wait(b, carry):
        fill(b).wait()
        return carry

    def zero_padding():
        zeros[...] = jnp.zeros_like(zeros)
        n_used = ends_ref[N_EXPERTS - 1]
        for do in (fill_start, fill_wait):
            for e in range(N_EXPERTS):
                first = ends_ref[e - 1] if e else 0

                @pl.when(ends_ref[e] > first)
                def _():
                    do(ends_ref[e] - 1, 0)

            lax.fori_loop(n_used, n_blocks, do, 0)

    pl.when(i == 0)(zero_padding)

    def scatter(h_ref):
        def body(jj, carry):
            for j in (jj * DMA_UNROLL + u for u in range(DMA_UNROLL)):
                for k in range(TOP_K):
                    pltpu.make_async_copy(h_ref.at[_slab_rows(j)], xs_hbm.at[_slab_rows(dest_ref[j * TOP_K + k])],
                                          sem.at[0]).start(priority=k % 2)
            return carry

        lax.fori_loop(0, tm // DMA_UNROLL, body, 0)
        for k in range(TOP_K):
            pltpu.make_async_copy(h_ref, xs_hbm.at[pl.ds(0, tm * SLAB)], sem.at[0]).wait()

    start = 0
    for h_ref, n_tiles in zip(h_refs, tiles):
        pl.when((i >= start) & (i < start + n_tiles))(functools.partial(scatter, h_ref))
        start += n_tiles


def _dispatch(hs, dest_flat, ends, n_blocks, tm_blk):
    tm = 256
    tiles = [h.shape[0] // SLAB // tm for h in hs]
    starts = [sum(tiles[:s]) for s in range(len(hs))]
    src = lambda s: pl.BlockSpec((tm * SLAB, LANES),
                                 lambda i, ends: (jnp.clip(i - starts[s], 0, tiles[s] - 1), 0))
    grid_spec = pltpu.PrefetchScalarGridSpec(
        num_scalar_prefetch=1,
        grid=(sum(tiles),),
        in_specs=[pl.BlockSpec((tm * TOP_K,), lambda i, ends: (i,), memory_space=pltpu.SMEM)]
        + [src(s) for s in range(len(hs))],
        out_specs=pl.BlockSpec(memory_space=pl.ANY),
        scratch_shapes=[pltpu.VMEM((tm_blk * SLAB, LANES), F32), pltpu.SemaphoreType.DMA((2,))],
    )
    return pl.pallas_call(
        functools.partial(_dispatch_kernel, tm=tm, tm_blk=tm_blk, n_blocks=n_blocks, tiles=tiles),
        grid_spec=grid_spec,
        out_shape=jax.ShapeDtypeStruct((n_blocks * tm_blk * SLAB, LANES), F32),
        compiler_params=_params(1),
        name="dispatch",
    )(ends, dest_flat, *hs)


def _moe_kernel(be_ref, nu_ref, valid_ref, x_ref, wgu_ref, wd_ref, bg_ref, bu_ref, bd_ref, o_ref,
                wt, wg_t, wu_t, wd_bf):
    i = pl.program_id(0)
    n_used = nu_ref[0]
    valid = valid_ref[i]
    tm = x_ref.shape[0] // SLAB
    d, d_gu = wgu_ref.shape
    d_ff = d_gu // 2
    new_expert = (i == 0) | (be_ref[i] != be_ref[jnp.maximum(i - 1, 0)])

    @pl.when((i < n_used) & new_expert)
    def _():
        for c in range(d // LANES):
            rows = slice(c * LANES, (c + 1) * LANES)
            wt[...] = wgu_ref[rows, :].T
            wg_t[:, rows] = wt[pl.ds(0, d_ff, stride=2), :].astype(BF16)
            wu_t[:, rows] = wt[pl.ds(1, d_ff, stride=2), :].astype(BF16)
        wd_bf[...] = wd_ref[...].astype(BF16)

    def experts_mlp(rows):
        x = _slab_load(x_ref, rows).astype(BF16)
        acc = bd_ref[...]
        for c in range(d_ff // MOE_COLS):
            cols = slice(c * MOE_COLS, (c + 1) * MOE_COLS)
            gate = jnp.minimum(_dot_nt(x, wg_t[cols, :]) + bg_ref[:, cols], SWIGLU_LIMIT)
            up = jnp.clip(_dot_nt(x, wu_t[cols, :]) + bu_ref[:, cols], -SWIGLU_LIMIT, SWIGLU_LIMIT)
            act = gate * jax.nn.sigmoid(SWIGLU_ALPHA * gate) * (up + 1.0)
            acc = acc + _dot(act.astype(BF16), wd_bf[cols, :])
        _slab_store(o_ref, acc)
        if rows < tm:
            o_ref[rows * SLAB:, :] = jnp.zeros(((tm - rows) * SLAB, LANES), o_ref.dtype)

    sizes = [tm // 4, tm // 2, tm]
    for lo, rows in zip([0] + sizes[:-1], sizes):
        pl.when((i < n_used) & (valid > lo) & (valid <= rows))(functools.partial(experts_mlp, rows))

    @pl.when(i >= n_used)
    def _():
        o_ref[...] = jnp.zeros_like(o_ref)


def _moe_blocks(xs, block_e, n_used, valid, w_gu, w_down, b_gu, b_down, tm):
    n_blocks = block_e.shape[0]
    n_e, d, d_gu = w_gu.shape
    d_ff = d_gu // 2
    wspec = lambda k, n: pl.BlockSpec((None, k, n), lambda i, be, nu, valid: (be[i], 0, 0))
    grid_spec = pltpu.PrefetchScalarGridSpec(
        num_scalar_prefetch=3,
        grid=(n_blocks,),
        in_specs=[pl.BlockSpec((tm * SLAB, LANES), lambda i, be, nu, valid: (jnp.minimum(i, nu[0] - 1), 0)),
                  wspec(d, d_gu), wspec(d_ff, d), wspec(1, d_ff), wspec(1, d_ff), wspec(1, d)],
        out_specs=pl.BlockSpec((tm * SLAB, LANES), lambda i, be, nu, valid: (i, 0)),
        scratch_shapes=[pltpu.VMEM((d_gu, LANES), F32), pltpu.VMEM((d_ff, d), BF16),
                        pltpu.VMEM((d_ff, d), BF16), pltpu.VMEM((d_ff, d), BF16)],
    )
    return pl.pallas_call(
        _moe_kernel,
        grid_spec=grid_spec,
        out_shape=jax.ShapeDtypeStruct((n_blocks * tm * SLAB, LANES), F32),
        compiler_params=_params(1, VMEM_LIMIT),
        name="moe_blocks",
    )(block_e, n_used, valid, xs, w_gu, w_down, b_gu[:, None, 0::2], b_gu[:, None, 1::2], b_down.reshape(n_e, 1, d))


def _fin_kernel(dest_ref, next_ref, rows_hbm, top_ref, x1_ref, gt_ref, g_ref, o_ref, buf, sem, *, tm):
    i = pl.program_id(0)
    slot = i % 2

    def gather(idx_ref, s):
        def body(jj, carry):
            for j in (jj * DMA_UNROLL + u for u in range(DMA_UNROLL)):
                for k in range(TOP_K):
                    pltpu.make_async_copy(rows_hbm.at[_slab_rows(idx_ref[j * TOP_K + k])],
                                          buf.at[s, k, _slab_rows(j)], sem.at[s]).start(priority=k % 2)
            return carry
        lax.fori_loop(0, tm // DMA_UNROLL, body, 0)

    @pl.when(i == 0)
    def _():
        gather(dest_ref, 0)

    @pl.when(i + 1 < pl.num_programs(0))
    def _():
        gather(next_ref, 1 - slot)

    for k in range(TOP_K):
        pltpu.make_async_copy(rows_hbm.at[pl.ds(0, tm * SLAB)], buf.at[slot, k], sem.at[slot]).wait()
    top = top_ref[...]
    f = sum(_slab_load(buf.at[slot, k], tm) * top[:, TOP_K + k:TOP_K + k + 1] for k in range(TOP_K))
    o_ref[...] = x1_ref[...] + gt_ref[...] * _rms(f, g_ref[...])


def _finish(dest, rows, top, x1, mod, g_post, per_row, rows_per_batch):
    n = x1.shape[0]
    tm = min(TM_FIN, n)
    full = pl.BlockSpec((tm, D_MODEL), lambda i: (i, 0))
    gate = _mod_spec(tm, MOD_GATE_FFN, per_row, 1 if per_row else rows_per_batch // tm)
    return pl.pallas_call(
        functools.partial(_fin_kernel, tm=tm),
        grid=(n // tm,),
        in_specs=[pl.BlockSpec((tm * TOP_K,), lambda i: (i,), memory_space=pltpu.SMEM),
                  pl.BlockSpec((tm * TOP_K,), lambda i: (jnp.minimum(i + 1, n // tm - 1),),
                               memory_space=pltpu.SMEM),
                  pl.BlockSpec(memory_space=pl.ANY),
                  pl.BlockSpec((tm, LANES), lambda i: (i, 0)), full, gate,
                  pl.BlockSpec((1, D_MODEL), lambda i: (0, 0))],
        out_specs=full,
        out_shape=jax.ShapeDtypeStruct((n, D_MODEL), F32),
        scratch_shapes=[pltpu.VMEM((2, TOP_K, tm * SLAB, LANES), F32), pltpu.SemaphoreType.DMA((2,))],
        compiler_params=_params(1),
        name="finish",
    )(dest, dest, rows, top, x1, mod, g_post)


def kernel(x_prompt, x_sample, state_ret, cache_win_k, cache_win_v, c_prompt, c_sample, w_ada, b_ada,
           g_pre_mix, g_post_mix, g_pre_ffn, g_post_ffn, w_in, g_ret, w_out, w_router, b_router,
           w_gate_up, b_gate_up, w_down, b_down):
    depth = w_in.shape[0]
    assert depth == 1, "single-layer step"
    n_b, seq, d = x_prompt.shape
    n_db, t_new, _ = x_sample.shape
    n_p, n_s = n_b * seq, n_db * t_new
    buf = cache_win_k.shape[2]
    keep = min(DIL_PATTERNS[-1][0], seq)
    vec = lambda g: g[0].reshape(1, -1)

    mod_p, mod_s = _ada(c_prompt, c_sample, w_ada[0], b_ada[0], t_new)
    mod_p = mod_p[:n_b].reshape(n_b, 1, -1)
    mod_s = mod_s.reshape(n_s, -1)

    w_in_bf = w_in[0].astype(BF16)
    w_out_bf = w_out[0].astype(BF16)
    xp = x_prompt.reshape(n_p, d)
    xs = x_sample.reshape(n_s, d)
    router = (w_router[0], b_router[0].reshape(1, -1))

    qr, kr, vr, gr, qa_pm, ka_pm, va_pm, k_win, v_win = _in_proj(xp, mod_p, vec(g_pre_mix), w_in_bf, False, seq, keep)
    ret_y_p, ret_state_p = _retention(qr, kr, vr, gr, None, g_ret[0], n_b, seq, RET_CHUNK, n_b)
    att_p = _att_prompt(qa_pm, ka_pm, va_pm, n_b, seq)
    x1_p, h_p, top_p = _mixer_out(ret_y_p, att_p, xp, mod_p, w_out_bf, vec(g_post_mix), vec(g_pre_ffn),
                                  *router, False, seq)

    qr_s, kr_s, vr_s, gr_s, qa_s, ka_s, va_s = _in_proj(xs, mod_s, vec(g_pre_mix), w_in_bf, True, t_new)
    ret_y_s, ret_state_s = _retention(qr_s, kr_s, vr_s, gr_s, state_ret[0], g_ret[0], n_db, t_new, t_new,
                                      RET_GROUP_SAMPLE)
    att_s = _att_sample(qa_s, ka_s, va_s, cache_win_k[0], cache_win_v[0], n_db, t_new)
    x1_s, h_s, top_s = _mixer_out(ret_y_s, att_s, xs, mod_s, w_out_bf, vec(g_post_mix), vec(g_pre_ffn),
                                  *router, True, t_new)

    n_blocks = (n_p + n_s) * TOP_K // TM_MOE + N_EXPERTS
    dest, ends, counts = _route(jnp.concatenate([top_p, top_s], axis=0), TM_MOE)
    dest = dest[:, :TOP_K].reshape(-1)
    ends, counts = ends[0, :N_EXPERTS], counts[0, :N_EXPERTS]
    blk = jnp.arange(n_blocks, dtype=jnp.int32)
    block_e = jnp.minimum(jnp.sum(blk[:, None] >= ends[None, :], axis=1), N_EXPERTS - 1).astype(jnp.int32)
    first_blk = ends - (counts + TM_MOE - 1) // TM_MOE
    valid = jnp.clip(counts[block_e] - (blk - first_blk[block_e]) * TM_MOE, 0, TM_MOE)
    valid = jnp.where(blk < ends[N_EXPERTS - 1], valid, 0).astype(jnp.int32)
    rows_in = _dispatch([h_p, h_s], dest, ends, n_blocks, TM_MOE)
    rows = _moe_blocks(rows_in, block_e, ends[N_EXPERTS - 1:], valid, w_gate_up[0], w_down[0],
                       b_gate_up[0], b_down[0], TM_MOE)
    y_p = _finish(dest[:n_p * TOP_K], rows, top_p, x1_p, mod_p, vec(g_post_ffn), False, seq)
    y_s = _finish(dest[n_p * TOP_K:], rows, top_s, x1_s, mod_s, vec(g_post_ffn), True, t_new)

    win = lambda t: t.reshape(n_b, N_HEADS, HEAD_DIM, keep).transpose(0, 3, 1, 2)[None]
    new = lambda t: t.reshape(n_db, t_new, N_HEADS, HEAD_DIM)[None]
    return (y_p.reshape(n_b, seq, d), y_s.reshape(n_db, t_new, d), ret_state_p[None], ret_state_s[None],
            win(k_win), win(v_win), new(ka_s), new(va_s))
```

```python
import functools

import numpy as np
import jax
import jax.numpy as jnp
from jax import lax
from jax.experimental import pallas as pl
from jax.experimental.pallas import tpu as pltpu

F32 = jnp.float32
BF16 = jnp.bfloat16

D_MODEL = 1024
HEAD_DIM = 64
N_HEADS = 8
D_HALF = N_HEADS * HEAD_DIM
N_SEG = 7
LANES = 128
N_PAIRS = D_HALF // LANES
RET_CHUNK = 128
RET_GROUP_SAMPLE = 8
ATT_BLOCK = 128
ATT_Q = 1024
ATT_BACK = 2
ATT_UNROLL = 2
DIL_PATTERNS = ((128, 1), (512, 4), (2048, 16))
N_EXPERTS = 32
TOP_K = 4
SWIGLU_LIMIT = 7.0
SWIGLU_ALPHA = 1.702
EPS = 1e-6
NEG_INF = -1e30
MOD_SHIFT_MIX, MOD_SCALE_MIX, MOD_GATE_MIX, MOD_SHIFT_FFN, MOD_SCALE_FFN, MOD_GATE_FFN = range(6)

TM_IN = 512
TM_OUT = 256
TM_MOE = 512
MOE_COLS = 512
TM_FIN = 128
DMA_UNROLL = 4
VMEM_LIMIT = 56 * 1024 * 1024


def _params(n_axes, vmem=None):
    return pltpu.CompilerParams(dimension_semantics=("arbitrary",) * n_axes, vmem_limit_bytes=vmem)


def _ret_log_decay():
    return np.log(1.0 - 2.0 ** (-5.0 - np.arange(N_HEADS, dtype=np.float64)))


def _alibi_slopes():
    return 2.0 ** (-8.0 * (np.arange(N_HEADS, dtype=np.float64) + 1.0) / N_HEADS)


def _per_head_lanes(a):
    return np.repeat(a, HEAD_DIM, axis=-1)


def _rms(x, g):
    return x * lax.rsqrt(jnp.mean(x * x, axis=-1, keepdims=True) + EPS) * g


def _dot(a, b):
    return jnp.dot(a, b, preferred_element_type=F32)


def _dot_nt(a, b):
    return lax.dot_general(a, b, (((1,), (1,)), ((), ())), preferred_element_type=F32)


def _dot_tn(a, b):
    return lax.dot_general(a, b, (((0,), (0,)), ((), ())), preferred_element_type=F32)


SLAB = D_MODEL // LANES


def _slab_rows(r):
    return pl.ds(pl.multiple_of(r * SLAB, SLAB), SLAB)


def _slab_load(ref, n):
    return jnp.concatenate([ref[pl.ds(s, n, stride=SLAB), :] for s in range(SLAB)], axis=1)


def _slab_store(ref, value):
    n = value.shape[0]
    for s in range(SLAB):
        ref[pl.ds(s, n, stride=SLAB), :] = value[:, s * LANES:(s + 1) * LANES]


def _mod_spec(tm, col, per_row, tiles_per_batch):
    if per_row:
        return pl.BlockSpec((tm, D_MODEL), lambda i: (i, col))
    return pl.BlockSpec((None, 1, D_MODEL), lambda i: (i // tiles_per_batch, 0, col))


def _ada_kernel(cp_ref, cs_ref, w_ref, b_ref, op_ref, os_ref):
    w = w_ref[...].astype(BF16)

    def mod(c):
        return _dot((c * jax.nn.sigmoid(c)).astype(BF16), w) + b_ref[...]

    op_ref[...] = mod(cp_ref[...])
    ms = mod(cs_ref[...])
    os_ref[...] = jnp.broadcast_to(ms[:, None, :], os_ref.shape)


def _ada(c_prompt, c_sample, w_ada, b_ada, t_new):
    n_b, d = c_prompt.shape
    n_db = c_sample.shape[0]
    n_out = w_ada.shape[1]
    cp = jnp.concatenate([c_prompt, jnp.zeros((-n_b % 8, d), F32)], axis=0)
    return pl.pallas_call(
        _ada_kernel,
        grid=(n_out // d,),
        in_specs=[pl.BlockSpec(cp.shape, lambda j: (0, 0)),
                  pl.BlockSpec((n_db, d), lambda j: (0, 0)),
                  pl.BlockSpec((d, d), lambda j: (0, j)),
                  pl.BlockSpec((1, d), lambda j: (0, j))],
        out_specs=[pl.BlockSpec(cp.shape, lambda j: (0, j)),
                   pl.BlockSpec((n_db, t_new, d), lambda j: (0, 0, j))],
        out_shape=[jax.ShapeDtypeStruct((cp.shape[0], n_out), F32),
                   jax.ShapeDtypeStruct((n_db, t_new, n_out), F32)],
        compiler_params=_params(1, VMEM_LIMIT),
        name="ada",
    )(cp, c_sample, w_ada, b_ada.reshape(1, n_out))


def _in_kernel(x_ref, sc_ref, sh_ref, g_ref, w_ref, *o_refs, n_plain, window):
    h = (_rms(x_ref[...], g_ref[...]) * (1.0 + sc_ref[...]) + sh_ref[...]).astype(BF16)
    segs = [_dot(h, w_ref[:, s * D_HALF:(s + 1) * D_HALF]) for s in range(N_SEG)]
    for o_ref, seg in zip(o_refs[:n_plain], segs):
        o_ref[...] = seg
    if n_plain == N_SEG:
        return
    for o_ref, seg in zip(o_refs[n_plain:N_SEG], segs[n_plain:]):
        for p in range(N_PAIRS):
            o_ref[p] = seg[:, p * LANES:(p + 1) * LANES]
    tiles_per_seq, first = window
    @pl.when(pl.program_id(0) % tiles_per_seq >= first)
    def _():
        for o_ref, seg in zip(o_refs[N_SEG:], segs[N_SEG - 2:]):
            o_ref[...] = seg.T


def _in_proj(x, mod, g, w_bf, per_row, rows_per_batch, keep=None):
    n = x.shape[0]
    tm = min(TM_IN, n)
    tpb = 1 if per_row else rows_per_batch // tm
    seg = pl.BlockSpec((tm, D_HALF), lambda i: (i, 0))
    seg_shape = jax.ShapeDtypeStruct((n, D_HALF), F32)
    if keep is None:
        n_plain, window = N_SEG, None
        out_specs, out_shape = [seg] * N_SEG, [seg_shape] * N_SEG
    else:
        n_plain, window = N_SEG - 3, (tpb, (rows_per_batch - keep) // tm)
        pm = pl.BlockSpec((N_PAIRS, tm, LANES), lambda i: (0, i, 0))
        win = pl.BlockSpec((None, D_HALF, tm), lambda i: (i // tpb, 0, jnp.maximum(i % tpb - window[1], 0)))
        out_specs = [seg] * n_plain + [pm] * 3 + [win] * 2
        out_shape = ([seg_shape] * n_plain + [jax.ShapeDtypeStruct((N_PAIRS, n, LANES), F32)] * 3
                     + [jax.ShapeDtypeStruct((n // rows_per_batch, D_HALF, keep), F32)] * 2)
    return pl.pallas_call(
        functools.partial(_in_kernel, n_plain=n_plain, window=window),
        grid=(n // tm,),
        in_specs=[pl.BlockSpec((tm, D_MODEL), lambda i: (i, 0)),
                  _mod_spec(tm, MOD_SCALE_MIX, per_row, tpb), _mod_spec(tm, MOD_SHIFT_MIX, per_row, tpb),
                  pl.BlockSpec((1, D_MODEL), lambda i: (0, 0)),
                  pl.BlockSpec((D_MODEL, N_SEG * D_HALF), lambda i: (0, 0))],
        out_specs=out_specs,
        out_shape=out_shape,
        compiler_params=_params(1, VMEM_LIMIT),
        name="in_proj",
    )(x, mod, mod, g, w_bf)


def _ret_tables(chunk):
    lg = _ret_log_decay()
    pos = np.arange(chunk, dtype=np.float64)
    diff = pos[:, None] - pos[None, :]
    dmask = np.where(diff >= 0, np.exp(np.maximum(diff, 0.0)[None] * lg[:, None, None]), 0.0)
    dmask = dmask.reshape(N_PAIRS, 2 * chunk, chunk)
    row_decay = _per_head_lanes(np.exp((pos[:, None] + 1.0) * lg[None, :]))
    key_decay = _per_head_lanes(np.exp((chunk - 1.0 - pos)[:, None] * lg[None, :]))
    state_decay = _per_head_lanes(np.exp(chunk * lg)[None, :])
    return tuple(jnp.asarray(t, F32) for t in (dmask, row_decay, key_decay, state_decay))


def _ret_kernel(q_ref, k_ref, v_ref, gate_ref, s0_ref, gret_ref, dm_ref, rd_ref, kd_ref, sd_ref,
                y_ref, st_ref, state, *, chunk, group, has_init):
    c = pl.program_id(1)
    lo = lax.broadcasted_iota(jnp.int32, (chunk, LANES), 1) < HEAD_DIM
    blk_r = lax.broadcasted_iota(jnp.int32, (LANES, LANES), 0) < HEAD_DIM
    blk_c = lax.broadcasted_iota(jnp.int32, (LANES, LANES), 1) < HEAD_DIM
    same_head = blk_r == blk_c

    @pl.when(c == 0)
    def _():
        if has_init:
            zero = jnp.zeros((HEAD_DIM, HEAD_DIM), F32)
            for b in range(group):
                for p in range(N_PAIRS):
                    top = jnp.concatenate([s0_ref[b, 2 * p], zero], axis=1)
                    bot = jnp.concatenate([zero, s0_ref[b, 2 * p + 1]], axis=1)
                    state[b * N_PAIRS + p] = jnp.concatenate([top, bot], axis=0)
        else:
            state[...] = jnp.zeros_like(state)

    new_states = []
    for b, p in [(b, p) for b in range(group) for p in range(N_PAIRS)]:
        sl = slice(p * LANES, (p + 1) * LANES)
        q = q_ref[b, :, sl]
        k = k_ref[b, :, sl] * HEAD_DIM ** -0.5
        vb = v_ref[b, :, sl].astype(BF16)
        kb = k.astype(BF16)
        q2 = jnp.concatenate([jnp.where(lo, q, 0.0), jnp.where(lo, 0.0, q)], axis=0).astype(BF16)
        inner = _dot((_dot_nt(q2, kb) * dm_ref[p]).astype(BF16), vb)
        s_old = state[b * N_PAIRS + p]
        cross = _dot(q.astype(BF16), s_old.astype(BF16)) * rd_ref[:, sl]
        o = jnp.where(lo, inner[:chunk], inner[chunk:]) + cross
        upd = _dot_tn((k * kd_ref[:, sl]).astype(BF16), vb)
        s_new = sd_ref[:, sl] * s_old + jnp.where(same_head, upd, 0.0)
        state[b * N_PAIRS + p] = s_new
        new_states.append(s_new)

        def head_mean(t):
            m_lo = jnp.sum(jnp.where(lo, t, 0.0), axis=-1, keepdims=True)
            m_hi = jnp.sum(jnp.where(lo, 0.0, t), axis=-1, keepdims=True)
            return jnp.where(lo, m_lo, m_hi) * (1.0 / HEAD_DIM)

        d = o - head_mean(o)
        n = d * lax.rsqrt(head_mean(d * d) + EPS) * gret_ref[:, sl]
        g = gate_ref[b, :, sl]
        y_ref[b, :, sl] = (g * jax.nn.sigmoid(g) * n).astype(y_ref.dtype)

    @pl.when(c == pl.num_programs(1) - 1)
    def _():
        for bp, s_new in enumerate(new_states):
            b, p = divmod(bp, N_PAIRS)
            st_ref[b, 2 * p] = s_new[:HEAD_DIM, :HEAD_DIM]
            st_ref[b, 2 * p + 1] = s_new[HEAD_DIM:, HEAD_DIM:]


def _retention(q, k, v, gate, state0, g_ret, n_batch, seq, chunk, group):
    nc = seq // chunk
    by_seq = lambda t: t.reshape(n_batch, seq, D_HALF)
    has_init = state0 is not None
    if state0 is None:
        state0 = jnp.zeros((group, N_HEADS, HEAD_DIM, HEAD_DIM), F32)
        s0_map = lambda b, c: (0, 0, 0, 0)
    else:
        s0_map = lambda b, c: (b, 0, 0, 0)
    dmask, row_decay, key_decay, state_decay = _ret_tables(chunk)
    rows = pl.BlockSpec((group, chunk, D_HALF), lambda b, c: (b, c, 0))
    const2 = lambda shape: pl.BlockSpec(shape, lambda b, c: (0, 0))
    st_block = pl.BlockSpec((group, N_HEADS, HEAD_DIM, HEAD_DIM), lambda b, c: (b, 0, 0, 0))
    y, state = pl.pallas_call(
        functools.partial(_ret_kernel, chunk=chunk, group=group, has_init=has_init),
        grid=(n_batch // group, nc),
        in_specs=[rows, rows, rows, rows,
                  pl.BlockSpec((group, N_HEADS, HEAD_DIM, HEAD_DIM), s0_map),
                  const2((1, D_HALF)),
                  pl.BlockSpec((N_PAIRS, 2 * chunk, chunk), lambda b, c: (0, 0, 0)),
                  const2((chunk, D_HALF)), const2((chunk, D_HALF)), const2((1, D_HALF))],
        out_specs=[rows, st_block],
        out_shape=[jax.ShapeDtypeStruct((n_batch, seq, D_HALF), BF16),
                   jax.ShapeDtypeStruct((n_batch, N_HEADS, HEAD_DIM, HEAD_DIM), F32)],
        scratch_shapes=[pltpu.VMEM((group * N_PAIRS, LANES, LANES), F32)],
        compiler_params=_params(2),
        name="retention",
    )(by_seq(q), by_seq(k), by_seq(v), by_seq(gate), state0, g_ret.reshape(1, D_HALF), dmask, row_decay,
      key_decay, state_decay)
    return y.reshape(n_batch * seq, D_HALF), state


def _band_table(n_q, n_k, dilation):
    step = np.arange(n_q)[:, None] + (n_k - n_q) - np.arange(n_k)[None, :]
    in_win = (step >= 0) & (step <= ATT_BLOCK)
    bias = -_alibi_slopes()[:, None, None] * (step * dilation).astype(np.float64)[None]
    return jnp.asarray(np.where(in_win[None], bias, NEG_INF).reshape(N_PAIRS, 2 * n_q, n_k), F32)


def _band_attention(q, k, v, table_ref, key_ok, pair):
    n_q = q.shape[0]
    lo = lax.broadcasted_iota(jnp.int32, q.shape, 1) < HEAD_DIM
    q2 = jnp.concatenate([jnp.where(lo, q, 0.0), jnp.where(lo, 0.0, q)], axis=0).astype(BF16)
    s = _dot_nt(q2, k.astype(BF16)) * HEAD_DIM ** -0.5 + table_ref[pair]
    s = jnp.where(key_ok, s, NEG_INF)
    m = jnp.max(s, axis=-1, keepdims=True)
    e = jnp.exp(s - m)
    l = jnp.sum(e, axis=-1, keepdims=True)
    o = _dot(e.astype(BF16), v.astype(BF16))
    return tuple(jnp.where(lo, t[:n_q], t[n_q:]) for t in (o, m, l))


def _attp_kernel(*refs):
    n_back = ATT_BACK
    q_ref = refs[0]
    k_refs = refs[1:2 + n_back]
    v_refs = refs[2 + n_back:3 + 2 * n_back]
    t_refs = refs[3 + 2 * n_back:3 + 2 * n_back + len(DIL_PATTERNS)]
    o_ref, parts, kcat, vcat = refs[3 + 2 * n_back + len(DIL_PATTERNS):]
    n = pl.program_id(1)

    def key_col(n_q, n_k):
        return lax.broadcasted_iota(jnp.int32, (2 * n_q, n_k), 1)

    def fold(first, rows, news):
        if not first:
            olds = [[parts[kind, p, rows, :] for kind in range(3)] for p in range(N_PAIRS)]
            merged = []
            for (o_old, m_old, l_old), (o_new, m_new, l_new) in zip(olds, news):
                m = jnp.maximum(m_old, m_new)
                a, b = jnp.exp(m_old - m), jnp.exp(m_new - m)
                merged.append((o_old * a + o_new * b, m, l_old * a + l_new * b))
            news = merged
        for p, new in enumerate(news):
            for kind, val in enumerate(new):
                parts[kind, p, rows, :] = val

    for pat, (window, dil) in enumerate(DIL_PATTERNS):
        t_ref = t_refs[pat]
        if dil == 1:
            for cat, src in ((kcat, k_refs), (vcat, v_refs)):
                cat[:, :ATT_BLOCK, :] = src[1][:, ATT_Q - ATT_BLOCK:, :]
                cat[:, ATT_BLOCK:, :] = src[0][...]

            def body(it, carry, t_ref=t_ref, first=pat == 0):
                for u in range(ATT_UNROLL):
                    sub = it * ATT_UNROLL + u
                    r0 = pl.multiple_of(sub * ATT_BLOCK, ATT_BLOCK)
                    ok = (key_col(ATT_BLOCK, 2 * ATT_BLOCK) >= ATT_BLOCK) | (n > 0) | (sub > 0)
                    fold(first, pl.ds(r0, ATT_BLOCK),
                         [_band_attention(q_ref[p, pl.ds(r0, ATT_BLOCK), :], kcat[p, pl.ds(r0, 2 * ATT_BLOCK), :],
                                          vcat[p, pl.ds(r0, 2 * ATT_BLOCK), :], t_ref, ok, p)
                          for p in range(N_PAIRS)])
                return carry

            lax.fori_loop(0, ATT_Q // ATT_BLOCK // ATT_UNROLL, body, 0)
            continue
        sub_rows = min(window, ATT_Q)
        n_q = sub_rows // dil
        n_prev = window // sub_rows
        for s in range(ATT_Q // sub_rows):
            pieces = []
            for i in range(n_prev, -1, -1):
                rel = (s - i) * sub_rows
                j = -(rel // ATT_Q)
                pieces.append((j, rel + j * ATT_Q))
            missing = sum(jnp.where(n < j, n_q, 0) for j, _ in pieces)

            unroll = ATT_UNROLL if n_q == ATT_BLOCK else 1

            def body(it, carry, dil=dil, n_q=n_q, t_ref=t_ref, first=pat == 0, pieces=pieces, missing=missing,
                     q_off=s * sub_rows, unroll=unroll):
                ok = key_col(n_q, len(pieces) * n_q) >= missing
                for u in range(unroll):
                    r = it * unroll + u
                    rows = pl.ds(q_off + r, n_q, stride=dil)
                    gather = lambda refs, p: jnp.concatenate(
                        [refs[j][p, pl.ds(off + r, n_q, stride=dil), :] for j, off in pieces], axis=0)
                    fold(first, rows, [_band_attention(q_ref[p, rows, :], gather(k_refs, p), gather(v_refs, p),
                                                       t_ref, ok, p) for p in range(N_PAIRS)])
                return carry

            lax.fori_loop(0, dil // unroll, body, 0)

    for p in range(N_PAIRS):
        o_ref[:, p * LANES:(p + 1) * LANES] = (parts[0, p] / parts[2, p]).astype(o_ref.dtype)


def _att_prompt(q, k, v, n_batch, seq):
    nb = seq // ATT_Q
    assert all(w // d == ATT_BLOCK and ATT_Q % min(w, ATT_Q) == 0 and w % min(w, ATT_Q) == 0
               for w, d in DIL_PATTERNS)
    assert ATT_BACK == DIL_PATTERNS[-1][0] // ATT_Q and DIL_PATTERNS[0] == (ATT_BLOCK, 1)
    blk = lambda j: pl.BlockSpec((N_PAIRS, ATT_Q, LANES),
                                 lambda b, i: (0, b * nb + jnp.maximum(i - j, 0), 0))
    tables = [_band_table(ATT_BLOCK, 2 * ATT_BLOCK, 1)] + [
        _band_table(min(w, ATT_Q) // d, (w // min(w, ATT_Q) + 1) * (min(w, ATT_Q) // d), d)
        for w, d in DIL_PATTERNS[1:]]
    back = [blk(j) for j in range(ATT_BACK + 1)]
    return pl.pallas_call(
        _attp_kernel,
        grid=(n_batch, nb),
        in_specs=[blk(0)] + back + back + [pl.BlockSpec(t.shape, lambda b, i: (0, 0, 0)) for t in tables],
        out_specs=pl.BlockSpec((ATT_Q, D_HALF), lambda b, i: (b * nb + i, 0)),
        out_shape=jax.ShapeDtypeStruct((n_batch * seq, D_HALF), BF16),
        scratch_shapes=[pltpu.VMEM((3, N_PAIRS, ATT_Q, LANES), F32),
                        pltpu.VMEM((N_PAIRS, ATT_Q + ATT_BLOCK, LANES), F32),
                        pltpu.VMEM((N_PAIRS, ATT_Q + ATT_BLOCK, LANES), F32)],
        compiler_params=_params(2, VMEM_LIMIT),
        name="att_prompt",
    )(q, *([k] * (ATT_BACK + 1)), *([v] * (ATT_BACK + 1)), *tables)


def _atts_tables(buf, t_new):
    pos_q = buf + np.arange(t_new)
    pos_k = np.concatenate([np.arange(buf), buf + np.arange(LANES)])
    dist = pos_q[:, None] - pos_k[None, :]
    count = np.zeros(dist.shape)
    for window, dilation in DIL_PATTERNS:
        count += (dist >= 0) & (dist <= window) & (dist % dilation == 0)
    count[:, buf + t_new:] = 0
    bias = -_alibi_slopes()[:, None, None] * dist[None].astype(np.float64)
    table = np.where(count[None] > 0, bias + np.log(np.maximum(count, 1.0))[None], NEG_INF)
    table = table.reshape(N_HEADS * t_new, buf + LANES)
    return jnp.asarray(table[:, :buf], F32), jnp.asarray(table[:, buf:], F32)


def _atts_kernel(q_ref, kt_ref, kn_ref, vt_ref, vn_ref, tc_ref, tn_ref, o_ref, *, t_new):
    rows = N_HEADS * t_new
    row_head = lax.broadcasted_iota(jnp.int32, (rows, D_HALF), 0) // t_new
    col_head = lax.broadcasted_iota(jnp.int32, (rows, D_HALF), 1) // HEAD_DIM
    own = row_head == col_head
    q_heads = jnp.where(own, jnp.concatenate([q_ref[...]] * N_HEADS, axis=0), 0.0).astype(BF16)
    pad = jnp.zeros((LANES - t_new, D_HALF), F32)
    kn = jnp.concatenate([kn_ref[...], pad], axis=0).astype(BF16)
    vn = jnp.concatenate([vn_ref[...], pad], axis=0).astype(BF16)
    scale = HEAD_DIM ** -0.5
    sc = _dot(q_heads, kt_ref[...].astype(BF16)) * scale + tc_ref[...]
    sn = _dot_nt(q_heads, kn) * scale + tn_ref[...]
    m = jnp.maximum(jnp.max(sc, axis=-1, keepdims=True), jnp.max(sn, axis=-1, keepdims=True))
    ec = jnp.exp(sc - m)
    en = jnp.exp(sn - m)
    l = jnp.sum(ec, axis=-1, keepdims=True) + jnp.sum(en, axis=-1, keepdims=True)
    acc = _dot_nt(ec.astype(BF16), vt_ref[...].astype(BF16)) + _dot(en.astype(BF16), vn)
    acc = jnp.where(own, acc, 0.0) / l
    out = acc[:t_new]
    for h in range(1, N_HEADS):
        out = out + acc[h * t_new:(h + 1) * t_new]
    o_ref[...] = out.astype(o_ref.dtype)


def _att_sample(q, k_new, v_new, cache_k, cache_v, n_batch, t_new):
    buf = cache_k.shape[1]
    tc, tn = _atts_tables(buf, t_new)
    by_pos = lambda c: c.transpose(0, 2, 3, 1).reshape(n_batch, D_HALF, buf)
    rows = pl.BlockSpec((t_new, D_HALF), lambda b: (b, 0))
    cache = pl.BlockSpec((None, D_HALF, buf), lambda b: (b, 0, 0))
    return pl.pallas_call(
        functools.partial(_atts_kernel, t_new=t_new),
        grid=(n_batch,),
        in_specs=[rows, cache, rows, cache, rows,
                  pl.BlockSpec((N_HEADS * t_new, buf), lambda b: (0, 0)),
                  pl.BlockSpec((N_HEADS * t_new, LANES), lambda b: (0, 0))],
        out_specs=rows,
        out_shape=jax.ShapeDtypeStruct((n_batch * t_new, D_HALF), BF16),
        compiler_params=_params(1, VMEM_LIMIT),
        name="att_sample",
    )(q, by_pos(cache_k), k_new, by_pos(cache_v), v_new, tc, tn)


def _out_kernel(ret_ref, att_ref, x_ref, gt_ref, sc_ref, sh_ref, w_ref, gpost_ref, gpre_ref, wr_ref, br_ref,
                x1_ref, h_ref, top_ref):
    tm = x_ref.shape[0]
    mix = _dot(ret_ref[...], w_ref[:D_HALF, :]) + _dot(att_ref[...], w_ref[D_HALF:, :])
    x1 = x_ref[...] + gt_ref[...] * _rms(mix, gpost_ref[...])
    x1_ref[...] = x1
    h = _rms(x1, gpre_ref[...]) * (1.0 + sc_ref[...]) + sh_ref[...]
    _slab_store(h_ref, h)
    split = lambda t: (t.astype(BF16), (t - t.astype(BF16).astype(F32)).astype(BF16))
    (h_hi, h_lo), (w_hi, w_lo) = split(h), split(wr_ref[...])
    logits = _dot(h_hi, w_hi) + _dot(h_lo, w_hi) + _dot(h_hi, w_lo) + br_ref[...]
    lane_e = lax.broadcasted_iota(jnp.int32, (tm, N_EXPERTS), 1).astype(F32)
    lane_o = lax.broadcasted_iota(jnp.int32, (tm, LANES), 1)
    vals, idxs = [], []
    work = logits
    for _ in range(TOP_K):
        v = jnp.max(work, axis=-1, keepdims=True)
        i = jnp.min(jnp.where(work == v, lane_e, float(N_EXPERTS)), axis=-1, keepdims=True)
        vals.append(v)
        idxs.append(i)
        work = jnp.where(lane_e == i, -jnp.inf, work)
    es = [jnp.exp(v - vals[0]) for v in vals]
    tot = sum(es)
    packed = jnp.zeros((tm, LANES), F32)
    for j in range(TOP_K):
        packed = jnp.where(lane_o == j, idxs[j], packed)
        packed = jnp.where(lane_o == TOP_K + j, es[j] / tot, packed)
    top_ref[...] = packed


def _mixer_out(ret_y, att, x, mod, w_out_bf, g_post, g_pre, w_router, b_router, per_row, rows_per_batch):
    n = x.shape[0]
    tm = min(TM_OUT, n)
    tpb = 1 if per_row else rows_per_batch // tm
    half = pl.BlockSpec((tm, D_HALF), lambda i: (i, 0))
    full = pl.BlockSpec((tm, D_MODEL), lambda i: (i, 0))
    stat = pl.BlockSpec((tm, LANES), lambda i: (i, 0))
    vec = pl.BlockSpec((1, D_MODEL), lambda i: (0, 0))
    mods = [_mod_spec(tm, c, per_row, tpb) for c in (MOD_GATE_MIX, MOD_SCALE_FFN, MOD_SHIFT_FFN)]
    return pl.pallas_call(
        _out_kernel,
        grid=(n // tm,),
        in_specs=[half, half, full] + mods + [
                  pl.BlockSpec((D_MODEL, D_MODEL), lambda i: (0, 0)), vec, vec,
                  pl.BlockSpec((D_MODEL, N_EXPERTS), lambda i: (0, 0)),
                  pl.BlockSpec((1, N_EXPERTS), lambda i: (0, 0))],
        out_specs=[full, pl.BlockSpec((tm * SLAB, LANES), lambda i: (i, 0)), stat],
        out_shape=[jax.ShapeDtypeStruct((n, D_MODEL), F32),
                   jax.ShapeDtypeStruct((n * SLAB, LANES), F32),
                   jax.ShapeDtypeStruct((n, LANES), F32)],
        compiler_params=_params(1, VMEM_LIMIT),
        name="mixer_out",
    )(ret_y, att, x, mod, mod, mod, w_out_bf, g_post, g_pre, w_router, b_router)


def _route_kernel(top_ref, dest_ref, ends_ref, counts_ref, running, starts, *, tm, tm_blk):
    ph, i = pl.program_id(0), pl.program_id(1)
    lane_f = lax.broadcasted_iota(jnp.int32, (tm, LANES), 1).astype(F32)
    top = top_ref[...]
    picks = [lane_f == top[:, k:k + 1] for k in range(TOP_K)]
    chosen = sum(p.astype(F32) for p in picks)
    tile_counts = jnp.sum(chosen, axis=0, keepdims=True)

    @pl.when((ph == 0) & (i == 0))
    def _():
        running[...] = jnp.zeros_like(running)

    @pl.when(ph == 0)
    def _():
        running[...] += tile_counts

    @pl.when((ph == 1) & (i == 0))
    def _():
        lane = lax.broadcasted_iota(jnp.int32, (1, LANES), 1)
        assert tm_blk & (tm_blk - 1) == 0
        counts = running[...].astype(jnp.int32)
        counts_ref[...] = counts
        blocks = (counts + (tm_blk - 1)) >> (tm_blk.bit_length() - 1)
        ends = blocks
        shift = 1
        while shift < N_EXPERTS:
            ends = ends + jnp.where(lane >= shift, pltpu.roll(ends, shift, axis=1), 0)
            shift *= 2
        ends_ref[...] = ends
        starts[...] = ((ends - blocks) * tm_blk).astype(F32)
        running[...] = jnp.zeros_like(running)

    @pl.when(ph == 1)
    def _():
        earlier = (lax.broadcasted_iota(jnp.int32, (tm, tm), 0)
                   > lax.broadcasted_iota(jnp.int32, (tm, tm), 1))
        ahead = _dot(jnp.where(earlier, 1.0, 0.0).astype(BF16), chosen.astype(BF16))
        slot = starts[...] + running[...] + ahead
        lane_o = lax.broadcasted_iota(jnp.int32, (tm, LANES), 1)
        packed = jnp.zeros((tm, LANES), F32)
        for k, pick in enumerate(picks):
            dest_k = jnp.sum(jnp.where(pick, slot, 0.0), axis=-1, keepdims=True)
            packed = jnp.where(lane_o == k, dest_k, packed)
        dest_ref[...] = packed.astype(jnp.int32)
        running[...] += tile_counts


def _route(top, tm_blk):
    n = top.shape[0]
    tm = 512
    per_expert = pl.BlockSpec((1, LANES), lambda ph, i: (0, 0))
    return pl.pallas_call(
        functools.partial(_route_kernel, tm=tm, tm_blk=tm_blk),
        grid=(2, n // tm),
        in_specs=[pl.BlockSpec((tm, LANES), lambda ph, i: (i, 0))],
        out_specs=[pl.BlockSpec((tm, LANES), lambda ph, i: (i * ph, 0)), per_expert, per_expert],
        out_shape=[jax.ShapeDtypeStruct((n, LANES), jnp.int32),
                   jax.ShapeDtypeStruct((1, LANES), jnp.int32), jax.ShapeDtypeStruct((1, LANES), jnp.int32)],
        scratch_shapes=[pltpu.VMEM((1, LANES), F32), pltpu.VMEM((1, LANES), F32)],
        compiler_params=_params(2),
        name="route",
    )(top)


def _dispatch_kernel(ends_ref, dest_ref, *rest, tm, tm_blk, n_blocks, tiles):
    h_refs, (xs_hbm, zeros, sem) = rest[:-3], rest[-3:]
    i = pl.program_id(0)

    def fill(b):
        rows = pl.ds(pl.multiple_of(b * (tm_blk * SLAB), tm_blk * SLAB), tm_blk * SLAB)
        return pltpu.make_async_copy(zeros, xs_hbm.at[rows], sem.at[1])

    def fill_start(b, carry):
        fill(b).start()
        return carry

    def fill_wait(b, carry):
        fill(b).wait()
        return carry

    def zero_padding():
        zeros[...] = jnp.zeros_like(zeros)
        n_used = ends_ref[N_EXPERTS - 1]
        for do in (fill_start, fill_wait):
            for e in range(N_EXPERTS):
                first = ends_ref[e - 1] if e else 0

                @pl.when(ends_ref[e] > first)
                def _():
                    do(ends_ref[e] - 1, 0)

            lax.fori_loop(n_used, n_blocks, do, 0)

    pl.when(i == 0)(zero_padding)

    def scatter(h_ref):
        def body(jj, carry):
            for j in (jj * DMA_UNROLL + u for u in range(DMA_UNROLL)):
                for k in range(TOP_K):
                    pltpu.make_async_copy(h_ref.at[_slab_rows(j)], xs_hbm.at[_slab_rows(dest_ref[j * TOP_K + k])],
                                          sem.at[0]).start(priority=k % 2)
            return carry

        lax.fori_loop(0, tm // DMA_UNROLL, body, 0)
        for k in range(TOP_K):
            pltpu.make_async_copy(h_ref, xs_hbm.at[pl.ds(0, tm * SLAB)], sem.at[0]).wait()

    start = 0
    for h_ref, n_tiles in zip(h_refs, tiles):
        pl.when((i >= start) & (i < start + n_tiles))(functools.partial(scatter, h_ref))
        start += n_tiles


def _dispatch(hs, dest_flat, ends, n_blocks, tm_blk):
    tm = 256
    tiles = [h.shape[0] // SLAB // tm for h in hs]
    starts = [sum(tiles[:s]) for s in range(len(hs))]
    src = lambda s: pl.BlockSpec((tm * SLAB, LANES),
                                 lambda i, ends: (jnp.clip(i - starts[s], 0, tiles[s] - 1), 0))
    grid_spec = pltpu.PrefetchScalarGridSpec(
        num_scalar_prefetch=1,
        grid=(sum(tiles),),
        in_specs=[pl.BlockSpec((tm * TOP_K,), lambda i, ends: (i,), memory_space=pltpu.SMEM)]
        + [src(s) for s in range(len(hs))],
        out_specs=pl.BlockSpec(memory_space=pl.ANY),
        scratch_shapes=[pltpu.VMEM((tm_blk * SLAB, LANES), F32), pltpu.SemaphoreType.DMA((2,))],
    )
    return pl.pallas_call(
        functools.partial(_dispatch_kernel, tm=tm, tm_blk=tm_blk, n_blocks=n_blocks, tiles=tiles),
        grid_spec=grid_spec,
        out_shape=jax.ShapeDtypeStruct((n_blocks * tm_blk * SLAB, LANES), F32),
        compiler_params=_params(1),
        name="dispatch",
    )(ends, dest_flat, *hs)


def _moe_kernel(be_ref, nu_ref, valid_ref, nxt_ref, slot_ref, x_ref, wgu_hbm, wd_hbm, bg_ref, bu_ref, bd_ref,
                o_ref, wgu_buf, wd_buf, wt, wg_t, wu_t, wd_bf, sem):
    i = pl.program_id(0)
    n_used = nu_ref[0]
    valid = valid_ref[i]
    tm = x_ref.shape[0] // SLAB
    _, d, d_gu = wgu_buf.shape
    d_ff = d_gu // 2
    new_expert = (i == 0) | (be_ref[i] != be_ref[jnp.maximum(i - 1, 0)])

    def weights(e, s):
        return (pltpu.make_async_copy(wgu_hbm.at[e], wgu_buf.at[s], sem.at[0, s]),
                pltpu.make_async_copy(wd_hbm.at[e], wd_buf.at[s], sem.at[1, s]))

    @pl.when(i == 0)
    def _():
        for copy in weights(be_ref[0], slot_ref[0]):
            copy.start()

    @pl.when((i < n_used) & new_expert)
    def _():
        s = slot_ref[i]
        for copy in weights(be_ref[i], s):
            copy.wait()

        @pl.when(nxt_ref[i] >= 0)
        def _():
            for copy in weights(nxt_ref[i], 1 - s):
                copy.start()

        for c in range(d // LANES):
            wt[...] = wgu_buf[s, pl.ds(c * LANES, LANES), :].T
            rows = slice(c * LANES, (c + 1) * LANES)
            wg_t[:, rows] = wt[pl.ds(0, d_ff, stride=2), :].astype(BF16)
            wu_t[:, rows] = wt[pl.ds(1, d_ff, stride=2), :].astype(BF16)
        wd_bf[...] = wd_buf[s].astype(BF16)

    def experts_mlp(rows):
        x = _slab_load(x_ref, rows).astype(BF16)
        acc = bd_ref[...]
        for c in range(d_ff // MOE_COLS):
            cols = slice(c * MOE_COLS, (c + 1) * MOE_COLS)
            gate = jnp.minimum(_dot_nt(x, wg_t[cols, :]) + bg_ref[:, cols], SWIGLU_LIMIT)
            up = jnp.clip(_dot_nt(x, wu_t[cols, :]) + bu_ref[:, cols], -SWIGLU_LIMIT, SWIGLU_LIMIT)
            act = gate * jax.nn.sigmoid(SWIGLU_ALPHA * gate) * (up + 1.0)
            acc = acc + _dot(act.astype(BF16), wd_bf[cols, :])
        _slab_store(o_ref, acc)
        if rows < tm:
            o_ref[rows * SLAB:, :] = jnp.zeros(((tm - rows) * SLAB, LANES), o_ref.dtype)

    sizes = [tm // 4, tm // 2, tm]
    for lo, rows in zip([0] + sizes[:-1], sizes):
        pl.when((i < n_used) & (valid > lo) & (valid <= rows))(functools.partial(experts_mlp, rows))

    @pl.when(i >= n_used)
    def _():
        o_ref[...] = jnp.zeros_like(o_ref)


def _moe_plan(ends, counts, n_blocks, tm):
    blk = jnp.arange(n_blocks, dtype=jnp.int32)
    e_ids = jnp.arange(N_EXPERTS, dtype=jnp.int32)
    block_e = jnp.minimum(jnp.sum(blk[:, None] >= ends[None, :], axis=1), N_EXPERTS - 1).astype(jnp.int32)
    mine = block_e[:, None] == e_ids[None, :]
    pick = lambda per_e: jnp.sum(jnp.where(mine, per_e[None, :], 0), axis=1)
    first_blk = ends - (counts + tm - 1) // tm
    valid = jnp.clip(pick(counts) - (blk - pick(first_blk)) * tm, 0, tm)
    valid = jnp.where(blk < ends[N_EXPERTS - 1], valid, 0).astype(jnp.int32)
    used = counts > 0
    slot_e = (jnp.cumsum(used.astype(jnp.int32)) - 1) % 2
    later = (e_ids[None, :] > e_ids[:, None]) & used[None, :]
    next_e = jnp.min(jnp.where(later, e_ids[None, :], N_EXPERTS), axis=1)
    next_e = jnp.where(next_e < N_EXPERTS, next_e, -1)
    nxt = jnp.where((blk == pick(first_blk)) & (valid > 0), pick(next_e), -1).astype(jnp.int32)
    return block_e, ends[N_EXPERTS - 1:], valid, nxt, pick(slot_e).astype(jnp.int32)


def _moe_blocks(xs, block_e, n_used, valid, nxt, slot, w_gu, w_down, b_gu, b_down, tm):
    n_blocks = block_e.shape[0]
    n_e, d, d_gu = w_gu.shape
    d_ff = d_gu // 2
    bias = lambda n: pl.BlockSpec((None, 1, n), lambda i, be, *_: (be[i], 0, 0))
    grid_spec = pltpu.PrefetchScalarGridSpec(
        num_scalar_prefetch=5,
        grid=(n_blocks,),
        in_specs=[pl.BlockSpec((tm * SLAB, LANES), lambda i, be, nu, *_: (jnp.minimum(i, nu[0] - 1), 0)),
                  pl.BlockSpec(memory_space=pl.ANY), pl.BlockSpec(memory_space=pl.ANY),
                  bias(d_ff), bias(d_ff), bias(d)],
        out_specs=pl.BlockSpec((tm * SLAB, LANES), lambda i, *_: (i, 0)),
        scratch_shapes=[pltpu.VMEM((2, d, d_gu), F32), pltpu.VMEM((2, d_ff, d), F32),
                        pltpu.VMEM((d_gu, LANES), F32), pltpu.VMEM((d_ff, d), BF16),
                        pltpu.VMEM((d_ff, d), BF16), pltpu.VMEM((d_ff, d), BF16),
                        pltpu.SemaphoreType.DMA((2, 2))],
    )
    return pl.pallas_call(
        _moe_kernel,
        grid_spec=grid_spec,
        out_shape=jax.ShapeDtypeStruct((n_blocks * tm * SLAB, LANES), F32),
        compiler_params=_params(1, VMEM_LIMIT),
        name="moe_blocks",
    )(block_e, n_used, valid, nxt, slot, xs, w_gu, w_down, b_gu[:, None, 0::2], b_gu[:, None, 1::2],
      b_down.reshape(n_e, 1, d))


def _fin_kernel(dest_ref, next_ref, rows_hbm, top_ref, x1_ref, gt_ref, g_ref, o_ref, buf, sem, *, tm):
    i = pl.program_id(0)
    slot = i % 2

    def gather(idx_ref, s):
        def body(jj, carry):
            for j in (jj * DMA_UNROLL + u for u in range(DMA_UNROLL)):
                for k in range(TOP_K):
                    pltpu.make_async_copy(rows_hbm.at[_slab_rows(idx_ref[j * TOP_K + k])],
                                          buf.at[s, k, _slab_rows(j)], sem.at[s]).start(priority=k % 2)
            return carry
        lax.fori_loop(0, tm // DMA_UNROLL, body, 0)

    @pl.when(i == 0)
    def _():
        gather(dest_ref, 0)

    @pl.when(i + 1 < pl.num_programs(0))
    def _():
        gather(next_ref, 1 - slot)

    for k in range(TOP_K):
        pltpu.make_async_copy(rows_hbm.at[pl.ds(0, tm * SLAB)], buf.at[slot, k], sem.at[slot]).wait()
    top = top_ref[...]
    f = sum(_slab_load(buf.at[slot, k], tm) * top[:, TOP_K + k:TOP_K + k + 1] for k in range(TOP_K))
    o_ref[...] = x1_ref[...] + gt_ref[...] * _rms(f, g_ref[...])


def _finish(dest, rows, top, x1, mod, g_post, per_row, rows_per_batch):
    n = x1.shape[0]
    tm = min(TM_FIN, n)
    full = pl.BlockSpec((tm, D_MODEL), lambda i: (i, 0))
    gate = _mod_spec(tm, MOD_GATE_FFN, per_row, 1 if per_row else rows_per_batch // tm)
    return pl.pallas_call(
        functools.partial(_fin_kernel, tm=tm),
        grid=(n // tm,),
        in_specs=[pl.BlockSpec((tm * TOP_K,), lambda i: (i,), memory_space=pltpu.SMEM),
                  pl.BlockSpec((tm * TOP_K,), lambda i: (jnp.minimum(i + 1, n // tm - 1),),
                               memory_space=pltpu.SMEM),
                  pl.BlockSpec(memory_space=pl.ANY),
                  pl.BlockSpec((tm, LANES), lambda i: (i, 0)), full, gate,
                  pl.BlockSpec((1, D_MODEL), lambda i: (0, 0))],
        out_specs=full,
        out_shape=jax.ShapeDtypeStruct((n, D_MODEL), F32),
        scratch_shapes=[pltpu.VMEM((2, TOP_K, tm * SLAB, LANES), F32), pltpu.SemaphoreType.DMA((2,))],
        compiler_params=_params(1),
        name="finish",
    )(dest, dest, rows, top, x1, mod, g_post)


def kernel(x_prompt, x_sample, state_ret, cache_win_k, cache_win_v, c_prompt, c_sample, w_ada, b_ada,
           g_pre_mix, g_post_mix, g_pre_ffn, g_post_ffn, w_in, g_ret, w_out, w_router, b_router,
           w_gate_up, b_gate_up, w_down, b_down):
    depth = w_in.shape[0]
    assert depth == 1, "single-layer step"
    n_b, seq, d = x_prompt.shape
    n_db, t_new, _ = x_sample.shape
    n_p, n_s = n_b * seq, n_db * t_new
    buf = cache_win_k.shape[2]
    keep = min(DIL_PATTERNS[-1][0], seq)
    vec = lambda g: g[0].reshape(1, -1)

    mod_p, mod_s = _ada(c_prompt, c_sample, w_ada[0], b_ada[0], t_new)
    mod_p = mod_p[:n_b].reshape(n_b, 1, -1)
    mod_s = mod_s.reshape(n_s, -1)

    w_in_bf = w_in[0].astype(BF16)
    w_out_bf = w_out[0].astype(BF16)
    xp = x_prompt.reshape(n_p, d)
    xs = x_sample.reshape(n_s, d)
    router = (w_router[0], b_router[0].reshape(1, -1))

    qr, kr, vr, gr, qa_pm, ka_pm, va_pm, k_win, v_win = _in_proj(xp, mod_p, vec(g_pre_mix), w_in_bf, False, seq, keep)
    ret_y_p, ret_state_p = _retention(qr, kr, vr, gr, None, g_ret[0], n_b, seq, RET_CHUNK, n_b)
    att_p = _att_prompt(qa_pm, ka_pm, va_pm, n_b, seq)
    x1_p, h_p, top_p = _mixer_out(ret_y_p, att_p, xp, mod_p, w_out_bf, vec(g_post_mix), vec(g_pre_ffn),
                                  *router, False, seq)

    qr_s, kr_s, vr_s, gr_s, qa_s, ka_s, va_s = _in_proj(xs, mod_s, vec(g_pre_mix), w_in_bf, True, t_new)
    ret_y_s, ret_state_s = _retention(qr_s, kr_s, vr_s, gr_s, state_ret[0], g_ret[0], n_db, t_new, t_new,
                                      RET_GROUP_SAMPLE)
    att_s = _att_sample(qa_s, ka_s, va_s, cache_win_k[0], cache_win_v[0], n_db, t_new)
    x1_s, h_s, top_s = _mixer_out(ret_y_s, att_s, xs, mod_s, w_out_bf, vec(g_post_mix), vec(g_pre_ffn),
                                  *router, True, t_new)

    n_blocks = (n_p + n_s) * TOP_K // TM_MOE + N_EXPERTS
    dest, ends, counts = _route(jnp.concatenate([top_p, top_s], axis=0), TM_MOE)
    dest = dest[:, :TOP_K].reshape(-1)
    ends, counts = ends[0, :N_EXPERTS], counts[0, :N_EXPERTS]
    rows_in = _dispatch([h_p, h_s], dest, ends, n_blocks, TM_MOE)
    rows = _moe_blocks(rows_in, *_moe_plan(ends, counts, n_blocks, TM_MOE), w_gate_up[0], w_down[0],
                       b_gate_up[0], b_down[0], TM_MOE)
    y_p = _finish(dest[:n_p * TOP_K], rows, top_p, x1_p, mod_p, vec(g_post_ffn), False, seq)
    y_s = _finish(dest[n_p * TOP_K:], rows, top_s, x1_s, mod_s, vec(g_post_ffn), True, t_new)

    win = lambda t: t.reshape(n_b, N_HEADS, HEAD_DIM, keep).transpose(0, 3, 1, 2)[None]
    new = lambda t: t.reshape(n_db, t_new, N_HEADS, HEAD_DIM)[None]
    return (y_p.reshape(n_b, seq, d), y_s.reshape(n_db, t_new, d), ret_state_p[None], ret_state_s[None],
            win(k_win), win(v_win), new(ka_s), new(va_s))
```

```python
import functools

import numpy as np
import jax
import jax.numpy as jnp
from jax import lax
from jax.experimental import pallas as pl
from jax.experimental.pallas import tpu as pltpu

F32 = jnp.float32
BF16 = jnp.bfloat16

D_MODEL = 1024
HEAD_DIM = 64
N_HEADS = 8
D_HALF = N_HEADS * HEAD_DIM
N_SEG = 7
LANES = 128
N_PAIRS = D_HALF // LANES
RET_CHUNK = 128
RET_GROUP_SAMPLE = 8
ATT_BLOCK = 128
ATT_Q = 1024
ATT_BACK = 2
ATT_UNROLL = 2
DIL_PATTERNS = ((128, 1), (512, 4), (2048, 16))
N_EXPERTS = 32
TOP_K = 4
SWIGLU_LIMIT = 7.0
SWIGLU_ALPHA = 1.702
EPS = 1e-6
NEG_INF = -1e30
MOD_SHIFT_MIX, MOD_SCALE_MIX, MOD_GATE_MIX, MOD_SHIFT_FFN, MOD_SCALE_FFN, MOD_GATE_FFN = range(6)

TM_IN = 512
TM_OUT = 512
TM_MOE = 512
MOE_COLS = 512
TM_FIN = 128
DMA_UNROLL = 4
VMEM_LIMIT = 56 * 1024 * 1024


def _params(n_axes, vmem=None):
    return pltpu.CompilerParams(dimension_semantics=("arbitrary",) * n_axes, vmem_limit_bytes=vmem)


def _ret_log_decay():
    return np.log(1.0 - 2.0 ** (-5.0 - np.arange(N_HEADS, dtype=np.float64)))


def _alibi_slopes():
    return 2.0 ** (-8.0 * (np.arange(N_HEADS, dtype=np.float64) + 1.0) / N_HEADS)


def _per_head_lanes(a):
    return np.repeat(a, HEAD_DIM, axis=-1)


def _rms(x, g):
    return x * lax.rsqrt(jnp.mean(x * x, axis=-1, keepdims=True) + EPS) * g


def _dot(a, b):
    return jnp.dot(a, b, preferred_element_type=F32)


def _dot_nt(a, b):
    return lax.dot_general(a, b, (((1,), (1,)), ((), ())), preferred_element_type=F32)


def _dot_tn(a, b):
    return lax.dot_general(a, b, (((0,), (0,)), ((), ())), preferred_element_type=F32)


SLAB = D_MODEL // LANES


def _slab_rows(r):
    return pl.ds(pl.multiple_of(r * SLAB, SLAB), SLAB)


def _slab_load(ref, n):
    return jnp.concatenate([ref[pl.ds(s, n, stride=SLAB), :] for s in range(SLAB)], axis=1)


def _slab_store(ref, value):
    n = value.shape[0]
    for s in range(SLAB):
        ref[pl.ds(s, n, stride=SLAB), :] = value[:, s * LANES:(s + 1) * LANES]


def _mod_spec(tm, col, per_row, tiles_per_batch):
    if per_row:
        return pl.BlockSpec((tm, D_MODEL), lambda i: (i, col))
    return pl.BlockSpec((None, 1, D_MODEL), lambda i: (i // tiles_per_batch, 0, col))


def _ada_kernel(cp_ref, cs_ref, w_ref, b_ref, op_ref, os_ref):
    w = w_ref[...].astype(BF16)

    def mod(c):
        return _dot((c * jax.nn.sigmoid(c)).astype(BF16), w) + b_ref[...]

    op_ref[...] = mod(cp_ref[...])
    ms = mod(cs_ref[...])
    os_ref[...] = jnp.broadcast_to(ms[:, None, :], os_ref.shape)


def _ada(c_prompt, c_sample, w_ada, b_ada, t_new):
    n_b, d = c_prompt.shape
    n_db = c_sample.shape[0]
    n_out = w_ada.shape[1]
    cp = jnp.concatenate([c_prompt, jnp.zeros((-n_b % 8, d), F32)], axis=0)
    return pl.pallas_call(
        _ada_kernel,
        grid=(n_out // d,),
        in_specs=[pl.BlockSpec(cp.shape, lambda j: (0, 0)),
                  pl.BlockSpec((n_db, d), lambda j: (0, 0)),
                  pl.BlockSpec((d, d), lambda j: (0, j)),
                  pl.BlockSpec((1, d), lambda j: (0, j))],
        out_specs=[pl.BlockSpec(cp.shape, lambda j: (0, j)),
                   pl.BlockSpec((n_db, t_new, d), lambda j: (0, 0, j))],
        out_shape=[jax.ShapeDtypeStruct((cp.shape[0], n_out), F32),
                   jax.ShapeDtypeStruct((n_db, t_new, n_out), F32)],
        compiler_params=_params(1, VMEM_LIMIT),
        name="ada",
    )(cp, c_sample, w_ada, b_ada.reshape(1, n_out))


def _in_kernel(x_ref, sc_ref, sh_ref, g_ref, w_ref, *o_refs, n_plain, window):
    h = (_rms(x_ref[...], g_ref[...]) * (1.0 + sc_ref[...]) + sh_ref[...]).astype(BF16)
    segs = [_dot(h, w_ref[:, s * D_HALF:(s + 1) * D_HALF]) for s in range(N_SEG)]
    for o_ref, seg in zip(o_refs[:n_plain], segs):
        o_ref[...] = seg
    if n_plain == N_SEG:
        return
    for o_ref, seg in zip(o_refs[n_plain:N_SEG], segs[n_plain:]):
        for p in range(N_PAIRS):
            o_ref[p] = seg[:, p * LANES:(p + 1) * LANES]
    tiles_per_seq, first = window
    @pl.when(pl.program_id(0) % tiles_per_seq >= first)
    def _():
        for o_ref, seg in zip(o_refs[N_SEG:], segs[N_SEG - 2:]):
            o_ref[...] = seg.T


def _in_proj(x, mod, g, w_bf, per_row, rows_per_batch, keep=None):
    n = x.shape[0]
    tm = min(TM_IN, n)
    tpb = 1 if per_row else rows_per_batch // tm
    seg = pl.BlockSpec((tm, D_HALF), lambda i: (i, 0))
    seg_shape = jax.ShapeDtypeStruct((n, D_HALF), F32)
    if keep is None:
        n_plain, window = N_SEG, None
        out_specs, out_shape = [seg] * N_SEG, [seg_shape] * N_SEG
    else:
        n_plain, window = N_SEG - 3, (tpb, (rows_per_batch - keep) // tm)
        pm = pl.BlockSpec((N_PAIRS, tm, LANES), lambda i: (0, i, 0))
        win = pl.BlockSpec((None, D_HALF, tm), lambda i: (i // tpb, 0, jnp.maximum(i % tpb - window[1], 0)))
        out_specs = [seg] * n_plain + [pm] * 3 + [win] * 2
        out_shape = ([seg_shape] * n_plain + [jax.ShapeDtypeStruct((N_PAIRS, n, LANES), F32)] * 3
                     + [jax.ShapeDtypeStruct((n // rows_per_batch, D_HALF, keep), F32)] * 2)
    return pl.pallas_call(
        functools.partial(_in_kernel, n_plain=n_plain, window=window),
        grid=(n // tm,),
        in_specs=[pl.BlockSpec((tm, D_MODEL), lambda i: (i, 0)),
                  _mod_spec(tm, MOD_SCALE_MIX, per_row, tpb), _mod_spec(tm, MOD_SHIFT_MIX, per_row, tpb),
                  pl.BlockSpec((1, D_MODEL), lambda i: (0, 0)),
                  pl.BlockSpec((D_MODEL, N_SEG * D_HALF), lambda i: (0, 0))],
        out_specs=out_specs,
        out_shape=out_shape,
        compiler_params=_params(1, VMEM_LIMIT),
        name="in_proj",
    )(x, mod, mod, g, w_bf)


def _ret_tables(chunk):
    lg = _ret_log_decay()
    pos = np.arange(chunk, dtype=np.float64)
    diff = pos[:, None] - pos[None, :]
    dmask = np.where(diff >= 0, np.exp(np.maximum(diff, 0.0)[None] * lg[:, None, None]), 0.0)
    dmask = dmask.reshape(N_PAIRS, 2 * chunk, chunk)
    row_decay = _per_head_lanes(np.exp((pos[:, None] + 1.0) * lg[None, :]))
    key_decay = _per_head_lanes(np.exp((chunk - 1.0 - pos)[:, None] * lg[None, :]))
    state_decay = _per_head_lanes(np.exp(chunk * lg)[None, :])
    return tuple(jnp.asarray(t, F32) for t in (dmask, row_decay, key_decay, state_decay))


def _ret_kernel(q_ref, k_ref, v_ref, gate_ref, s0_ref, gret_ref, dm_ref, rd_ref, kd_ref, sd_ref,
                y_ref, st_ref, state, *, chunk, group, has_init):
    c = pl.program_id(1)
    lo = lax.broadcasted_iota(jnp.int32, (chunk, LANES), 1) < HEAD_DIM
    blk_r = lax.broadcasted_iota(jnp.int32, (LANES, LANES), 0) < HEAD_DIM
    blk_c = lax.broadcasted_iota(jnp.int32, (LANES, LANES), 1) < HEAD_DIM
    same_head = blk_r == blk_c

    @pl.when(c == 0)
    def _():
        if has_init:
            zero = jnp.zeros((HEAD_DIM, HEAD_DIM), F32)
            for b in range(group):
                for p in range(N_PAIRS):
                    top = jnp.concatenate([s0_ref[b, 2 * p], zero], axis=1)
                    bot = jnp.concatenate([zero, s0_ref[b, 2 * p + 1]], axis=1)
                    state[b * N_PAIRS + p] = jnp.concatenate([top, bot], axis=0)
        else:
            state[...] = jnp.zeros_like(state)

    new_states = []
    for b, p in [(b, p) for b in range(group) for p in range(N_PAIRS)]:
        sl = slice(p * LANES, (p + 1) * LANES)
        q = q_ref[b, :, sl]
        k = k_ref[b, :, sl] * HEAD_DIM ** -0.5
        vb = v_ref[b, :, sl].astype(BF16)
        kb = k.astype(BF16)
        q2 = jnp.concatenate([jnp.where(lo, q, 0.0), jnp.where(lo, 0.0, q)], axis=0).astype(BF16)
        inner = _dot((_dot_nt(q2, kb) * dm_ref[p]).astype(BF16), vb)
        s_old = state[b * N_PAIRS + p]
        cross = _dot(q.astype(BF16), s_old.astype(BF16)) * rd_ref[:, sl]
        o = jnp.where(lo, inner[:chunk], inner[chunk:]) + cross
        upd = _dot_tn((k * kd_ref[:, sl]).astype(BF16), vb)
        s_new = sd_ref[:, sl] * s_old + jnp.where(same_head, upd, 0.0)
        state[b * N_PAIRS + p] = s_new
        new_states.append(s_new)

        def head_mean(t):
            m_lo = jnp.sum(jnp.where(lo, t, 0.0), axis=-1, keepdims=True)
            m_hi = jnp.sum(jnp.where(lo, 0.0, t), axis=-1, keepdims=True)
            return jnp.where(lo, m_lo, m_hi) * (1.0 / HEAD_DIM)

        d = o - head_mean(o)
        n = d * lax.rsqrt(head_mean(d * d) + EPS) * gret_ref[:, sl]
        g = gate_ref[b, :, sl]
        y_ref[b, :, sl] = (g * jax.nn.sigmoid(g) * n).astype(y_ref.dtype)

    @pl.when(c == pl.num_programs(1) - 1)
    def _():
        for bp, s_new in enumerate(new_states):
            b, p = divmod(bp, N_PAIRS)
            st_ref[b, 2 * p] = s_new[:HEAD_DIM, :HEAD_DIM]
            st_ref[b, 2 * p + 1] = s_new[HEAD_DIM:, HEAD_DIM:]


def _retention(q, k, v, gate, state0, g_ret, n_batch, seq, chunk, group):
    nc = seq // chunk
    by_seq = lambda t: t.reshape(n_batch, seq, D_HALF)
    has_init = state0 is not None
    if state0 is None:
        state0 = jnp.zeros((group, N_HEADS, HEAD_DIM, HEAD_DIM), F32)
        s0_map = lambda b, c: (0, 0, 0, 0)
    else:
        s0_map = lambda b, c: (b, 0, 0, 0)
    dmask, row_decay, key_decay, state_decay = _ret_tables(chunk)
    rows = pl.BlockSpec((group, chunk, D_HALF), lambda b, c: (b, c, 0))
    const2 = lambda shape: pl.BlockSpec(shape, lambda b, c: (0, 0))
    st_block = pl.BlockSpec((group, N_HEADS, HEAD_DIM, HEAD_DIM), lambda b, c: (b, 0, 0, 0))
    y, state = pl.pallas_call(
        functools.partial(_ret_kernel, chunk=chunk, group=group, has_init=has_init),
        grid=(n_batch // group, nc),
        in_specs=[rows, rows, rows, rows,
                  pl.BlockSpec((group, N_HEADS, HEAD_DIM, HEAD_DIM), s0_map),
                  const2((1, D_HALF)),
                  pl.BlockSpec((N_PAIRS, 2 * chunk, chunk), lambda b, c: (0, 0, 0)),
                  const2((chunk, D_HALF)), const2((chunk, D_HALF)), const2((1, D_HALF))],
        out_specs=[rows, st_block],
        out_shape=[jax.ShapeDtypeStruct((n_batch, seq, D_HALF), BF16),
                   jax.ShapeDtypeStruct((n_batch, N_HEADS, HEAD_DIM, HEAD_DIM), F32)],
        scratch_shapes=[pltpu.VMEM((group * N_PAIRS, LANES, LANES), F32)],
        compiler_params=_params(2),
        name="retention",
    )(by_seq(q), by_seq(k), by_seq(v), by_seq(gate), state0, g_ret.reshape(1, D_HALF), dmask, row_decay,
      key_decay, state_decay)
    return y.reshape(n_batch * seq, D_HALF), state


def _band_table(n_q, n_k, dilation):
    step = np.arange(n_q)[:, None] + (n_k - n_q) - np.arange(n_k)[None, :]
    in_win = (step >= 0) & (step <= ATT_BLOCK)
    bias = -_alibi_slopes()[:, None, None] * (step * dilation).astype(np.float64)[None]
    return jnp.asarray(np.where(in_win[None], bias, NEG_INF).reshape(N_PAIRS, 2 * n_q, n_k), F32)


def _band_attention(q, k, v, table_ref, key_ok, pair):
    n_q = q.shape[0]
    lo = lax.broadcasted_iota(jnp.int32, q.shape, 1) < HEAD_DIM
    q = q * HEAD_DIM ** -0.5
    q2 = jnp.concatenate([jnp.where(lo, q, 0.0), jnp.where(lo, 0.0, q)], axis=0).astype(BF16)
    s = _dot_nt(q2, k.astype(BF16)) + table_ref[pair]
    s = jnp.where(key_ok, s, NEG_INF)
    m = jnp.max(s, axis=-1, keepdims=True)
    e = jnp.exp(s - m)
    l = jnp.sum(e, axis=-1, keepdims=True)
    o = _dot(e.astype(BF16), v.astype(BF16))
    return tuple(jnp.where(lo, t[:n_q], t[n_q:]) for t in (o, m, l))


def _attp_kernel(*refs):
    n_back = ATT_BACK
    q_ref = refs[0]
    k_refs = refs[1:2 + n_back]
    v_refs = refs[2 + n_back:3 + 2 * n_back]
    t_refs = refs[3 + 2 * n_back:3 + 2 * n_back + len(DIL_PATTERNS)]
    o_ref, parts, kcat, vcat = refs[3 + 2 * n_back + len(DIL_PATTERNS):]
    n = pl.program_id(1)

    def key_col(n_q, n_k):
        return lax.broadcasted_iota(jnp.int32, (2 * n_q, n_k), 1)

    def fold(first, rows, news):
        if not first:
            olds = [[parts[kind, p, rows, :] for kind in range(3)] for p in range(N_PAIRS)]
            merged = []
            for (o_old, m_old, l_old), (o_new, m_new, l_new) in zip(olds, news):
                m = jnp.maximum(m_old, m_new)
                a, b = jnp.exp(m_old - m), jnp.exp(m_new - m)
                merged.append((o_old * a + o_new * b, m, l_old * a + l_new * b))
            news = merged
        for p, new in enumerate(news):
            for kind, val in enumerate(new):
                parts[kind, p, rows, :] = val

    for pat, (window, dil) in enumerate(DIL_PATTERNS):
        t_ref = t_refs[pat]
        if dil == 1:
            for cat, src in ((kcat, k_refs), (vcat, v_refs)):
                cat[:, :ATT_BLOCK, :] = src[1][:, ATT_Q - ATT_BLOCK:, :]
                cat[:, ATT_BLOCK:, :] = src[0][...]

            def body(it, carry, t_ref=t_ref, first=pat == 0):
                for u in range(ATT_UNROLL):
                    sub = it * ATT_UNROLL + u
                    r0 = pl.multiple_of(sub * ATT_BLOCK, ATT_BLOCK)
                    ok = (key_col(ATT_BLOCK, 2 * ATT_BLOCK) >= ATT_BLOCK) | (n > 0) | (sub > 0)
                    fold(first, pl.ds(r0, ATT_BLOCK),
                         [_band_attention(q_ref[p, pl.ds(r0, ATT_BLOCK), :], kcat[p, pl.ds(r0, 2 * ATT_BLOCK), :],
                                          vcat[p, pl.ds(r0, 2 * ATT_BLOCK), :], t_ref, ok, p)
                          for p in range(N_PAIRS)])
                return carry

            lax.fori_loop(0, ATT_Q // ATT_BLOCK // ATT_UNROLL, body, 0)
            continue
        sub_rows = min(window, ATT_Q)
        n_q = sub_rows // dil
        n_prev = window // sub_rows
        for s in range(ATT_Q // sub_rows):
            pieces = []
            for i in range(n_prev, -1, -1):
                rel = (s - i) * sub_rows
                j = -(rel // ATT_Q)
                pieces.append((j, rel + j * ATT_Q))
            missing = sum(jnp.where(n < j, n_q, 0) for j, _ in pieces)

            unroll = ATT_UNROLL if n_q == ATT_BLOCK else 1

            def body(it, carry, dil=dil, n_q=n_q, t_ref=t_ref, first=pat == 0, pieces=pieces, missing=missing,
                     q_off=s * sub_rows, unroll=unroll):
                ok = key_col(n_q, len(pieces) * n_q) >= missing
                for u in range(unroll):
                    r = it * unroll + u
                    rows = pl.ds(q_off + r, n_q, stride=dil)
                    gather = lambda refs, p: jnp.concatenate(
                        [refs[j][p, pl.ds(off + r, n_q, stride=dil), :] for j, off in pieces], axis=0)
                    fold(first, rows, [_band_attention(q_ref[p, rows, :], gather(k_refs, p), gather(v_refs, p),
                                                       t_ref, ok, p) for p in range(N_PAIRS)])
                return carry

            lax.fori_loop(0, dil // unroll, body, 0)

    for p in range(N_PAIRS):
        o_ref[:, p * LANES:(p + 1) * LANES] = (parts[0, p] / parts[2, p]).astype(o_ref.dtype)


def _att_prompt(q, k, v, n_batch, seq):
    nb = seq // ATT_Q
    assert all(w // d == ATT_BLOCK and ATT_Q % min(w, ATT_Q) == 0 and w % min(w, ATT_Q) == 0
               for w, d in DIL_PATTERNS)
    assert ATT_BACK == DIL_PATTERNS[-1][0] // ATT_Q and DIL_PATTERNS[0] == (ATT_BLOCK, 1)
    blk = lambda j: pl.BlockSpec((N_PAIRS, ATT_Q, LANES),
                                 lambda b, i: (0, b * nb + jnp.maximum(i - j, 0), 0))
    tables = [_band_table(ATT_BLOCK, 2 * ATT_BLOCK, 1)] + [
        _band_table(min(w, ATT_Q) // d, (w // min(w, ATT_Q) + 1) * (min(w, ATT_Q) // d), d)
        for w, d in DIL_PATTERNS[1:]]
    back = [blk(j) for j in range(ATT_BACK + 1)]
    return pl.pallas_call(
        _attp_kernel,
        grid=(n_batch, nb),
        in_specs=[blk(0)] + back + back + [pl.BlockSpec(t.shape, lambda b, i: (0, 0, 0)) for t in tables],
        out_specs=pl.BlockSpec((ATT_Q, D_HALF), lambda b, i: (b * nb + i, 0)),
        out_shape=jax.ShapeDtypeStruct((n_batch * seq, D_HALF), BF16),
        scratch_shapes=[pltpu.VMEM((3, N_PAIRS, ATT_Q, LANES), F32),
                        pltpu.VMEM((N_PAIRS, ATT_Q + ATT_BLOCK, LANES), F32),
                        pltpu.VMEM((N_PAIRS, ATT_Q + ATT_BLOCK, LANES), F32)],
        compiler_params=_params(2, VMEM_LIMIT),
        name="att_prompt",
    )(q, *([k] * (ATT_BACK + 1)), *([v] * (ATT_BACK + 1)), *tables)


def _atts_tables(buf, t_new):
    pos_q = buf + np.arange(t_new)
    pos_k = np.concatenate([np.arange(buf), buf + np.arange(LANES)])
    dist = pos_q[:, None] - pos_k[None, :]
    count = np.zeros(dist.shape)
    for window, dilation in DIL_PATTERNS:
        count += (dist >= 0) & (dist <= window) & (dist % dilation == 0)
    count[:, buf + t_new:] = 0
    bias = -_alibi_slopes()[:, None, None] * dist[None].astype(np.float64)
    table = np.where(count[None] > 0, bias + np.log(np.maximum(count, 1.0))[None], NEG_INF)
    table = table.reshape(N_HEADS * t_new, buf + LANES)
    return jnp.asarray(table[:, :buf], F32), jnp.asarray(table[:, buf:], F32)


def _atts_kernel(q_ref, kt_ref, kn_ref, vt_ref, vn_ref, tc_ref, tn_ref, o_ref, *, t_new):
    rows = N_HEADS * t_new
    row_head = lax.broadcasted_iota(jnp.int32, (rows, D_HALF), 0) // t_new
    col_head = lax.broadcasted_iota(jnp.int32, (rows, D_HALF), 1) // HEAD_DIM
    own = row_head == col_head
    q = q_ref[...] * HEAD_DIM ** -0.5
    q_heads = jnp.where(own, jnp.concatenate([q] * N_HEADS, axis=0), 0.0).astype(BF16)
    pad = jnp.zeros((LANES - t_new, D_HALF), F32)
    kn = jnp.concatenate([kn_ref[...], pad], axis=0).astype(BF16)
    vn = jnp.concatenate([vn_ref[...], pad], axis=0).astype(BF16)
    sc = _dot(q_heads, kt_ref[...].astype(BF16)) + tc_ref[...]
    sn = _dot_nt(q_heads, kn) + tn_ref[...]
    m = jnp.maximum(jnp.max(sc, axis=-1, keepdims=True), jnp.max(sn, axis=-1, keepdims=True))
    ec = jnp.exp(sc - m)
    en = jnp.exp(sn - m)
    l = jnp.sum(ec, axis=-1, keepdims=True) + jnp.sum(en, axis=-1, keepdims=True)
    acc = _dot_nt(ec.astype(BF16), vt_ref[...].astype(BF16)) + _dot(en.astype(BF16), vn)
    acc = jnp.where(own, acc, 0.0) / l
    out = acc[:t_new]
    for h in range(1, N_HEADS):
        out = out + acc[h * t_new:(h + 1) * t_new]
    o_ref[...] = out.astype(o_ref.dtype)


def _att_sample(q, k_new, v_new, cache_k, cache_v, n_batch, t_new):
    buf = cache_k.shape[1]
    tc, tn = _atts_tables(buf, t_new)
    by_pos = lambda c: c.transpose(0, 2, 3, 1).reshape(n_batch, D_HALF, buf)
    rows = pl.BlockSpec((t_new, D_HALF), lambda b: (b, 0))
    cache = pl.BlockSpec((None, D_HALF, buf), lambda b: (b, 0, 0))
    return pl.pallas_call(
        functools.partial(_atts_kernel, t_new=t_new),
        grid=(n_batch,),
        in_specs=[rows, cache, rows, cache, rows,
                  pl.BlockSpec((N_HEADS * t_new, buf), lambda b: (0, 0)),
                  pl.BlockSpec((N_HEADS * t_new, LANES), lambda b: (0, 0))],
        out_specs=rows,
        out_shape=jax.ShapeDtypeStruct((n_batch * t_new, D_HALF), BF16),
        compiler_params=_params(1, VMEM_LIMIT),
        name="att_sample",
    )(q, by_pos(cache_k), k_new, by_pos(cache_v), v_new, tc, tn)


def _out_kernel(ret_ref, att_ref, x_ref, gt_ref, sc_ref, sh_ref, w_ref, gpost_ref, gpre_ref, wr_ref, br_ref,
                x1_ref, h_ref, top_ref):
    tm = x_ref.shape[0]
    mix = _dot(ret_ref[...], w_ref[:D_HALF, :]) + _dot(att_ref[...], w_ref[D_HALF:, :])
    x1 = x_ref[...] + gt_ref[...] * _rms(mix, gpost_ref[...])
    x1_ref[...] = x1
    h = _rms(x1, gpre_ref[...]) * (1.0 + sc_ref[...]) + sh_ref[...]
    _slab_store(h_ref, h)
    split = lambda t: (t.astype(BF16), (t - t.astype(BF16).astype(F32)).astype(BF16))
    (h_hi, h_lo), (w_hi, w_lo) = split(h), split(wr_ref[...])
    logits = _dot(h_hi, w_hi) + _dot(h_lo, w_hi) + _dot(h_hi, w_lo) + br_ref[...]
    lane_e = lax.broadcasted_iota(jnp.int32, (tm, N_EXPERTS), 1).astype(F32)
    lane_o = lax.broadcasted_iota(jnp.int32, (tm, LANES), 1)
    vals, idxs = [], []
    work = logits
    for _ in range(TOP_K):
        v = jnp.max(work, axis=-1, keepdims=True)
        i = jnp.min(jnp.where(work == v, lane_e, float(N_EXPERTS)), axis=-1, keepdims=True)
        vals.append(v)
        idxs.append(i)
        work = jnp.where(lane_e == i, -jnp.inf, work)
    es = [jnp.exp(v - vals[0]) for v in vals]
    tot = sum(es)
    packed = jnp.zeros((tm, LANES), F32)
    for j in range(TOP_K):
        packed = jnp.where(lane_o == j, idxs[j], packed)
        packed = jnp.where(lane_o == TOP_K + j, es[j] / tot, packed)
    top_ref[...] = packed


def _mixer_out(ret_y, att, x, mod, w_out_bf, g_post, g_pre, w_router, b_router, per_row, rows_per_batch):
    n = x.shape[0]
    tm = min(TM_OUT, n)
    tpb = 1 if per_row else rows_per_batch // tm
    half = pl.BlockSpec((tm, D_HALF), lambda i: (i, 0))
    full = pl.BlockSpec((tm, D_MODEL), lambda i: (i, 0))
    stat = pl.BlockSpec((tm, LANES), lambda i: (i, 0))
    vec = pl.BlockSpec((1, D_MODEL), lambda i: (0, 0))
    mods = [_mod_spec(tm, c, per_row, tpb) for c in (MOD_GATE_MIX, MOD_SCALE_FFN, MOD_SHIFT_FFN)]
    return pl.pallas_call(
        _out_kernel,
        grid=(n // tm,),
        in_specs=[half, half, full] + mods + [
                  pl.BlockSpec((D_MODEL, D_MODEL), lambda i: (0, 0)), vec, vec,
                  pl.BlockSpec((D_MODEL, N_EXPERTS), lambda i: (0, 0)),
                  pl.BlockSpec((1, N_EXPERTS), lambda i: (0, 0))],
        out_specs=[full, pl.BlockSpec((tm * SLAB, LANES), lambda i: (i, 0)), stat],
        out_shape=[jax.ShapeDtypeStruct((n, D_MODEL), F32),
                   jax.ShapeDtypeStruct((n * SLAB, LANES), F32),
                   jax.ShapeDtypeStruct((n, LANES), F32)],
        compiler_params=_params(1, VMEM_LIMIT),
        name="mixer_out",
    )(ret_y, att, x, mod, mod, mod, w_out_bf, g_post, g_pre, w_router, b_router)


def _route_kernel(top_ref, dest_ref, ends_ref, counts_ref, running, starts, *, tm, tm_blk):
    ph, i = pl.program_id(0), pl.program_id(1)
    lane_f = lax.broadcasted_iota(jnp.int32, (tm, LANES), 1).astype(F32)
    top = top_ref[...]
    picks = [lane_f == top[:, k:k + 1] for k in range(TOP_K)]
    chosen = sum(p.astype(F32) for p in picks)
    tile_counts = jnp.sum(chosen, axis=0, keepdims=True)

    @pl.when((ph == 0) & (i == 0))
    def _():
        running[...] = jnp.zeros_like(running)

    @pl.when(ph == 0)
    def _():
        running[...] += tile_counts

    @pl.when((ph == 1) & (i == 0))
    def _():
        lane = lax.broadcasted_iota(jnp.int32, (1, LANES), 1)
        assert tm_blk & (tm_blk - 1) == 0
        counts = running[...].astype(jnp.int32)
        counts_ref[...] = counts
        blocks = (counts + (tm_blk - 1)) >> (tm_blk.bit_length() - 1)
        ends = blocks
        shift = 1
        while shift < N_EXPERTS:
            ends = ends + jnp.where(lane >= shift, pltpu.roll(ends, shift, axis=1), 0)
            shift *= 2
        ends_ref[...] = ends
        starts[...] = ((ends - blocks) * tm_blk).astype(F32)
        running[...] = jnp.zeros_like(running)

    @pl.when(ph == 1)
    def _():
        earlier = (lax.broadcasted_iota(jnp.int32, (tm, tm), 0)
                   > lax.broadcasted_iota(jnp.int32, (tm, tm), 1))
        ahead = _dot(jnp.where(earlier, 1.0, 0.0).astype(BF16), chosen.astype(BF16))
        slot = starts[...] + running[...] + ahead
        lane_o = lax.broadcasted_iota(jnp.int32, (tm, LANES), 1)
        packed = jnp.zeros((tm, LANES), F32)
        for k, pick in enumerate(picks):
            dest_k = jnp.sum(jnp.where(pick, slot, 0.0), axis=-1, keepdims=True)
            packed = jnp.where(lane_o == k, dest_k, packed)
        dest_ref[...] = packed.astype(jnp.int32)
        running[...] += tile_counts


def _route(top, tm_blk):
    n = top.shape[0]
    tm = 512
    per_expert = pl.BlockSpec((1, LANES), lambda ph, i: (0, 0))
    return pl.pallas_call(
        functools.partial(_route_kernel, tm=tm, tm_blk=tm_blk),
        grid=(2, n // tm),
        in_specs=[pl.BlockSpec((tm, LANES), lambda ph, i: (i, 0))],
        out_specs=[pl.BlockSpec((tm, LANES), lambda ph, i: (i * ph, 0)), per_expert, per_expert],
        out_shape=[jax.ShapeDtypeStruct((n, LANES), jnp.int32),
                   jax.ShapeDtypeStruct((1, LANES), jnp.int32), jax.ShapeDtypeStruct((1, LANES), jnp.int32)],
        scratch_shapes=[pltpu.VMEM((1, LANES), F32), pltpu.VMEM((1, LANES), F32)],
        compiler_params=_params(2),
        name="route",
    )(top)


def _dispatch_kernel(ends_ref, dest_ref, *rest, tm, tm_blk, n_blocks, tiles):
    h_refs, (xs_hbm, zeros, sem) = rest[:-3], rest[-3:]
    i = pl.program_id(0)

    def fill(b):
        rows = pl.ds(pl.multiple_of(b * (tm_blk * SLAB), tm_blk * SLAB), tm_blk * SLAB)
        return pltpu.make_async_copy(zeros, xs_hbm.at[rows], sem.at[1])

    def fill_start(b, carry):
        fill(b).start()
        return carry

    def fill_wait(b, carry):
        fill(b).wait()
        return carry

    def zero_padding():
        zeros[...] = jnp.zeros_like(zeros)
        n_used = ends_ref[N_EXPERTS - 1]
        for do in (fill_start, fill_wait):
            for e in range(N_EXPERTS):
                first = ends_ref[e - 1] if e else 0

                @pl.when(ends_ref[e] > first)
                def _():
                    do(ends_ref[e] - 1, 0)

            lax.fori_loop(n_used, n_blocks, do, 0)

    pl.when(i == 0)(zero_padding)

    def scatter(h_ref):
        def body(jj, carry):
            for j in (jj * DMA_UNROLL + u for u in range(DMA_UNROLL)):
                for k in range(TOP_K):
                    pltpu.make_async_copy(h_ref.at[_slab_rows(j)], xs_hbm.at[_slab_rows(dest_ref[j * TOP_K + k])],
                                          sem.at[0]).start(priority=k % 2)
            return carry

        lax.fori_loop(0, tm // DMA_UNROLL, body, 0)
        for k in range(TOP_K):
            pltpu.make_async_copy(h_ref, xs_hbm.at[pl.ds(0, tm * SLAB)], sem.at[0]).wait()

    start = 0
    for h_ref, n_tiles in zip(h_refs, tiles):
        pl.when((i >= start) & (i < start + n_tiles))(functools.partial(scatter, h_ref))
        start += n_tiles


def _dispatch(hs, dest_flat, ends, n_blocks, tm_blk):
    tm = 256
    tiles = [h.shape[0] // SLAB // tm for h in hs]
    starts = [sum(tiles[:s]) for s in range(len(hs))]
    src = lambda s: pl.BlockSpec((tm * SLAB, LANES),
                                 lambda i, ends: (jnp.clip(i - starts[s], 0, tiles[s] - 1), 0))
    grid_spec = pltpu.PrefetchScalarGridSpec(
        num_scalar_prefetch=1,
        grid=(sum(tiles),),
        in_specs=[pl.BlockSpec((tm * TOP_K,), lambda i, ends: (i,), memory_space=pltpu.SMEM)]
        + [src(s) for s in range(len(hs))],
        out_specs=pl.BlockSpec(memory_space=pl.ANY),
        scratch_shapes=[pltpu.VMEM((tm_blk * SLAB, LANES), F32), pltpu.SemaphoreType.DMA((2,))],
    )
    return pl.pallas_call(
        functools.partial(_dispatch_kernel, tm=tm, tm_blk=tm_blk, n_blocks=n_blocks, tiles=tiles),
        grid_spec=grid_spec,
        out_shape=jax.ShapeDtypeStruct((n_blocks * tm_blk * SLAB, LANES), F32),
        compiler_params=_params(1),
        name="dispatch",
    )(ends, dest_flat, *hs)


def _moe_kernel(be_ref, nu_ref, valid_ref, nxt_ref, slot_ref, x_ref, wgu_hbm, wd_hbm, bg_ref, bu_ref, bd_ref,
                o_ref, wgu_buf, wd_buf, wt, wg_t, wu_t, wd_bf, sem):
    i = pl.program_id(0)
    n_used = nu_ref[0]
    valid = valid_ref[i]
    tm = x_ref.shape[0] // SLAB
    _, d, d_gu = wgu_buf.shape
    d_ff = d_gu // 2
    new_expert = (i == 0) | (be_ref[i] != be_ref[jnp.maximum(i - 1, 0)])

    def weights(e, s):
        return (pltpu.make_async_copy(wgu_hbm.at[e], wgu_buf.at[s], sem.at[0, s]),
                pltpu.make_async_copy(wd_hbm.at[e], wd_buf.at[s], sem.at[1, s]))

    @pl.when(i == 0)
    def _():
        for copy in weights(be_ref[0], slot_ref[0]):
            copy.start()

    @pl.when((i < n_used) & new_expert)
    def _():
        s = slot_ref[i]
        for copy in weights(be_ref[i], s):
            copy.wait()

        @pl.when(nxt_ref[i] >= 0)
        def _():
            for copy in weights(nxt_ref[i], 1 - s):
                copy.start()

        for c in range(d // LANES):
            wt[...] = wgu_buf[s, pl.ds(c * LANES, LANES), :].T
            rows = slice(c * LANES, (c + 1) * LANES)
            wg_t[:, rows] = wt[pl.ds(0, d_ff, stride=2), :].astype(BF16)
            wu_t[:, rows] = wt[pl.ds(1, d_ff, stride=2), :].astype(BF16)
        wd_bf[...] = wd_buf[s].astype(BF16)

    def experts_mlp(rows):
        x = _slab_load(x_ref, rows).astype(BF16)
        acc = bd_ref[...]
        for c in range(d_ff // MOE_COLS):
            cols = slice(c * MOE_COLS, (c + 1) * MOE_COLS)
            gate = jnp.minimum(_dot_nt(x, wg_t[cols, :]) + bg_ref[:, cols], SWIGLU_LIMIT)
            up = jnp.clip(_dot_nt(x, wu_t[cols, :]) + bu_ref[:, cols], -SWIGLU_LIMIT, SWIGLU_LIMIT)
            act = gate * jax.nn.sigmoid(SWIGLU_ALPHA * gate) * (up + 1.0)
            acc = acc + _dot(act.astype(BF16), wd_bf[cols, :])
        _slab_store(o_ref, acc)
        if rows < tm:
            o_ref[rows * SLAB:, :] = jnp.zeros(((tm - rows) * SLAB, LANES), o_ref.dtype)

    sizes = [tm // 4, tm // 2, tm]
    for lo, rows in zip([0] + sizes[:-1], sizes):
        pl.when((i < n_used) & (valid > lo) & (valid <= rows))(functools.partial(experts_mlp, rows))

    @pl.when(i >= n_used)
    def _():
        o_ref[...] = jnp.zeros_like(o_ref)


def _moe_plan(ends, counts, n_blocks, tm):
    blk = jnp.arange(n_blocks, dtype=jnp.int32)
    e_ids = jnp.arange(N_EXPERTS, dtype=jnp.int32)
    block_e = jnp.minimum(jnp.sum(blk[:, None] >= ends[None, :], axis=1), N_EXPERTS - 1).astype(jnp.int32)
    mine = block_e[:, None] == e_ids[None, :]
    pick = lambda per_e: jnp.sum(jnp.where(mine, per_e[None, :], 0), axis=1)
    first_blk = ends - (counts + tm - 1) // tm
    valid = jnp.clip(pick(counts) - (blk - pick(first_blk)) * tm, 0, tm)
    valid = jnp.where(blk < ends[N_EXPERTS - 1], valid, 0).astype(jnp.int32)
    used = counts > 0
    slot_e = (jnp.cumsum(used.astype(jnp.int32)) - 1) % 2
    later = (e_ids[None, :] > e_ids[:, None]) & used[None, :]
    next_e = jnp.min(jnp.where(later, e_ids[None, :], N_EXPERTS), axis=1)
    next_e = jnp.where(next_e < N_EXPERTS, next_e, -1)
    nxt = jnp.where((blk == pick(first_blk)) & (valid > 0), pick(next_e), -1).astype(jnp.int32)
    return block_e, ends[N_EXPERTS - 1:], valid, nxt, pick(slot_e).astype(jnp.int32)


def _moe_blocks(xs, block_e, n_used, valid, nxt, slot, w_gu, w_down, b_gu, b_down, tm):
    n_blocks = block_e.shape[0]
    n_e, d, d_gu = w_gu.shape
    d_ff = d_gu // 2
    bias = lambda n: pl.BlockSpec((None, 1, n), lambda i, be, *_: (be[i], 0, 0))
    grid_spec = pltpu.PrefetchScalarGridSpec(
        num_scalar_prefetch=5,
        grid=(n_blocks,),
        in_specs=[pl.BlockSpec((tm * SLAB, LANES), lambda i, be, nu, *_: (jnp.minimum(i, nu[0] - 1), 0)),
                  pl.BlockSpec(memory_space=pl.ANY), pl.BlockSpec(memory_space=pl.ANY),
                  bias(d_ff), bias(d_ff), bias(d)],
        out_specs=pl.BlockSpec((tm * SLAB, LANES), lambda i, *_: (i, 0)),
        scratch_shapes=[pltpu.VMEM((2, d, d_gu), F32), pltpu.VMEM((2, d_ff, d), F32),
                        pltpu.VMEM((d_gu, LANES), F32), pltpu.VMEM((d_ff, d), BF16),
                        pltpu.VMEM((d_ff, d), BF16), pltpu.VMEM((d_ff, d), BF16),
                        pltpu.SemaphoreType.DMA((2, 2))],
    )
    return pl.pallas_call(
        _moe_kernel,
        grid_spec=grid_spec,
        out_shape=jax.ShapeDtypeStruct((n_blocks * tm * SLAB, LANES), F32),
        compiler_params=_params(1, VMEM_LIMIT),
        name="moe_blocks",
    )(block_e, n_used, valid, nxt, slot, xs, w_gu, w_down, b_gu[:, None, 0::2], b_gu[:, None, 1::2],
      b_down.reshape(n_e, 1, d))


def _fin_kernel(dest_ref, next_ref, rows_hbm, top_ref, x1_ref, gt_ref, g_ref, o_ref, buf, sem, *, tm):
    i = pl.program_id(0)
    slot = i % 2

    def gather(idx_ref, s):
        def body(jj, carry):
            for j in (jj * DMA_UNROLL + u for u in range(DMA_UNROLL)):
                for k in range(TOP_K):
                    pltpu.make_async_copy(rows_hbm.at[_slab_rows(idx_ref[j * TOP_K + k])],
                                          buf.at[s, k, _slab_rows(j)], sem.at[s]).start(priority=k % 2)
            return carry
        lax.fori_loop(0, tm // DMA_UNROLL, body, 0)

    @pl.when(i == 0)
    def _():
        gather(dest_ref, 0)

    @pl.when(i + 1 < pl.num_programs(0))
    def _():
        gather(next_ref, 1 - slot)

    for k in range(TOP_K):
        pltpu.make_async_copy(rows_hbm.at[pl.ds(0, tm * SLAB)], buf.at[slot, k], sem.at[slot]).wait()
    top = top_ref[...]
    f = sum(_slab_load(buf.at[slot, k], tm) * top[:, TOP_K + k:TOP_K + k + 1] for k in range(TOP_K))
    o_ref[...] = x1_ref[...] + gt_ref[...] * _rms(f, g_ref[...])


def _finish(dest, rows, top, x1, mod, g_post, per_row, rows_per_batch):
    n = x1.shape[0]
    tm = min(TM_FIN, n)
    full = pl.BlockSpec((tm, D_MODEL), lambda i: (i, 0))
    gate = _mod_spec(tm, MOD_GATE_FFN, per_row, 1 if per_row else rows_per_batch // tm)
    return pl.pallas_call(
        functools.partial(_fin_kernel, tm=tm),
        grid=(n // tm,),
        in_specs=[pl.BlockSpec((tm * TOP_K,), lambda i: (i,), memory_space=pltpu.SMEM),
                  pl.BlockSpec((tm * TOP_K,), lambda i: (jnp.minimum(i + 1, n // tm - 1),),
                               memory_space=pltpu.SMEM),
                  pl.BlockSpec(memory_space=pl.ANY),
                  pl.BlockSpec((tm, LANES), lambda i: (i, 0)), full, gate,
                  pl.BlockSpec((1, D_MODEL), lambda i: (0, 0))],
        out_specs=full,
        out_shape=jax.ShapeDtypeStruct((n, D_MODEL), F32),
        scratch_shapes=[pltpu.VMEM((2, TOP_K, tm * SLAB, LANES), F32), pltpu.SemaphoreType.DMA((2,))],
        compiler_params=_params(1),
        name="finish",
    )(dest, dest, rows, top, x1, mod, g_post)


def kernel(x_prompt, x_sample, state_ret, cache_win_k, cache_win_v, c_prompt, c_sample, w_ada, b_ada,
           g_pre_mix, g_post_mix, g_pre_ffn, g_post_ffn, w_in, g_ret, w_out, w_router, b_router,
           w_gate_up, b_gate_up, w_down, b_down):
    depth = w_in.shape[0]
    assert depth == 1, "single-layer step"
    n_b, seq, d = x_prompt.shape
    n_db, t_new, _ = x_sample.shape
    n_p, n_s = n_b * seq, n_db * t_new
    buf = cache_win_k.shape[2]
    keep = min(DIL_PATTERNS[-1][0], seq)
    vec = lambda g: g[0].reshape(1, -1)

    mod_p, mod_s = _ada(c_prompt, c_sample, w_ada[0], b_ada[0], t_new)
    mod_p = mod_p[:n_b].reshape(n_b, 1, -1)
    mod_s = mod_s.reshape(n_s, -1)

    w_in_bf = w_in[0].astype(BF16)
    w_out_bf = w_out[0].astype(BF16)
    xp = x_prompt.reshape(n_p, d)
    xs = x_sample.reshape(n_s, d)
    router = (w_router[0], b_router[0].reshape(1, -1))

    qr, kr, vr, gr, qa_pm, ka_pm, va_pm, k_win, v_win = _in_proj(xp, mod_p, vec(g_pre_mix), w_in_bf, False, seq, keep)
    ret_y_p, ret_state_p = _retention(qr, kr, vr, gr, None, g_ret[0], n_b, seq, RET_CHUNK, n_b)
    att_p = _att_prompt(qa_pm, ka_pm, va_pm, n_b, seq)
    x1_p, h_p, top_p = _mixer_out(ret_y_p, att_p, xp, mod_p, w_out_bf, vec(g_post_mix), vec(g_pre_ffn),
                                  *router, False, seq)

    qr_s, kr_s, vr_s, gr_s, qa_s, ka_s, va_s = _in_proj(xs, mod_s, vec(g_pre_mix), w_in_bf, True, t_new)
    ret_y_s, ret_state_s = _retention(qr_s, kr_s, vr_s, gr_s, state_ret[0], g_ret[0], n_db, t_new, t_new,
                                      RET_GROUP_SAMPLE)
    att_s = _att_sample(qa_s, ka_s, va_s, cache_win_k[0], cache_win_v[0], n_db, t_new)
    x1_s, h_s, top_s = _mixer_out(ret_y_s, att_s, xs, mod_s, w_out_bf, vec(g_post_mix), vec(g_pre_ffn),
                                  *router, True, t_new)

    n_blocks = (n_p + n_s) * TOP_K // TM_MOE + N_EXPERTS
    dest, ends, counts = _route(jnp.concatenate([top_p, top_s], axis=0), TM_MOE)
    dest = dest[:, :TOP_K].reshape(-1)
    ends, counts = ends[0, :N_EXPERTS], counts[0, :N_EXPERTS]
    rows_in = _dispatch([h_p, h_s], dest, ends, n_blocks, TM_MOE)
    rows = _moe_blocks(rows_in, *_moe_plan(ends, counts, n_blocks, TM_MOE), w_gate_up[0], w_down[0],
                       b_gate_up[0], b_down[0], TM_MOE)
    y_p = _finish(dest[:n_p * TOP_K], rows, top_p, x1_p, mod_p, vec(g_post_ffn), False, seq)
    y_s = _finish(dest[n_p * TOP_K:], rows, top_s, x1_s, mod_s, vec(g_post_ffn), True, t_new)

    win = lambda t: t.reshape(n_b, N_HEADS, HEAD_DIM, keep).transpose(0, 3, 1, 2)[None]
    new = lambda t: t.reshape(n_db, t_new, N_HEADS, HEAD_DIM)[None]
    return (y_p.reshape(n_b, seq, d), y_s.reshape(n_db, t_new, d), ret_state_p[None], ret_state_s[None],
            win(k_win), win(v_win), new(ka_s), new(va_s))
```

```python
import functools

import numpy as np
import jax
import jax.numpy as jnp
from jax import lax
from jax.experimental import pallas as pl
from jax.experimental.pallas import tpu as pltpu

F32 = jnp.float32
BF16 = jnp.bfloat16

D_MODEL = 1024
HEAD_DIM = 64
N_HEADS = 8
D_HALF = N_HEADS * HEAD_DIM
N_SEG = 7
LANES = 128
N_PAIRS = D_HALF // LANES
RET_CHUNK = 128
RET_GROUP_SAMPLE = 8
ATT_BLOCK = 128
ATT_Q = 1024
ATT_BACK = 2
ATT_UNROLL = 2
DIL_PATTERNS = ((128, 1), (512, 4), (2048, 16))
N_EXPERTS = 32
TOP_K = 4
SWIGLU_LIMIT = 7.0
SWIGLU_ALPHA = 1.702
EPS = 1e-6
NEG_INF = -1e30
MOD_SHIFT_MIX, MOD_SCALE_MIX, MOD_GATE_MIX, MOD_SHIFT_FFN, MOD_SCALE_FFN, MOD_GATE_FFN = range(6)

TM_IN = 512
TM_OUT = 512
TM_MOE = 512
MOE_COLS = 512
TM_FIN = 128
DMA_UNROLL = 4
VMEM_LIMIT = 56 * 1024 * 1024


def _params(n_axes, vmem=None):
    return pltpu.CompilerParams(dimension_semantics=("arbitrary",) * n_axes, vmem_limit_bytes=vmem)


def _ret_log_decay():
    return np.log(1.0 - 2.0 ** (-5.0 - np.arange(N_HEADS, dtype=np.float64)))


def _alibi_slopes():
    return 2.0 ** (-8.0 * (np.arange(N_HEADS, dtype=np.float64) + 1.0) / N_HEADS)


def _per_head_lanes(a):
    return np.repeat(a, HEAD_DIM, axis=-1)


def _rms(x, g):
    return x * lax.rsqrt(jnp.mean(x * x, axis=-1, keepdims=True) + EPS) * g


def _dot(a, b):
    return jnp.dot(a, b, preferred_element_type=F32)


def _dot_nt(a, b):
    return lax.dot_general(a, b, (((1,), (1,)), ((), ())), preferred_element_type=F32)


def _dot_tn(a, b):
    return lax.dot_general(a, b, (((0,), (0,)), ((), ())), preferred_element_type=F32)


SLAB = D_MODEL // LANES


def _slab_rows(r):
    return pl.ds(pl.multiple_of(r * SLAB, SLAB), SLAB)


def _slab_load(ref, n):
    return jnp.concatenate([ref[pl.ds(s, n, stride=SLAB), :] for s in range(SLAB)], axis=1)


def _slab_store(ref, value):
    n = value.shape[0]
    for s in range(SLAB):
        ref[pl.ds(s, n, stride=SLAB), :] = value[:, s * LANES:(s + 1) * LANES]


def _mod_spec(tm, col, per_row, tiles_per_batch):
    if per_row:
        return pl.BlockSpec((tm, D_MODEL), lambda i: (i, col))
    return pl.BlockSpec((None, 1, D_MODEL), lambda i: (i // tiles_per_batch, 0, col))


def _ada_kernel(cp_ref, cs_ref, w_ref, b_ref, op_ref, os_ref):
    w = w_ref[...].astype(BF16)

    def mod(c):
        return _dot((c * jax.nn.sigmoid(c)).astype(BF16), w) + b_ref[...]

    op_ref[...] = mod(cp_ref[...])
    ms = mod(cs_ref[...])
    os_ref[...] = jnp.broadcast_to(ms[:, None, :], os_ref.shape)


def _ada(c_prompt, c_sample, w_ada, b_ada, t_new):
    n_b, d = c_prompt.shape
    n_db = c_sample.shape[0]
    n_out = w_ada.shape[1]
    cp = jnp.concatenate([c_prompt, jnp.zeros((-n_b % 8, d), F32)], axis=0)
    return pl.pallas_call(
        _ada_kernel,
        grid=(n_out // d,),
        in_specs=[pl.BlockSpec(cp.shape, lambda j: (0, 0)),
                  pl.BlockSpec((n_db, d), lambda j: (0, 0)),
                  pl.BlockSpec((d, d), lambda j: (0, j)),
                  pl.BlockSpec((1, d), lambda j: (0, j))],
        out_specs=[pl.BlockSpec(cp.shape, lambda j: (0, j)),
                   pl.BlockSpec((n_db, t_new, d), lambda j: (0, 0, j))],
        out_shape=[jax.ShapeDtypeStruct((cp.shape[0], n_out), F32),
                   jax.ShapeDtypeStruct((n_db, t_new, n_out), F32)],
        compiler_params=_params(1, VMEM_LIMIT),
        name="ada",
    )(cp, c_sample, w_ada, b_ada.reshape(1, n_out))


def _in_kernel(x_ref, sc_ref, sh_ref, g_ref, w_ref, *o_refs, n_plain, window):
    h = (_rms(x_ref[...], g_ref[...]) * (1.0 + sc_ref[...]) + sh_ref[...]).astype(BF16)
    segs = [_dot(h, w_ref[:, s * D_HALF:(s + 1) * D_HALF]) for s in range(N_SEG)]
    for o_ref, seg in zip(o_refs[:n_plain], segs):
        o_ref[...] = seg
    if n_plain == N_SEG:
        return
    for o_ref, seg in zip(o_refs[n_plain:N_SEG], segs[n_plain:]):
        for p in range(N_PAIRS):
            o_ref[p] = seg[:, p * LANES:(p + 1) * LANES]
    tiles_per_seq, first = window
    @pl.when(pl.program_id(0) % tiles_per_seq >= first)
    def _():
        for o_ref, seg in zip(o_refs[N_SEG:], segs[N_SEG - 2:]):
            o_ref[...] = seg.T


def _in_proj(x, mod, g, w_bf, per_row, rows_per_batch, keep=None):
    n = x.shape[0]
    tm = min(TM_IN, n)
    tpb = 1 if per_row else rows_per_batch // tm
    seg = pl.BlockSpec((tm, D_HALF), lambda i: (i, 0))
    seg_shape = jax.ShapeDtypeStruct((n, D_HALF), F32)
    if keep is None:
        n_plain, window = N_SEG, None
        out_specs, out_shape = [seg] * N_SEG, [seg_shape] * N_SEG
    else:
        n_plain, window = N_SEG - 3, (tpb, (rows_per_batch - keep) // tm)
        pm = pl.BlockSpec((N_PAIRS, tm, LANES), lambda i: (0, i, 0))
        win = pl.BlockSpec((None, D_HALF, tm), lambda i: (i // tpb, 0, jnp.maximum(i % tpb - window[1], 0)))
        out_specs = [seg] * n_plain + [pm] * 3 + [win] * 2
        out_shape = ([seg_shape] * n_plain + [jax.ShapeDtypeStruct((N_PAIRS, n, LANES), F32)] * 3
                     + [jax.ShapeDtypeStruct((n // rows_per_batch, D_HALF, keep), F32)] * 2)
    return pl.pallas_call(
        functools.partial(_in_kernel, n_plain=n_plain, window=window),
        grid=(n // tm,),
        in_specs=[pl.BlockSpec((tm, D_MODEL), lambda i: (i, 0)),
                  _mod_spec(tm, MOD_SCALE_MIX, per_row, tpb), _mod_spec(tm, MOD_SHIFT_MIX, per_row, tpb),
                  pl.BlockSpec((1, D_MODEL), lambda i: (0, 0)),
                  pl.BlockSpec((D_MODEL, N_SEG * D_HALF), lambda i: (0, 0))],
        out_specs=out_specs,
        out_shape=out_shape,
        compiler_params=_params(1, VMEM_LIMIT),
        name="in_proj",
    )(x, mod, mod, g, w_bf)


def _ret_tables(chunk):
    lg = _ret_log_decay()
    pos = np.arange(chunk, dtype=np.float64)
    diff = pos[:, None] - pos[None, :]
    dmask = np.where(diff >= 0, np.exp(np.maximum(diff, 0.0)[None] * lg[:, None, None]), 0.0)
    dmask = dmask.reshape(N_PAIRS, 2 * chunk, chunk)
    row_decay = _per_head_lanes(np.exp((pos[:, None] + 1.0) * lg[None, :]))
    key_decay = _per_head_lanes(np.exp((chunk - 1.0 - pos)[:, None] * lg[None, :]))
    state_decay = _per_head_lanes(np.exp(chunk * lg)[None, :])
    return tuple(jnp.asarray(t, F32) for t in (dmask, row_decay, key_decay, state_decay))


def _ret_kernel(q_ref, k_ref, v_ref, gate_ref, s0_ref, gret_ref, dm_ref, rd_ref, kd_ref, sd_ref,
                y_ref, st_ref, state, *, chunk, group, has_init):
    c = pl.program_id(1)
    lo = lax.broadcasted_iota(jnp.int32, (chunk, LANES), 1) < HEAD_DIM
    blk_r = lax.broadcasted_iota(jnp.int32, (LANES, LANES), 0) < HEAD_DIM
    blk_c = lax.broadcasted_iota(jnp.int32, (LANES, LANES), 1) < HEAD_DIM
    same_head = blk_r == blk_c

    @pl.when(c == 0)
    def _():
        if has_init:
            zero = jnp.zeros((HEAD_DIM, HEAD_DIM), F32)
            for b in range(group):
                for p in range(N_PAIRS):
                    top = jnp.concatenate([s0_ref[b, 2 * p], zero], axis=1)
                    bot = jnp.concatenate([zero, s0_ref[b, 2 * p + 1]], axis=1)
                    state[b * N_PAIRS + p] = jnp.concatenate([top, bot], axis=0)
        else:
            state[...] = jnp.zeros_like(state)

    new_states = []
    for b, p in [(b, p) for b in range(group) for p in range(N_PAIRS)]:
        sl = slice(p * LANES, (p + 1) * LANES)
        q = q_ref[b, :, sl]
        k = k_ref[b, :, sl] * HEAD_DIM ** -0.5
        vb = v_ref[b, :, sl].astype(BF16)
        kb = k.astype(BF16)
        q2 = jnp.concatenate([jnp.where(lo, q, 0.0), jnp.where(lo, 0.0, q)], axis=0).astype(BF16)
        inner = _dot((_dot_nt(q2, kb) * dm_ref[p]).astype(BF16), vb)
        s_old = state[b * N_PAIRS + p]
        cross = _dot(q.astype(BF16), s_old.astype(BF16)) * rd_ref[:, sl]
        o = jnp.where(lo, inner[:chunk], inner[chunk:]) + cross
        upd = _dot_tn((k * kd_ref[:, sl]).astype(BF16), vb)
        s_new = sd_ref[:, sl] * s_old + jnp.where(same_head, upd, 0.0)
        state[b * N_PAIRS + p] = s_new
        new_states.append(s_new)

        def head_mean(t):
            m_lo = jnp.sum(jnp.where(lo, t, 0.0), axis=-1, keepdims=True)
            m_hi = jnp.sum(jnp.where(lo, 0.0, t), axis=-1, keepdims=True)
            return jnp.where(lo, m_lo, m_hi) * (1.0 / HEAD_DIM)

        d = o - head_mean(o)
        n = d * lax.rsqrt(head_mean(d * d) + EPS) * gret_ref[:, sl]
        g = gate_ref[b, :, sl]
        y_ref[b, :, sl] = (g * jax.nn.sigmoid(g) * n).astype(y_ref.dtype)

    @pl.when(c == pl.num_programs(1) - 1)
    def _():
        for bp, s_new in enumerate(new_states):
            b, p = divmod(bp, N_PAIRS)
            st_ref[b, 2 * p] = s_new[:HEAD_DIM, :HEAD_DIM]
            st_ref[b, 2 * p + 1] = s_new[HEAD_DIM:, HEAD_DIM:]


def _retention(q, k, v, gate, state0, g_ret, n_batch, seq, chunk, group):
    nc = seq // chunk
    by_seq = lambda t: t.reshape(n_batch, seq, D_HALF)
    has_init = state0 is not None
    if state0 is None:
        state0 = jnp.zeros((group, N_HEADS, HEAD_DIM, HEAD_DIM), F32)
        s0_map = lambda b, c: (0, 0, 0, 0)
    else:
        s0_map = lambda b, c: (b, 0, 0, 0)
    dmask, row_decay, key_decay, state_decay = _ret_tables(chunk)
    rows = pl.BlockSpec((group, chunk, D_HALF), lambda b, c: (b, c, 0))
    const2 = lambda shape: pl.BlockSpec(shape, lambda b, c: (0, 0))
    st_block = pl.BlockSpec((group, N_HEADS, HEAD_DIM, HEAD_DIM), lambda b, c: (b, 0, 0, 0))
    y, state = pl.pallas_call(
        functools.partial(_ret_kernel, chunk=chunk, group=group, has_init=has_init),
        grid=(n_batch // group, nc),
        in_specs=[rows, rows, rows, rows,
                  pl.BlockSpec((group, N_HEADS, HEAD_DIM, HEAD_DIM), s0_map),
                  const2((1, D_HALF)),
                  pl.BlockSpec((N_PAIRS, 2 * chunk, chunk), lambda b, c: (0, 0, 0)),
                  const2((chunk, D_HALF)), const2((chunk, D_HALF)), const2((1, D_HALF))],
        out_specs=[rows, st_block],
        out_shape=[jax.ShapeDtypeStruct((n_batch, seq, D_HALF), BF16),
                   jax.ShapeDtypeStruct((n_batch, N_HEADS, HEAD_DIM, HEAD_DIM), F32)],
        scratch_shapes=[pltpu.VMEM((group * N_PAIRS, LANES, LANES), F32)],
        compiler_params=_params(2),
        name="retention",
    )(by_seq(q), by_seq(k), by_seq(v), by_seq(gate), state0, g_ret.reshape(1, D_HALF), dmask, row_decay,
      key_decay, state_decay)
    return y.reshape(n_batch * seq, D_HALF), state


def _band_table(n_q, n_k, dilation):
    step = np.arange(n_q)[:, None] + (n_k - n_q) - np.arange(n_k)[None, :]
    in_win = (step >= 0) & (step <= ATT_BLOCK)
    bias = -_alibi_slopes()[:, None, None] * (step * dilation).astype(np.float64)[None]
    return jnp.asarray(np.where(in_win[None], bias, NEG_INF).reshape(N_PAIRS, 2 * n_q, n_k), F32)


def _band_attention(q, k, v, table_ref, key_ok, pair):
    n_q = q.shape[0]
    lo = lax.broadcasted_iota(jnp.int32, q.shape, 1) < HEAD_DIM
    q = q * HEAD_DIM ** -0.5
    q2 = jnp.concatenate([jnp.where(lo, q, 0.0), jnp.where(lo, 0.0, q)], axis=0).astype(BF16)
    s = _dot_nt(q2, k.astype(BF16)) + table_ref[pair]
    s = jnp.where(key_ok, s, NEG_INF)
    m = jnp.max(s, axis=-1, keepdims=True)
    e = jnp.exp(s - m)
    l = jnp.sum(e, axis=-1, keepdims=True)
    o = _dot(e.astype(BF16), v.astype(BF16))
    return tuple(jnp.where(lo, t[:n_q], t[n_q:]) for t in (o, m, l))


def _attp_kernel(*refs):
    n_back = ATT_BACK
    q_ref = refs[0]
    k_refs = refs[1:2 + n_back]
    v_refs = refs[2 + n_back:3 + 2 * n_back]
    t_refs = refs[3 + 2 * n_back:3 + 2 * n_back + len(DIL_PATTERNS)]
    o_ref, parts, kcat, vcat = refs[3 + 2 * n_back + len(DIL_PATTERNS):]
    n = pl.program_id(1)

    def key_col(n_q, n_k):
        return lax.broadcasted_iota(jnp.int32, (2 * n_q, n_k), 1)

    def fold(first, rows, news):
        if not first:
            olds = [[parts[kind, p, rows, :] for kind in range(3)] for p in range(N_PAIRS)]
            merged = []
            for (o_old, m_old, l_old), (o_new, m_new, l_new) in zip(olds, news):
                m = jnp.maximum(m_old, m_new)
                a, b = jnp.exp(m_old - m), jnp.exp(m_new - m)
                merged.append((o_old * a + o_new * b, m, l_old * a + l_new * b))
            news = merged
        for p, new in enumerate(news):
            for kind, val in enumerate(new):
                parts[kind, p, rows, :] = val

    for pat, (window, dil) in enumerate(DIL_PATTERNS):
        t_ref = t_refs[pat]
        if dil == 1:
            for cat, src in ((kcat, k_refs), (vcat, v_refs)):
                cat[:, :ATT_BLOCK, :] = src[1][:, ATT_Q - ATT_BLOCK:, :]
                cat[:, ATT_BLOCK:, :] = src[0][...]

            def body(it, carry, t_ref=t_ref, first=pat == 0):
                for u in range(ATT_UNROLL):
                    sub = it * ATT_UNROLL + u
                    r0 = pl.multiple_of(sub * ATT_BLOCK, ATT_BLOCK)
                    ok = (key_col(ATT_BLOCK, 2 * ATT_BLOCK) >= ATT_BLOCK) | (n > 0) | (sub > 0)
                    fold(first, pl.ds(r0, ATT_BLOCK),
                         [_band_attention(q_ref[p, pl.ds(r0, ATT_BLOCK), :], kcat[p, pl.ds(r0, 2 * ATT_BLOCK), :],
                                          vcat[p, pl.ds(r0, 2 * ATT_BLOCK), :], t_ref, ok, p)
                          for p in range(N_PAIRS)])
                return carry

            lax.fori_loop(0, ATT_Q // ATT_BLOCK // ATT_UNROLL, body, 0)
            continue
        sub_rows = min(window, ATT_Q)
        n_q = sub_rows // dil
        n_prev = window // sub_rows
        for s in range(ATT_Q // sub_rows):
            pieces = []
            for i in range(n_prev, -1, -1):
                rel = (s - i) * sub_rows
                j = -(rel // ATT_Q)
                pieces.append((j, rel + j * ATT_Q))
            missing = sum(jnp.where(n < j, n_q, 0) for j, _ in pieces)

            unroll = ATT_UNROLL if n_q == ATT_BLOCK else 1

            def body(it, carry, dil=dil, n_q=n_q, t_ref=t_ref, first=pat == 0, pieces=pieces, missing=missing,
                     q_off=s * sub_rows, unroll=unroll):
                ok = key_col(n_q, len(pieces) * n_q) >= missing
                for u in range(unroll):
                    r = it * unroll + u
                    rows = pl.ds(q_off + r, n_q, stride=dil)
                    gather = lambda refs, p: jnp.concatenate(
                        [refs[j][p, pl.ds(off + r, n_q, stride=dil), :] for j, off in pieces], axis=0)
                    fold(first, rows, [_band_attention(q_ref[p, rows, :], gather(k_refs, p), gather(v_refs, p),
                                                       t_ref, ok, p) for p in range(N_PAIRS)])
                return carry

            lax.fori_loop(0, dil // unroll, body, 0)

    for p in range(N_PAIRS):
        o_ref[:, p * LANES:(p + 1) * LANES] = (parts[0, p] / parts[2, p]).astype(o_ref.dtype)


def _att_prompt(q, k, v, n_batch, seq):
    nb = seq // ATT_Q
    assert all(w // d == ATT_BLOCK and ATT_Q % min(w, ATT_Q) == 0 and w % min(w, ATT_Q) == 0
               for w, d in DIL_PATTERNS)
    assert ATT_BACK == DIL_PATTERNS[-1][0] // ATT_Q and DIL_PATTERNS[0] == (ATT_BLOCK, 1)
    blk = lambda j: pl.BlockSpec((N_PAIRS, ATT_Q, LANES),
                                 lambda b, i: (0, b * nb + jnp.maximum(i - j, 0), 0))
    tables = [_band_table(ATT_BLOCK, 2 * ATT_BLOCK, 1)] + [
        _band_table(min(w, ATT_Q) // d, (w // min(w, ATT_Q) + 1) * (min(w, ATT_Q) // d), d)
        for w, d in DIL_PATTERNS[1:]]
    back = [blk(j) for j in range(ATT_BACK + 1)]
    return pl.pallas_call(
        _attp_kernel,
        grid=(n_batch, nb),
        in_specs=[blk(0)] + back + back + [pl.BlockSpec(t.shape, lambda b, i: (0, 0, 0)) for t in tables],
        out_specs=pl.BlockSpec((ATT_Q, D_HALF), lambda b, i: (b * nb + i, 0)),
        out_shape=jax.ShapeDtypeStruct((n_batch * seq, D_HALF), BF16),
        scratch_shapes=[pltpu.VMEM((3, N_PAIRS, ATT_Q, LANES), F32),
                        pltpu.VMEM((N_PAIRS, ATT_Q + ATT_BLOCK, LANES), F32),
                        pltpu.VMEM((N_PAIRS, ATT_Q + ATT_BLOCK, LANES), F32)],
        compiler_params=_params(2, VMEM_LIMIT),
        name="att_prompt",
    )(q, *([k] * (ATT_BACK + 1)), *([v] * (ATT_BACK + 1)), *tables)


def _atts_tables(buf, t_new):
    pos_q = buf + np.arange(t_new)
    pos_k = np.concatenate([np.arange(buf), buf + np.arange(LANES)])
    dist = pos_q[:, None] - pos_k[None, :]
    count = np.zeros(dist.shape)
    for window, dilation in DIL_PATTERNS:
        count += (dist >= 0) & (dist <= window) & (dist % dilation == 0)
    count[:, buf + t_new:] = 0
    bias = -_alibi_slopes()[:, None, None] * dist[None].astype(np.float64)
    table = np.where(count[None] > 0, bias + np.log(np.maximum(count, 1.0))[None], NEG_INF)
    table = table.reshape(N_HEADS * t_new, buf + LANES)
    return jnp.asarray(table[:, :buf], F32), jnp.asarray(table[:, buf:], F32)


def _atts_kernel(q_ref, kt_ref, kn_ref, vt_ref, vn_ref, tc_ref, tn_ref, o_ref, *, t_new):
    rows = N_HEADS * t_new
    row_head = lax.broadcasted_iota(jnp.int32, (rows, D_HALF), 0) // t_new
    col_head = lax.broadcasted_iota(jnp.int32, (rows, D_HALF), 1) // HEAD_DIM
    own = row_head == col_head
    q = q_ref[...] * HEAD_DIM ** -0.5
    q_heads = jnp.where(own, jnp.concatenate([q] * N_HEADS, axis=0), 0.0).astype(BF16)
    pad = jnp.zeros((LANES - t_new, D_HALF), F32)
    kn = jnp.concatenate([kn_ref[...], pad], axis=0).astype(BF16)
    vn = jnp.concatenate([vn_ref[...], pad], axis=0).astype(BF16)
    sc = _dot(q_heads, kt_ref[...].astype(BF16)) + tc_ref[...]
    sn = _dot_nt(q_heads, kn) + tn_ref[...]
    m = jnp.maximum(jnp.max(sc, axis=-1, keepdims=True), jnp.max(sn, axis=-1, keepdims=True))
    ec = jnp.exp(sc - m)
    en = jnp.exp(sn - m)
    l = jnp.sum(ec, axis=-1, keepdims=True) + jnp.sum(en, axis=-1, keepdims=True)
    acc = _dot_nt(ec.astype(BF16), vt_ref[...].astype(BF16)) + _dot(en.astype(BF16), vn)
    acc = jnp.where(own, acc, 0.0) / l
    out = acc[:t_new]
    for h in range(1, N_HEADS):
        out = out + acc[h * t_new:(h + 1) * t_new]
    o_ref[...] = out.astype(o_ref.dtype)


def _att_sample(q, k_new, v_new, cache_k, cache_v, n_batch, t_new):
    buf = cache_k.shape[1]
    tc, tn = _atts_tables(buf, t_new)
    by_pos = lambda c: c.transpose(0, 2, 3, 1).reshape(n_batch, D_HALF, buf)
    rows = pl.BlockSpec((t_new, D_HALF), lambda b: (b, 0))
    cache = pl.BlockSpec((None, D_HALF, buf), lambda b: (b, 0, 0))
    return pl.pallas_call(
        functools.partial(_atts_kernel, t_new=t_new),
        grid=(n_batch,),
        in_specs=[rows, cache, rows, cache, rows,
                  pl.BlockSpec((N_HEADS * t_new, buf), lambda b: (0, 0)),
                  pl.BlockSpec((N_HEADS * t_new, LANES), lambda b: (0, 0))],
        out_specs=rows,
        out_shape=jax.ShapeDtypeStruct((n_batch * t_new, D_HALF), BF16),
        compiler_params=_params(1, VMEM_LIMIT),
        name="att_sample",
    )(q, by_pos(cache_k), k_new, by_pos(cache_v), v_new, tc, tn)


def _out_kernel(ret_ref, att_ref, x_ref, gt_ref, sc_ref, sh_ref, w_ref, gpost_ref, gpre_ref, wr_ref, br_ref,
                x1_ref, h_ref, top_ref):
    tm = x_ref.shape[0]
    mix = _dot(ret_ref[...], w_ref[:D_HALF, :]) + _dot(att_ref[...], w_ref[D_HALF:, :])
    x1 = x_ref[...] + gt_ref[...] * _rms(mix, gpost_ref[...])
    x1_ref[...] = x1
    h = _rms(x1, gpre_ref[...]) * (1.0 + sc_ref[...]) + sh_ref[...]
    _slab_store(h_ref, h)
    split = lambda t: (t.astype(BF16), (t - t.astype(BF16).astype(F32)).astype(BF16))
    (h_hi, h_lo), (w_hi, w_lo) = split(h), split(wr_ref[...])
    logits = _dot(h_hi, w_hi) + _dot(h_lo, w_hi) + _dot(h_hi, w_lo) + br_ref[...]
    lane_e = lax.broadcasted_iota(jnp.int32, (tm, N_EXPERTS), 1).astype(F32)
    lane_o = lax.broadcasted_iota(jnp.int32, (tm, LANES), 1)
    vals, idxs = [], []
    work = logits
    for _ in range(TOP_K):
        v = jnp.max(work, axis=-1, keepdims=True)
        i = jnp.min(jnp.where(work == v, lane_e, float(N_EXPERTS)), axis=-1, keepdims=True)
        vals.append(v)
        idxs.append(i)
        work = jnp.where(lane_e == i, -jnp.inf, work)
    es = [jnp.exp(v - vals[0]) for v in vals]
    tot = sum(es)
    packed = jnp.zeros((tm, LANES), F32)
    for j in range(TOP_K):
        packed = jnp.where(lane_o == j, idxs[j], packed)
        packed = jnp.where(lane_o == TOP_K + j, es[j] / tot, packed)
    top_ref[...] = packed


def _mixer_out(ret_y, att, x, mod, w_out_bf, g_post, g_pre, w_router, b_router, per_row, rows_per_batch):
    n = x.shape[0]
    tm = min(TM_OUT, n)
    tpb = 1 if per_row else rows_per_batch // tm
    half = pl.BlockSpec((tm, D_HALF), lambda i: (i, 0))
    full = pl.BlockSpec((tm, D_MODEL), lambda i: (i, 0))
    stat = pl.BlockSpec((tm, LANES), lambda i: (i, 0))
    vec = pl.BlockSpec((1, D_MODEL), lambda i: (0, 0))
    mods = [_mod_spec(tm, c, per_row, tpb) for c in (MOD_GATE_MIX, MOD_SCALE_FFN, MOD_SHIFT_FFN)]
    return pl.pallas_call(
        _out_kernel,
        grid=(n // tm,),
        in_specs=[half, half, full] + mods + [
                  pl.BlockSpec((D_MODEL, D_MODEL), lambda i: (0, 0)), vec, vec,
                  pl.BlockSpec((D_MODEL, N_EXPERTS), lambda i: (0, 0)),
                  pl.BlockSpec((1, N_EXPERTS), lambda i: (0, 0))],
        out_specs=[full, pl.BlockSpec((tm * SLAB, LANES), lambda i: (i, 0)), stat],
        out_shape=[jax.ShapeDtypeStruct((n, D_MODEL), F32),
                   jax.ShapeDtypeStruct((n * SLAB, LANES), F32),
                   jax.ShapeDtypeStruct((n, LANES), F32)],
        compiler_params=_params(1, VMEM_LIMIT),
        name="mixer_out",
    )(ret_y, att, x, mod, mod, mod, w_out_bf, g_post, g_pre, w_router, b_router)


def _route_kernel(top_ref, dest_ref, ends_ref, counts_ref, running, starts, *, tm, tm_blk):
    ph, i = pl.program_id(0), pl.program_id(1)
    lane_f = lax.broadcasted_iota(jnp.int32, (tm, LANES), 1).astype(F32)
    top = top_ref[...]
    picks = [lane_f == top[:, k:k + 1] for k in range(TOP_K)]
    chosen = sum(p.astype(F32) for p in picks)
    tile_counts = jnp.sum(chosen, axis=0, keepdims=True)

    @pl.when((ph == 0) & (i == 0))
    def _():
        running[...] = jnp.zeros_like(running)

    @pl.when(ph == 0)
    def _():
        running[...] += tile_counts

    @pl.when((ph == 1) & (i == 0))
    def _():
        lane = lax.broadcasted_iota(jnp.int32, (1, LANES), 1)
        assert tm_blk & (tm_blk - 1) == 0
        counts = running[...].astype(jnp.int32)
        counts_ref[...] = counts
        blocks = (counts + (tm_blk - 1)) >> (tm_blk.bit_length() - 1)
        ends = blocks
        shift = 1
        while shift < N_EXPERTS:
            ends = ends + jnp.where(lane >= shift, pltpu.roll(ends, shift, axis=1), 0)
            shift *= 2
        ends_ref[...] = ends
        starts[...] = ((ends - blocks) * tm_blk).astype(F32)
        running[...] = jnp.zeros_like(running)

    @pl.when(ph == 1)
    def _():
        earlier = (lax.broadcasted_iota(jnp.int32, (tm, tm), 0)
                   > lax.broadcasted_iota(jnp.int32, (tm, tm), 1))
        ahead = _dot(jnp.where(earlier, 1.0, 0.0).astype(BF16), chosen.astype(BF16))
        slot = starts[...] + running[...] + ahead
        lane_o = lax.broadcasted_iota(jnp.int32, (tm, LANES), 1)
        packed = jnp.zeros((tm, LANES), F32)
        for k, pick in enumerate(picks):
            dest_k = jnp.sum(jnp.where(pick, slot, 0.0), axis=-1, keepdims=True)
            packed = jnp.where(lane_o == k, dest_k, packed)
        dest_ref[...] = packed.astype(jnp.int32)
        running[...] += tile_counts


def _route(top, tm_blk):
    n = top.shape[0]
    tm = 512
    per_expert = pl.BlockSpec((1, LANES), lambda ph, i: (0, 0))
    return pl.pallas_call(
        functools.partial(_route_kernel, tm=tm, tm_blk=tm_blk),
        grid=(2, n // tm),
        in_specs=[pl.BlockSpec((tm, LANES), lambda ph, i: (i, 0))],
        out_specs=[pl.BlockSpec((tm, LANES), lambda ph, i: (i * ph, 0)), per_expert, per_expert],
        out_shape=[jax.ShapeDtypeStruct((n, LANES), jnp.int32),
                   jax.ShapeDtypeStruct((1, LANES), jnp.int32), jax.ShapeDtypeStruct((1, LANES), jnp.int32)],
        scratch_shapes=[pltpu.VMEM((1, LANES), F32), pltpu.VMEM((1, LANES), F32)],
        compiler_params=_params(2),
        name="route",
    )(top)


def _dispatch_kernel(ends_ref, dest_ref, *rest, tm, tm_blk, n_blocks, tiles):
    h_refs, (xs_hbm, zeros, sem) = rest[:-3], rest[-3:]
    i = pl.program_id(0)

    def fill(b):
        rows = pl.ds(pl.multiple_of(b * (tm_blk * SLAB), tm_blk * SLAB), tm_blk * SLAB)
        return pltpu.make_async_copy(zeros, xs_hbm.at[rows], sem.at[1])

    def fill_start(b, carry):
        fill(b).start()
        return carry

    def fill_wait(b, carry):
        fill(b).wait()
        return carry

    def zero_padding():
        zeros[...] = jnp.zeros_like(zeros)
        n_used = ends_ref[N_EXPERTS - 1]
        for do in (fill_start, fill_wait):
            for e in range(N_EXPERTS):
                first = ends_ref[e - 1] if e else 0

                @pl.when(ends_ref[e] > first)
                def _():
                    do(ends_ref[e] - 1, 0)

            lax.fori_loop(n_used, n_blocks, do, 0)

    pl.when(i == 0)(zero_padding)

    def scatter(h_ref):
        def body(jj, carry):
            for j in (jj * DMA_UNROLL + u for u in range(DMA_UNROLL)):
                for k in range(TOP_K):
                    pltpu.make_async_copy(h_ref.at[_slab_rows(j)], xs_hbm.at[_slab_rows(dest_ref[j * TOP_K + k])],
                                          sem.at[0]).start(priority=k % 2)
            return carry

        lax.fori_loop(0, tm // DMA_UNROLL, body, 0)
        for k in range(TOP_K):
            pltpu.make_async_copy(h_ref, xs_hbm.at[pl.ds(0, tm * SLAB)], sem.at[0]).wait()

    start = 0
    for h_ref, n_tiles in zip(h_refs, tiles):
        pl.when((i >= start) & (i < start + n_tiles))(functools.partial(scatter, h_ref))
        start += n_tiles


def _dispatch(hs, dest_flat, ends, n_blocks, tm_blk):
    tm = 256
    tiles = [h.shape[0] // SLAB // tm for h in hs]
    starts = [sum(tiles[:s]) for s in range(len(hs))]
    src = lambda s: pl.BlockSpec((tm * SLAB, LANES),
                                 lambda i, ends: (jnp.clip(i - starts[s], 0, tiles[s] - 1), 0))
    grid_spec = pltpu.PrefetchScalarGridSpec(
        num_scalar_prefetch=1,
        grid=(sum(tiles),),
        in_specs=[pl.BlockSpec((tm * TOP_K,), lambda i, ends: (i,), memory_space=pltpu.SMEM)]
        + [src(s) for s in range(len(hs))],
        out_specs=pl.BlockSpec(memory_space=pl.ANY),
        scratch_shapes=[pltpu.VMEM((tm_blk * SLAB, LANES), F32), pltpu.SemaphoreType.DMA((2,))],
    )
    return pl.pallas_call(
        functools.partial(_dispatch_kernel, tm=tm, tm_blk=tm_blk, n_blocks=n_blocks, tiles=tiles),
        grid_spec=grid_spec,
        out_shape=jax.ShapeDtypeStruct((n_blocks * tm_blk * SLAB, LANES), F32),
        compiler_params=_params(1),
        name="dispatch",
    )(ends, dest_flat, *hs)


def _moe_kernel(be_ref, nu_ref, valid_ref, nxt_ref, prep_ref, slot_ref, x_ref, wgu_hbm, wd_hbm,
                bg_ref, bu_ref, bd_ref, o_ref, wgu_buf, wd_buf, wt, wg_t, wu_t, wd_bf, sem):
    i = pl.program_id(0)
    used = i < nu_ref[0]
    valid = valid_ref[i]
    s = slot_ref[i]
    tm = x_ref.shape[0] // SLAB
    _, d, d_gu = wgu_buf.shape
    d_ff = d_gu // 2
    new_expert = (i == 0) | (be_ref[i] != be_ref[jnp.maximum(i - 1, 0)])

    def weights(e, slot):
        return (pltpu.make_async_copy(wgu_hbm.at[e], wgu_buf.at[slot], sem.at[0, slot]),
                pltpu.make_async_copy(wd_hbm.at[e], wd_buf.at[slot], sem.at[1, slot]))

    def arrived(e, slot):
        for copy in weights(e, slot):
            copy.wait()

    def prepare(slot):
        for c in range(d // LANES):
            wt[...] = wgu_buf[slot, pl.ds(c * LANES, LANES), :].T
            rows = slice(c * LANES, (c + 1) * LANES)
            wg_t[slot, :, rows] = wt[pl.ds(0, d_ff, stride=2), :].astype(BF16)
            wu_t[slot, :, rows] = wt[pl.ds(1, d_ff, stride=2), :].astype(BF16)
        wd_bf[slot] = wd_buf[slot].astype(BF16)

    @pl.when(i == 0)
    def _():
        for copy in weights(be_ref[0], s):
            copy.start()

    @pl.when(used & new_expert & (nxt_ref[i] >= 0))
    def _():
        for copy in weights(nxt_ref[i], 1 - s):
            copy.start()

    @pl.when(i == 0)
    def _():
        arrived(be_ref[0], s)
        prepare(s)

    def experts_mlp(rows, prep_next):
        if prep_next:
            arrived(prep_ref[i], 1 - s)
        x = _slab_load(x_ref, rows).astype(BF16)
        acc = bd_ref[...]
        for c in range(d_ff // MOE_COLS):
            cols = slice(c * MOE_COLS, (c + 1) * MOE_COLS)
            gate = jnp.minimum(_dot_nt(x, wg_t[s, cols, :]) + bg_ref[:, cols], SWIGLU_LIMIT)
            up = jnp.clip(_dot_nt(x, wu_t[s, cols, :]) + bu_ref[:, cols], -SWIGLU_LIMIT, SWIGLU_LIMIT)
            act = gate * jax.nn.sigmoid(SWIGLU_ALPHA * gate) * (up + 1.0)
            acc = acc + _dot(act.astype(BF16), wd_bf[s, cols, :])
        _slab_store(o_ref, acc)
        if rows < tm:
            o_ref[rows * SLAB:, :] = jnp.zeros(((tm - rows) * SLAB, LANES), o_ref.dtype)
        if prep_next:
            prepare(1 - s)

    sizes = [tm // 4, tm // 2, tm]
    prep = used & (prep_ref[i] >= 0)
    full = valid > sizes[-2]
    @pl.when(prep & ~full)
    def _():
        arrived(prep_ref[i], 1 - s)
        prepare(1 - s)
    pl.when(prep & full)(functools.partial(experts_mlp, tm, True))
    for lo, rows in zip([0] + sizes[:-1], sizes):
        fits = used & (valid > lo) & (valid <= rows)
        pl.when(fits & ~(prep & full))(functools.partial(experts_mlp, rows, False))

    @pl.when(~used)
    def _():
        o_ref[...] = jnp.zeros_like(o_ref)


def _moe_plan(ends, counts, n_blocks, tm):
    blk = jnp.arange(n_blocks, dtype=jnp.int32)
    e_ids = jnp.arange(N_EXPERTS, dtype=jnp.int32)
    block_e = jnp.minimum(jnp.sum(blk[:, None] >= ends[None, :], axis=1), N_EXPERTS - 1).astype(jnp.int32)
    mine = block_e[:, None] == e_ids[None, :]
    pick = lambda per_e: jnp.sum(jnp.where(mine, per_e[None, :], 0), axis=1)
    first_blk = ends - (counts + tm - 1) // tm
    valid = jnp.clip(pick(counts) - (blk - pick(first_blk)) * tm, 0, tm)
    valid = jnp.where(blk < ends[N_EXPERTS - 1], valid, 0).astype(jnp.int32)
    used = counts > 0
    slot_e = (jnp.cumsum(used.astype(jnp.int32)) - 1) % 2
    later = (e_ids[None, :] > e_ids[:, None]) & used[None, :]
    next_e = jnp.min(jnp.where(later, e_ids[None, :], N_EXPERTS), axis=1)
    next_e = jnp.where(next_e < N_EXPERTS, next_e, -1)
    in_use = valid > 0
    nxt = jnp.where((blk == pick(first_blk)) & in_use, pick(next_e), -1).astype(jnp.int32)
    n_blk_e = (counts + tm - 1) // tm
    prep_blk = pick(first_blk) + jnp.minimum(1, pick(n_blk_e) - 1)
    prep = jnp.where((blk == prep_blk) & in_use, pick(next_e), -1).astype(jnp.int32)
    return block_e, ends[N_EXPERTS - 1:], valid, nxt, prep, pick(slot_e).astype(jnp.int32)


def _moe_blocks(xs, block_e, n_used, valid, nxt, prep, slot, w_gu, w_down, b_gu, b_down, tm):
    n_blocks = block_e.shape[0]
    n_e, d, d_gu = w_gu.shape
    d_ff = d_gu // 2
    bias = lambda n: pl.BlockSpec((None, 1, n), lambda i, be, *_: (be[i], 0, 0))
    grid_spec = pltpu.PrefetchScalarGridSpec(
        num_scalar_prefetch=6,
        grid=(n_blocks,),
        in_specs=[pl.BlockSpec((tm * SLAB, LANES), lambda i, be, nu, *_: (jnp.minimum(i, nu[0] - 1), 0)),
                  pl.BlockSpec(memory_space=pl.ANY), pl.BlockSpec(memory_space=pl.ANY),
                  bias(d_ff), bias(d_ff), bias(d)],
        out_specs=pl.BlockSpec((tm * SLAB, LANES), lambda i, *_: (i, 0)),
        scratch_shapes=[pltpu.VMEM((2, d, d_gu), F32), pltpu.VMEM((2, d_ff, d), F32),
                        pltpu.VMEM((d_gu, LANES), F32), pltpu.VMEM((2, d_ff, d), BF16),
                        pltpu.VMEM((2, d_ff, d), BF16), pltpu.VMEM((2, d_ff, d), BF16),
                        pltpu.SemaphoreType.DMA((2, 2))],
    )
    return pl.pallas_call(
        _moe_kernel,
        grid_spec=grid_spec,
        out_shape=jax.ShapeDtypeStruct((n_blocks * tm * SLAB, LANES), F32),
        compiler_params=_params(1, VMEM_LIMIT),
        name="moe_blocks",
    )(block_e, n_used, valid, nxt, prep, slot, xs, w_gu, w_down, b_gu[:, None, 0::2], b_gu[:, None, 1::2],
      b_down.reshape(n_e, 1, d))


def _fin_kernel(dest_ref, next_ref, rows_hbm, top_ref, x1_ref, gt_ref, g_ref, o_ref, buf, sem, *, tm):
    i = pl.program_id(0)
    slot = i % 2

    def gather(idx_ref, s):
        def body(jj, carry):
            for j in (jj * DMA_UNROLL + u for u in range(DMA_UNROLL)):
                for k in range(TOP_K):
                    pltpu.make_async_copy(rows_hbm.at[_slab_rows(idx_ref[j * TOP_K + k])],
                                          buf.at[s, k, _slab_rows(j)], sem.at[s]).start(priority=k % 2)
            return carry
        lax.fori_loop(0, tm // DMA_UNROLL, body, 0)

    @pl.when(i == 0)
    def _():
        gather(dest_ref, 0)

    @pl.when(i + 1 < pl.num_programs(0))
    def _():
        gather(next_ref, 1 - slot)

    for k in range(TOP_K):
        pltpu.make_async_copy(rows_hbm.at[pl.ds(0, tm * SLAB)], buf.at[slot, k], sem.at[slot]).wait()
    top = top_ref[...]
    f = sum(_slab_load(buf.at[slot, k], tm) * top[:, TOP_K + k:TOP_K + k + 1] for k in range(TOP_K))
    o_ref[...] = x1_ref[...] + gt_ref[...] * _rms(f, g_ref[...])


def _finish(dest, rows, top, x1, mod, g_post, per_row, rows_per_batch):
    n = x1.shape[0]
    tm = min(TM_FIN, n)
    full = pl.BlockSpec((tm, D_MODEL), lambda i: (i, 0))
    gate = _mod_spec(tm, MOD_GATE_FFN, per_row, 1 if per_row else rows_per_batch // tm)
    return pl.pallas_call(
        functools.partial(_fin_kernel, tm=tm),
        grid=(n // tm,),
        in_specs=[pl.BlockSpec((tm * TOP_K,), lambda i: (i,), memory_space=pltpu.SMEM),
                  pl.BlockSpec((tm * TOP_K,), lambda i: (jnp.minimum(i + 1, n // tm - 1),),
                               memory_space=pltpu.SMEM),
                  pl.BlockSpec(memory_space=pl.ANY),
                  pl.BlockSpec((tm, LANES), lambda i: (i, 0)), full, gate,
                  pl.BlockSpec((1, D_MODEL), lambda i: (0, 0))],
        out_specs=full,
        out_shape=jax.ShapeDtypeStruct((n, D_MODEL), F32),
        scratch_shapes=[pltpu.VMEM((2, TOP_K, tm * SLAB, LANES), F32), pltpu.SemaphoreType.DMA((2,))],
        compiler_params=_params(1),
        name="finish",
    )(dest, dest, rows, top, x1, mod, g_post)


def kernel(x_prompt, x_sample, state_ret, cache_win_k, cache_win_v, c_prompt, c_sample, w_ada, b_ada,
           g_pre_mix, g_post_mix, g_pre_ffn, g_post_ffn, w_in, g_ret, w_out, w_router, b_router,
           w_gate_up, b_gate_up, w_down, b_down):
    depth = w_in.shape[0]
    assert depth == 1, "single-layer step"
    n_b, seq, d = x_prompt.shape
    n_db, t_new, _ = x_sample.shape
    n_p, n_s = n_b * seq, n_db * t_new
    buf = cache_win_k.shape[2]
    keep = min(DIL_PATTERNS[-1][0], seq)
    vec = lambda g: g[0].reshape(1, -1)

    mod_p, mod_s = _ada(c_prompt, c_sample, w_ada[0], b_ada[0], t_new)
    mod_p = mod_p[:n_b].reshape(n_b, 1, -1)
    mod_s = mod_s.reshape(n_s, -1)

    w_in_bf = w_in[0].astype(BF16)
    w_out_bf = w_out[0].astype(BF16)
    xp = x_prompt.reshape(n_p, d)
    xs = x_sample.reshape(n_s, d)
    router = (w_router[0], b_router[0].reshape(1, -1))

    qr, kr, vr, gr, qa_pm, ka_pm, va_pm, k_win, v_win = _in_proj(xp, mod_p, vec(g_pre_mix), w_in_bf, False, seq, keep)
    ret_y_p, ret_state_p = _retention(qr, kr, vr, gr, None, g_ret[0], n_b, seq, RET_CHUNK, n_b)
    att_p = _att_prompt(qa_pm, ka_pm, va_pm, n_b, seq)
    x1_p, h_p, top_p = _mixer_out(ret_y_p, att_p, xp, mod_p, w_out_bf, vec(g_post_mix), vec(g_pre_ffn),
                                  *router, False, seq)

    qr_s, kr_s, vr_s, gr_s, qa_s, ka_s, va_s = _in_proj(xs, mod_s, vec(g_pre_mix), w_in_bf, True, t_new)
    ret_y_s, ret_state_s = _retention(qr_s, kr_s, vr_s, gr_s, state_ret[0], g_ret[0], n_db, t_new, t_new,
                                      RET_GROUP_SAMPLE)
    att_s = _att_sample(qa_s, ka_s, va_s, cache_win_k[0], cache_win_v[0], n_db, t_new)
    x1_s, h_s, top_s = _mixer_out(ret_y_s, att_s, xs, mod_s, w_out_bf, vec(g_post_mix), vec(g_pre_ffn),
                                  *router, True, t_new)

    n_blocks = (n_p + n_s) * TOP_K // TM_MOE + N_EXPERTS
    dest, ends, counts = _route(jnp.concatenate([top_p, top_s], axis=0), TM_MOE)
    dest = dest[:, :TOP_K].reshape(-1)
    ends, counts = ends[0, :N_EXPERTS], counts[0, :N_EXPERTS]
    rows_in = _dispatch([h_p, h_s], dest, ends, n_blocks, TM_MOE)
    rows = _moe_blocks(rows_in, *_moe_plan(ends, counts, n_blocks, TM_MOE), w_gate_up[0], w_down[0],
                       b_gate_up[0], b_down[0], TM_MOE)
    y_p = _finish(dest[:n_p * TOP_K], rows, top_p, x1_p, mod_p, vec(g_post_ffn), False, seq)
    y_s = _finish(dest[n_p * TOP_K:], rows, top_s, x1_s, mod_s, vec(g_post_ffn), True, t_new)

    win = lambda t: t.reshape(n_b, N_HEADS, HEAD_DIM, keep).transpose(0, 3, 1, 2)[None]
    new = lambda t: t.reshape(n_db, t_new, N_HEADS, HEAD_DIM)[None]
    return (y_p.reshape(n_b, seq, d), y_s.reshape(n_db, t_new, d), ret_state_p[None], ret_state_s[None],
            win(k_win), win(v_win), new(ka_s), new(va_s))
```

```python
import functools

import numpy as np
import jax
import jax.numpy as jnp
from jax import lax
from jax.experimental import pallas as pl
from jax.experimental.pallas import tpu as pltpu

F32 = jnp.float32
BF16 = jnp.bfloat16

D_MODEL = 1024
HEAD_DIM = 64
N_HEADS = 8
D_HALF = N_HEADS * HEAD_DIM
N_SEG = 7
LANES = 128
N_PAIRS = D_HALF // LANES
RET_CHUNK = 128
RET_GROUP_SAMPLE = 8
ATT_BLOCK = 128
ATT_Q = 1024
ATT_BACK = 2
ATT_UNROLL = 2
DIL_PATTERNS = ((128, 1), (512, 4), (2048, 16))
N_EXPERTS = 32
TOP_K = 4
SWIGLU_LIMIT = 7.0
SWIGLU_ALPHA = 1.702
EPS = 1e-6
NEG_INF = -1e30
MOD_SHIFT_MIX, MOD_SCALE_MIX, MOD_GATE_MIX, MOD_SHIFT_FFN, MOD_SCALE_FFN, MOD_GATE_FFN = range(6)

TM_IN = 512
TM_OUT = 512
TM_MOE = 512
MOE_COLS = 512
TM_FIN = 128
DMA_UNROLL = 4
VMEM_LIMIT = 56 * 1024 * 1024


def _params(n_axes, vmem=None):
    return pltpu.CompilerParams(dimension_semantics=("arbitrary",) * n_axes, vmem_limit_bytes=vmem)


def _ret_log_decay():
    return np.log(1.0 - 2.0 ** (-5.0 - np.arange(N_HEADS, dtype=np.float64)))


def _alibi_slopes():
    return 2.0 ** (-8.0 * (np.arange(N_HEADS, dtype=np.float64) + 1.0) / N_HEADS)


def _per_head_lanes(a):
    return np.repeat(a, HEAD_DIM, axis=-1)


def _rms(x, g):
    return x * lax.rsqrt(jnp.mean(x * x, axis=-1, keepdims=True) + EPS) * g


def _dot(a, b):
    return jnp.dot(a, b, preferred_element_type=F32)


def _dot_nt(a, b):
    return lax.dot_general(a, b, (((1,), (1,)), ((), ())), preferred_element_type=F32)


def _dot_tn(a, b):
    return lax.dot_general(a, b, (((0,), (0,)), ((), ())), preferred_element_type=F32)


SLAB = D_MODEL // LANES


def _slab_rows(r):
    return pl.ds(pl.multiple_of(r * SLAB, SLAB), SLAB)


def _slab_load(ref, n):
    return jnp.concatenate([ref[pl.ds(s, n, stride=SLAB), :] for s in range(SLAB)], axis=1)


def _slab_store(ref, value):
    n = value.shape[0]
    for s in range(SLAB):
        ref[pl.ds(s, n, stride=SLAB), :] = value[:, s * LANES:(s + 1) * LANES]


def _mod_spec(tm, col, per_row, tiles_per_batch):
    if per_row:
        return pl.BlockSpec((tm, D_MODEL), lambda i: (i, col))
    return pl.BlockSpec((None, 1, D_MODEL), lambda i: (i // tiles_per_batch, 0, col))


def _ada_kernel(cp_ref, cs_ref, w_ref, b_ref, op_ref, os_ref):
    w = w_ref[...].astype(BF16)

    def mod(c):
        return _dot((c * jax.nn.sigmoid(c)).astype(BF16), w) + b_ref[...]

    op_ref[...] = mod(cp_ref[...])
    ms = mod(cs_ref[...])
    os_ref[...] = jnp.broadcast_to(ms[:, None, :], os_ref.shape)


def _ada(c_prompt, c_sample, w_ada, b_ada, t_new):
    n_b, d = c_prompt.shape
    n_db = c_sample.shape[0]
    n_out = w_ada.shape[1]
    cp = jnp.concatenate([c_prompt, jnp.zeros((-n_b % 8, d), F32)], axis=0)
    return pl.pallas_call(
        _ada_kernel,
        grid=(n_out // d,),
        in_specs=[pl.BlockSpec(cp.shape, lambda j: (0, 0)),
                  pl.BlockSpec((n_db, d), lambda j: (0, 0)),
                  pl.BlockSpec((d, d), lambda j: (0, j)),
                  pl.BlockSpec((1, d), lambda j: (0, j))],
        out_specs=[pl.BlockSpec(cp.shape, lambda j: (0, j)),
                   pl.BlockSpec((n_db, t_new, d), lambda j: (0, 0, j))],
        out_shape=[jax.ShapeDtypeStruct((cp.shape[0], n_out), F32),
                   jax.ShapeDtypeStruct((n_db, t_new, n_out), F32)],
        compiler_params=_params(1, VMEM_LIMIT),
        name="ada",
    )(cp, c_sample, w_ada, b_ada.reshape(1, n_out))


def _in_kernel(x_ref, sc_ref, sh_ref, g_ref, w_ref, *o_refs, n_plain, window):
    h = (_rms(x_ref[...], g_ref[...]) * (1.0 + sc_ref[...]) + sh_ref[...]).astype(BF16)
    segs = [_dot(h, w_ref[:, s * D_HALF:(s + 1) * D_HALF]) for s in range(N_SEG)]
    for o_ref, seg in zip(o_refs[:n_plain], segs):
        o_ref[...] = seg
    if n_plain == N_SEG:
        return
    for o_ref, seg in zip(o_refs[n_plain:N_SEG], segs[n_plain:]):
        for p in range(N_PAIRS):
            o_ref[p] = seg[:, p * LANES:(p + 1) * LANES]
    tiles_per_seq, first = window
    @pl.when(pl.program_id(0) % tiles_per_seq >= first)
    def _():
        for o_ref, seg in zip(o_refs[N_SEG:], segs[N_SEG - 2:]):
            o_ref[...] = seg.T


def _in_proj(x, mod, g, w_bf, per_row, rows_per_batch, keep=None):
    n = x.shape[0]
    tm = min(TM_IN, n)
    tpb = 1 if per_row else rows_per_batch // tm
    seg = pl.BlockSpec((tm, D_HALF), lambda i: (i, 0))
    seg_shape = jax.ShapeDtypeStruct((n, D_HALF), F32)
    if keep is None:
        n_plain, window = N_SEG, None
        out_specs, out_shape = [seg] * N_SEG, [seg_shape] * N_SEG
    else:
        n_plain, window = N_SEG - 3, (tpb, (rows_per_batch - keep) // tm)
        pm = pl.BlockSpec((N_PAIRS, tm, LANES), lambda i: (0, i, 0))
        win = pl.BlockSpec((None, D_HALF, tm), lambda i: (i // tpb, 0, jnp.maximum(i % tpb - window[1], 0)))
        out_specs = [seg] * n_plain + [pm] * 3 + [win] * 2
        out_shape = ([seg_shape] * n_plain + [jax.ShapeDtypeStruct((N_PAIRS, n, LANES), F32)] * 3
                     + [jax.ShapeDtypeStruct((n // rows_per_batch, D_HALF, keep), F32)] * 2)
    return pl.pallas_call(
        functools.partial(_in_kernel, n_plain=n_plain, window=window),
        grid=(n // tm,),
        in_specs=[pl.BlockSpec((tm, D_MODEL), lambda i: (i, 0)),
                  _mod_spec(tm, MOD_SCALE_MIX, per_row, tpb), _mod_spec(tm, MOD_SHIFT_MIX, per_row, tpb),
                  pl.BlockSpec((1, D_MODEL), lambda i: (0, 0)),
                  pl.BlockSpec((D_MODEL, N_SEG * D_HALF), lambda i: (0, 0))],
        out_specs=out_specs,
        out_shape=out_shape,
        compiler_params=_params(1, VMEM_LIMIT),
        name="in_proj",
    )(x, mod, mod, g, w_bf)


def _ret_tables(chunk):
    lg = _ret_log_decay()
    pos = np.arange(chunk, dtype=np.float64)
    diff = pos[:, None] - pos[None, :]
    dmask = np.where(diff >= 0, np.exp(np.maximum(diff, 0.0)[None] * lg[:, None, None]), 0.0)
    dmask = dmask.reshape(N_PAIRS, 2 * chunk, chunk)
    row_decay = _per_head_lanes(np.exp((pos[:, None] + 1.0) * lg[None, :]))
    key_decay = _per_head_lanes(np.exp((chunk - 1.0 - pos)[:, None] * lg[None, :]))
    state_decay = _per_head_lanes(np.exp(chunk * lg)[None, :])
    return tuple(jnp.asarray(t, F32) for t in (dmask, row_decay, key_decay, state_decay))


def _ret_kernel(q_ref, k_ref, v_ref, gate_ref, s0_ref, gret_ref, dm_ref, rd_ref, kd_ref, sd_ref,
                y_ref, st_ref, state, *, chunk, group, has_init):
    c = pl.program_id(1)
    lo = lax.broadcasted_iota(jnp.int32, (chunk, LANES), 1) < HEAD_DIM
    blk_r = lax.broadcasted_iota(jnp.int32, (LANES, LANES), 0) < HEAD_DIM
    blk_c = lax.broadcasted_iota(jnp.int32, (LANES, LANES), 1) < HEAD_DIM
    same_head = blk_r == blk_c

    @pl.when(c == 0)
    def _():
        if has_init:
            zero = jnp.zeros((HEAD_DIM, HEAD_DIM), F32)
            for b in range(group):
                for p in range(N_PAIRS):
                    top = jnp.concatenate([s0_ref[b, 2 * p], zero], axis=1)
                    bot = jnp.concatenate([zero, s0_ref[b, 2 * p + 1]], axis=1)
                    state[b * N_PAIRS + p] = jnp.concatenate([top, bot], axis=0)
        else:
            state[...] = jnp.zeros_like(state)

    new_states = []
    for b, p in [(b, p) for b in range(group) for p in range(N_PAIRS)]:
        sl = slice(p * LANES, (p + 1) * LANES)
        q = q_ref[b, :, sl]
        k = k_ref[b, :, sl] * HEAD_DIM ** -0.5
        vb = v_ref[b, :, sl].astype(BF16)
        kb = k.astype(BF16)
        q2 = jnp.concatenate([jnp.where(lo, q, 0.0), jnp.where(lo, 0.0, q)], axis=0).astype(BF16)
        inner = _dot((_dot_nt(q2, kb) * dm_ref[p]).astype(BF16), vb)
        s_old = state[b * N_PAIRS + p]
        cross = _dot(q.astype(BF16), s_old.astype(BF16)) * rd_ref[:, sl]
        o = jnp.where(lo, inner[:chunk], inner[chunk:]) + cross
        upd = _dot_tn((k * kd_ref[:, sl]).astype(BF16), vb)
        s_new = sd_ref[:, sl] * s_old + jnp.where(same_head, upd, 0.0)
        state[b * N_PAIRS + p] = s_new
        new_states.append(s_new)

        def head_mean(t):
            m_lo = jnp.sum(jnp.where(lo, t, 0.0), axis=-1, keepdims=True)
            m_hi = jnp.sum(jnp.where(lo, 0.0, t), axis=-1, keepdims=True)
            return jnp.where(lo, m_lo, m_hi) * (1.0 / HEAD_DIM)

        d = o - head_mean(o)
        n = d * lax.rsqrt(head_mean(d * d) + EPS) * gret_ref[:, sl]
        g = gate_ref[b, :, sl]
        y_ref[b, :, sl] = (g * jax.nn.sigmoid(g) * n).astype(y_ref.dtype)

    @pl.when(c == pl.num_programs(1) - 1)
    def _():
        for bp, s_new in enumerate(new_states):
            b, p = divmod(bp, N_PAIRS)
            st_ref[b, 2 * p] = s_new[:HEAD_DIM, :HEAD_DIM]
            st_ref[b, 2 * p + 1] = s_new[HEAD_DIM:, HEAD_DIM:]


def _retention(q, k, v, gate, state0, g_ret, n_batch, seq, chunk, group):
    nc = seq // chunk
    by_seq = lambda t: t.reshape(n_batch, seq, D_HALF)
    has_init = state0 is not None
    if state0 is None:
        state0 = jnp.zeros((group, N_HEADS, HEAD_DIM, HEAD_DIM), F32)
        s0_map = lambda b, c: (0, 0, 0, 0)
    else:
        s0_map = lambda b, c: (b, 0, 0, 0)
    dmask, row_decay, key_decay, state_decay = _ret_tables(chunk)
    rows = pl.BlockSpec((group, chunk, D_HALF), lambda b, c: (b, c, 0))
    const2 = lambda shape: pl.BlockSpec(shape, lambda b, c: (0, 0))
    st_block = pl.BlockSpec((group, N_HEADS, HEAD_DIM, HEAD_DIM), lambda b, c: (b, 0, 0, 0))
    y, state = pl.pallas_call(
        functools.partial(_ret_kernel, chunk=chunk, group=group, has_init=has_init),
        grid=(n_batch // group, nc),
        in_specs=[rows, rows, rows, rows,
                  pl.BlockSpec((group, N_HEADS, HEAD_DIM, HEAD_DIM), s0_map),
                  const2((1, D_HALF)),
                  pl.BlockSpec((N_PAIRS, 2 * chunk, chunk), lambda b, c: (0, 0, 0)),
                  const2((chunk, D_HALF)), const2((chunk, D_HALF)), const2((1, D_HALF))],
        out_specs=[rows, st_block],
        out_shape=[jax.ShapeDtypeStruct((n_batch, seq, D_HALF), BF16),
                   jax.ShapeDtypeStruct((n_batch, N_HEADS, HEAD_DIM, HEAD_DIM), F32)],
        scratch_shapes=[pltpu.VMEM((group * N_PAIRS, LANES, LANES), F32)],
        compiler_params=_params(2),
        name="retention",
    )(by_seq(q), by_seq(k), by_seq(v), by_seq(gate), state0, g_ret.reshape(1, D_HALF), dmask, row_decay,
      key_decay, state_decay)
    return y.reshape(n_batch * seq, D_HALF), state


def _band_table(n_q, n_k, dilation):
    step = np.arange(n_q)[:, None] + (n_k - n_q) - np.arange(n_k)[None, :]
    in_win = (step >= 0) & (step <= ATT_BLOCK)
    bias = -_alibi_slopes()[:, None, None] * (step * dilation).astype(np.float64)[None]
    return jnp.asarray(np.where(in_win[None], bias, NEG_INF).reshape(N_PAIRS, 2 * n_q, n_k), F32)


def _band_attention(q, k, v, table_ref, key_ok, pair):
    n_q = q.shape[0]
    lo = lax.broadcasted_iota(jnp.int32, q.shape, 1) < HEAD_DIM
    q = q * HEAD_DIM ** -0.5
    q2 = jnp.concatenate([jnp.where(lo, q, 0.0), jnp.where(lo, 0.0, q)], axis=0).astype(BF16)
    s = _dot_nt(q2, k.astype(BF16)) + table_ref[pair]
    s = jnp.where(key_ok, s, NEG_INF)
    m = jnp.max(s, axis=-1, keepdims=True)
    e = jnp.exp(s - m)
    l = jnp.sum(e, axis=-1, keepdims=True)
    o = _dot(e.astype(BF16), v.astype(BF16))
    return tuple(jnp.where(lo, t[:n_q], t[n_q:]) for t in (o, m, l))


def _attp_kernel(*refs):
    n_back = ATT_BACK
    q_ref = refs[0]
    k_refs = refs[1:2 + n_back]
    v_refs = refs[2 + n_back:3 + 2 * n_back]
    t_refs = refs[3 + 2 * n_back:3 + 2 * n_back + len(DIL_PATTERNS)]
    o_ref, parts, kcat, vcat = refs[3 + 2 * n_back + len(DIL_PATTERNS):]
    n = pl.program_id(1)

    def key_col(n_q, n_k):
        return lax.broadcasted_iota(jnp.int32, (2 * n_q, n_k), 1)

    def fold(first, rows, news):
        if not first:
            olds = [[parts[kind, p, rows, :] for kind in range(3)] for p in range(N_PAIRS)]
            merged = []
            for (o_old, m_old, l_old), (o_new, m_new, l_new) in zip(olds, news):
                m = jnp.maximum(m_old, m_new)
                a, b = jnp.exp(m_old - m), jnp.exp(m_new - m)
                merged.append((o_old * a + o_new * b, m, l_old * a + l_new * b))
            news = merged
        for p, new in enumerate(news):
            for kind, val in enumerate(new):
                parts[kind, p, rows, :] = val

    for pat, (window, dil) in enumerate(DIL_PATTERNS):
        t_ref = t_refs[pat]
        if dil == 1:
            for cat, src in ((kcat, k_refs), (vcat, v_refs)):
                cat[:, :ATT_BLOCK, :] = src[1][:, ATT_Q - ATT_BLOCK:, :]
                cat[:, ATT_BLOCK:, :] = src[0][...]

            def body(it, carry, t_ref=t_ref, first=pat == 0):
                for u in range(ATT_UNROLL):
                    sub = it * ATT_UNROLL + u
                    r0 = pl.multiple_of(sub * ATT_BLOCK, ATT_BLOCK)
                    ok = (key_col(ATT_BLOCK, 2 * ATT_BLOCK) >= ATT_BLOCK) | (n > 0) | (sub > 0)
                    fold(first, pl.ds(r0, ATT_BLOCK),
                         [_band_attention(q_ref[p, pl.ds(r0, ATT_BLOCK), :], kcat[p, pl.ds(r0, 2 * ATT_BLOCK), :],
                                          vcat[p, pl.ds(r0, 2 * ATT_BLOCK), :], t_ref, ok, p)
                          for p in range(N_PAIRS)])
                return carry

            lax.fori_loop(0, ATT_Q // ATT_BLOCK // ATT_UNROLL, body, 0)
            continue
        sub_rows = min(window, ATT_Q)
        n_q = sub_rows // dil
        n_prev = window // sub_rows
        for s in range(ATT_Q // sub_rows):
            pieces = []
            for i in range(n_prev, -1, -1):
                rel = (s - i) * sub_rows
                j = -(rel // ATT_Q)
                pieces.append((j, rel + j * ATT_Q))
            missing = sum(jnp.where(n < j, n_q, 0) for j, _ in pieces)

            unroll = ATT_UNROLL if n_q == ATT_BLOCK else 1

            def body(it, carry, dil=dil, n_q=n_q, t_ref=t_ref, first=pat == 0, pieces=pieces, missing=missing,
                     q_off=s * sub_rows, unroll=unroll):
                ok = key_col(n_q, len(pieces) * n_q) >= missing
                for u in range(unroll):
                    r = it * unroll + u
                    rows = pl.ds(q_off + r, n_q, stride=dil)
                    gather = lambda refs, p: jnp.concatenate(
                        [refs[j][p, pl.ds(off + r, n_q, stride=dil), :] for j, off in pieces], axis=0)
                    fold(first, rows, [_band_attention(q_ref[p, rows, :], gather(k_refs, p), gather(v_refs, p),
                                                       t_ref, ok, p) for p in range(N_PAIRS)])
                return carry

            lax.fori_loop(0, dil // unroll, body, 0)

    for p in range(N_PAIRS):
        o_ref[:, p * LANES:(p + 1) * LANES] = (parts[0, p] / parts[2, p]).astype(o_ref.dtype)


def _att_prompt(q, k, v, n_batch, seq):
    nb = seq // ATT_Q
    assert all(w // d == ATT_BLOCK and ATT_Q % min(w, ATT_Q) == 0 and w % min(w, ATT_Q) == 0
               for w, d in DIL_PATTERNS)
    assert ATT_BACK == DIL_PATTERNS[-1][0] // ATT_Q and DIL_PATTERNS[0] == (ATT_BLOCK, 1)
    blk = lambda j: pl.BlockSpec((N_PAIRS, ATT_Q, LANES),
                                 lambda b, i: (0, b * nb + jnp.maximum(i - j, 0), 0))
    tables = [_band_table(ATT_BLOCK, 2 * ATT_BLOCK, 1)] + [
        _band_table(min(w, ATT_Q) // d, (w // min(w, ATT_Q) + 1) * (min(w, ATT_Q) // d), d)
        for w, d in DIL_PATTERNS[1:]]
    back = [blk(j) for j in range(ATT_BACK + 1)]
    return pl.pallas_call(
        _attp_kernel,
        grid=(n_batch, nb),
        in_specs=[blk(0)] + back + back + [pl.BlockSpec(t.shape, lambda b, i: (0, 0, 0)) for t in tables],
        out_specs=pl.BlockSpec((ATT_Q, D_HALF), lambda b, i: (b * nb + i, 0)),
        out_shape=jax.ShapeDtypeStruct((n_batch * seq, D_HALF), BF16),
        scratch_shapes=[pltpu.VMEM((3, N_PAIRS, ATT_Q, LANES), F32),
                        pltpu.VMEM((N_PAIRS, ATT_Q + ATT_BLOCK, LANES), F32),
                        pltpu.VMEM((N_PAIRS, ATT_Q + ATT_BLOCK, LANES), F32)],
        compiler_params=_params(2, VMEM_LIMIT),
        name="att_prompt",
    )(q, *([k] * (ATT_BACK + 1)), *([v] * (ATT_BACK + 1)), *tables)


def _atts_tables(buf, t_new):
    pos_q = buf + np.arange(t_new)
    pos_k = np.concatenate([np.arange(buf), buf + np.arange(LANES)])
    dist = pos_q[:, None] - pos_k[None, :]
    count = np.zeros(dist.shape)
    for window, dilation in DIL_PATTERNS:
        count += (dist >= 0) & (dist <= window) & (dist % dilation == 0)
    count[:, buf + t_new:] = 0
    bias = -_alibi_slopes()[:, None, None] * dist[None].astype(np.float64)
    table = np.where(count[None] > 0, bias + np.log(np.maximum(count, 1.0))[None], NEG_INF)
    table = table.reshape(N_HEADS * t_new, buf + LANES)
    return jnp.asarray(table[:, :buf], F32), jnp.asarray(table[:, buf:], F32)


def _atts_kernel(q_ref, kt_ref, kn_ref, vt_ref, vn_ref, tc_ref, tn_ref, o_ref, *, t_new):
    rows = N_HEADS * t_new
    row_head = lax.broadcasted_iota(jnp.int32, (rows, D_HALF), 0) // t_new
    col_head = lax.broadcasted_iota(jnp.int32, (rows, D_HALF), 1) // HEAD_DIM
    own = row_head == col_head
    q = q_ref[...] * HEAD_DIM ** -0.5
    q_heads = jnp.where(own, jnp.concatenate([q] * N_HEADS, axis=0), 0.0).astype(BF16)
    pad = jnp.zeros((LANES - t_new, D_HALF), F32)
    kn = jnp.concatenate([kn_ref[...], pad], axis=0).astype(BF16)
    vn = jnp.concatenate([vn_ref[...], pad], axis=0).astype(BF16)
    sc = _dot(q_heads, kt_ref[...].astype(BF16)) + tc_ref[...]
    sn = _dot_nt(q_heads, kn) + tn_ref[...]
    m = jnp.maximum(jnp.max(sc, axis=-1, keepdims=True), jnp.max(sn, axis=-1, keepdims=True))
    ec = jnp.exp(sc - m)
    en = jnp.exp(sn - m)
    l = jnp.sum(ec, axis=-1, keepdims=True) + jnp.sum(en, axis=-1, keepdims=True)
    acc = _dot_nt(ec.astype(BF16), vt_ref[...].astype(BF16)) + _dot(en.astype(BF16), vn)
    acc = jnp.where(own, acc, 0.0) / l
    out = acc[:t_new]
    for h in range(1, N_HEADS):
        out = out + acc[h * t_new:(h + 1) * t_new]
    o_ref[...] = out.astype(o_ref.dtype)


def _att_sample(q, k_new, v_new, cache_k, cache_v, n_batch, t_new):
    buf = cache_k.shape[1]
    tc, tn = _atts_tables(buf, t_new)
    by_pos = lambda c: c.transpose(0, 2, 3, 1).reshape(n_batch, D_HALF, buf)
    rows = pl.BlockSpec((t_new, D_HALF), lambda b: (b, 0))
    cache = pl.BlockSpec((None, D_HALF, buf), lambda b: (b, 0, 0))
    return pl.pallas_call(
        functools.partial(_atts_kernel, t_new=t_new),
        grid=(n_batch,),
        in_specs=[rows, cache, rows, cache, rows,
                  pl.BlockSpec((N_HEADS * t_new, buf), lambda b: (0, 0)),
                  pl.BlockSpec((N_HEADS * t_new, LANES), lambda b: (0, 0))],
        out_specs=rows,
        out_shape=jax.ShapeDtypeStruct((n_batch * t_new, D_HALF), BF16),
        compiler_params=_params(1, VMEM_LIMIT),
        name="att_sample",
    )(q, by_pos(cache_k), k_new, by_pos(cache_v), v_new, tc, tn)


def _out_kernel(ret_ref, att_ref, x_ref, gt_ref, sc_ref, sh_ref, w_ref, gpost_ref, gpre_ref, wr_ref, br_ref,
                x1_ref, h_ref, top_ref):
    tm = x_ref.shape[0]
    mix = _dot(ret_ref[...], w_ref[:D_HALF, :]) + _dot(att_ref[...], w_ref[D_HALF:, :])
    x1 = x_ref[...] + gt_ref[...] * _rms(mix, gpost_ref[...])
    x1_ref[...] = x1
    h = _rms(x1, gpre_ref[...]) * (1.0 + sc_ref[...]) + sh_ref[...]
    _slab_store(h_ref, h)
    split = lambda t: (t.astype(BF16), (t - t.astype(BF16).astype(F32)).astype(BF16))
    (h_hi, h_lo), (w_hi, w_lo) = split(h), split(wr_ref[...])
    logits = _dot(h_hi, w_hi) + _dot(h_lo, w_hi) + _dot(h_hi, w_lo) + br_ref[...]
    lane_e = lax.broadcasted_iota(jnp.int32, (tm, N_EXPERTS), 1).astype(F32)
    lane_o = lax.broadcasted_iota(jnp.int32, (tm, LANES), 1)
    vals, idxs = [], []
    work = logits
    for _ in range(TOP_K):
        v = jnp.max(work, axis=-1, keepdims=True)
        i = jnp.min(jnp.where(work == v, lane_e, float(N_EXPERTS)), axis=-1, keepdims=True)
        vals.append(v)
        idxs.append(i)
        work = jnp.where(lane_e == i, -jnp.inf, work)
    es = [jnp.exp(v - vals[0]) for v in vals]
    tot = sum(es)
    packed = jnp.zeros((tm, LANES), F32)
    for j in range(TOP_K):
        packed = jnp.where(lane_o == j, idxs[j], packed)
        packed = jnp.where(lane_o == TOP_K + j, es[j] / tot, packed)
    top_ref[...] = packed


def _mixer_out(ret_y, att, x, mod, w_out_bf, g_post, g_pre, w_router, b_router, per_row, rows_per_batch):
    n = x.shape[0]
    tm = min(TM_OUT, n)
    tpb = 1 if per_row else rows_per_batch // tm
    half = pl.BlockSpec((tm, D_HALF), lambda i: (i, 0))
    full = pl.BlockSpec((tm, D_MODEL), lambda i: (i, 0))
    stat = pl.BlockSpec((tm, LANES), lambda i: (i, 0))
    vec = pl.BlockSpec((1, D_MODEL), lambda i: (0, 0))
    mods = [_mod_spec(tm, c, per_row, tpb) for c in (MOD_GATE_MIX, MOD_SCALE_FFN, MOD_SHIFT_FFN)]
    return pl.pallas_call(
        _out_kernel,
        grid=(n // tm,),
        in_specs=[half, half, full] + mods + [
                  pl.BlockSpec((D_MODEL, D_MODEL), lambda i: (0, 0)), vec, vec,
                  pl.BlockSpec((D_MODEL, N_EXPERTS), lambda i: (0, 0)),
                  pl.BlockSpec((1, N_EXPERTS), lambda i: (0, 0))],
        out_specs=[full, pl.BlockSpec((tm * SLAB, LANES), lambda i: (i, 0)), stat],
        out_shape=[jax.ShapeDtypeStruct((n, D_MODEL), F32),
                   jax.ShapeDtypeStruct((n * SLAB, LANES), F32),
                   jax.ShapeDtypeStruct((n, LANES), F32)],
        compiler_params=_params(1, VMEM_LIMIT),
        name="mixer_out",
    )(ret_y, att, x, mod, mod, mod, w_out_bf, g_post, g_pre, w_router, b_router)


def _route_kernel(top_ref, dest_ref, ends_ref, counts_ref, running, starts, *, tm, tm_blk):
    ph, i = pl.program_id(0), pl.program_id(1)
    lane_f = lax.broadcasted_iota(jnp.int32, (tm, LANES), 1).astype(F32)
    top = top_ref[...]
    picks = [lane_f == top[:, k:k + 1] for k in range(TOP_K)]
    chosen = sum(p.astype(F32) for p in picks)
    tile_counts = jnp.sum(chosen, axis=0, keepdims=True)

    @pl.when((ph == 0) & (i == 0))
    def _():
        running[...] = jnp.zeros_like(running)

    @pl.when(ph == 0)
    def _():
        running[...] += tile_counts

    @pl.when((ph == 1) & (i == 0))
    def _():
        lane = lax.broadcasted_iota(jnp.int32, (1, LANES), 1)
        assert tm_blk & (tm_blk - 1) == 0
        counts = running[...].astype(jnp.int32)
        counts_ref[...] = counts
        blocks = (counts + (tm_blk - 1)) >> (tm_blk.bit_length() - 1)
        ends = blocks
        shift = 1
        while shift < N_EXPERTS:
            ends = ends + jnp.where(lane >= shift, pltpu.roll(ends, shift, axis=1), 0)
            shift *= 2
        ends_ref[...] = ends
        starts[...] = ((ends - blocks) * tm_blk).astype(F32)
        running[...] = jnp.zeros_like(running)

    @pl.when(ph == 1)
    def _():
        earlier = (lax.broadcasted_iota(jnp.int32, (tm, tm), 0)
                   > lax.broadcasted_iota(jnp.int32, (tm, tm), 1))
        ahead = _dot(jnp.where(earlier, 1.0, 0.0).astype(BF16), chosen.astype(BF16))
        slot = starts[...] + running[...] + ahead
        lane_o = lax.broadcasted_iota(jnp.int32, (tm, LANES), 1)
        packed = jnp.zeros((tm, LANES), F32)
        for k, pick in enumerate(picks):
            dest_k = jnp.sum(jnp.where(pick, slot, 0.0), axis=-1, keepdims=True)
            packed = jnp.where(lane_o == k, dest_k, packed)
        dest_ref[...] = packed.astype(jnp.int32)
        running[...] += tile_counts


def _route(top, tm_blk):
    n = top.shape[0]
    tm = 512
    per_expert = pl.BlockSpec((1, LANES), lambda ph, i: (0, 0))
    return pl.pallas_call(
        functools.partial(_route_kernel, tm=tm, tm_blk=tm_blk),
        grid=(2, n // tm),
        in_specs=[pl.BlockSpec((tm, LANES), lambda ph, i: (i, 0))],
        out_specs=[pl.BlockSpec((tm, LANES), lambda ph, i: (i * ph, 0)), per_expert, per_expert],
        out_shape=[jax.ShapeDtypeStruct((n, LANES), jnp.int32),
                   jax.ShapeDtypeStruct((1, LANES), jnp.int32), jax.ShapeDtypeStruct((1, LANES), jnp.int32)],
        scratch_shapes=[pltpu.VMEM((1, LANES), F32), pltpu.VMEM((1, LANES), F32)],
        compiler_params=_params(2),
        name="route",
    )(top)


def _dispatch_kernel(ends_ref, dest_ref, *rest, tm, tm_blk, n_blocks, tiles):
    h_refs, (xs_hbm, zeros, sem) = rest[:-3], rest[-3:]
    i = pl.program_id(0)

    def fill(b):
        rows = pl.ds(pl.multiple_of(b * (tm_blk * SLAB), tm_blk * SLAB), tm_blk * SLAB)
        return pltpu.make_async_copy(zeros, xs_hbm.at[rows], sem.at[1])

    def fill_start(b, carry):
        fill(b).start()
        return carry

    def fill_wait(b, carry):
        fill(b).wait()
        return carry

    def zero_padding():
        zeros[...] = jnp.zeros_like(zeros)
        n_used = ends_ref[N_EXPERTS - 1]
        for do in (fill_start, fill_wait):
            for e in range(N_EXPERTS):
                first = ends_ref[e - 1] if e else 0

                @pl.when(ends_ref[e] > first)
                def _():
                    do(ends_ref[e] - 1, 0)

            lax.fori_loop(n_used, n_blocks, do, 0)

    pl.when(i == 0)(zero_padding)

    def scatter(h_ref):
        def body(jj, carry):
            for j in (jj * DMA_UNROLL + u for u in range(DMA_UNROLL)):
                for k in range(TOP_K):
                    pltpu.make_async_copy(h_ref.at[_slab_rows(j)], xs_hbm.at[_slab_rows(dest_ref[j * TOP_K + k])],
                                          sem.at[0]).start(priority=k % 2)
            return carry

        lax.fori_loop(0, tm // DMA_UNROLL, body, 0)
        for k in range(TOP_K):
            pltpu.make_async_copy(h_ref, xs_hbm.at[pl.ds(0, tm * SLAB)], sem.at[0]).wait()

    start = 0
    for h_ref, n_tiles in zip(h_refs, tiles):
        pl.when((i >= start) & (i < start + n_tiles))(functools.partial(scatter, h_ref))
        start += n_tiles


def _dispatch(hs, dest_flat, ends, n_blocks, tm_blk):
    tm = 256
    tiles = [h.shape[0] // SLAB // tm for h in hs]
    starts = [sum(tiles[:s]) for s in range(len(hs))]
    src = lambda s: pl.BlockSpec((tm * SLAB, LANES),
                                 lambda i, ends: (jnp.clip(i - starts[s], 0, tiles[s] - 1), 0))
    grid_spec = pltpu.PrefetchScalarGridSpec(
        num_scalar_prefetch=1,
        grid=(sum(tiles),),
        in_specs=[pl.BlockSpec((tm * TOP_K,), lambda i, ends: (i,), memory_space=pltpu.SMEM)]
        + [src(s) for s in range(len(hs))],
        out_specs=pl.BlockSpec(memory_space=pl.ANY),
        scratch_shapes=[pltpu.VMEM((tm_blk * SLAB, LANES), F32), pltpu.SemaphoreType.DMA((2,))],
    )
    return pl.pallas_call(
        functools.partial(_dispatch_kernel, tm=tm, tm_blk=tm_blk, n_blocks=n_blocks, tiles=tiles),
        grid_spec=grid_spec,
        out_shape=jax.ShapeDtypeStruct((n_blocks * tm_blk * SLAB, LANES), F32),
        compiler_params=_params(1),
        name="dispatch",
    )(ends, dest_flat, *hs)


def _moe_kernel(be_ref, nu_ref, valid_ref, nxt_ref, prep_ref, slot_ref, x_ref, wgu_hbm, wd_hbm,
                bg_ref, bu_ref, bd_ref, o_ref, wgu_buf, wd_buf, wt, wg_t, wu_t, wd_bf, sem):
    i = pl.program_id(0)
    used = i < nu_ref[0]
    valid = valid_ref[i]
    s = slot_ref[i]
    tm = x_ref.shape[0] // SLAB
    _, d, d_gu = wgu_buf.shape
    d_ff = d_gu // 2
    new_expert = (i == 0) | (be_ref[i] != be_ref[jnp.maximum(i - 1, 0)])

    def weights(e, slot):
        return (pltpu.make_async_copy(wgu_hbm.at[e], wgu_buf.at[slot], sem.at[0, slot]),
                pltpu.make_async_copy(wd_hbm.at[e], wd_buf.at[slot], sem.at[1, slot]))

    def arrived(e, slot):
        for copy in weights(e, slot):
            copy.wait()

    def prepare(slot):
        for c in range(d // LANES):
            wt[...] = wgu_buf[slot, pl.ds(c * LANES, LANES), :].T
            rows = slice(c * LANES, (c + 1) * LANES)
            wg_t[slot, :, rows] = wt[pl.ds(0, d_ff, stride=2), :].astype(BF16)
            wu_t[slot, :, rows] = wt[pl.ds(1, d_ff, stride=2), :].astype(BF16)
        wd_bf[slot] = wd_buf[slot].astype(BF16)

    @pl.when(i == 0)
    def _():
        for copy in weights(be_ref[0], s):
            copy.start()

    @pl.when(used & new_expert & (nxt_ref[i] >= 0))
    def _():
        for copy in weights(nxt_ref[i], 1 - s):
            copy.start()

    @pl.when(i == 0)
    def _():
        arrived(be_ref[0], s)
        prepare(s)

    def experts_mlp(rows, prep_next):
        if prep_next:
            arrived(prep_ref[i], 1 - s)
        x = _slab_load(x_ref, rows).astype(BF16)
        acc = bd_ref[...]
        for c in range(d_ff // MOE_COLS):
            cols = slice(c * MOE_COLS, (c + 1) * MOE_COLS)
            gate = jnp.minimum(_dot_nt(x, wg_t[s, cols, :]) + bg_ref[:, cols], SWIGLU_LIMIT)
            up = jnp.clip(_dot_nt(x, wu_t[s, cols, :]) + bu_ref[:, cols], -SWIGLU_LIMIT, SWIGLU_LIMIT)
            act = gate * jax.nn.sigmoid(SWIGLU_ALPHA * gate) * (up + 1.0)
            acc = acc + _dot(act.astype(BF16), wd_bf[s, cols, :])
        _slab_store(o_ref, acc)
        if rows < tm:
            o_ref[rows * SLAB:, :] = jnp.zeros(((tm - rows) * SLAB, LANES), o_ref.dtype)
        if prep_next:
            prepare(1 - s)

    sizes = [tm // 4, tm // 2, tm]
    prep = used & (prep_ref[i] >= 0)
    full = valid > sizes[-2]
    @pl.when(prep & ~full)
    def _():
        arrived(prep_ref[i], 1 - s)
        prepare(1 - s)
    pl.when(prep & full)(functools.partial(experts_mlp, tm, True))
    for lo, rows in zip([0] + sizes[:-1], sizes):
        fits = used & (valid > lo) & (valid <= rows)
        pl.when(fits & ~(prep & full))(functools.partial(experts_mlp, rows, False))

    @pl.when(~used)
    def _():
        o_ref[...] = jnp.zeros_like(o_ref)


def _moe_plan(ends, counts, n_blocks, tm):
    blk = jnp.arange(n_blocks, dtype=jnp.int32)
    e_ids = jnp.arange(N_EXPERTS, dtype=jnp.int32)
    block_e = jnp.minimum(jnp.sum(blk[:, None] >= ends[None, :], axis=1), N_EXPERTS - 1).astype(jnp.int32)
    mine = block_e[:, None] == e_ids[None, :]
    pick = lambda per_e: jnp.sum(jnp.where(mine, per_e[None, :], 0), axis=1)
    first_blk = ends - (counts + tm - 1) // tm
    valid = jnp.clip(pick(counts) - (blk - pick(first_blk)) * tm, 0, tm)
    valid = jnp.where(blk < ends[N_EXPERTS - 1], valid, 0).astype(jnp.int32)
    used = counts > 0
    slot_e = (jnp.cumsum(used.astype(jnp.int32)) - 1) % 2
    later = (e_ids[None, :] > e_ids[:, None]) & used[None, :]
    next_e = jnp.min(jnp.where(later, e_ids[None, :], N_EXPERTS), axis=1)
    next_e = jnp.where(next_e < N_EXPERTS, next_e, -1)
    in_use = valid > 0
    nxt = jnp.where((blk == pick(first_blk)) & in_use, pick(next_e), -1).astype(jnp.int32)
    n_blk = pick((counts + tm - 1) // tm)
    prep_blk = pick(first_blk) + jnp.where(n_blk >= 3, n_blk - 2, n_blk - 1)
    prep = jnp.where((blk == prep_blk) & in_use, pick(next_e), -1).astype(jnp.int32)
    return block_e, ends[N_EXPERTS - 1:], valid, nxt, prep, pick(slot_e).astype(jnp.int32)


def _moe_blocks(xs, block_e, n_used, valid, nxt, prep, slot, w_gu, w_down, b_gu, b_down, tm):
    n_blocks = block_e.shape[0]
    n_e, d, d_gu = w_gu.shape
    d_ff = d_gu // 2
    bias = lambda n: pl.BlockSpec((None, 1, n), lambda i, be, *_: (be[i], 0, 0))
    grid_spec = pltpu.PrefetchScalarGridSpec(
        num_scalar_prefetch=6,
        grid=(n_blocks,),
        in_specs=[pl.BlockSpec((tm * SLAB, LANES), lambda i, be, nu, *_: (jnp.minimum(i, nu[0] - 1), 0)),
                  pl.BlockSpec(memory_space=pl.ANY), pl.BlockSpec(memory_space=pl.ANY),
                  bias(d_ff), bias(d_ff), bias(d)],
        out_specs=pl.BlockSpec((tm * SLAB, LANES), lambda i, *_: (i, 0)),
        scratch_shapes=[pltpu.VMEM((2, d, d_gu), F32), pltpu.VMEM((2, d_ff, d), F32),
                        pltpu.VMEM((d_gu, LANES), F32), pltpu.VMEM((2, d_ff, d), BF16),
                        pltpu.VMEM((2, d_ff, d), BF16), pltpu.VMEM((2, d_ff, d), BF16),
                        pltpu.SemaphoreType.DMA((2, 2))],
    )
    return pl.pallas_call(
        _moe_kernel,
        grid_spec=grid_spec,
        out_shape=jax.ShapeDtypeStruct((n_blocks * tm * SLAB, LANES), F32),
        compiler_params=_params(1, VMEM_LIMIT),
        name="moe_blocks",
    )(block_e, n_used, valid, nxt, prep, slot, xs, w_gu, w_down, b_gu[:, None, 0::2], b_gu[:, None, 1::2],
      b_down.reshape(n_e, 1, d))


def _fin_kernel(dest_ref, next_ref, rows_hbm, top_ref, x1_ref, gt_ref, g_ref, o_ref, buf, sem, *, tm):
    i = pl.program_id(0)
    slot = i % 2

    def gather(idx_ref, s):
        def body(jj, carry):
            for j in (jj * DMA_UNROLL + u for u in range(DMA_UNROLL)):
                for k in range(TOP_K):
                    pltpu.make_async_copy(rows_hbm.at[_slab_rows(idx_ref[j * TOP_K + k])],
                                          buf.at[s, k, _slab_rows(j)], sem.at[s]).start(priority=k % 2)
            return carry
        lax.fori_loop(0, tm // DMA_UNROLL, body, 0)

    @pl.when(i == 0)
    def _():
        gather(dest_ref, 0)

    @pl.when(i + 1 < pl.num_programs(0))
    def _():
        gather(next_ref, 1 - slot)

    for k in range(TOP_K):
        pltpu.make_async_copy(rows_hbm.at[pl.ds(0, tm * SLAB)], buf.at[slot, k], sem.at[slot]).wait()
    top = top_ref[...]
    f = sum(_slab_load(buf.at[slot, k], tm) * top[:, TOP_K + k:TOP_K + k + 1] for k in range(TOP_K))
    o_ref[...] = x1_ref[...] + gt_ref[...] * _rms(f, g_ref[...])


def _finish(dest, rows, top, x1, mod, g_post, per_row, rows_per_batch):
    n = x1.shape[0]
    tm = min(TM_FIN, n)
    full = pl.BlockSpec((tm, D_MODEL), lambda i: (i, 0))
    gate = _mod_spec(tm, MOD_GATE_FFN, per_row, 1 if per_row else rows_per_batch // tm)
    return pl.pallas_call(
        functools.partial(_fin_kernel, tm=tm),
        grid=(n // tm,),
        in_specs=[pl.BlockSpec((tm * TOP_K,), lambda i: (i,), memory_space=pltpu.SMEM),
                  pl.BlockSpec((tm * TOP_K,), lambda i: (jnp.minimum(i + 1, n // tm - 1),),
                               memory_space=pltpu.SMEM),
                  pl.BlockSpec(memory_space=pl.ANY),
                  pl.BlockSpec((tm, LANES), lambda i: (i, 0)), full, gate,
                  pl.BlockSpec((1, D_MODEL), lambda i: (0, 0))],
        out_specs=full,
        out_shape=jax.ShapeDtypeStruct((n, D_MODEL), F32),
        scratch_shapes=[pltpu.VMEM((2, TOP_K, tm * SLAB, LANES), F32), pltpu.SemaphoreType.DMA((2,))],
        compiler_params=_params(1),
        name="finish",
    )(dest, dest, rows, top, x1, mod, g_post)


def kernel(x_prompt, x_sample, state_ret, cache_win_k, cache_win_v, c_prompt, c_sample, w_ada, b_ada,
           g_pre_mix, g_post_mix, g_pre_ffn, g_post_ffn, w_in, g_ret, w_out, w_router, b_router,
           w_gate_up, b_gate_up, w_down, b_down):
    depth = w_in.shape[0]
    assert depth == 1, "single-layer step"
    n_b, seq, d = x_prompt.shape
    n_db, t_new, _ = x_sample.shape
    n_p, n_s = n_b * seq, n_db * t_new
    buf = cache_win_k.shape[2]
    keep = min(DIL_PATTERNS[-1][0], seq)
    vec = lambda g: g[0].reshape(1, -1)

    mod_p, mod_s = _ada(c_prompt, c_sample, w_ada[0], b_ada[0], t_new)
    mod_p = mod_p[:n_b].reshape(n_b, 1, -1)
    mod_s = mod_s.reshape(n_s, -1)

    w_in_bf = w_in[0].astype(BF16)
    w_out_bf = w_out[0].astype(BF16)
    xp = x_prompt.reshape(n_p, d)
    xs = x_sample.reshape(n_s, d)
    router = (w_router[0], b_router[0].reshape(1, -1))

    qr, kr, vr, gr, qa_pm, ka_pm, va_pm, k_win, v_win = _in_proj(xp, mod_p, vec(g_pre_mix), w_in_bf, False, seq, keep)
    ret_y_p, ret_state_p = _retention(qr, kr, vr, gr, None, g_ret[0], n_b, seq, RET_CHUNK, n_b)
    att_p = _att_prompt(qa_pm, ka_pm, va_pm, n_b, seq)
    x1_p, h_p, top_p = _mixer_out(ret_y_p, att_p, xp, mod_p, w_out_bf, vec(g_post_mix), vec(g_pre_ffn),
                                  *router, False, seq)

    qr_s, kr_s, vr_s, gr_s, qa_s, ka_s, va_s = _in_proj(xs, mod_s, vec(g_pre_mix), w_in_bf, True, t_new)
    ret_y_s, ret_state_s = _retention(qr_s, kr_s, vr_s, gr_s, state_ret[0], g_ret[0], n_db, t_new, t_new,
                                      RET_GROUP_SAMPLE)
    att_s = _att_sample(qa_s, ka_s, va_s, cache_win_k[0], cache_win_v[0], n_db, t_new)
    x1_s, h_s, top_s = _mixer_out(ret_y_s, att_s, xs, mod_s, w_out_bf, vec(g_post_mix), vec(g_pre_ffn),
                                  *router, True, t_new)

    n_blocks = (n_p + n_s) * TOP_K // TM_MOE + N_EXPERTS
    dest, ends, counts = _route(jnp.concatenate([top_p, top_s], axis=0), TM_MOE)
    dest = dest[:, :TOP_K].reshape(-1)
    ends, counts = ends[0, :N_EXPERTS], counts[0, :N_EXPERTS]
    rows_in = _dispatch([h_p, h_s], dest, ends, n_blocks, TM_MOE)
    rows = _moe_blocks(rows_in, *_moe_plan(ends, counts, n_blocks, TM_MOE), w_gate_up[0], w_down[0],
                       b_gate_up[0], b_down[0], TM_MOE)
    y_p = _finish(dest[:n_p * TOP_K], rows, top_p, x1_p, mod_p, vec(g_post_ffn), False, seq)
    y_s = _finish(dest[n_p * TOP_K:], rows, top_s, x1_s, mod_s, vec(g_post_ffn), True, t_new)

    win = lambda t: t.reshape(n_b, N_HEADS, HEAD_DIM, keep).transpose(0, 3, 1, 2)[None]
    new = lambda t: t.reshape(n_db, t_new, N_HEADS, HEAD_DIM)[None]
    return (y_p.reshape(n_b, seq, d), y_s.reshape(n_db, t_new, d), ret_state_p[None], ret_state_s[None],
            win(k_win), win(v_win), new(ka_s), new(va_s))
```

```python
import functools

import numpy as np
import jax
import jax.numpy as jnp
from jax import lax
from jax.experimental import pallas as pl
from jax.experimental.pallas import tpu as pltpu

F32 = jnp.float32
BF16 = jnp.bfloat16

D_MODEL = 1024
HEAD_DIM = 64
N_HEADS = 8
D_HALF = N_HEADS * HEAD_DIM
N_SEG = 7
LANES = 128
N_PAIRS = D_HALF // LANES
RET_CHUNK = 128
RET_GROUP_SAMPLE = 8
ATT_BLOCK = 128
ATT_Q = 1024
ATT_BACK = 2
ATT_UNROLL = 2
DIL_PATTERNS = ((128, 1), (512, 4), (2048, 16))
N_EXPERTS = 32
TOP_K = 4
SWIGLU_LIMIT = 7.0
SWIGLU_ALPHA = 1.702
EPS = 1e-6
NEG_INF = -1e30
MOD_SHIFT_MIX, MOD_SCALE_MIX, MOD_GATE_MIX, MOD_SHIFT_FFN, MOD_SCALE_FFN, MOD_GATE_FFN = range(6)

TM_IN = 512
TM_OUT = 512
TM_MOE = 512
MOE_COLS = 512
TM_FIN = 256
TM_ROUTE = 512
TM_DISPATCH = 512
DMA_UNROLL = 4
VMEM_LIMIT = 56 * 1024 * 1024


def _params(n_axes, vmem=None):
    return pltpu.CompilerParams(dimension_semantics=("arbitrary",) * n_axes, vmem_limit_bytes=vmem)


def _ret_log_decay():
    return np.log(1.0 - 2.0 ** (-5.0 - np.arange(N_HEADS, dtype=np.float64)))


def _alibi_slopes():
    return 2.0 ** (-8.0 * (np.arange(N_HEADS, dtype=np.float64) + 1.0) / N_HEADS)


def _per_head_lanes(a):
    return np.repeat(a, HEAD_DIM, axis=-1)


def _rms(x, g):
    return x * lax.rsqrt(jnp.mean(x * x, axis=-1, keepdims=True) + EPS) * g


def _dot(a, b):
    return jnp.dot(a, b, preferred_element_type=F32)


def _dot_nt(a, b):
    return lax.dot_general(a, b, (((1,), (1,)), ((), ())), preferred_element_type=F32)


def _dot_tn(a, b):
    return lax.dot_general(a, b, (((0,), (0,)), ((), ())), preferred_element_type=F32)


SLAB = D_MODEL // LANES


def _slab_rows(r):
    return pl.ds(pl.multiple_of(r * SLAB, SLAB), SLAB)


def _slab_load(ref, n):
    return jnp.concatenate([ref[pl.ds(s, n, stride=SLAB), :] for s in range(SLAB)], axis=1)


def _slab_store(ref, value):
    n = value.shape[0]
    for s in range(SLAB):
        ref[pl.ds(s, n, stride=SLAB), :] = value[:, s * LANES:(s + 1) * LANES]


def _mod_spec(tm, col, per_row, tiles_per_batch):
    if per_row:
        return pl.BlockSpec((tm, D_MODEL), lambda i: (i, col))
    return pl.BlockSpec((None, 1, D_MODEL), lambda i: (i // tiles_per_batch, 0, col))


def _ada_kernel(cp_ref, cs_ref, w_ref, b_ref, op_ref, os_ref):
    w = w_ref[...].astype(BF16)

    def mod(c):
        return _dot((c * jax.nn.sigmoid(c)).astype(BF16), w) + b_ref[...]

    op_ref[...] = mod(cp_ref[...])
    ms = mod(cs_ref[...])
    os_ref[...] = jnp.broadcast_to(ms[:, None, :], os_ref.shape)


def _ada(c_prompt, c_sample, w_ada, b_ada, t_new):
    n_b, d = c_prompt.shape
    n_db = c_sample.shape[0]
    n_out = w_ada.shape[1]
    cp = jnp.concatenate([c_prompt, jnp.zeros((-n_b % 8, d), F32)], axis=0)
    return pl.pallas_call(
        _ada_kernel,
        grid=(n_out // d,),
        in_specs=[pl.BlockSpec(cp.shape, lambda j: (0, 0)),
                  pl.BlockSpec((n_db, d), lambda j: (0, 0)),
                  pl.BlockSpec((d, d), lambda j: (0, j)),
                  pl.BlockSpec((1, d), lambda j: (0, j))],
        out_specs=[pl.BlockSpec(cp.shape, lambda j: (0, j)),
                   pl.BlockSpec((n_db, t_new, d), lambda j: (0, 0, j))],
        out_shape=[jax.ShapeDtypeStruct((cp.shape[0], n_out), F32),
                   jax.ShapeDtypeStruct((n_db, t_new, n_out), F32)],
        compiler_params=_params(1, VMEM_LIMIT),
        name="ada",
    )(cp, c_sample, w_ada, b_ada.reshape(1, n_out))


def _in_kernel(x_ref, sc_ref, sh_ref, g_ref, w_ref, *o_refs, n_plain, window):
    h = (_rms(x_ref[...], g_ref[...]) * (1.0 + sc_ref[...]) + sh_ref[...]).astype(BF16)
    segs = [_dot(h, w_ref[:, s * D_HALF:(s + 1) * D_HALF]) for s in range(N_SEG)]
    for o_ref, seg in zip(o_refs[:n_plain], segs):
        o_ref[...] = seg
    if n_plain == N_SEG:
        return
    for o_ref, seg in zip(o_refs[n_plain:N_SEG], segs[n_plain:]):
        for p in range(N_PAIRS):
            o_ref[p] = seg[:, p * LANES:(p + 1) * LANES]
    tiles_per_seq, first = window
    @pl.when(pl.program_id(0) % tiles_per_seq >= first)
    def _():
        for o_ref, seg in zip(o_refs[N_SEG:], segs[N_SEG - 2:]):
            o_ref[...] = seg.T


def _in_proj(x, mod, g, w_bf, per_row, rows_per_batch, keep=None):
    n = x.shape[0]
    tm = min(TM_IN, n)
    tpb = 1 if per_row else rows_per_batch // tm
    seg = pl.BlockSpec((tm, D_HALF), lambda i: (i, 0))
    seg_shape = jax.ShapeDtypeStruct((n, D_HALF), F32)
    if keep is None:
        n_plain, window = N_SEG, None
        out_specs, out_shape = [seg] * N_SEG, [seg_shape] * N_SEG
    else:
        n_plain, window = N_SEG - 3, (tpb, (rows_per_batch - keep) // tm)
        pm = pl.BlockSpec((N_PAIRS, tm, LANES), lambda i: (0, i, 0))
        win = pl.BlockSpec((None, D_HALF, tm), lambda i: (i // tpb, 0, jnp.maximum(i % tpb - window[1], 0)))
        out_specs = [seg] * n_plain + [pm] * 3 + [win] * 2
        out_shape = ([seg_shape] * n_plain + [jax.ShapeDtypeStruct((N_PAIRS, n, LANES), F32)] * 3
                     + [jax.ShapeDtypeStruct((n // rows_per_batch, D_HALF, keep), F32)] * 2)
    return pl.pallas_call(
        functools.partial(_in_kernel, n_plain=n_plain, window=window),
        grid=(n // tm,),
        in_specs=[pl.BlockSpec((tm, D_MODEL), lambda i: (i, 0)),
                  _mod_spec(tm, MOD_SCALE_MIX, per_row, tpb), _mod_spec(tm, MOD_SHIFT_MIX, per_row, tpb),
                  pl.BlockSpec((1, D_MODEL), lambda i: (0, 0)),
                  pl.BlockSpec((D_MODEL, N_SEG * D_HALF), lambda i: (0, 0))],
        out_specs=out_specs,
        out_shape=out_shape,
        compiler_params=_params(1, VMEM_LIMIT),
        name="in_proj",
    )(x, mod, mod, g, w_bf)


def _ret_tables(chunk):
    lg = _ret_log_decay()
    pos = np.arange(chunk, dtype=np.float64)
    diff = pos[:, None] - pos[None, :]
    dmask = np.where(diff >= 0, np.exp(np.maximum(diff, 0.0)[None] * lg[:, None, None]), 0.0)
    dmask = dmask.reshape(N_PAIRS, 2 * chunk, chunk)
    row_decay = _per_head_lanes(np.exp((pos[:, None] + 1.0) * lg[None, :]))
    key_decay = _per_head_lanes(np.exp((chunk - 1.0 - pos)[:, None] * lg[None, :]))
    state_decay = _per_head_lanes(np.exp(chunk * lg)[None, :])
    return tuple(jnp.asarray(t, F32) for t in (dmask, row_decay, key_decay, state_decay))


def _ret_kernel(q_ref, k_ref, v_ref, gate_ref, s0_ref, gret_ref, dm_ref, rd_ref, kd_ref, sd_ref,
                y_ref, st_ref, state, *, chunk, group, has_init):
    c = pl.program_id(1)
    lo = lax.broadcasted_iota(jnp.int32, (chunk, LANES), 1) < HEAD_DIM
    blk_r = lax.broadcasted_iota(jnp.int32, (LANES, LANES), 0) < HEAD_DIM
    blk_c = lax.broadcasted_iota(jnp.int32, (LANES, LANES), 1) < HEAD_DIM
    same_head = blk_r == blk_c

    @pl.when(c == 0)
    def _():
        if has_init:
            zero = jnp.zeros((HEAD_DIM, HEAD_DIM), F32)
            for b in range(group):
                for p in range(N_PAIRS):
                    top = jnp.concatenate([s0_ref[b, 2 * p], zero], axis=1)
                    bot = jnp.concatenate([zero, s0_ref[b, 2 * p + 1]], axis=1)
                    state[b * N_PAIRS + p] = jnp.concatenate([top, bot], axis=0)
        else:
            state[...] = jnp.zeros_like(state)

    new_states = []
    for b, p in [(b, p) for b in range(group) for p in range(N_PAIRS)]:
        sl = slice(p * LANES, (p + 1) * LANES)
        q = q_ref[b, :, sl]
        k = k_ref[b, :, sl] * HEAD_DIM ** -0.5
        vb = v_ref[b, :, sl].astype(BF16)
        kb = k.astype(BF16)
        q2 = jnp.concatenate([jnp.where(lo, q, 0.0), jnp.where(lo, 0.0, q)], axis=0).astype(BF16)
        inner = _dot((_dot_nt(q2, kb) * dm_ref[p]).astype(BF16), vb)
        s_old = state[b * N_PAIRS + p]
        cross = _dot(q.astype(BF16), s_old.astype(BF16)) * rd_ref[:, sl]
        o = jnp.where(lo, inner[:chunk], inner[chunk:]) + cross
        upd = _dot_tn((k * kd_ref[:, sl]).astype(BF16), vb)
        s_new = sd_ref[:, sl] * s_old + jnp.where(same_head, upd, 0.0)
        state[b * N_PAIRS + p] = s_new
        new_states.append(s_new)

        def head_mean(t):
            m_lo = jnp.sum(jnp.where(lo, t, 0.0), axis=-1, keepdims=True)
            m_hi = jnp.sum(jnp.where(lo, 0.0, t), axis=-1, keepdims=True)
            return jnp.where(lo, m_lo, m_hi) * (1.0 / HEAD_DIM)

        d = o - head_mean(o)
        n = d * lax.rsqrt(head_mean(d * d) + EPS) * gret_ref[:, sl]
        g = gate_ref[b, :, sl]
        y_ref[b, :, sl] = (g * jax.nn.sigmoid(g) * n).astype(y_ref.dtype)

    @pl.when(c == pl.num_programs(1) - 1)
    def _():
        for bp, s_new in enumerate(new_states):
            b, p = divmod(bp, N_PAIRS)
            st_ref[b, 2 * p] = s_new[:HEAD_DIM, :HEAD_DIM]
            st_ref[b, 2 * p + 1] = s_new[HEAD_DIM:, HEAD_DIM:]


def _retention(q, k, v, gate, state0, g_ret, n_batch, seq, chunk, group):
    nc = seq // chunk
    by_seq = lambda t: t.reshape(n_batch, seq, D_HALF)
    has_init = state0 is not None
    if state0 is None:
        state0 = jnp.zeros((group, N_HEADS, HEAD_DIM, HEAD_DIM), F32)
        s0_map = lambda b, c: (0, 0, 0, 0)
    else:
        s0_map = lambda b, c: (b, 0, 0, 0)
    dmask, row_decay, key_decay, state_decay = _ret_tables(chunk)
    rows = pl.BlockSpec((group, chunk, D_HALF), lambda b, c: (b, c, 0))
    const2 = lambda shape: pl.BlockSpec(shape, lambda b, c: (0, 0))
    st_block = pl.BlockSpec((group, N_HEADS, HEAD_DIM, HEAD_DIM), lambda b, c: (b, 0, 0, 0))
    y, state = pl.pallas_call(
        functools.partial(_ret_kernel, chunk=chunk, group=group, has_init=has_init),
        grid=(n_batch // group, nc),
        in_specs=[rows, rows, rows, rows,
                  pl.BlockSpec((group, N_HEADS, HEAD_DIM, HEAD_DIM), s0_map),
                  const2((1, D_HALF)),
                  pl.BlockSpec((N_PAIRS, 2 * chunk, chunk), lambda b, c: (0, 0, 0)),
                  const2((chunk, D_HALF)), const2((chunk, D_HALF)), const2((1, D_HALF))],
        out_specs=[rows, st_block],
        out_shape=[jax.ShapeDtypeStruct((n_batch, seq, D_HALF), BF16),
                   jax.ShapeDtypeStruct((n_batch, N_HEADS, HEAD_DIM, HEAD_DIM), F32)],
        scratch_shapes=[pltpu.VMEM((group * N_PAIRS, LANES, LANES), F32)],
        compiler_params=_params(2),
        name="retention",
    )(by_seq(q), by_seq(k), by_seq(v), by_seq(gate), state0, g_ret.reshape(1, D_HALF), dmask, row_decay,
      key_decay, state_decay)
    return y.reshape(n_batch * seq, D_HALF), state


def _band_table(n_q, n_k, dilation):
    step = np.arange(n_q)[:, None] + (n_k - n_q) - np.arange(n_k)[None, :]
    in_win = (step >= 0) & (step <= ATT_BLOCK)
    bias = -_alibi_slopes()[:, None, None] * (step * dilation).astype(np.float64)[None]
    return jnp.asarray(np.where(in_win[None], bias, NEG_INF).reshape(N_PAIRS, 2 * n_q, n_k), F32)


def _band_attention(q, k, v, table_ref, key_ok, pair):
    n_q = q.shape[0]
    lo = lax.broadcasted_iota(jnp.int32, q.shape, 1) < HEAD_DIM
    q = q * HEAD_DIM ** -0.5
    q2 = jnp.concatenate([jnp.where(lo, q, 0.0), jnp.where(lo, 0.0, q)], axis=0).astype(BF16)
    s = _dot_nt(q2, k.astype(BF16)) + table_ref[pair]
    s = jnp.where(key_ok, s, NEG_INF)
    m = jnp.max(s, axis=-1, keepdims=True)
    e = jnp.exp(s - m)
    l = jnp.sum(e, axis=-1, keepdims=True)
    o = _dot(e.astype(BF16), v.astype(BF16))
    return tuple(jnp.where(lo, t[:n_q], t[n_q:]) for t in (o, m, l))


def _attp_kernel(*refs):
    n_back = ATT_BACK
    q_ref = refs[0]
    k_refs = refs[1:2 + n_back]
    v_refs = refs[2 + n_back:3 + 2 * n_back]
    t_refs = refs[3 + 2 * n_back:3 + 2 * n_back + len(DIL_PATTERNS)]
    o_ref, parts, kcat, vcat = refs[3 + 2 * n_back + len(DIL_PATTERNS):]
    n = pl.program_id(1)

    def key_col(n_q, n_k):
        return lax.broadcasted_iota(jnp.int32, (2 * n_q, n_k), 1)

    def fold(first, rows, news):
        if not first:
            olds = [[parts[kind, p, rows, :] for kind in range(3)] for p in range(N_PAIRS)]
            merged = []
            for (o_old, m_old, l_old), (o_new, m_new, l_new) in zip(olds, news):
                m = jnp.maximum(m_old, m_new)
                a, b = jnp.exp(m_old - m), jnp.exp(m_new - m)
                merged.append((o_old * a + o_new * b, m, l_old * a + l_new * b))
            news = merged
        for p, new in enumerate(news):
            for kind, val in enumerate(new):
                parts[kind, p, rows, :] = val

    for pat, (window, dil) in enumerate(DIL_PATTERNS):
        t_ref = t_refs[pat]
        if dil == 1:
            for cat, src in ((kcat, k_refs), (vcat, v_refs)):
                cat[:, :ATT_BLOCK, :] = src[1][:, ATT_Q - ATT_BLOCK:, :]
                cat[:, ATT_BLOCK:, :] = src[0][...]

            def body(it, carry, t_ref=t_ref, first=pat == 0):
                for u in range(ATT_UNROLL):
                    sub = it * ATT_UNROLL + u
                    r0 = pl.multiple_of(sub * ATT_BLOCK, ATT_BLOCK)
                    ok = (key_col(ATT_BLOCK, 2 * ATT_BLOCK) >= ATT_BLOCK) | (n > 0) | (sub > 0)
                    fold(first, pl.ds(r0, ATT_BLOCK),
                         [_band_attention(q_ref[p, pl.ds(r0, ATT_BLOCK), :], kcat[p, pl.ds(r0, 2 * ATT_BLOCK), :],
                                          vcat[p, pl.ds(r0, 2 * ATT_BLOCK), :], t_ref, ok, p)
                          for p in range(N_PAIRS)])
                return carry

            lax.fori_loop(0, ATT_Q // ATT_BLOCK // ATT_UNROLL, body, 0)
            continue
        sub_rows = min(window, ATT_Q)
        n_q = sub_rows // dil
        n_prev = window // sub_rows
        for s in range(ATT_Q // sub_rows):
            pieces = []
            for i in range(n_prev, -1, -1):
                rel = (s - i) * sub_rows
                j = -(rel // ATT_Q)
                pieces.append((j, rel + j * ATT_Q))
            missing = sum(jnp.where(n < j, n_q, 0) for j, _ in pieces)

            unroll = ATT_UNROLL if n_q == ATT_BLOCK else 1

            def body(it, carry, dil=dil, n_q=n_q, t_ref=t_ref, first=pat == 0, pieces=pieces, missing=missing,
                     q_off=s * sub_rows, unroll=unroll):
                ok = key_col(n_q, len(pieces) * n_q) >= missing
                for u in range(unroll):
                    r = it * unroll + u
                    rows = pl.ds(q_off + r, n_q, stride=dil)
                    gather = lambda refs, p: jnp.concatenate(
                        [refs[j][p, pl.ds(off + r, n_q, stride=dil), :] for j, off in pieces], axis=0)
                    fold(first, rows, [_band_attention(q_ref[p, rows, :], gather(k_refs, p), gather(v_refs, p),
                                                       t_ref, ok, p) for p in range(N_PAIRS)])
                return carry

            lax.fori_loop(0, dil // unroll, body, 0)

    for p in range(N_PAIRS):
        o_ref[:, p * LANES:(p + 1) * LANES] = (parts[0, p] / parts[2, p]).astype(o_ref.dtype)


def _att_prompt(q, k, v, n_batch, seq):
    nb = seq // ATT_Q
    assert all(w // d == ATT_BLOCK and ATT_Q % min(w, ATT_Q) == 0 and w % min(w, ATT_Q) == 0
               for w, d in DIL_PATTERNS)
    assert ATT_BACK == DIL_PATTERNS[-1][0] // ATT_Q and DIL_PATTERNS[0] == (ATT_BLOCK, 1)
    blk = lambda j: pl.BlockSpec((N_PAIRS, ATT_Q, LANES),
                                 lambda b, i: (0, b * nb + jnp.maximum(i - j, 0), 0))
    tables = [_band_table(ATT_BLOCK, 2 * ATT_BLOCK, 1)] + [
        _band_table(min(w, ATT_Q) // d, (w // min(w, ATT_Q) + 1) * (min(w, ATT_Q) // d), d)
        for w, d in DIL_PATTERNS[1:]]
    back = [blk(j) for j in range(ATT_BACK + 1)]
    return pl.pallas_call(
        _attp_kernel,
        grid=(n_batch, nb),
        in_specs=[blk(0)] + back + back + [pl.BlockSpec(t.shape, lambda b, i: (0, 0, 0)) for t in tables],
        out_specs=pl.BlockSpec((ATT_Q, D_HALF), lambda b, i: (b * nb + i, 0)),
        out_shape=jax.ShapeDtypeStruct((n_batch * seq, D_HALF), BF16),
        scratch_shapes=[pltpu.VMEM((3, N_PAIRS, ATT_Q, LANES), F32),
                        pltpu.VMEM((N_PAIRS, ATT_Q + ATT_BLOCK, LANES), F32),
                        pltpu.VMEM((N_PAIRS, ATT_Q + ATT_BLOCK, LANES), F32)],
        compiler_params=_params(2, VMEM_LIMIT),
        name="att_prompt",
    )(q, *([k] * (ATT_BACK + 1)), *([v] * (ATT_BACK + 1)), *tables)


def _atts_tables(buf, t_new):
    pos_q = buf + np.arange(t_new)
    pos_k = np.concatenate([np.arange(buf), buf + np.arange(LANES)])
    dist = pos_q[:, None] - pos_k[None, :]
    count = np.zeros(dist.shape)
    for window, dilation in DIL_PATTERNS:
        count += (dist >= 0) & (dist <= window) & (dist % dilation == 0)
    count[:, buf + t_new:] = 0
    bias = -_alibi_slopes()[:, None, None] * dist[None].astype(np.float64)
    table = np.where(count[None] > 0, bias + np.log(np.maximum(count, 1.0))[None], NEG_INF)
    table = table.reshape(N_HEADS * t_new, buf + LANES)
    return jnp.asarray(table[:, :buf], F32), jnp.asarray(table[:, buf:], F32)


def _atts_kernel(q_ref, kt_ref, kn_ref, vt_ref, vn_ref, tc_ref, tn_ref, o_ref, *, t_new):
    rows = N_HEADS * t_new
    row_head = lax.broadcasted_iota(jnp.int32, (rows, D_HALF), 0) // t_new
    col_head = lax.broadcasted_iota(jnp.int32, (rows, D_HALF), 1) // HEAD_DIM
    own = row_head == col_head
    q = q_ref[...] * HEAD_DIM ** -0.5
    q_heads = jnp.where(own, jnp.concatenate([q] * N_HEADS, axis=0), 0.0).astype(BF16)
    pad = jnp.zeros((LANES - t_new, D_HALF), F32)
    kn = jnp.concatenate([kn_ref[...], pad], axis=0).astype(BF16)
    vn = jnp.concatenate([vn_ref[...], pad], axis=0).astype(BF16)
    sc = _dot(q_heads, kt_ref[...].astype(BF16)) + tc_ref[...]
    sn = _dot_nt(q_heads, kn) + tn_ref[...]
    m = jnp.maximum(jnp.max(sc, axis=-1, keepdims=True), jnp.max(sn, axis=-1, keepdims=True))
    ec = jnp.exp(sc - m)
    en = jnp.exp(sn - m)
    l = jnp.sum(ec, axis=-1, keepdims=True) + jnp.sum(en, axis=-1, keepdims=True)
    acc = _dot_nt(ec.astype(BF16), vt_ref[...].astype(BF16)) + _dot(en.astype(BF16), vn)
    acc = jnp.where(own, acc, 0.0) / l
    out = acc[:t_new]
    for h in range(1, N_HEADS):
        out = out + acc[h * t_new:(h + 1) * t_new]
    o_ref[...] = out.astype(o_ref.dtype)


def _att_sample(q, k_new, v_new, cache_k, cache_v, n_batch, t_new):
    buf = cache_k.shape[1]
    tc, tn = _atts_tables(buf, t_new)
    by_pos = lambda c: c.transpose(0, 2, 3, 1).reshape(n_batch, D_HALF, buf)
    rows = pl.BlockSpec((t_new, D_HALF), lambda b: (b, 0))
    cache = pl.BlockSpec((None, D_HALF, buf), lambda b: (b, 0, 0))
    return pl.pallas_call(
        functools.partial(_atts_kernel, t_new=t_new),
        grid=(n_batch,),
        in_specs=[rows, cache, rows, cache, rows,
                  pl.BlockSpec((N_HEADS * t_new, buf), lambda b: (0, 0)),
                  pl.BlockSpec((N_HEADS * t_new, LANES), lambda b: (0, 0))],
        out_specs=rows,
        out_shape=jax.ShapeDtypeStruct((n_batch * t_new, D_HALF), BF16),
        compiler_params=_params(1, VMEM_LIMIT),
        name="att_sample",
    )(q, by_pos(cache_k), k_new, by_pos(cache_v), v_new, tc, tn)


def _out_kernel(ret_ref, att_ref, x_ref, gt_ref, sc_ref, sh_ref, w_ref, gpost_ref, gpre_ref, wr_ref, br_ref,
                x1_ref, h_ref, top_ref):
    tm = x_ref.shape[0]
    mix = _dot(ret_ref[...], w_ref[:D_HALF, :]) + _dot(att_ref[...], w_ref[D_HALF:, :])
    x1 = x_ref[...] + gt_ref[...] * _rms(mix, gpost_ref[...])
    x1_ref[...] = x1
    h = _rms(x1, gpre_ref[...]) * (1.0 + sc_ref[...]) + sh_ref[...]
    _slab_store(h_ref, h)
    split = lambda t: (t.astype(BF16), (t - t.astype(BF16).astype(F32)).astype(BF16))
    (h_hi, h_lo), (w_hi, w_lo) = split(h), split(wr_ref[...])
    logits = _dot(h_hi, w_hi) + _dot(h_lo, w_hi) + _dot(h_hi, w_lo) + br_ref[...]
    lane_e = lax.broadcasted_iota(jnp.int32, (tm, N_EXPERTS), 1).astype(F32)
    lane_o = lax.broadcasted_iota(jnp.int32, (tm, LANES), 1)
    vals, idxs = [], []
    work = logits
    for _ in range(TOP_K):
        v = jnp.max(work, axis=-1, keepdims=True)
        i = jnp.min(jnp.where(work == v, lane_e, float(N_EXPERTS)), axis=-1, keepdims=True)
        vals.append(v)
        idxs.append(i)
        work = jnp.where(lane_e == i, -jnp.inf, work)
    es = [jnp.exp(v - vals[0]) for v in vals]
    tot = sum(es)
    packed = jnp.zeros((tm, LANES), F32)
    for j in range(TOP_K):
        packed = jnp.where(lane_o == j, idxs[j], packed)
        packed = jnp.where(lane_o == TOP_K + j, es[j] / tot, packed)
    top_ref[...] = packed


def _mixer_out(ret_y, att, x, mod, w_out_bf, g_post, g_pre, w_router, b_router, per_row, rows_per_batch):
    n = x.shape[0]
    tm = min(TM_OUT, n)
    tpb = 1 if per_row else rows_per_batch // tm
    half = pl.BlockSpec((tm, D_HALF), lambda i: (i, 0))
    full = pl.BlockSpec((tm, D_MODEL), lambda i: (i, 0))
    stat = pl.BlockSpec((tm, LANES), lambda i: (i, 0))
    vec = pl.BlockSpec((1, D_MODEL), lambda i: (0, 0))
    mods = [_mod_spec(tm, c, per_row, tpb) for c in (MOD_GATE_MIX, MOD_SCALE_FFN, MOD_SHIFT_FFN)]
    return pl.pallas_call(
        _out_kernel,
        grid=(n // tm,),
        in_specs=[half, half, full] + mods + [
                  pl.BlockSpec((D_MODEL, D_MODEL), lambda i: (0, 0)), vec, vec,
                  pl.BlockSpec((D_MODEL, N_EXPERTS), lambda i: (0, 0)),
                  pl.BlockSpec((1, N_EXPERTS), lambda i: (0, 0))],
        out_specs=[full, pl.BlockSpec((tm * SLAB, LANES), lambda i: (i, 0)), stat],
        out_shape=[jax.ShapeDtypeStruct((n, D_MODEL), F32),
                   jax.ShapeDtypeStruct((n * SLAB, LANES), F32),
                   jax.ShapeDtypeStruct((n, LANES), F32)],
        compiler_params=_params(1, VMEM_LIMIT),
        name="mixer_out",
    )(ret_y, att, x, mod, mod, mod, w_out_bf, g_post, g_pre, w_router, b_router)


def _route_kernel(top_ref, dest_ref, ends_ref, counts_ref, running, starts, *, tm, tm_blk):
    ph, i = pl.program_id(0), pl.program_id(1)
    lane_f = lax.broadcasted_iota(jnp.int32, (tm, LANES), 1).astype(F32)
    top = top_ref[...]
    picks = [lane_f == top[:, k:k + 1] for k in range(TOP_K)]
    chosen = sum(p.astype(F32) for p in picks)
    tile_counts = jnp.sum(chosen, axis=0, keepdims=True)

    @pl.when((ph == 0) & (i == 0))
    def _():
        running[...] = jnp.zeros_like(running)

    @pl.when(ph == 0)
    def _():
        running[...] += tile_counts

    @pl.when((ph == 1) & (i == 0))
    def _():
        lane = lax.broadcasted_iota(jnp.int32, (1, LANES), 1)
        assert tm_blk & (tm_blk - 1) == 0
        counts = running[...].astype(jnp.int32)
        counts_ref[...] = counts
        blocks = (counts + (tm_blk - 1)) >> (tm_blk.bit_length() - 1)
        ends = blocks
        shift = 1
        while shift < N_EXPERTS:
            ends = ends + jnp.where(lane >= shift, pltpu.roll(ends, shift, axis=1), 0)
            shift *= 2
        ends_ref[...] = ends
        starts[...] = ((ends - blocks) * tm_blk).astype(F32)
        running[...] = jnp.zeros_like(running)

    @pl.when(ph == 1)
    def _():
        earlier = (lax.broadcasted_iota(jnp.int32, (tm, tm), 0)
                   > lax.broadcasted_iota(jnp.int32, (tm, tm), 1))
        ahead = _dot(jnp.where(earlier, 1.0, 0.0).astype(BF16), chosen.astype(BF16))
        slot = starts[...] + running[...] + ahead
        lane_o = lax.broadcasted_iota(jnp.int32, (tm, LANES), 1)
        packed = jnp.zeros((tm, LANES), F32)
        for k, pick in enumerate(picks):
            dest_k = jnp.sum(jnp.where(pick, slot, 0.0), axis=-1, keepdims=True)
            packed = jnp.where(lane_o == k, dest_k, packed)
        dest_ref[...] = packed.astype(jnp.int32)
        running[...] += tile_counts


def _route(top, tm_blk):
    n = top.shape[0]
    tm = TM_ROUTE
    per_expert = pl.BlockSpec((1, LANES), lambda ph, i: (0, 0))
    return pl.pallas_call(
        functools.partial(_route_kernel, tm=tm, tm_blk=tm_blk),
        grid=(2, n // tm),
        in_specs=[pl.BlockSpec((tm, LANES), lambda ph, i: (i, 0))],
        out_specs=[pl.BlockSpec((tm, LANES), lambda ph, i: (i * ph, 0)), per_expert, per_expert],
        out_shape=[jax.ShapeDtypeStruct((n, LANES), jnp.int32),
                   jax.ShapeDtypeStruct((1, LANES), jnp.int32), jax.ShapeDtypeStruct((1, LANES), jnp.int32)],
        scratch_shapes=[pltpu.VMEM((1, LANES), F32), pltpu.VMEM((1, LANES), F32)],
        compiler_params=_params(2),
        name="route",
    )(top)


def _dispatch_kernel(ends_ref, dest_ref, *rest, tm, tm_blk, n_blocks, tiles):
    h_refs, (xs_hbm, zeros, sem) = rest[:-3], rest[-3:]
    i = pl.program_id(0)

    def fill(b):
        rows = pl.ds(pl.multiple_of(b * (tm_blk * SLAB), tm_blk * SLAB), tm_blk * SLAB)
        return pltpu.make_async_copy(zeros, xs_hbm.at[rows], sem.at[1])

    def fill_start(b, carry):
        fill(b).start()
        return carry

    def fill_wait(b, carry):
        fill(b).wait()
        return carry

    def zero_padding():
        zeros[...] = jnp.zeros_like(zeros)
        n_used = ends_ref[N_EXPERTS - 1]
        for do in (fill_start, fill_wait):
            for e in range(N_EXPERTS):
                first = ends_ref[e - 1] if e else 0

                @pl.when(ends_ref[e] > first)
                def _():
                    do(ends_ref[e] - 1, 0)

            lax.fori_loop(n_used, n_blocks, do, 0)

    pl.when(i == 0)(zero_padding)

    def scatter(h_ref):
        def body(jj, carry):
            for j in (jj * DMA_UNROLL + u for u in range(DMA_UNROLL)):
                for k in range(TOP_K):
                    pltpu.make_async_copy(h_ref.at[_slab_rows(j)], xs_hbm.at[_slab_rows(dest_ref[j * TOP_K + k])],
                                          sem.at[0]).start(priority=k % 2)
            return carry

        lax.fori_loop(0, tm // DMA_UNROLL, body, 0)
        for k in range(TOP_K):
            pltpu.make_async_copy(h_ref, xs_hbm.at[pl.ds(0, tm * SLAB)], sem.at[0]).wait()

    start = 0
    for h_ref, n_tiles in zip(h_refs, tiles):
        pl.when((i >= start) & (i < start + n_tiles))(functools.partial(scatter, h_ref))
        start += n_tiles


def _dispatch(hs, dest_flat, ends, n_blocks, tm_blk):
    tm = TM_DISPATCH
    tiles = [h.shape[0] // SLAB // tm for h in hs]
    starts = [sum(tiles[:s]) for s in range(len(hs))]
    src = lambda s: pl.BlockSpec((tm * SLAB, LANES),
                                 lambda i, ends: (jnp.clip(i - starts[s], 0, tiles[s] - 1), 0))
    grid_spec = pltpu.PrefetchScalarGridSpec(
        num_scalar_prefetch=1,
        grid=(sum(tiles),),
        in_specs=[pl.BlockSpec((tm * TOP_K,), lambda i, ends: (i,), memory_space=pltpu.SMEM)]
        + [src(s) for s in range(len(hs))],
        out_specs=pl.BlockSpec(memory_space=pl.ANY),
        scratch_shapes=[pltpu.VMEM((tm_blk * SLAB, LANES), F32), pltpu.SemaphoreType.DMA((2,))],
    )
    return pl.pallas_call(
        functools.partial(_dispatch_kernel, tm=tm, tm_blk=tm_blk, n_blocks=n_blocks, tiles=tiles),
        grid_spec=grid_spec,
        out_shape=jax.ShapeDtypeStruct((n_blocks * tm_blk * SLAB, LANES), F32),
        compiler_params=_params(1),
        name="dispatch",
    )(ends, dest_flat, *hs)


def _moe_kernel(be_ref, nu_ref, valid_ref, nxt_ref, prep_ref, slot_ref, x_ref, wgu_hbm, wd_hbm,
                bg_ref, bu_ref, bd_ref, o_ref, wgu_buf, wd_buf, wt, wg_t, wu_t, wd_bf, sem):
    i = pl.program_id(0)
    used = i < nu_ref[0]
    valid = valid_ref[i]
    s = slot_ref[i]
    tm = x_ref.shape[0] // SLAB
    _, d, d_gu = wgu_buf.shape
    d_ff = d_gu // 2
    new_expert = (i == 0) | (be_ref[i] != be_ref[jnp.maximum(i - 1, 0)])

    def weights(e, slot):
        return (pltpu.make_async_copy(wgu_hbm.at[e], wgu_buf.at[slot], sem.at[0, slot]),
                pltpu.make_async_copy(wd_hbm.at[e], wd_buf.at[slot], sem.at[1, slot]))

    def arrived(e, slot):
        for copy in weights(e, slot):
            copy.wait()

    def prepare(slot):
        for c in range(d // LANES):
            wt[...] = wgu_buf[slot, pl.ds(c * LANES, LANES), :].T
            rows = slice(c * LANES, (c + 1) * LANES)
            wg_t[slot, :, rows] = wt[pl.ds(0, d_ff, stride=2), :].astype(BF16)
            wu_t[slot, :, rows] = wt[pl.ds(1, d_ff, stride=2), :].astype(BF16)
        wd_bf[slot] = wd_buf[slot].astype(BF16)

    @pl.when(i == 0)
    def _():
        for copy in weights(be_ref[0], s):
            copy.start()

    @pl.when(used & new_expert & (nxt_ref[i] >= 0))
    def _():
        for copy in weights(nxt_ref[i], 1 - s):
            copy.start()

    @pl.when(i == 0)
    def _():
        arrived(be_ref[0], s)
        prepare(s)

    def experts_mlp(rows, prep_next):
        if prep_next:
            arrived(prep_ref[i], 1 - s)
        x = _slab_load(x_ref, rows).astype(BF16)
        acc = bd_ref[...]
        for c in range(d_ff // MOE_COLS):
            cols = slice(c * MOE_COLS, (c + 1) * MOE_COLS)
            gate = jnp.minimum(_dot_nt(x, wg_t[s, cols, :]) + bg_ref[:, cols], SWIGLU_LIMIT)
            up = jnp.clip(_dot_nt(x, wu_t[s, cols, :]) + bu_ref[:, cols], -SWIGLU_LIMIT, SWIGLU_LIMIT)
            act = gate * jax.nn.sigmoid(SWIGLU_ALPHA * gate) * (up + 1.0)
            acc = acc + _dot(act.astype(BF16), wd_bf[s, cols, :])
        _slab_store(o_ref, acc)
        if rows < tm:
            o_ref[rows * SLAB:, :] = jnp.zeros(((tm - rows) * SLAB, LANES), o_ref.dtype)
        if prep_next:
            prepare(1 - s)

    sizes = [tm // 4, tm // 2, tm]
    prep = used & (prep_ref[i] >= 0)
    full = valid > sizes[-2]
    @pl.when(prep & ~full)
    def _():
        arrived(prep_ref[i], 1 - s)
        prepare(1 - s)
    pl.when(prep & full)(functools.partial(experts_mlp, tm, True))
    for lo, rows in zip([0] + sizes[:-1], sizes):
        fits = used & (valid > lo) & (valid <= rows)
        pl.when(fits & ~(prep & full))(functools.partial(experts_mlp, rows, False))

    @pl.when(~used)
    def _():
        o_ref[...] = jnp.zeros_like(o_ref)


def _moe_plan(ends, counts, n_blocks, tm):
    blk = jnp.arange(n_blocks, dtype=jnp.int32)
    e_ids = jnp.arange(N_EXPERTS, dtype=jnp.int32)
    block_e = jnp.minimum(jnp.sum(blk[:, None] >= ends[None, :], axis=1), N_EXPERTS - 1).astype(jnp.int32)
    mine = block_e[:, None] == e_ids[None, :]
    pick = lambda per_e: jnp.sum(jnp.where(mine, per_e[None, :], 0), axis=1)
    first_blk = ends - (counts + tm - 1) // tm
    valid = jnp.clip(pick(counts) - (blk - pick(first_blk)) * tm, 0, tm)
    valid = jnp.where(blk < ends[N_EXPERTS - 1], valid, 0).astype(jnp.int32)
    used = counts > 0
    slot_e = (jnp.cumsum(used.astype(jnp.int32)) - 1) % 2
    later = (e_ids[None, :] > e_ids[:, None]) & used[None, :]
    next_e = jnp.min(jnp.where(later, e_ids[None, :], N_EXPERTS), axis=1)
    next_e = jnp.where(next_e < N_EXPERTS, next_e, -1)
    in_use = valid > 0
    nxt = jnp.where((blk == pick(first_blk)) & in_use, pick(next_e), -1).astype(jnp.int32)
    n_blk = pick((counts + tm - 1) // tm)
    prep_blk = pick(first_blk) + jnp.where(n_blk >= 3, n_blk - 2, n_blk - 1)
    prep = jnp.where((blk == prep_blk) & in_use, pick(next_e), -1).astype(jnp.int32)
    return block_e, ends[N_EXPERTS - 1:], valid, nxt, prep, pick(slot_e).astype(jnp.int32)


def _moe_blocks(xs, block_e, n_used, valid, nxt, prep, slot, w_gu, w_down, b_gu, b_down, tm):
    n_blocks = block_e.shape[0]
    n_e, d, d_gu = w_gu.shape
    d_ff = d_gu // 2
    bias = lambda n: pl.BlockSpec((None, 1, n), lambda i, be, *_: (be[i], 0, 0))
    grid_spec = pltpu.PrefetchScalarGridSpec(
        num_scalar_prefetch=6,
        grid=(n_blocks,),
        in_specs=[pl.BlockSpec((tm * SLAB, LANES), lambda i, be, nu, *_: (jnp.minimum(i, nu[0] - 1), 0)),
                  pl.BlockSpec(memory_space=pl.ANY), pl.BlockSpec(memory_space=pl.ANY),
                  bias(d_ff), bias(d_ff), bias(d)],
        out_specs=pl.BlockSpec((tm * SLAB, LANES), lambda i, *_: (i, 0)),
        scratch_shapes=[pltpu.VMEM((2, d, d_gu), F32), pltpu.VMEM((2, d_ff, d), F32),
                        pltpu.VMEM((d_gu, LANES), F32), pltpu.VMEM((2, d_ff, d), BF16),
                        pltpu.VMEM((2, d_ff, d), BF16), pltpu.VMEM((2, d_ff, d), BF16),
                        pltpu.SemaphoreType.DMA((2, 2))],
    )
    return pl.pallas_call(
        _moe_kernel,
        grid_spec=grid_spec,
        out_shape=jax.ShapeDtypeStruct((n_blocks * tm * SLAB, LANES), F32),
        compiler_params=_params(1, VMEM_LIMIT),
        name="moe_blocks",
    )(block_e, n_used, valid, nxt, prep, slot, xs, w_gu, w_down, b_gu[:, None, 0::2], b_gu[:, None, 1::2],
      b_down.reshape(n_e, 1, d))


def _fin_kernel(dest_ref, next_ref, rows_hbm, top_ref, x1_ref, gt_ref, g_ref, o_ref, buf, sem, *, tm):
    i = pl.program_id(0)
    slot = i % 2

    def gather(idx_ref, s):
        def body(jj, carry):
            for j in (jj * DMA_UNROLL + u for u in range(DMA_UNROLL)):
                for k in range(TOP_K):
                    pltpu.make_async_copy(rows_hbm.at[_slab_rows(idx_ref[j * TOP_K + k])],
                                          buf.at[s, k, _slab_rows(j)], sem.at[s]).start(priority=k % 2)
            return carry
        lax.fori_loop(0, tm // DMA_UNROLL, body, 0)

    @pl.when(i == 0)
    def _():
        gather(dest_ref, 0)

    @pl.when(i + 1 < pl.num_programs(0))
    def _():
        gather(next_ref, 1 - slot)

    for k in range(TOP_K):
        pltpu.make_async_copy(rows_hbm.at[pl.ds(0, tm * SLAB)], buf.at[slot, k], sem.at[slot]).wait()
    top = top_ref[...]
    f = sum(_slab_load(buf.at[slot, k], tm) * top[:, TOP_K + k:TOP_K + k + 1] for k in range(TOP_K))
    o_ref[...] = x1_ref[...] + gt_ref[...] * _rms(f, g_ref[...])


def _finish(dest, rows, top, x1, mod, g_post, per_row, rows_per_batch):
    n = x1.shape[0]
    tm = min(TM_FIN, n)
    full = pl.BlockSpec((tm, D_MODEL), lambda i: (i, 0))
    gate = _mod_spec(tm, MOD_GATE_FFN, per_row, 1 if per_row else rows_per_batch // tm)
    return pl.pallas_call(
        functools.partial(_fin_kernel, tm=tm),
        grid=(n // tm,),
        in_specs=[pl.BlockSpec((tm * TOP_K,), lambda i: (i,), memory_space=pltpu.SMEM),
                  pl.BlockSpec((tm * TOP_K,), lambda i: (jnp.minimum(i + 1, n // tm - 1),),
                               memory_space=pltpu.SMEM),
                  pl.BlockSpec(memory_space=pl.ANY),
                  pl.BlockSpec((tm, LANES), lambda i: (i, 0)), full, gate,
                  pl.BlockSpec((1, D_MODEL), lambda i: (0, 0))],
        out_specs=full,
        out_shape=jax.ShapeDtypeStruct((n, D_MODEL), F32),
        scratch_shapes=[pltpu.VMEM((2, TOP_K, tm * SLAB, LANES), F32), pltpu.SemaphoreType.DMA((2,))],
        compiler_params=_params(1),
        name="finish",
    )(dest, dest, rows, top, x1, mod, g_post)


def kernel(x_prompt, x_sample, state_ret, cache_win_k, cache_win_v, c_prompt, c_sample, w_ada, b_ada,
           g_pre_mix, g_post_mix, g_pre_ffn, g_post_ffn, w_in, g_ret, w_out, w_router, b_router,
           w_gate_up, b_gate_up, w_down, b_down):
    depth = w_in.shape[0]
    assert depth == 1, "single-layer step"
    n_b, seq, d = x_prompt.shape
    n_db, t_new, _ = x_sample.shape
    n_p, n_s = n_b * seq, n_db * t_new
    buf = cache_win_k.shape[2]
    keep = min(DIL_PATTERNS[-1][0], seq)
    vec = lambda g: g[0].reshape(1, -1)

    mod_p, mod_s = _ada(c_prompt, c_sample, w_ada[0], b_ada[0], t_new)
    mod_p = mod_p[:n_b].reshape(n_b, 1, -1)
    mod_s = mod_s.reshape(n_s, -1)

    w_in_bf = w_in[0].astype(BF16)
    w_out_bf = w_out[0].astype(BF16)
    xp = x_prompt.reshape(n_p, d)
    xs = x_sample.reshape(n_s, d)
    router = (w_router[0], b_router[0].reshape(1, -1))

    qr, kr, vr, gr, qa_pm, ka_pm, va_pm, k_win, v_win = _in_proj(xp, mod_p, vec(g_pre_mix), w_in_bf, False, seq, keep)
    ret_y_p, ret_state_p = _retention(qr, kr, vr, gr, None, g_ret[0], n_b, seq, RET_CHUNK, n_b)
    att_p = _att_prompt(qa_pm, ka_pm, va_pm, n_b, seq)
    x1_p, h_p, top_p = _mixer_out(ret_y_p, att_p, xp, mod_p, w_out_bf, vec(g_post_mix), vec(g_pre_ffn),
                                  *router, False, seq)

    qr_s, kr_s, vr_s, gr_s, qa_s, ka_s, va_s = _in_proj(xs, mod_s, vec(g_pre_mix), w_in_bf, True, t_new)
    ret_y_s, ret_state_s = _retention(qr_s, kr_s, vr_s, gr_s, state_ret[0], g_ret[0], n_db, t_new, t_new,
                                      RET_GROUP_SAMPLE)
    att_s = _att_sample(qa_s, ka_s, va_s, cache_win_k[0], cache_win_v[0], n_db, t_new)
    x1_s, h_s, top_s = _mixer_out(ret_y_s, att_s, xs, mod_s, w_out_bf, vec(g_post_mix), vec(g_pre_ffn),
                                  *router, True, t_new)

    n_blocks = (n_p + n_s) * TOP_K // TM_MOE + N_EXPERTS
    dest, ends, counts = _route(jnp.concatenate([top_p, top_s], axis=0), TM_MOE)
    dest = dest[:, :TOP_K].reshape(-1)
    ends, counts = ends[0, :N_EXPERTS], counts[0, :N_EXPERTS]
    rows_in = _dispatch([h_p, h_s], dest, ends, n_blocks, TM_MOE)
    rows = _moe_blocks(rows_in, *_moe_plan(ends, counts, n_blocks, TM_MOE), w_gate_up[0], w_down[0],
                       b_gate_up[0], b_down[0], TM_MOE)
    y_p = _finish(dest[:n_p * TOP_K], rows, top_p, x1_p, mod_p, vec(g_post_ffn), False, seq)
    y_s = _finish(dest[n_p * TOP_K:], rows, top_s, x1_s, mod_s, vec(g_post_ffn), True, t_new)

    win = lambda t: t.reshape(n_b, N_HEADS, HEAD_DIM, keep).transpose(0, 3, 1, 2)[None]
    new = lambda t: t.reshape(n_db, t_new, N_HEADS, HEAD_DIM)[None]
    return (y_p.reshape(n_b, seq, d), y_s.reshape(n_db, t_new, d), ret_state_p[None], ret_state_s[None],
            win(k_win), win(v_win), new(ka_s), new(va_s))
```
